```python
import math
import jax, jax.numpy as jnp
from jax import lax
import numpy as np

D_MODEL = 1024
BATCH = 16
SEQ = 2048
DEPTH = 4
DEC_BATCH = 16
DEC_SEQ = 32
PAST_LEN = 2048

CHUNK = 64
QBLK = 128
EPS = 1e-6
MLA_HEADS = 8
NOPE_DIM = 64
ROPE_DIM = 32
V_DIM = 64
Q_LORA = 384
KV_LORA = 256
ROPE_THETA = 10000.0
MLA_SCALE = (NOPE_DIM + ROPE_DIM) ** -0.5
DSA_HEADS = 8
DSA_KV_HEADS = 2
DSA_GROUP = DSA_HEADS // DSA_KV_HEADS
DSA_HEAD_DIM = 64
DSA_SCALE = DSA_HEAD_DIM ** -0.5
IDX_HEADS = 8
IDX_DIM = 64
IDX_W_SCALE = (IDX_HEADS ** -0.5) * (IDX_DIM ** -0.5)
TOPK_MAX = 256
NUM_BUCKETS = 32
MAX_DISTANCE = 128
W_A = MLA_HEADS * V_DIM
W_B = DSA_HEADS * DSA_HEAD_DIM
SPLIT_SIZES = (Q_LORA, KV_LORA, ROPE_DIM, W_A,
               W_B, DSA_KV_HEADS * DSA_HEAD_DIM, DSA_KV_HEADS * DSA_HEAD_DIM,
               IDX_HEADS * IDX_DIM, IDX_DIM, IDX_HEADS, W_B,
               D_MODEL, D_MODEL)
IN_WIDTH = sum(SPLIT_SIZES)

kernel_name = "hybrid_mla_dsa_streaming_step"


def rmsnorm(x, g):
    xf = x.astype(jnp.float32)
    y = xf * lax.rsqrt(jnp.mean(xf * xf, axis=-1, keepdims=True) + EPS)
    return y.astype(x.dtype) * g


def rope(x, pos):
    half = x.shape[-1] // 2
    inv = ROPE_THETA ** (-jnp.arange(half, dtype=jnp.float32) / half)
    ang = pos.astype(jnp.float32)[:, None] * inv
    ang = ang.reshape((ang.shape[0],) + (1,) * (x.ndim - 3) + (half,))
    cos = jnp.cos(ang).astype(x.dtype)
    sin = jnp.sin(ang).astype(x.dtype)
    x1, x2 = x[..., :half], x[..., half:]
    return jnp.concatenate([x1 * cos - x2 * sin, x2 * cos + x1 * sin], axis=-1)


def chunk_visible(qpos, kpos):
    return (kpos[None, :] // CHUNK) <= (qpos[:, None] // CHUNK)


def rel_bucket(rel):
    half = NUM_BUCKETS // 2
    max_exact = half // 2
    n = jnp.abs(rel)
    large = max_exact + (jnp.log(jnp.maximum(n, 1).astype(jnp.float32) / max_exact)
                         / math.log(MAX_DISTANCE / max_exact) * (half - max_exact)).astype(jnp.int32)
    large = jnp.minimum(large, half - 1)
    return jnp.where(n < max_exact, n, large) + jnp.where(rel > 0, half, 0)


def mixer_inputs(x, c, pos, w_ada, b_ada, g_norm, w_in, g_qnorm, w_uq, g_kvnorm):
    B, T, _ = x.shape
    shift, scale, gate = jnp.split(jax.nn.silu(c) @ w_ada + b_ada, 3, axis=-1)
    h = rmsnorm(x, g_norm) * (1 + scale[:, None]) + shift[:, None]
    z = h @ w_in
    split_idx = np.cumsum(SPLIT_SIZES)[:-1].tolist()
    cq, ckv, kr, ga, qb, kb, vb, qi, ki, wi, gb, ma, mb = jnp.split(z, split_idx, axis=-1)
    q = (rmsnorm(cq, g_qnorm) @ w_uq).reshape(B, T, MLA_HEADS, NOPE_DIM + ROPE_DIM)
    q_nope = q[..., :NOPE_DIM]
    q_rope = rope(q[..., NOPE_DIM:], pos)
    ckv = rmsnorm(ckv, g_kvnorm)
    kr = rope(kr, pos)
    qb = qb.reshape(B, T, DSA_KV_HEADS, DSA_GROUP, DSA_HEAD_DIM)
    kb = kb.reshape(B, T, DSA_KV_HEADS, DSA_HEAD_DIM)
    vb = vb.reshape(B, T, DSA_KV_HEADS, DSA_HEAD_DIM)
    qi = qi.reshape(B, T, IDX_HEADS, IDX_DIM)
    wi = wi * IDX_W_SCALE
    return gate, q_nope, q_rope, ckv, kr, ga, qb, kb, vb, qi, ki, wi, gb, ma, mb


def mla_expand(ckv, w_ukv):
    B, S, _ = ckv.shape
    kv = (ckv @ w_ukv).reshape(B, S, MLA_HEADS, NOPE_DIM + V_DIM)
    return kv[..., :NOPE_DIM], kv[..., NOPE_DIM:]


def mla_attend(qpos, q_nope, q_rope, kpos, k_nope, k_rope, v):
    B, Q = q_nope.shape[:2]
    s = (jnp.einsum('bqhn,bshn->bhqs', q_nope, k_nope)
         + jnp.einsum('bqhr,bsr->bhqs', q_rope, k_rope)).astype(jnp.float32) * MLA_SCALE
    s = jnp.where(chunk_visible(qpos, kpos), s, -jnp.inf)
    p = jax.nn.softmax(s, axis=-1).astype(v.dtype)
    o = jnp.einsum('bhqs,bshv->bqhv', p, v)
    return o.reshape(B, Q, MLA_HEADS * V_DIM)


def dsa_attend(qpos, q, qi, wi, kpos, k, v, ki, topk, rel_table):
    B, Q = q.shape[:2]
    score = jnp.einsum('bqhd,bsd->bqhs', qi, ki)
    idx_score = jnp.einsum('bqh,bqhs->bqs', wi, jax.nn.relu(score)).astype(jnp.float32)
    idx_score = jnp.where(chunk_visible(qpos, kpos)[None], idx_score, -jnp.inf)
    vals, idx = lax.top_k(idx_score, topk)
    valid = jnp.isfinite(vals)
    kg = jax.vmap(lambda kb_, ib_: kb_[ib_])(k, idx)
    vg = jax.vmap(lambda vb_, ib_: vb_[ib_])(v, idx)
    logits = jnp.einsum('bqgrd,bqkgd->bqgrk', q, kg).astype(jnp.float32) * DSA_SCALE
    bucket = rel_bucket(kpos[idx] - qpos[None, :, None])
    bias = rel_table[bucket].astype(jnp.float32)
    bias = jnp.moveaxis(bias.reshape(B, Q, topk, DSA_KV_HEADS, DSA_GROUP), 2, -1)
    logits = jnp.where(valid[:, :, None, None, :], logits + bias, -jnp.inf)
    p = jax.nn.softmax(logits, axis=-1).astype(vg.dtype)
    o = jnp.einsum('bqgrk,bqkgd->bqgrd', p, vg)
    return o.reshape(B, Q, DSA_HEADS * DSA_HEAD_DIM)


def over_query_blocks(fn, qpos, *qs):
    T = qpos.shape[0]
    nb = T // QBLK

    def split(a):
        return jnp.moveaxis(a.reshape((a.shape[0], nb, QBLK) + a.shape[2:]), 1, 0)

    out = lax.map(lambda args: fn(*args), (qpos.reshape(nb, QBLK),) + tuple(split(a) for a in qs))
    out = jnp.moveaxis(out, 0, 1)
    return out.reshape((out.shape[0], T) + out.shape[3:])


def mixer_output(x, gate, oa, ga, ob, gb, ma, mb, w_oa, w_ob, w_out):
    ya = (oa * jax.nn.silu(ga)) @ w_oa
    yb = (ob * jax.nn.silu(gb)) @ w_ob
    m = jax.nn.sigmoid(ma) * ya + jax.nn.sigmoid(mb) * yb
    return x + gate[:, None] * (m @ w_out)


def setup_inputs(seed: int = 0) -> dict:
    key = jax.random.key(seed)
    ks = jax.random.split(key, 24)
    nrm = jax.random.normal
    f32 = jnp.float32
    H = MLA_HEADS
    return {
        "x_prompt": nrm(ks[0], (BATCH, SEQ, D_MODEL), f32),
        "x_sample": nrm(ks[1], (DEC_BATCH, DEC_SEQ, D_MODEL), f32),
        "c_prompt": nrm(ks[2], (BATCH, D_MODEL), f32),
        "c_sample": nrm(ks[3], (DEC_BATCH, D_MODEL), f32),
        "cache_ckv": nrm(ks[4], (DEPTH, DEC_BATCH, PAST_LEN, KV_LORA), f32),
        "cache_krope": nrm(ks[5], (DEPTH, DEC_BATCH, PAST_LEN, ROPE_DIM), f32),
        "cache_kb": nrm(ks[6], (DEPTH, DEC_BATCH, PAST_LEN, DSA_KV_HEADS, DSA_HEAD_DIM), f32),
        "cache_vb": nrm(ks[7], (DEPTH, DEC_BATCH, PAST_LEN, DSA_KV_HEADS, DSA_HEAD_DIM), f32),
        "cache_kidx": nrm(ks[8], (DEPTH, DEC_BATCH, PAST_LEN, IDX_DIM), f32),
        "w_ada": nrm(ks[9], (DEPTH, D_MODEL, 3 * D_MODEL), f32) * (0.5 * D_MODEL ** -0.5),
        "b_ada": nrm(ks[10], (DEPTH, 3 * D_MODEL), f32) * 0.01,
        "g_norm": 1.0 + 0.02 * nrm(ks[11], (DEPTH, D_MODEL), f32),
        "w_in": nrm(ks[12], (DEPTH, D_MODEL, IN_WIDTH), f32) * D_MODEL ** -0.5,
        "g_qnorm": 1.0 + 0.02 * nrm(ks[13], (DEPTH, Q_LORA), f32),
        "w_uq": nrm(ks[14], (DEPTH, Q_LORA, H * (NOPE_DIM + ROPE_DIM)), f32) * Q_LORA ** -0.5,
        "g_kvnorm": 1.0 + 0.02 * nrm(ks[15], (DEPTH, KV_LORA), f32),
        "w_ukv": nrm(ks[16], (DEPTH, KV_LORA, H * (NOPE_DIM + V_DIM)), f32) * KV_LORA ** -0.5,
        "w_oa": nrm(ks[17], (DEPTH, W_A, D_MODEL), f32) * W_A ** -0.5,
        "w_ob": nrm(ks[18], (DEPTH, W_B, D_MODEL), f32) * W_B ** -0.5,
        "w_out": nrm(ks[19], (DEPTH, D_MODEL, D_MODEL), f32) * D_MODEL ** -0.5,
        "rel_bias": nrm(ks[20], (NUM_BUCKETS, DSA_HEADS), f32) * 0.5,
        "g_final": 1.0 + 0.02 * nrm(ks[21], (D_MODEL,), f32),
    }


def reference(x_prompt, x_sample, c_prompt, c_sample, cache_ckv, cache_krope, cache_kb, cache_vb,
              cache_kidx, w_ada, b_ada, g_norm, w_in, g_qnorm, w_uq, g_kvnorm, w_ukv, w_oa, w_ob,
              w_out, rel_bias, g_final):
    T_p = x_prompt.shape[1]
    T_s = x_sample.shape[1]
    past = cache_ckv.shape[2]
    pos_p = jnp.arange(T_p, dtype=jnp.int32)
    pos_s = past + jnp.arange(T_s, dtype=jnp.int32)
    kpos_s = jnp.arange(past + T_s, dtype=jnp.int32)
    topk_p = min(TOPK_MAX, T_p // 4)
    topk_s = min(TOPK_MAX, (past + T_s) // 4)

    xp, xs = x_prompt, x_sample
    p_ckv, p_kr, p_kb, p_vb, p_ki = [], [], [], [], []
    s_ckv, s_kr, s_kb, s_vb, s_ki = [], [], [], [], []
    for l in range(DEPTH):
        (gate, qn, qr, ckv, kr, ga, qb, kb, vb, qi, ki, wi, gb, ma, mb) = mixer_inputs(
            xp, c_prompt, pos_p, w_ada[l], b_ada[l], g_norm[l], w_in[l], g_qnorm[l], w_uq[l], g_kvnorm[l])
        kn, va = mla_expand(ckv, w_ukv[l])
        oa = over_query_blocks(
            lambda qp, a, b: mla_attend(qp, a, b, pos_p, kn, kr, va), pos_p, qn, qr)
        ob = over_query_blocks(
            lambda qp, a, b, w: dsa_attend(qp, a, b, w, pos_p, kb, vb, ki, topk_p, rel_bias),
            pos_p, qb, qi, wi)
        xp = mixer_output(xp, gate, oa, ga, ob, gb, ma, mb, w_oa[l], w_ob[l], w_out[l])
        p_ckv.append(ckv); p_kr.append(kr); p_kb.append(kb); p_vb.append(vb); p_ki.append(ki)

        (gate, qn, qr, ckv, kr, ga, qb, kb, vb, qi, ki, wi, gb, ma, mb) = mixer_inputs(
            xs, c_sample, pos_s, w_ada[l], b_ada[l], g_norm[l], w_in[l], g_qnorm[l], w_uq[l], g_kvnorm[l])
        ckv_all = jnp.concatenate([cache_ckv[l], ckv], axis=1)
        kr_all = jnp.concatenate([cache_krope[l], kr], axis=1)
        kn_all, va_all = mla_expand(ckv_all, w_ukv[l])
        oa = mla_attend(pos_s, qn, qr, kpos_s, kn_all, kr_all, va_all)
        kb_all = jnp.concatenate([cache_kb[l], kb], axis=1)
        vb_all = jnp.concatenate([cache_vb[l], vb], axis=1)
        ki_all = jnp.concatenate([cache_kidx[l], ki], axis=1)
        ob = dsa_attend(pos_s, qb, qi, wi, kpos_s, kb_all, vb_all, ki_all, topk_s, rel_bias)
        xs = mixer_output(xs, gate, oa, ga, ob, gb, ma, mb, w_oa[l], w_ob[l], w_out[l])
        s_ckv.append(ckv); s_kr.append(kr); s_kb.append(kb); s_vb.append(vb); s_ki.append(ki)

    y_prompt = rmsnorm(xp, g_final)
    y_sample = rmsnorm(xs, g_final)
    return (y_prompt, y_sample,
            jnp.stack(p_ckv), jnp.stack(p_kr), jnp.stack(p_kb), jnp.stack(p_vb), jnp.stack(p_ki),
            jnp.stack(s_ckv), jnp.stack(s_kr), jnp.stack(s_kb), jnp.stack(s_vb), jnp.stack(s_ki))
```

```python
import functools

import numpy as np
import jax
import jax.numpy as jnp
from jax import lax
from jax.experimental import pallas as pl
from jax.experimental.pallas import tpu as pltpu

F32 = jnp.float32
BF16 = jnp.bfloat16

D_MODEL = 1024
CHUNK = 64
EPS = 1e-6
MLA_HEADS = 8
NOPE_DIM = 64
ROPE_DIM = 32
V_DIM = 64
Q_LORA = 384
KV_LORA = 256
ROPE_THETA = 10000.0
MLA_SCALE = (NOPE_DIM + ROPE_DIM) ** -0.5
DSA_HEADS = 8
DSA_KV_HEADS = 2
DSA_GROUP = DSA_HEADS // DSA_KV_HEADS
DSA_HEAD_DIM = 64
DSA_SCALE = DSA_HEAD_DIM ** -0.5
IDX_HEADS = 8
IDX_DIM = 64
IDX_W_SCALE = (IDX_HEADS ** -0.5) * (IDX_DIM ** -0.5)
TOPK_MAX = 256
NUM_BUCKETS = 32
MAX_DISTANCE = 128
W_A = MLA_HEADS * V_DIM
W_B = DSA_HEADS * DSA_HEAD_DIM
SPLIT_SIZES = (Q_LORA, KV_LORA, ROPE_DIM, W_A, W_B, DSA_KV_HEADS * DSA_HEAD_DIM, DSA_KV_HEADS * DSA_HEAD_DIM,
               IDX_HEADS * IDX_DIM, IDX_DIM, IDX_HEADS, W_B, D_MODEL, D_MODEL)

LANES = 128
VMEM_LIMIT_BYTES = 56 * 1024 * 1024

TQ = 256
TK = 256
TM = 512
NEG = -1e30
MAX_BISECT = 200

QK_PAD = MLA_HEADS * LANES
KV_DIM = DSA_KV_HEADS * DSA_HEAD_DIM

A_CQ = (0, Q_LORA)
A_CKV = (A_CQ[1], A_CQ[1] + KV_LORA)
A_KR = (A_CKV[1], A_CKV[1] + LANES)
A_SM = (A_KR[1], A_KR[1] + LANES)
A_QB = (A_SM[1], A_SM[1] + W_B)
A_KB = (A_QB[1], A_QB[1] + KV_DIM)
A_VB = (A_KB[1], A_KB[1] + KV_DIM)
A_QI = (A_VB[1], A_VB[1] + IDX_HEADS * IDX_DIM)
A_WIDTH = A_QI[1]
G_GA = (0, W_A)
G_GB = (G_GA[1], G_GA[1] + W_B)
G_MA = (G_GB[1], G_GB[1] + D_MODEL)
G_MB = (G_MA[1], G_MA[1] + D_MODEL)
G_WIDTH = G_MB[1]


def _params(n_grid):
    return pltpu.CompilerParams(dimension_semantics=("arbitrary",) * n_grid, vmem_limit_bytes=VMEM_LIMIT_BYTES)


def _sigmoid(x):
    return 1.0 / (1.0 + jnp.exp(-x))


def _rms(x, g):
    return x * lax.rsqrt(jnp.mean(x * x, axis=-1, keepdims=True) + EPS) * g


def _dot(a, b):
    return jnp.dot(a, b, preferred_element_type=F32)


def _dot_nt(a, b):
    return lax.dot_general(a, b, (((1,), (1,)), ((), ())), preferred_element_type=F32)


def _key_rows(j, tk):
    start = j * tk if isinstance(j, int) else pl.multiple_of(j * tk, tk)
    return pl.ds(start, tk)


def _colsum(x):
    rows, cols = x.shape
    return jnp.sum(x.reshape(rows // 8, 8, cols).sum(axis=0), axis=0, keepdims=True)


def _mod_kernel(c_ref, w_ref, b_ref, o_ref):
    c = c_ref[...]
    a = c * _sigmoid(c)
    w = w_ref[0]
    a_hi = a.astype(BF16)
    a_lo = (a - a_hi.astype(F32)).astype(BF16)
    w_hi = w.astype(BF16)
    w_lo = (w - w_hi.astype(F32)).astype(BF16)
    o_ref[0] = _dot(a_hi, w_hi) + (_dot(a_hi, w_lo) + _dot(a_lo, w_hi)) + b_ref[0]


def _modulation(c_all, w_ada, b_ada):
    depth = w_ada.shape[0]
    n = c_all.shape[0]
    tn = D_MODEL
    return pl.pallas_call(
        _mod_kernel,
        out_shape=jax.ShapeDtypeStruct((depth, n, 3 * D_MODEL), F32),
        grid=(depth, 3 * D_MODEL // tn),
        in_specs=[
            pl.BlockSpec((n, D_MODEL), lambda l, j: (0, 0)),
            pl.BlockSpec((1, D_MODEL, tn), lambda l, j: (l, 0, j)),
            pl.BlockSpec((1, 1, tn), lambda l, j: (l, 0, j)),
        ],
        out_specs=pl.BlockSpec((1, n, tn), lambda l, j: (l, 0, j)),
        compiler_params=_params(2),
        name="adaln_mod",
    )(c_all, w_ada, b_ada.reshape(depth, 1, 3 * D_MODEL))


def _bucket_tiles(tk, tq):
    half = NUM_BUCKETS // 2
    max_exact = half // 2
    r = np.arange(tk)[:, None]
    c = np.arange(tq)[None, :]

    def bucket(rel):
        n = np.abs(rel)
        large = max_exact + (np.log(np.maximum(n, 1).astype(np.float32) / max_exact)
                             / np.float32(np.log(MAX_DISTANCE / max_exact)) * (half - max_exact)).astype(np.int32)
        large = np.minimum(large, half - 1)
        return np.where(n < max_exact, n, large) + np.where(rel > 0, half, 0)

    far = bucket(np.full((tk, tq), -(tk + tq)))
    return np.stack([far, bucket(r - tk - c), bucket(r - c)]).astype(np.int32)


def _bias_kernel(bucket_ref, rel_ref, o_ref):
    h = pl.program_id(0)
    for t in range(3):
        b = bucket_ref[t]
        acc = jnp.zeros(b.shape, F32)
        for i in range(NUM_BUCKETS):
            acc = jnp.where(b == i, rel_ref[i, h], acc)
        o_ref[0, t] = acc


def _bias_tiles(rel_bias, tk, tq):
    buckets = jnp.asarray(_bucket_tiles(tk, tq))
    return pl.pallas_call(
        _bias_kernel,
        out_shape=jax.ShapeDtypeStruct((DSA_HEADS, 3, tk, tq), F32),
        grid=(DSA_HEADS,),
        in_specs=[
            pl.BlockSpec((3, tk, tq), lambda h: (0, 0, 0)),
            pl.BlockSpec(memory_space=pltpu.SMEM),
        ],
        out_specs=pl.BlockSpec((1, 3, tk, tq), lambda h: (h, 0, 0, 0)),
        compiler_params=_params(1),
        name="t5_bias_tiles",
    )(buckets, rel_bias)


def _rope_group(g, c_ref, s1_ref, s2_ref):
    return (g * c_ref[...] + pltpu.roll(g, LANES - ROPE_DIM // 2, 1) * s1_ref[...]
            + pltpu.roll(g, ROPE_DIM // 2, 1) * s2_ref[...])


def _proj_kernel(x_ref, shift_ref, scale_ref, gn_ref, wa_ref, gq_ref, wuq_ref, gkv_ref, wkn_ref, wv_ref,
                 qc_ref, qs1_ref, qs2_ref, kc_ref, ks1_ref, ks2_ref,
                 ckv_o, kr_o, kb_o, vb_o, ki_o, q_o, qb_o, qi_o, wit_o, *kv_outs, emit_kv, tk):
    x = x_ref[0]
    h = _rms(x, gn_ref[...]) * (1.0 + scale_ref[0]) + shift_ref[0]
    h16 = h.astype(BF16)

    def zcols(a):
        return _dot(h16, wa_ref[:, a[0]:a[1]])

    cqn = _rms(zcols(A_CQ), gq_ref[...]).astype(BF16)
    q = _dot(cqn, wuq_ref[...])
    for hd in range(MLA_HEADS):
        sl = slice(hd * LANES, (hd + 1) * LANES)
        q_o[0, :, sl] = _rope_group(q[:, sl], qc_ref, qs1_ref, qs2_ref).astype(BF16)

    ckvn = _rms(zcols(A_CKV), gkv_ref[...])
    ckv_o[0] = ckvn
    krg = _rope_group(zcols(A_KR), kc_ref, ks1_ref, ks2_ref)
    kr_o[0] = krg[:, 0:ROPE_DIM]

    sm = zcols(A_SM)
    ki = sm[:, 0:IDX_DIM]
    ki_o[0] = ki
    wit_o[0] = sm.T[IDX_DIM:IDX_DIM + IDX_HEADS, :] * IDX_W_SCALE
    qb_o[0] = (zcols(A_QB) * DSA_SCALE).astype(BF16)
    qi_o[0] = zcols(A_QI).astype(BF16)
    kb = zcols(A_KB)
    vb = zcols(A_VB)
    kb_o[0] = kb
    vb_o[0] = vb

    if emit_kv:
        kcat_o, vat_o, kb16_o, vbt_o, ki16_o = kv_outs
        ckv16 = ckvn.astype(BF16)
        kn = _dot(ckv16, wkn_ref[...])
        lane = lax.broadcasted_iota(jnp.int32, krg.shape, 1)
        k_add = jnp.where(lane >= NOPE_DIM, krg, 0.0)
        for hd in range(MLA_HEADS):
            sl = slice(hd * LANES, (hd + 1) * LANES)
            kcat_o[0, :, sl] = (kn[:, sl] + k_add).astype(BF16)
        va = _dot(ckv16, wv_ref[...])
        kb16_o[0] = kb.astype(BF16)
        ki16_o[0] = ki.astype(BF16)
        for c in range(x.shape[0] // tk):
            rows = slice(c * tk, (c + 1) * tk)
            vat_o[0, c] = va[rows, :].T.astype(BF16)
            vbt_o[0, c] = vb[rows, :].T.astype(BF16)


def _proj(x, shift, scale, gn, wa, gq, wuq, gkv, wkn, wv, q_tabs, k_tabs, *, tm, tk, emit_kv):
    nb, t, _ = x.shape
    r = shift.shape[1]
    rb = 1 if r == 1 else tm
    grid = (nb, t // tm)
    tok = lambda w: pl.BlockSpec((1, tm, w), lambda b, i: (b, i, 0))
    mod = pl.BlockSpec((1, rb, D_MODEL), (lambda b, i: (b, 0, 0)) if r == 1 else (lambda b, i: (b, i, 0)))
    const = lambda a: pl.BlockSpec(a.shape, lambda b, i: (0,) * a.ndim)
    tab = pl.BlockSpec((tm, LANES), lambda b, i: (i, 0))
    in_specs = [tok(D_MODEL), mod, mod, const(gn), const(wa), const(gq), const(wuq), const(gkv), const(wkn),
                const(wv)] + [tab] * 6
    out_shape = [
        jax.ShapeDtypeStruct((nb, t, KV_LORA), F32),
        jax.ShapeDtypeStruct((nb, t, ROPE_DIM), F32),
        jax.ShapeDtypeStruct((nb, t, KV_DIM), F32),
        jax.ShapeDtypeStruct((nb, t, KV_DIM), F32),
        jax.ShapeDtypeStruct((nb, t, IDX_DIM), F32),
        jax.ShapeDtypeStruct((nb, t, QK_PAD), BF16),
        jax.ShapeDtypeStruct((nb, t, W_B), BF16),
        jax.ShapeDtypeStruct((nb, t, IDX_HEADS * IDX_DIM), BF16),
        jax.ShapeDtypeStruct((nb, IDX_HEADS, t), F32),
    ]
    out_specs = [tok(KV_LORA), tok(ROPE_DIM), tok(KV_DIM), tok(KV_DIM), tok(IDX_DIM), tok(QK_PAD), tok(W_B),
                 tok(IDX_HEADS * IDX_DIM), pl.BlockSpec((1, IDX_HEADS, tm), lambda b, i: (b, 0, i))]
    if emit_kv:
        nkt = tm // tk
        out_shape += [
            jax.ShapeDtypeStruct((nb, t, QK_PAD), BF16),
            jax.ShapeDtypeStruct((nb, t // tk, W_A, tk), BF16),
            jax.ShapeDtypeStruct((nb, t, KV_DIM), BF16),
            jax.ShapeDtypeStruct((nb, t // tk, KV_DIM, tk), BF16),
            jax.ShapeDtypeStruct((nb, t, IDX_DIM), BF16),
        ]
        out_specs += [tok(QK_PAD), pl.BlockSpec((1, nkt, W_A, tk), lambda b, i: (b, i, 0, 0)), tok(KV_DIM),
                      pl.BlockSpec((1, nkt, KV_DIM, tk), lambda b, i: (b, i, 0, 0)), tok(IDX_DIM)]
    return pl.pallas_call(
        functools.partial(_proj_kernel, emit_kv=emit_kv, tk=tk),
        out_shape=out_shape, grid=grid, in_specs=in_specs, out_specs=out_specs,
        compiler_params=_params(2), name="proj_kv" if emit_kv else "proj_q",
    )(x, shift, scale, gn, wa, gq, wuq, gkv, wkn, wv, *q_tabs, *k_tabs)


def _expand_kernel(ckv_ref, kr_ref, wkn_ref, wv_ref, place_ref, kcat_o, vat_o, *, tk):
    ckv16 = ckv_ref[0].astype(BF16)
    kcat_o[0] = (_dot(ckv16, wkn_ref[...]) + _dot(kr_ref[0].astype(BF16), place_ref[...])).astype(BF16)
    va = _dot(ckv16, wv_ref[...])
    for c in range(va.shape[0] // tk):
        vat_o[0, c] = va[c * tk:(c + 1) * tk, :].T.astype(BF16)


def _expand(ckv, kr, wkn, wv, place, *, tm, tk):
    nb, t, _ = ckv.shape
    const = lambda a: pl.BlockSpec(a.shape, lambda b, i: (0,) * a.ndim)
    return pl.pallas_call(
        functools.partial(_expand_kernel, tk=tk),
        out_shape=[jax.ShapeDtypeStruct((nb, t, QK_PAD), BF16),
                   jax.ShapeDtypeStruct((nb, t // tk, W_A, tk), BF16)],
        grid=(nb, t // tm),
        in_specs=[pl.BlockSpec((1, tm, KV_LORA), lambda b, i: (b, i, 0)),
                  pl.BlockSpec((1, tm, ROPE_DIM), lambda b, i: (b, i, 0)),
                  const(wkn), const(wv), const(place)],
        out_specs=[pl.BlockSpec((1, tm, QK_PAD), lambda b, i: (b, i, 0)),
                   pl.BlockSpec((1, tm // tk, W_A, tk), lambda b, i: (b, i, 0, 0))],
        compiler_params=_params(2), name="kv_expand",
    )(ckv, kr, wkn, wv, place)


def _flash_step(carry, k_j, q_h, vt_j, add):
    m, l, acc = carry
    s = _dot_nt(k_j, q_h)
    if add is not None:
        s = s + add
    m_new = jnp.maximum(m, jnp.max(s, axis=0, keepdims=True))
    alpha = jnp.exp(m - m_new)
    p = jnp.exp(s - m_new)
    l = alpha * l + _colsum(p)
    acc = alpha * acc + _dot(vt_j, p.astype(BF16))
    return m_new, l, acc


def _flash_init(dv, tq):
    return jnp.full((1, tq), NEG, F32), jnp.zeros((1, tq), F32), jnp.zeros((dv, tq), F32)


def _last_tile_visible(tk, tq, causal, n_valid_last):
    r = lax.broadcasted_iota(jnp.int32, (tk, tq), 0)
    if causal:
        c = lax.broadcasted_iota(jnp.int32, (tk, tq), 1)
        return (r // CHUNK) <= (c // CHUNK)
    return r < n_valid_last


def _mla_kernel(q_ref, k_ref, vt_ref, o_ref, *, tq, tk, n_tiles, causal, n_valid_last, transpose_out):
    last = pl.program_id(1) if causal else n_tiles - 1
    last_add = jnp.where(_last_tile_visible(tk, tq, causal, n_valid_last), 0.0, NEG)
    outs = []
    for hd in range(MLA_HEADS):
        sl = slice(hd * LANES, (hd + 1) * LANES)
        vs = slice(hd * V_DIM, (hd + 1) * V_DIM)
        q_h = q_ref[0, :, sl]

        def tile(j, carry, add, q_h=q_h, sl=sl, vs=vs):
            k_j = k_ref[0, _key_rows(j, tk), sl]
            return _flash_step(carry, k_j, q_h, vt_ref[0, j, vs, :], add)

        carry = lax.fori_loop(0, last, lambda j, c, tile=tile: tile(j, c, None), _flash_init(V_DIM, tq))
        _, l, acc = tile(last, carry, last_add)
        outs.append(acc / l)
    o_t = jnp.concatenate(outs, axis=0)
    o_ref[0] = o_t.T if transpose_out else o_t


def _mla(q, k, vt, *, tq, tk, causal, n_valid_last, transpose_out):
    nb, t, _ = q.shape
    tkeys = k.shape[1]
    n_tiles = tkeys // tk
    if transpose_out:
        out_shape = jax.ShapeDtypeStruct((nb, t, W_A), F32)
        out_spec = pl.BlockSpec((1, tq, W_A), lambda b, i: (b, i, 0))
    else:
        out_shape = jax.ShapeDtypeStruct((nb, W_A, t), F32)
        out_spec = pl.BlockSpec((1, W_A, tq), lambda b, i: (b, 0, i))
    return pl.pallas_call(
        functools.partial(_mla_kernel, tq=tq, tk=tk, n_tiles=n_tiles, causal=causal, n_valid_last=n_valid_last,
                          transpose_out=transpose_out),
        out_shape=out_shape, grid=(nb, t // tq),
        in_specs=[pl.BlockSpec((1, tq, QK_PAD), lambda b, i: (b, i, 0)),
                  pl.BlockSpec((1, tkeys, QK_PAD), lambda b, i: (b, 0, 0)),
                  pl.BlockSpec((1, n_tiles, W_A, tk), lambda b, i: (b, 0, 0, 0))],
        out_specs=out_spec, compiler_params=_params(2), name="mla_causal" if causal else "mla_cached",
    )(q, k, vt)


def _dsa_kernel(qi_ref, wit_ref, ki_ref, qb_ref, kb_ref, vbt_ref, bias_ref, o_ref, idx_scr, *,
                tq, tk, n_tiles, causal, n_valid_last, topk, transpose_out):
    i = pl.program_id(1)
    last = i if causal else n_tiles - 1
    n_used = last + 1
    visible = _last_tile_visible(tk, tq, causal, n_valid_last)
    kf = float(topk)

    wit = wit_ref[0]

    def idx_tile(j):
        k_j = ki_ref[0, _key_rows(j, tk), :]
        acc = jnp.zeros((tk, tq), F32)
        for hd in range(IDX_HEADS):
            s = _dot_nt(k_j, qi_ref[0, :, hd * IDX_DIM:(hd + 1) * IDX_DIM])
            acc = acc + jnp.maximum(s, 0.0) * wit[hd:hd + 1, :]
        return acc

    def idx_body(j, carry):
        lo, hi = carry
        t = idx_tile(j)
        idx_scr[j] = t
        return jnp.minimum(lo, jnp.min(t, axis=0, keepdims=True)), jnp.maximum(hi, jnp.max(t, axis=0, keepdims=True))

    lo, hi = lax.fori_loop(0, last, idx_body, (jnp.full((1, tq), jnp.inf, F32), jnp.full((1, tq), -jnp.inf, F32)))
    t_last = idx_tile(last)
    idx_scr[last] = jnp.where(visible, t_last, -jnp.inf)
    lo = jnp.minimum(lo, jnp.min(jnp.where(visible, t_last, jnp.inf), axis=0, keepdims=True))
    hi = jnp.maximum(hi, jnp.max(jnp.where(visible, t_last, -jnp.inf), axis=0, keepdims=True))

    def count(pred):
        def body(j, c):
            return c + _colsum(jnp.where(pred(idx_scr[j]), 1.0, 0.0))
        return lax.fori_loop(0, n_used, body, jnp.zeros((1, tq), F32))

    c_lo = count(lambda t: t >= lo)
    c_hi = count(lambda t: t >= hi)
    top_heavy = c_hi >= kf
    lo = jnp.where(top_heavy, hi, lo)
    c_lo = jnp.where(top_heavy, c_hi, c_lo)

    def settled(lo, hi, c_lo):
        mid = 0.5 * lo + 0.5 * hi
        return jnp.where((c_lo <= kf) | (mid <= lo) | (mid >= hi), 1.0, 0.0)

    def bis_cond(st):
        _, _, _, done, it = st
        return jnp.logical_and(it < MAX_BISECT, jnp.min(done) < 0.5)

    def bis_body(st):
        lo, hi, c_lo, done, it = st
        mid = 0.5 * lo + 0.5 * hi
        c = count(lambda t: t >= mid)
        live = done < 0.5
        up = live & (c >= kf)
        dn = live & (c < kf)
        lo = jnp.where(up, mid, lo)
        c_lo = jnp.where(up, c, c_lo)
        hi = jnp.where(dn, mid, hi)
        return lo, hi, c_lo, jnp.maximum(done, settled(lo, hi, c_lo)), it + 1

    lo, hi, c_lo, _, _ = lax.while_loop(bis_cond, bis_body, (lo, hi, c_lo, settled(lo, hi, c_lo), jnp.int32(0)))

    any_tie = jnp.max(jnp.where(c_lo > kf, 1.0, 0.0)) > 0.5

    @pl.when(jnp.logical_not(any_tie))
    def _():
        def body(j, _):
            idx_scr[j] = jnp.where(idx_scr[j] >= lo, 0.0, NEG)
            return 0
        lax.fori_loop(0, n_used, body, 0)

    @pl.when(any_tie)
    def _():
        need = kf - count(lambda t: t > lo)
        r = lax.broadcasted_iota(jnp.int32, (tk, tk), 0)
        c = lax.broadcasted_iota(jnp.int32, (tk, tk), 1)
        before = jnp.where(c < r, 1.0, 0.0).astype(BF16)

        def body(j, seen):
            t = idx_scr[j]
            eq = jnp.where(t == lo, 1.0, 0.0)
            rank = _dot(before, eq.astype(BF16)) + seen
            take = (t > lo) | ((t == lo) & (rank < need))
            idx_scr[j] = jnp.where(take, 0.0, NEG)
            return seen + _colsum(eq)
        lax.fori_loop(0, n_used, body, jnp.zeros((1, tq), F32))

    outs = []
    for hd in range(DSA_HEADS):
        g = hd // DSA_GROUP
        ks = slice(g * DSA_HEAD_DIM, (g + 1) * DSA_HEAD_DIM)
        q_h = qb_ref[0, :, hd * DSA_HEAD_DIM:(hd + 1) * DSA_HEAD_DIM]

        def tile(j, carry, q_h=q_h, ks=ks, hd=hd):
            k_j = kb_ref[0, _key_rows(j, tk), ks]
            add = idx_scr[j] + bias_ref[hd, jnp.clip(j - last + 2, 0, 2)]
            return _flash_step(carry, k_j, q_h, vbt_ref[0, j, ks, :], add)

        _, l, acc = lax.fori_loop(0, n_used, tile, _flash_init(DSA_HEAD_DIM, tq))
        outs.append(acc / l)
    o_t = jnp.concatenate(outs, axis=0)
    o_ref[0] = o_t.T if transpose_out else o_t


def _dsa(qi, wit, ki, qb, kb, vbt, bias, *, tq, tk, causal, n_valid_last, topk, transpose_out):
    nb, t, _ = qb.shape
    tkeys = kb.shape[1]
    n_tiles = tkeys // tk
    if transpose_out:
        out_shape = jax.ShapeDtypeStruct((nb, t, W_B), F32)
        out_spec = pl.BlockSpec((1, tq, W_B), lambda b, i: (b, i, 0))
    else:
        out_shape = jax.ShapeDtypeStruct((nb, W_B, t), F32)
        out_spec = pl.BlockSpec((1, W_B, tq), lambda b, i: (b, 0, i))
    return pl.pallas_call(
        functools.partial(_dsa_kernel, tq=tq, tk=tk, n_tiles=n_tiles, causal=causal, n_valid_last=n_valid_last,
                          topk=topk, transpose_out=transpose_out),
        out_shape=out_shape, grid=(nb, t // tq),
        in_specs=[pl.BlockSpec((1, tq, IDX_HEADS * IDX_DIM), lambda b, i: (b, i, 0)),
                  pl.BlockSpec((1, IDX_HEADS, tq), lambda b, i: (b, 0, i)),
                  pl.BlockSpec((1, tkeys, IDX_DIM), lambda b, i: (b, 0, 0)),
                  pl.BlockSpec((1, tq, W_B), lambda b, i: (b, i, 0)),
                  pl.BlockSpec((1, tkeys, KV_DIM), lambda b, i: (b, 0, 0)),
                  pl.BlockSpec((1, n_tiles, KV_DIM, tk), lambda b, i: (b, 0, 0, 0)),
                  pl.BlockSpec(bias.shape, lambda b, i: (0, 0, 0, 0))],
        out_specs=out_spec,
        scratch_shapes=[pltpu.VMEM((n_tiles, tk, tq), F32)],
        compiler_params=_params(2), name="dsa_causal" if causal else "dsa_cached",
    )(qi, wit, ki, qb, kb, vbt, bias)


def _out_kernel(x_ref, shift_ref, scale_ref, gate_ref, gn_ref, oa_ref, ob_ref, wg_ref, woa_ref, wob_ref, wout_ref,
                gf_ref, o_ref, *, final):
    x = x_ref[0]
    h16 = (_rms(x, gn_ref[...]) * (1.0 + scale_ref[0]) + shift_ref[0]).astype(BF16)

    def gcols(a):
        return _dot(h16, wg_ref[:, a[0]:a[1]])

    ga = gcols(G_GA)
    ya = _dot((oa_ref[0] * (ga * _sigmoid(ga))).astype(BF16), woa_ref[...])
    gb = gcols(G_GB)
    yb = _dot((ob_ref[0] * (gb * _sigmoid(gb))).astype(BF16), wob_ref[...])
    m = _sigmoid(gcols(G_MA)) * ya + _sigmoid(gcols(G_MB)) * yb
    y = x + gate_ref[0] * _dot(m.astype(BF16), wout_ref[...])
    o_ref[0] = _rms(y, gf_ref[...]) if final else y


def _out(x, shift, scale, gate, gn, oa, ob, wg, woa, wob, wout, gf, *, tm, final):
    nb, t, _ = x.shape
    r = shift.shape[1]
    rb = 1 if r == 1 else tm
    tok = lambda w: pl.BlockSpec((1, tm, w), lambda b, i: (b, i, 0))
    mod = pl.BlockSpec((1, rb, D_MODEL), (lambda b, i: (b, 0, 0)) if r == 1 else (lambda b, i: (b, i, 0)))
    const = lambda a: pl.BlockSpec(a.shape, lambda b, i: (0,) * a.ndim)
    return pl.pallas_call(
        functools.partial(_out_kernel, final=final),
        out_shape=jax.ShapeDtypeStruct(x.shape, F32), grid=(nb, t // tm),
        in_specs=[tok(D_MODEL), mod, mod, mod, const(gn), tok(W_A), tok(W_B), const(wg), const(woa), const(wob),
                  const(wout), const(gf)],
        out_specs=tok(D_MODEL), compiler_params=_params(2), name="out_final" if final else "out_mix",
    )(x, shift, scale, gate, gn, oa, ob, wg, woa, wob, wout, gf)


def _pack_weights(w_in, w_uq, w_ukv, w_oa, w_ob, w_out):
    depth = w_in.shape[0]
    offs = np.concatenate([[0], np.cumsum(SPLIT_SIZES)])
    cq, ckv, kr, ga, qb, kb, vb, qi, ki, wi, gb, ma, mb = [w_in[:, :, offs[n]:offs[n + 1]] for n in range(13)]
    z = lambda w: jnp.zeros((depth, D_MODEL, w), F32)
    krg = jnp.concatenate([kr, z(NOPE_DIM - ROPE_DIM), kr, z(LANES - NOPE_DIM - ROPE_DIM)], axis=-1)
    sm = jnp.concatenate([ki, wi, z(LANES - IDX_DIM - IDX_HEADS)], axis=-1)
    wa = jnp.concatenate([cq, ckv, krg, sm, qb, kb, vb, qi], axis=-1).astype(BF16)
    wg = jnp.concatenate([ga, gb, ma, mb], axis=-1).astype(BF16)
    qd = NOPE_DIM + ROPE_DIM
    wuq = jnp.pad(w_uq.reshape(depth, Q_LORA, MLA_HEADS, qd), ((0, 0), (0, 0), (0, 0), (0, LANES - qd)))
    wuq = wuq.reshape(depth, Q_LORA, QK_PAD).astype(BF16)
    ukv = w_ukv.reshape(depth, KV_LORA, MLA_HEADS, NOPE_DIM + V_DIM)
    wkn = jnp.pad(ukv[..., :NOPE_DIM], ((0, 0), (0, 0), (0, 0), (0, LANES - NOPE_DIM)))
    wkn = wkn.reshape(depth, KV_LORA, QK_PAD).astype(BF16)
    wv = ukv[..., NOPE_DIM:].reshape(depth, KV_LORA, W_A).astype(BF16)
    return wa, wg, wuq, wkn, wv, w_oa.astype(BF16), w_ob.astype(BF16), w_out.astype(BF16)


def _rope_tables(pos):
    half = ROPE_DIM // 2
    inv = ROPE_THETA ** (-jnp.arange(half, dtype=F32) / half)
    ang = pos.astype(F32)[:, None] * inv
    cos, sin = jnp.cos(ang), jnp.sin(ang)
    n = pos.shape[0]
    z = lambda w: jnp.zeros((n, w), F32)
    pad = LANES - NOPE_DIM - ROPE_DIM
    q_c = jnp.concatenate([jnp.ones((n, NOPE_DIM), F32), cos, cos, z(pad)], axis=-1) * MLA_SCALE
    q_s1 = jnp.concatenate([z(NOPE_DIM), -sin, z(half), z(pad)], axis=-1) * MLA_SCALE
    q_s2 = jnp.concatenate([z(NOPE_DIM), z(half), sin, z(pad)], axis=-1) * MLA_SCALE
    gap = NOPE_DIM - ROPE_DIM
    k_c = jnp.concatenate([cos, cos, z(gap), cos, cos, z(pad)], axis=-1)
    k_s1 = jnp.concatenate([-sin, z(half), z(gap), -sin, z(half), z(pad)], axis=-1)
    k_s2 = jnp.concatenate([z(half), sin, z(gap), z(half), sin, z(pad)], axis=-1)
    return (q_c, q_s1, q_s2), (k_c, k_s1, k_s2)


def _placement():
    p = np.zeros((ROPE_DIM, QK_PAD), np.float32)
    for hd in range(MLA_HEADS):
        p[np.arange(ROPE_DIM), hd * LANES + NOPE_DIM + np.arange(ROPE_DIM)] = 1.0
    return jnp.asarray(p, dtype=BF16)


def kernel(x_prompt, x_sample, c_prompt, c_sample, cache_ckv, cache_krope, cache_kb, cache_vb, cache_kidx, w_ada,
           b_ada, g_norm, w_in, g_qnorm, w_uq, g_kvnorm, w_ukv, w_oa, w_ob, w_out, rel_bias, g_final):
    depth = w_in.shape[0]
    bp, t_p, _ = x_prompt.shape
    bs, t_s, _ = x_sample.shape
    past = cache_ckv.shape[2]
    topk_p = min(TOPK_MAX, t_p // 4)
    topk_s = min(TOPK_MAX, (past + t_s) // 4)
    assert t_p % TM == 0 and t_p % TQ == 0 and past % TK == 0 and t_s <= TK and t_s % 8 == 0
    keys_s = past + TK
    n_s = bs * t_s

    wa, wg, wuq, wkn, wv, woa, wob, wout = _pack_weights(w_in, w_uq, w_ukv, w_oa, w_ob, w_out)
    place = _placement()
    mod = _modulation(jnp.concatenate([c_prompt, c_sample], axis=0), w_ada, b_ada)
    mod = mod.reshape(depth, bp + bs, 3, 1, D_MODEL)
    tabs_p = _rope_tables(jnp.arange(t_p, dtype=jnp.int32))
    tabs_s = _rope_tables(jnp.tile(past + jnp.arange(t_s, dtype=jnp.int32), bs))
    bias_p = _bias_tiles(rel_bias, TK, TQ)
    bias_s = _bias_tiles(rel_bias, TK, t_s)
    row = lambda v: v.reshape(1, -1)

    xp = x_prompt
    xs = x_sample.reshape(1, n_s, D_MODEL)
    outs_p = [[] for _ in range(5)]
    outs_s = [[] for _ in range(5)]
    for l in range(depth):
        final = l == depth - 1
        gn, gq, gkv, gf = row(g_norm[l]), row(g_qnorm[l]), row(g_kvnorm[l]), row(g_final)

        shift, scale, gate = (mod[l, :bp, n] for n in range(3))
        (ckv, kr, kb, vb, ki, q, qb, qi, wit, kcat, vat, kb16, vbt, ki16) = _proj(
            xp, shift, scale, gn, wa[l], gq, wuq[l], gkv, wkn[l], wv[l], *tabs_p, tm=TM, tk=TK, emit_kv=True)
        oa = _mla(q, kcat, vat, tq=TQ, tk=TK, causal=True, n_valid_last=TK, transpose_out=True)
        ob = _dsa(qi, wit, ki16, qb, kb16, vbt, bias_p, tq=TQ, tk=TK, causal=True, n_valid_last=TK, topk=topk_p,
                  transpose_out=True)
        xp = _out(xp, shift, scale, gate, gn, oa, ob, wg[l], woa[l], wob[l], wout[l], gf, tm=TM, final=final)
        for lst, v in zip(outs_p, (ckv, kr, kb, vb, ki)):
            lst.append(v)

        shift, scale, gate = (jnp.repeat(mod[l, bp:, n, 0], t_s, axis=0)[None] for n in range(3))
        (ckv, kr, kb, vb, ki, q, qb, qi, wit) = _proj(
            xs, shift, scale, gn, wa[l], gq, wuq[l], gkv, wkn[l], wv[l], *tabs_s, tm=n_s, tk=TK, emit_kv=False)
        per_seq = lambda a: a.reshape((bs, t_s) + a.shape[2:])
        ckv, kr, kb, vb, ki, q, qb, qi = (per_seq(a) for a in (ckv, kr, kb, vb, ki, q, qb, qi))
        wit = wit.reshape(IDX_HEADS, bs, t_s).transpose(1, 0, 2)

        def with_cache(cache, new):
            pad = jnp.zeros((bs, TK - t_s) + new.shape[2:], new.dtype)
            return jnp.concatenate([cache, new, pad], axis=1)

        kcat, vat = _expand(with_cache(cache_ckv[l], ckv), with_cache(cache_krope[l], kr), wkn[l], wv[l], place,
                            tm=TK, tk=TK)
        oa = _mla(q, kcat, vat, tq=t_s, tk=TK, causal=False, n_valid_last=t_s, transpose_out=False)
        kb_all = with_cache(cache_kb[l].reshape(bs, past, KV_DIM), kb).astype(BF16)
        vb_all = with_cache(cache_vb[l].reshape(bs, past, KV_DIM), vb).astype(BF16)
        ki_all = with_cache(cache_kidx[l], ki).astype(BF16)
        vbt = vb_all.reshape(bs, keys_s // TK, TK, KV_DIM).transpose(0, 1, 3, 2)
        ob = _dsa(qi, wit, ki_all, qb, kb_all, vbt, bias_s, tq=t_s, tk=TK, causal=False, n_valid_last=t_s,
                  topk=topk_s, transpose_out=False)
        flat = lambda a: a.transpose(0, 2, 1).reshape(1, n_s, a.shape[1])
        xs = _out(xs, shift, scale, gate, gn, flat(oa), flat(ob), wg[l], woa[l], wob[l], wout[l], gf, tm=n_s,
                  final=final)
        for lst, v in zip(outs_s, (ckv, kr, kb, vb, ki)):
            lst.append(v)

    def stacked(lists, nb, t):
        ckv, kr, kb, vb, ki = (jnp.stack(v) for v in lists)
        kvh = (depth, nb, t, DSA_KV_HEADS, DSA_HEAD_DIM)
        return ckv, kr, kb.reshape(kvh), vb.reshape(kvh), ki

    return (xp, xs.reshape(bs, t_s, D_MODEL)) + stacked(outs_p, bp, t_p) + stacked(outs_s, bs, t_s)
```

```python
import functools

import numpy as np
import jax
import jax.numpy as jnp
from jax import lax
from jax.experimental import pallas as pl
from jax.experimental.pallas import tpu as pltpu

F32 = jnp.float32
BF16 = jnp.bfloat16

D_MODEL = 1024
CHUNK = 64
EPS = 1e-6
MLA_HEADS = 8
NOPE_DIM = 64
ROPE_DIM = 32
V_DIM = 64
Q_LORA = 384
KV_LORA = 256
ROPE_THETA = 10000.0
MLA_SCALE = (NOPE_DIM + ROPE_DIM) ** -0.5
DSA_HEADS = 8
DSA_KV_HEADS = 2
DSA_GROUP = DSA_HEADS // DSA_KV_HEADS
DSA_HEAD_DIM = 64
DSA_SCALE = DSA_HEAD_DIM ** -0.5
IDX_HEADS = 8
IDX_DIM = 64
IDX_W_SCALE = (IDX_HEADS ** -0.5) * (IDX_DIM ** -0.5)
TOPK_MAX = 256
NUM_BUCKETS = 32
MAX_DISTANCE = 128
W_A = MLA_HEADS * V_DIM
W_B = DSA_HEADS * DSA_HEAD_DIM
SPLIT_SIZES = (Q_LORA, KV_LORA, ROPE_DIM, W_A, W_B, DSA_KV_HEADS * DSA_HEAD_DIM, DSA_KV_HEADS * DSA_HEAD_DIM,
               IDX_HEADS * IDX_DIM, IDX_DIM, IDX_HEADS, W_B, D_MODEL, D_MODEL)

LANES = 128
SUBLANES = 8
VMEM_LIMIT_BYTES = 56 * 1024 * 1024

TQ = 256
TK = 256
TM = 512
NEG = -1e30
MAX_BISECT = 200
BISECT_PER_CHECK = 4

QK_PAD = MLA_HEADS * LANES
KV_DIM = DSA_KV_HEADS * DSA_HEAD_DIM

A_CQ = (0, Q_LORA)
A_CKV = (A_CQ[1], A_CQ[1] + KV_LORA)
A_KR = (A_CKV[1], A_CKV[1] + LANES)
A_SM = (A_KR[1], A_KR[1] + LANES)
A_QB = (A_SM[1], A_SM[1] + W_B)
A_KB = (A_QB[1], A_QB[1] + KV_DIM)
A_VB = (A_KB[1], A_KB[1] + KV_DIM)
A_QI = (A_VB[1], A_VB[1] + IDX_HEADS * IDX_DIM)
A_WIDTH = A_QI[1]
G_GA = (0, W_A)
G_GB = (G_GA[1], G_GA[1] + W_B)
G_MA = (G_GB[1], G_GB[1] + D_MODEL)
G_MB = (G_MA[1], G_MA[1] + D_MODEL)
G_WIDTH = G_MB[1]


def _params(n_grid):
    return pltpu.CompilerParams(dimension_semantics=("arbitrary",) * n_grid, vmem_limit_bytes=VMEM_LIMIT_BYTES)


def _sigmoid(x):
    return 1.0 / (1.0 + jnp.exp(-x))


def _rms(x, g):
    return x * lax.rsqrt(jnp.mean(x * x, axis=-1, keepdims=True) + EPS) * g


def _dot(a, b):
    return jnp.dot(a, b, preferred_element_type=F32)


def _dot_nt(a, b):
    return lax.dot_general(a, b, (((1,), (1,)), ((), ())), preferred_element_type=F32)


def _key_rows(j, tk):
    start = j * tk if isinstance(j, int) else pl.multiple_of(j * tk, tk)
    return pl.ds(start, tk)


def _colsum(x):
    rows, cols = x.shape
    return jnp.sum(x.reshape(rows // SUBLANES, SUBLANES, cols).sum(axis=0), axis=0, keepdims=True)


def _mod_kernel(c_ref, w_ref, b_ref, o_ref):
    c = c_ref[...]
    a = c * _sigmoid(c)
    w = w_ref[0]
    a_hi = a.astype(BF16)
    a_lo = (a - a_hi.astype(F32)).astype(BF16)
    w_hi = w.astype(BF16)
    w_lo = (w - w_hi.astype(F32)).astype(BF16)
    o_ref[0] = _dot(a_hi, w_hi) + (_dot(a_hi, w_lo) + _dot(a_lo, w_hi)) + b_ref[0]


def _modulation(c_all, w_ada, b_ada):
    depth = w_ada.shape[0]
    n = c_all.shape[0]
    tn = D_MODEL
    return pl.pallas_call(
        _mod_kernel,
        out_shape=jax.ShapeDtypeStruct((depth, n, 3 * D_MODEL), F32),
        grid=(depth, 3 * D_MODEL // tn),
        in_specs=[
            pl.BlockSpec((n, D_MODEL), lambda l, j: (0, 0)),
            pl.BlockSpec((1, D_MODEL, tn), lambda l, j: (l, 0, j)),
            pl.BlockSpec((1, 1, tn), lambda l, j: (l, 0, j)),
        ],
        out_specs=pl.BlockSpec((1, n, tn), lambda l, j: (l, 0, j)),
        compiler_params=_params(2),
        name="adaln_mod",
    )(c_all, w_ada, b_ada.reshape(depth, 1, 3 * D_MODEL))


def _rel_bucket(rel):
    half = NUM_BUCKETS // 2
    max_exact = half // 2
    n = np.abs(rel)
    large = max_exact + (np.log(np.maximum(n, 1).astype(np.float32) / max_exact)
                         / np.float32(np.log(MAX_DISTANCE / max_exact)) * (half - max_exact)).astype(np.int32)
    large = np.minimum(large, half - 1)
    return np.where(n < max_exact, n, large) + np.where(rel > 0, half, 0)


def _bias_kernel(bucket_ref, rel_ref, o_ref, *, far_bucket):
    h = pl.program_id(0)
    for t in range(2):
        b = bucket_ref[t]
        acc = jnp.zeros(b.shape, F32)
        for i in range(NUM_BUCKETS):
            acc = jnp.where(b == i, rel_ref[i, h], acc)
        o_ref[0, t] = acc - rel_ref[far_bucket, h]


def _bias_tiles(rel_bias, tk, tq):
    r = np.arange(tk)[:, None]
    c = np.arange(tq)[None, :]
    far_bucket = int(_rel_bucket(np.array(-tk)))
    assert np.all(_rel_bucket(-tk - np.arange(1, tk + tq)) == far_bucket)
    buckets = jnp.asarray(np.stack([_rel_bucket(r - tk - c), _rel_bucket(r - c)]).astype(np.int32))
    return pl.pallas_call(
        functools.partial(_bias_kernel, far_bucket=far_bucket),
        out_shape=jax.ShapeDtypeStruct((DSA_HEADS, 2, tk, tq), F32),
        grid=(DSA_HEADS,),
        in_specs=[
            pl.BlockSpec((2, tk, tq), lambda h: (0, 0, 0)),
            pl.BlockSpec(memory_space=pltpu.SMEM),
        ],
        out_specs=pl.BlockSpec((1, 2, tk, tq), lambda h: (h, 0, 0, 0)),
        compiler_params=_params(1),
        name="t5_bias_tiles",
    )(buckets, rel_bias)


def _rope_group(g, c_ref, s1_ref, s2_ref):
    return (g * c_ref[...] + pltpu.roll(g, LANES - ROPE_DIM // 2, 1) * s1_ref[...]
            + pltpu.roll(g, ROPE_DIM // 2, 1) * s2_ref[...])


def _proj_kernel(x_ref, shift_ref, scale_ref, gn_ref, wa_ref, gq_ref, wuq_ref, gkv_ref, wkn_ref, wv_ref,
                 qc_ref, qs1_ref, qs2_ref, kc_ref, ks1_ref, ks2_ref,
                 ckv_o, kr_o, kb_o, vb_o, ki_o, q_o, qb_o, qi_o, wit_o, *kv_outs, emit_kv, tk):
    x = x_ref[0]
    h = _rms(x, gn_ref[...]) * (1.0 + scale_ref[0]) + shift_ref[0]
    h16 = h.astype(BF16)

    def zcols(a):
        return _dot(h16, wa_ref[:, a[0]:a[1]])

    cqn = _rms(zcols(A_CQ), gq_ref[...]).astype(BF16)
    q = _dot(cqn, wuq_ref[...])
    for hd in range(MLA_HEADS):
        sl = slice(hd * LANES, (hd + 1) * LANES)
        q_o[0, :, sl] = _rope_group(q[:, sl], qc_ref, qs1_ref, qs2_ref).astype(BF16)

    ckvn = _rms(zcols(A_CKV), gkv_ref[...])
    ckv_o[0] = ckvn
    krg = _rope_group(zcols(A_KR), kc_ref, ks1_ref, ks2_ref)
    kr_o[0] = krg[:, 0:ROPE_DIM]

    sm = zcols(A_SM)
    ki = sm[:, 0:IDX_DIM]
    ki_o[0] = ki
    wit_o[0] = sm.T[IDX_DIM:IDX_DIM + IDX_HEADS, :] * IDX_W_SCALE
    qb_o[0] = (zcols(A_QB) * DSA_SCALE).astype(BF16)
    qi_o[0] = zcols(A_QI).astype(BF16)
    kb = zcols(A_KB)
    vb = zcols(A_VB)
    kb_o[0] = kb
    vb_o[0] = vb

    if emit_kv:
        kcat_o, vat_o, kb16_o, vbt_o, ki16_o = kv_outs
        ckv16 = ckvn.astype(BF16)
        kn = _dot(ckv16, wkn_ref[...])
        lane = lax.broadcasted_iota(jnp.int32, krg.shape, 1)
        k_add = jnp.where(lane >= NOPE_DIM, krg, 0.0)
        for hd in range(MLA_HEADS):
            sl = slice(hd * LANES, (hd + 1) * LANES)
            kcat_o[0, :, sl] = (kn[:, sl] + k_add).astype(BF16)
        va = _dot(ckv16, wv_ref[...])
        kb16_o[0] = kb.astype(BF16)
        ki16_o[0] = ki.astype(BF16)
        for c in range(x.shape[0] // tk):
            rows = slice(c * tk, (c + 1) * tk)
            vat_o[0, c] = va[rows, :].T.astype(BF16)
            vbt_o[0, c] = vb[rows, :].T.astype(BF16)


def _proj(x, shift, scale, gn, wa, gq, wuq, gkv, wkn, wv, q_tabs, k_tabs, *, tm, tk, emit_kv):
    nb, t, _ = x.shape
    r = shift.shape[1]
    rb = 1 if r == 1 else tm
    grid = (nb, t // tm)
    tok = lambda w: pl.BlockSpec((1, tm, w), lambda b, i: (b, i, 0))
    mod = pl.BlockSpec((1, rb, D_MODEL), (lambda b, i: (b, 0, 0)) if r == 1 else (lambda b, i: (b, i, 0)))
    const = lambda a: pl.BlockSpec(a.shape, lambda b, i: (0,) * a.ndim)
    tab = pl.BlockSpec((tm, LANES), lambda b, i: (i, 0))
    in_specs = [tok(D_MODEL), mod, mod, const(gn), const(wa), const(gq), const(wuq), const(gkv), const(wkn),
                const(wv)] + [tab] * 6
    out_shape = [
        jax.ShapeDtypeStruct((nb, t, KV_LORA), F32),
        jax.ShapeDtypeStruct((nb, t, ROPE_DIM), F32),
        jax.ShapeDtypeStruct((nb, t, KV_DIM), F32),
        jax.ShapeDtypeStruct((nb, t, KV_DIM), F32),
        jax.ShapeDtypeStruct((nb, t, IDX_DIM), F32),
        jax.ShapeDtypeStruct((nb, t, QK_PAD), BF16),
        jax.ShapeDtypeStruct((nb, t, W_B), BF16),
        jax.ShapeDtypeStruct((nb, t, IDX_HEADS * IDX_DIM), BF16),
        jax.ShapeDtypeStruct((nb, IDX_HEADS, t), F32),
    ]
    out_specs = [tok(KV_LORA), tok(ROPE_DIM), tok(KV_DIM), tok(KV_DIM), tok(IDX_DIM), tok(QK_PAD), tok(W_B),
                 tok(IDX_HEADS * IDX_DIM), pl.BlockSpec((1, IDX_HEADS, tm), lambda b, i: (b, 0, i))]
    if emit_kv:
        nkt = tm // tk
        out_shape += [
            jax.ShapeDtypeStruct((nb, t, QK_PAD), BF16),
            jax.ShapeDtypeStruct((nb, t // tk, W_A, tk), BF16),
            jax.ShapeDtypeStruct((nb, t, KV_DIM), BF16),
            jax.ShapeDtypeStruct((nb, t // tk, KV_DIM, tk), BF16),
            jax.ShapeDtypeStruct((nb, t, IDX_DIM), BF16),
        ]
        out_specs += [tok(QK_PAD), pl.BlockSpec((1, nkt, W_A, tk), lambda b, i: (b, i, 0, 0)), tok(KV_DIM),
                      pl.BlockSpec((1, nkt, KV_DIM, tk), lambda b, i: (b, i, 0, 0)), tok(IDX_DIM)]
    return pl.pallas_call(
        functools.partial(_proj_kernel, emit_kv=emit_kv, tk=tk),
        out_shape=out_shape, grid=grid, in_specs=in_specs, out_specs=out_specs,
        compiler_params=_params(2), name="proj_kv" if emit_kv else "proj_q",
    )(x, shift, scale, gn, wa, gq, wuq, gkv, wkn, wv, *q_tabs, *k_tabs)


def _expand_kernel(ckv_ref, kr_ref, wkn_ref, wv_ref, place_ref, kcat_o, vat_o, *, tk):
    ckv16 = ckv_ref[0].astype(BF16)
    kcat_o[0] = (_dot(ckv16, wkn_ref[...]) + _dot(kr_ref[0].astype(BF16), place_ref[...])).astype(BF16)
    va = _dot(ckv16, wv_ref[...])
    for c in range(va.shape[0] // tk):
        vat_o[0, c] = va[c * tk:(c + 1) * tk, :].T.astype(BF16)


def _expand(ckv, kr, wkn, wv, place, *, tm, tk):
    nb, t, _ = ckv.shape
    const = lambda a: pl.BlockSpec(a.shape, lambda b, i: (0,) * a.ndim)
    return pl.pallas_call(
        functools.partial(_expand_kernel, tk=tk),
        out_shape=[jax.ShapeDtypeStruct((nb, t, QK_PAD), BF16),
                   jax.ShapeDtypeStruct((nb, t // tk, W_A, tk), BF16)],
        grid=(nb, t // tm),
        in_specs=[pl.BlockSpec((1, tm, KV_LORA), lambda b, i: (b, i, 0)),
                  pl.BlockSpec((1, tm, ROPE_DIM), lambda b, i: (b, i, 0)),
                  const(wkn), const(wv), const(place)],
        out_specs=[pl.BlockSpec((1, tm, QK_PAD), lambda b, i: (b, i, 0)),
                   pl.BlockSpec((1, tm // tk, W_A, tk), lambda b, i: (b, i, 0, 0))],
        compiler_params=_params(2), name="kv_expand",
    )(ckv, kr, wkn, wv, place)


def _rows8(x, op):
    rows, cols = x.shape
    return op(x.reshape(rows // SUBLANES, SUBLANES, cols), axis=0)


def _softmax_reset(mx_scr, sum_scr, acc_scr):
    mx_scr[...] = jnp.full(mx_scr.shape, NEG, F32)
    sum_scr[...] = jnp.zeros(sum_scr.shape, F32)
    acc_scr[...] = jnp.zeros(acc_scr.shape, F32)


def _softmax_pass1(hd, j, s, s_scr, mx_scr):
    s_scr[hd, j] = s
    mx_scr[hd] = jnp.maximum(mx_scr[hd], _rows8(s, jnp.max))


def _softmax_fix_max(mx_scr):
    for hd in range(mx_scr.shape[0]):
        mx_scr[hd] = jnp.broadcast_to(jnp.max(mx_scr[hd], axis=0, keepdims=True), mx_scr.shape[1:])


def _softmax_pass2(hd, j, vt_j, s_scr, mx_scr, sum_scr, acc_scr):
    p = jnp.exp(s_scr[hd, j] - mx_scr[hd, 0:1, :])
    sum_scr[hd] = sum_scr[hd] + _rows8(p, jnp.sum)
    acc_scr[hd] = acc_scr[hd] + _dot(vt_j, p.astype(BF16))


def _softmax_finish(sum_scr, acc_scr, o_ref, transpose_out):
    o_t = jnp.concatenate([acc_scr[hd] / jnp.sum(sum_scr[hd], axis=0, keepdims=True)
                           for hd in range(acc_scr.shape[0])], axis=0)
    o_ref[0] = o_t.T if transpose_out else o_t


def _softmax_scratch(n_heads, n_tiles, dv, tk, tq):
    return [pltpu.VMEM((n_heads, n_tiles, tk, tq), F32), pltpu.VMEM((n_heads, SUBLANES, tq), F32),
            pltpu.VMEM((n_heads, SUBLANES, tq), F32), pltpu.VMEM((n_heads, dv, tq), F32)]


def _last_tile_visible(tk, tq, causal, n_valid_last):
    r = lax.broadcasted_iota(jnp.int32, (tk, tq), 0)
    if causal:
        c = lax.broadcasted_iota(jnp.int32, (tk, tq), 1)
        return (r // CHUNK) <= (c // CHUNK)
    return r < n_valid_last


def _mla_kernel(q_ref, k_ref, vt_ref, o_ref, s_scr, mx_scr, sum_scr, acc_scr, *, tq, tk, n_tiles, causal,
                n_valid_last, transpose_out):
    last = pl.program_id(1) if causal else n_tiles - 1
    _softmax_reset(mx_scr, sum_scr, acc_scr)

    def logits(j, add):
        for hd in range(MLA_HEADS):
            sl = slice(hd * LANES, (hd + 1) * LANES)
            s = _dot_nt(k_ref[0, _key_rows(j, tk), sl], q_ref[0, :, sl])
            _softmax_pass1(hd, j, s if add is None else s + add, s_scr, mx_scr)

    def pass1(j, _):
        logits(j, None)
        return 0

    def pass2(j, _):
        for hd in range(MLA_HEADS):
            _softmax_pass2(hd, j, vt_ref[0, j, hd * V_DIM:(hd + 1) * V_DIM, :], s_scr, mx_scr, sum_scr, acc_scr)
        return 0

    lax.fori_loop(0, last, pass1, 0)
    logits(last, jnp.where(_last_tile_visible(tk, tq, causal, n_valid_last), 0.0, NEG))
    _softmax_fix_max(mx_scr)
    lax.fori_loop(0, last + 1, pass2, 0)
    _softmax_finish(sum_scr, acc_scr, o_ref, transpose_out)


def _mla(q, k, vt, *, tq, tk, causal, n_valid_last, transpose_out):
    nb, t, _ = q.shape
    tkeys = k.shape[1]
    n_tiles = tkeys // tk
    if transpose_out:
        out_shape = jax.ShapeDtypeStruct((nb, t, W_A), F32)
        out_spec = pl.BlockSpec((1, tq, W_A), lambda b, i: (b, i, 0))
    else:
        out_shape = jax.ShapeDtypeStruct((nb, W_A, t), F32)
        out_spec = pl.BlockSpec((1, W_A, tq), lambda b, i: (b, 0, i))
    return pl.pallas_call(
        functools.partial(_mla_kernel, tq=tq, tk=tk, n_tiles=n_tiles, causal=causal, n_valid_last=n_valid_last,
                          transpose_out=transpose_out),
        out_shape=out_shape, grid=(nb, t // tq),
        in_specs=[pl.BlockSpec((1, tq, QK_PAD), lambda b, i: (b, i, 0)),
                  pl.BlockSpec((1, tkeys, QK_PAD), lambda b, i: (b, 0, 0)),
                  pl.BlockSpec((1, n_tiles, W_A, tk), lambda b, i: (b, 0, 0, 0))],
        out_specs=out_spec, scratch_shapes=_softmax_scratch(MLA_HEADS, n_tiles, V_DIM, tk, tq),
        compiler_params=_params(2), name="mla_causal" if causal else "mla_cached",
    )(q, k, vt)


def _dsa_kernel(qi_ref, wit_ref, ki_ref, qb_ref, kb_ref, vbt_ref, bias_ref, o_ref, idx_scr, s_scr, mx_scr, sum_scr, acc_scr, *,
                tq, tk, n_tiles, causal, n_valid_last, topk, transpose_out):
    i = pl.program_id(1)
    last = i if causal else n_tiles - 1
    n_used = last + 1
    visible = _last_tile_visible(tk, tq, causal, n_valid_last)
    kf = float(topk)

    wit = wit_ref[0]

    def idx_tile(j):
        k_j = ki_ref[0, _key_rows(j, tk), :]
        acc = jnp.zeros((tk, tq), F32)
        for hd in range(IDX_HEADS):
            s = _dot_nt(k_j, qi_ref[0, :, hd * IDX_DIM:(hd + 1) * IDX_DIM])
            acc = acc + jnp.maximum(s, 0.0) * wit[hd:hd + 1, :]
        return acc

    def idx_body(j, carry):
        lo, hi = carry
        t = idx_tile(j)
        idx_scr[j] = t
        return jnp.minimum(lo, jnp.min(t, axis=0, keepdims=True)), jnp.maximum(hi, jnp.max(t, axis=0, keepdims=True))

    lo, hi = lax.fori_loop(0, last, idx_body, (jnp.full((1, tq), jnp.inf, F32), jnp.full((1, tq), -jnp.inf, F32)))
    t_last = idx_tile(last)
    idx_scr[last] = jnp.where(visible, t_last, -jnp.inf)
    lo = jnp.minimum(lo, jnp.min(jnp.where(visible, t_last, jnp.inf), axis=0, keepdims=True))
    hi = jnp.maximum(hi, jnp.max(jnp.where(visible, t_last, -jnp.inf), axis=0, keepdims=True))

    def count(pred):
        def body(j, c):
            return c + _colsum(jnp.where(pred(idx_scr[j]), 1.0, 0.0))
        return lax.fori_loop(0, n_used, body, jnp.zeros((1, tq), F32))

    c_lo = count(lambda t: t >= lo)
    c_hi = count(lambda t: t >= hi)
    top_heavy = c_hi >= kf
    lo = jnp.where(top_heavy, hi, lo)
    c_lo = jnp.where(top_heavy, c_hi, c_lo)

    def settled(lo, hi, c_lo):
        mid = 0.5 * lo + 0.5 * hi
        return jnp.where((c_lo <= kf) | (mid <= lo) | (mid >= hi), 1.0, 0.0)

    def bis_cond(st):
        _, _, _, done, it = st
        return jnp.logical_and(it < MAX_BISECT, jnp.min(done) < 0.5)

    def bis_body(st):
        lo, hi, c_lo, done, it = st
        for _ in range(BISECT_PER_CHECK):
            mid = 0.5 * lo + 0.5 * hi
            c = count(lambda t: t >= mid)
            live = done < 0.5
            up = live & (c >= kf)
            dn = live & (c < kf)
            lo = jnp.where(up, mid, lo)
            c_lo = jnp.where(up, c, c_lo)
            hi = jnp.where(dn, mid, hi)
            done = jnp.maximum(done, settled(lo, hi, c_lo))
        return lo, hi, c_lo, done, it + BISECT_PER_CHECK

    lo, hi, c_lo, _, _ = lax.while_loop(bis_cond, bis_body, (lo, hi, c_lo, settled(lo, hi, c_lo), jnp.int32(0)))

    any_tie = jnp.max(jnp.where(c_lo > kf, 1.0, 0.0)) > 0.5

    @pl.when(jnp.logical_not(any_tie))
    def _():
        def body(j, _):
            idx_scr[j] = jnp.where(idx_scr[j] >= lo, 0.0, NEG)
            return 0
        lax.fori_loop(0, n_used, body, 0)

    @pl.when(any_tie)
    def _():
        need = kf - count(lambda t: t > lo)
        r = lax.broadcasted_iota(jnp.int32, (tk, tk), 0)
        c = lax.broadcasted_iota(jnp.int32, (tk, tk), 1)
        before = jnp.where(c < r, 1.0, 0.0).astype(BF16)

        def body(j, seen):
            t = idx_scr[j]
            eq = jnp.where(t == lo, 1.0, 0.0)
            rank = _dot(before, eq.astype(BF16)) + seen
            take = (t > lo) | ((t == lo) & (rank < need))
            idx_scr[j] = jnp.where(take, 0.0, NEG)
            return seen + _colsum(eq)
        lax.fori_loop(0, n_used, body, jnp.zeros((1, tq), F32))

    _softmax_reset(mx_scr, sum_scr, acc_scr)
    kv_lanes = lambda hd: slice((hd // DSA_GROUP) * DSA_HEAD_DIM, (hd // DSA_GROUP + 1) * DSA_HEAD_DIM)

    def logits(j, near):
        mask = idx_scr[j]
        for hd in range(DSA_HEADS):
            s = _dot_nt(kb_ref[0, _key_rows(j, tk), kv_lanes(hd)],
                        qb_ref[0, :, hd * DSA_HEAD_DIM:(hd + 1) * DSA_HEAD_DIM])
            s = s + mask if near is None else s + (mask + bias_ref[hd, near])
            _softmax_pass1(hd, j, s, s_scr, mx_scr)

    def far_body(j, _):
        logits(j, None)
        return 0

    def pass2(j, _):
        for hd in range(DSA_HEADS):
            _softmax_pass2(hd, j, vbt_ref[0, j, kv_lanes(hd), :], s_scr, mx_scr, sum_scr, acc_scr)
        return 0

    lax.fori_loop(0, last - 1, far_body, 0)
    if causal:
        @pl.when(last >= 1)
        def _():
            logits(last - 1, 0)
    elif last >= 1:
        logits(last - 1, 0)
    logits(last, 1)
    _softmax_fix_max(mx_scr)
    lax.fori_loop(0, n_used, pass2, 0)
    _softmax_finish(sum_scr, acc_scr, o_ref, transpose_out)


def _dsa(qi, wit, ki, qb, kb, vbt, bias, *, tq, tk, causal, n_valid_last, topk, transpose_out):
    nb, t, _ = qb.shape
    tkeys = kb.shape[1]
    n_tiles = tkeys // tk
    if transpose_out:
        out_shape = jax.ShapeDtypeStruct((nb, t, W_B), F32)
        out_spec = pl.BlockSpec((1, tq, W_B), lambda b, i: (b, i, 0))
    else:
        out_shape = jax.ShapeDtypeStruct((nb, W_B, t), F32)
        out_spec = pl.BlockSpec((1, W_B, tq), lambda b, i: (b, 0, i))
    return pl.pallas_call(
        functools.partial(_dsa_kernel, tq=tq, tk=tk, n_tiles=n_tiles, causal=causal, n_valid_last=n_valid_last,
                          topk=topk, transpose_out=transpose_out),
        out_shape=out_shape, grid=(nb, t // tq),
        in_specs=[pl.BlockSpec((1, tq, IDX_HEADS * IDX_DIM), lambda b, i: (b, i, 0)),
                  pl.BlockSpec((1, IDX_HEADS, tq), lambda b, i: (b, 0, i)),
                  pl.BlockSpec((1, tkeys, IDX_DIM), lambda b, i: (b, 0, 0)),
                  pl.BlockSpec((1, tq, W_B), lambda b, i: (b, i, 0)),
                  pl.BlockSpec((1, tkeys, KV_DIM), lambda b, i: (b, 0, 0)),
                  pl.BlockSpec((1, n_tiles, KV_DIM, tk), lambda b, i: (b, 0, 0, 0)),
                  pl.BlockSpec(bias.shape, lambda b, i: (0, 0, 0, 0))],
        out_specs=out_spec,
        scratch_shapes=([pltpu.VMEM((n_tiles, tk, tq), F32)]
                        + _softmax_scratch(DSA_HEADS, n_tiles, DSA_HEAD_DIM, tk, tq)),
        compiler_params=_params(2), name="dsa_causal" if causal else "dsa_cached",
    )(qi, wit, ki, qb, kb, vbt, bias)


def _out_kernel(x_ref, shift_ref, scale_ref, gate_ref, gn_ref, oa_ref, ob_ref, wg_ref, woa_ref, wob_ref, wout_ref,
                gf_ref, o_ref, *, final):
    x = x_ref[0]
    h16 = (_rms(x, gn_ref[...]) * (1.0 + scale_ref[0]) + shift_ref[0]).astype(BF16)

    def gcols(a):
        return _dot(h16, wg_ref[:, a[0]:a[1]])

    ga = gcols(G_GA)
    ya = _dot((oa_ref[0] * (ga * _sigmoid(ga))).astype(BF16), woa_ref[...])
    gb = gcols(G_GB)
    yb = _dot((ob_ref[0] * (gb * _sigmoid(gb))).astype(BF16), wob_ref[...])
    m = _sigmoid(gcols(G_MA)) * ya + _sigmoid(gcols(G_MB)) * yb
    y = x + gate_ref[0] * _dot(m.astype(BF16), wout_ref[...])
    o_ref[0] = _rms(y, gf_ref[...]) if final else y


def _out(x, shift, scale, gate, gn, oa, ob, wg, woa, wob, wout, gf, *, tm, final):
    nb, t, _ = x.shape
    r = shift.shape[1]
    rb = 1 if r == 1 else tm
    tok = lambda w: pl.BlockSpec((1, tm, w), lambda b, i: (b, i, 0))
    mod = pl.BlockSpec((1, rb, D_MODEL), (lambda b, i: (b, 0, 0)) if r == 1 else (lambda b, i: (b, i, 0)))
    const = lambda a: pl.BlockSpec(a.shape, lambda b, i: (0,) * a.ndim)
    return pl.pallas_call(
        functools.partial(_out_kernel, final=final),
        out_shape=jax.ShapeDtypeStruct(x.shape, F32), grid=(nb, t // tm),
        in_specs=[tok(D_MODEL), mod, mod, mod, const(gn), tok(W_A), tok(W_B), const(wg), const(woa), const(wob),
                  const(wout), const(gf)],
        out_specs=tok(D_MODEL), compiler_params=_params(2), name="out_final" if final else "out_mix",
    )(x, shift, scale, gate, gn, oa, ob, wg, woa, wob, wout, gf)


def _pack_weights(w_in, w_uq, w_ukv, w_oa, w_ob, w_out):
    depth = w_in.shape[0]
    offs = np.concatenate([[0], np.cumsum(SPLIT_SIZES)])
    cq, ckv, kr, ga, qb, kb, vb, qi, ki, wi, gb, ma, mb = [w_in[:, :, offs[n]:offs[n + 1]] for n in range(13)]
    z = lambda w: jnp.zeros((depth, D_MODEL, w), F32)
    krg = jnp.concatenate([kr, z(NOPE_DIM - ROPE_DIM), kr, z(LANES - NOPE_DIM - ROPE_DIM)], axis=-1)
    sm = jnp.concatenate([ki, wi, z(LANES - IDX_DIM - IDX_HEADS)], axis=-1)
    wa = jnp.concatenate([cq, ckv, krg, sm, qb, kb, vb, qi], axis=-1).astype(BF16)
    wg = jnp.concatenate([ga, gb, ma, mb], axis=-1).astype(BF16)
    qd = NOPE_DIM + ROPE_DIM
    wuq = jnp.pad(w_uq.reshape(depth, Q_LORA, MLA_HEADS, qd), ((0, 0), (0, 0), (0, 0), (0, LANES - qd)))
    wuq = wuq.reshape(depth, Q_LORA, QK_PAD).astype(BF16)
    ukv = w_ukv.reshape(depth, KV_LORA, MLA_HEADS, NOPE_DIM + V_DIM)
    wkn = jnp.pad(ukv[..., :NOPE_DIM], ((0, 0), (0, 0), (0, 0), (0, LANES - NOPE_DIM)))
    wkn = wkn.reshape(depth, KV_LORA, QK_PAD).astype(BF16)
    wv = ukv[..., NOPE_DIM:].reshape(depth, KV_LORA, W_A).astype(BF16)
    return wa, wg, wuq, wkn, wv, w_oa.astype(BF16), w_ob.astype(BF16), w_out.astype(BF16)


def _rope_tables(pos):
    half = ROPE_DIM // 2
    inv = ROPE_THETA ** (-jnp.arange(half, dtype=F32) / half)
    ang = pos.astype(F32)[:, None] * inv
    cos, sin = jnp.cos(ang), jnp.sin(ang)
    n = pos.shape[0]
    z = lambda w: jnp.zeros((n, w), F32)
    pad = LANES - NOPE_DIM - ROPE_DIM
    q_c = jnp.concatenate([jnp.ones((n, NOPE_DIM), F32), cos, cos, z(pad)], axis=-1) * MLA_SCALE
    q_s1 = jnp.concatenate([z(NOPE_DIM), -sin, z(half), z(pad)], axis=-1) * MLA_SCALE
    q_s2 = jnp.concatenate([z(NOPE_DIM), z(half), sin, z(pad)], axis=-1) * MLA_SCALE
    gap = NOPE_DIM - ROPE_DIM
    k_c = jnp.concatenate([cos, cos, z(gap), cos, cos, z(pad)], axis=-1)
    k_s1 = jnp.concatenate([-sin, z(half), z(gap), -sin, z(half), z(pad)], axis=-1)
    k_s2 = jnp.concatenate([z(half), sin, z(gap), z(half), sin, z(pad)], axis=-1)
    return (q_c, q_s1, q_s2), (k_c, k_s1, k_s2)


def _placement():
    p = np.zeros((ROPE_DIM, QK_PAD), np.float32)
    for hd in range(MLA_HEADS):
        p[np.arange(ROPE_DIM), hd * LANES + NOPE_DIM + np.arange(ROPE_DIM)] = 1.0
    return jnp.asarray(p, dtype=BF16)


def kernel(x_prompt, x_sample, c_prompt, c_sample, cache_ckv, cache_krope, cache_kb, cache_vb, cache_kidx, w_ada,
           b_ada, g_norm, w_in, g_qnorm, w_uq, g_kvnorm, w_ukv, w_oa, w_ob, w_out, rel_bias, g_final):
    depth = w_in.shape[0]
    bp, t_p, _ = x_prompt.shape
    bs, t_s, _ = x_sample.shape
    past = cache_ckv.shape[2]
    topk_p = min(TOPK_MAX, t_p // 4)
    topk_s = min(TOPK_MAX, (past + t_s) // 4)
    assert t_p % TM == 0 and t_p % TQ == 0 and past % TK == 0 and t_s <= TK and t_s % 8 == 0
    keys_s = past + TK
    n_s = bs * t_s

    wa, wg, wuq, wkn, wv, woa, wob, wout = _pack_weights(w_in, w_uq, w_ukv, w_oa, w_ob, w_out)
    place = _placement()
    mod = _modulation(jnp.concatenate([c_prompt, c_sample], axis=0), w_ada, b_ada)
    mod = mod.reshape(depth, bp + bs, 3, 1, D_MODEL)
    tabs_p = _rope_tables(jnp.arange(t_p, dtype=jnp.int32))
    tabs_s = _rope_tables(jnp.tile(past + jnp.arange(t_s, dtype=jnp.int32), bs))
    bias_p = _bias_tiles(rel_bias, TK, TQ)
    bias_s = _bias_tiles(rel_bias, TK, t_s)
    row = lambda v: v.reshape(1, -1)

    xp = x_prompt
    xs = x_sample.reshape(1, n_s, D_MODEL)
    outs_p = [[] for _ in range(5)]
    outs_s = [[] for _ in range(5)]
    for l in range(depth):
        final = l == depth - 1
        gn, gq, gkv, gf = row(g_norm[l]), row(g_qnorm[l]), row(g_kvnorm[l]), row(g_final)

        shift, scale, gate = (mod[l, :bp, n] for n in range(3))
        (ckv, kr, kb, vb, ki, q, qb, qi, wit, kcat, vat, kb16, vbt, ki16) = _proj(
            xp, shift, scale, gn, wa[l], gq, wuq[l], gkv, wkn[l], wv[l], *tabs_p, tm=TM, tk=TK, emit_kv=True)
        oa = _mla(q, kcat, vat, tq=TQ, tk=TK, causal=True, n_valid_last=TK, transpose_out=True)
        ob = _dsa(qi, wit, ki16, qb, kb16, vbt, bias_p, tq=TQ, tk=TK, causal=True, n_valid_last=TK, topk=topk_p,
                  transpose_out=True)
        xp = _out(xp, shift, scale, gate, gn, oa, ob, wg[l], woa[l], wob[l], wout[l], gf, tm=TM, final=final)
        for lst, v in zip(outs_p, (ckv, kr, kb, vb, ki)):
            lst.append(v)

        shift, scale, gate = (jnp.repeat(mod[l, bp:, n, 0], t_s, axis=0)[None] for n in range(3))
        (ckv, kr, kb, vb, ki, q, qb, qi, wit) = _proj(
            xs, shift, scale, gn, wa[l], gq, wuq[l], gkv, wkn[l], wv[l], *tabs_s, tm=n_s, tk=TK, emit_kv=False)
        per_seq = lambda a: a.reshape((bs, t_s) + a.shape[2:])
        ckv, kr, kb, vb, ki, q, qb, qi = (per_seq(a) for a in (ckv, kr, kb, vb, ki, q, qb, qi))
        wit = wit.reshape(IDX_HEADS, bs, t_s).transpose(1, 0, 2)

        def with_cache(cache, new):
            pad = jnp.zeros((bs, TK - t_s) + new.shape[2:], new.dtype)
            return jnp.concatenate([cache, new, pad], axis=1)

        kcat, vat = _expand(with_cache(cache_ckv[l], ckv), with_cache(cache_krope[l], kr), wkn[l], wv[l], place,
                            tm=TK, tk=TK)
        oa = _mla(q, kcat, vat, tq=t_s, tk=TK, causal=False, n_valid_last=t_s, transpose_out=False)
        kb_all = with_cache(cache_kb[l].reshape(bs, past, KV_DIM), kb).astype(BF16)
        vb_all = with_cache(cache_vb[l].reshape(bs, past, KV_DIM), vb).astype(BF16)
        ki_all = with_cache(cache_kidx[l], ki).astype(BF16)
        vbt = vb_all.reshape(bs, keys_s // TK, TK, KV_DIM).transpose(0, 1, 3, 2)
        ob = _dsa(qi, wit, ki_all, qb, kb_all, vbt, bias_s, tq=t_s, tk=TK, causal=False, n_valid_last=t_s,
                  topk=topk_s, transpose_out=False)
        flat = lambda a: a.transpose(0, 2, 1).reshape(1, n_s, a.shape[1])
        xs = _out(xs, shift, scale, gate, gn, flat(oa), flat(ob), wg[l], woa[l], wob[l], wout[l], gf, tm=n_s,
                  final=final)
        for lst, v in zip(outs_s, (ckv, kr, kb, vb, ki)):
            lst.append(v)

    def stacked(lists, nb, t):
        ckv, kr, kb, vb, ki = (jnp.stack(v) for v in lists)
        kvh = (depth, nb, t, DSA_KV_HEADS, DSA_HEAD_DIM)
        return ckv, kr, kb.reshape(kvh), vb.reshape(kvh), ki

    return (xp, xs.reshape(bs, t_s, D_MODEL)) + stacked(outs_p, bp, t_p) + stacked(outs_s, bs, t_s)
```

```python
import functools

import numpy as np
import jax
import jax.numpy as jnp
from jax import lax
from jax.experimental import pallas as pl
from jax.experimental.pallas import tpu as pltpu

F32 = jnp.float32
BF16 = jnp.bfloat16

D_MODEL = 1024
CHUNK = 64
EPS = 1e-6
MLA_HEADS = 8
NOPE_DIM = 64
ROPE_DIM = 32
V_DIM = 64
Q_LORA = 384
KV_LORA = 256
ROPE_THETA = 10000.0
MLA_SCALE = (NOPE_DIM + ROPE_DIM) ** -0.5
DSA_HEADS = 8
DSA_KV_HEADS = 2
DSA_GROUP = DSA_HEADS // DSA_KV_HEADS
DSA_HEAD_DIM = 64
DSA_SCALE = DSA_HEAD_DIM ** -0.5
IDX_HEADS = 8
IDX_DIM = 64
IDX_W_SCALE = (IDX_HEADS ** -0.5) * (IDX_DIM ** -0.5)
TOPK_MAX = 256
NUM_BUCKETS = 32
MAX_DISTANCE = 128
W_A = MLA_HEADS * V_DIM
W_B = DSA_HEADS * DSA_HEAD_DIM
SPLIT_SIZES = (Q_LORA, KV_LORA, ROPE_DIM, W_A, W_B, DSA_KV_HEADS * DSA_HEAD_DIM, DSA_KV_HEADS * DSA_HEAD_DIM,
               IDX_HEADS * IDX_DIM, IDX_DIM, IDX_HEADS, W_B, D_MODEL, D_MODEL)

LANES = 128
SUBLANES = 8
VMEM_LIMIT_BYTES = 56 * 1024 * 1024

TQ = 256
TK = 256
TM = 512
NEG = -1e30
LOG2E = 1.4426950408889634
MAX_BISECT = 200
BISECT_PER_CHECK = 4

QK_PAD = MLA_HEADS * LANES
KV_DIM = DSA_KV_HEADS * DSA_HEAD_DIM

A_CQ = (0, Q_LORA)
A_CKV = (A_CQ[1], A_CQ[1] + KV_LORA)
A_KR = (A_CKV[1], A_CKV[1] + LANES)
A_SM = (A_KR[1], A_KR[1] + LANES)
A_QB = (A_SM[1], A_SM[1] + W_B)
A_KB = (A_QB[1], A_QB[1] + KV_DIM)
A_VB = (A_KB[1], A_KB[1] + KV_DIM)
A_QI = (A_VB[1], A_VB[1] + IDX_HEADS * IDX_DIM)
A_WIDTH = A_QI[1]
G_GA = (0, W_A)
G_GB = (G_GA[1], G_GA[1] + W_B)
G_MA = (G_GB[1], G_GB[1] + D_MODEL)
G_MB = (G_MA[1], G_MA[1] + D_MODEL)
G_WIDTH = G_MB[1]


def _params(n_grid):
    return pltpu.CompilerParams(dimension_semantics=("arbitrary",) * n_grid, vmem_limit_bytes=VMEM_LIMIT_BYTES)


def _sigmoid(x):
    return 1.0 / (1.0 + jnp.exp(-x))


def _rms(x, g):
    return x * lax.rsqrt(jnp.mean(x * x, axis=-1, keepdims=True) + EPS) * g


def _dot(a, b):
    return jnp.dot(a, b, preferred_element_type=F32)


def _dot_nt(a, b):
    return lax.dot_general(a, b, (((1,), (1,)), ((), ())), preferred_element_type=F32)


def _key_rows(j, tk):
    start = j * tk if isinstance(j, int) else pl.multiple_of(j * tk, tk)
    return pl.ds(start, tk)


def _rows8(x, op):
    parts = [x[r:r + SUBLANES] for r in range(0, x.shape[0], SUBLANES)]
    while len(parts) > 1:
        parts = [op(parts[n], parts[n + 1]) for n in range(0, len(parts) - 1, 2)] + parts[len(parts) & ~1:]
    return parts[0]


def _mod_kernel(c_ref, w_ref, b_ref, o_ref):
    c = c_ref[...]
    a = c * _sigmoid(c)
    w = w_ref[0]
    a_hi = a.astype(BF16)
    a_lo = (a - a_hi.astype(F32)).astype(BF16)
    w_hi = w.astype(BF16)
    w_lo = (w - w_hi.astype(F32)).astype(BF16)
    o_ref[0] = _dot(a_hi, w_hi) + (_dot(a_hi, w_lo) + _dot(a_lo, w_hi)) + b_ref[0]


def _modulation(c_all, w_ada, b_ada):
    depth = w_ada.shape[0]
    n = c_all.shape[0]
    tn = D_MODEL
    return pl.pallas_call(
        _mod_kernel,
        out_shape=jax.ShapeDtypeStruct((depth, n, 3 * D_MODEL), F32),
        grid=(depth, 3 * D_MODEL // tn),
        in_specs=[
            pl.BlockSpec((n, D_MODEL), lambda l, j: (0, 0)),
            pl.BlockSpec((1, D_MODEL, tn), lambda l, j: (l, 0, j)),
            pl.BlockSpec((1, 1, tn), lambda l, j: (l, 0, j)),
        ],
        out_specs=pl.BlockSpec((1, n, tn), lambda l, j: (l, 0, j)),
        compiler_params=_params(2),
        name="adaln_mod",
    )(c_all, w_ada, b_ada.reshape(depth, 1, 3 * D_MODEL))


def _rel_bucket(rel):
    half = NUM_BUCKETS // 2
    max_exact = half // 2
    n = np.abs(rel)
    large = max_exact + (np.log(np.maximum(n, 1).astype(np.float32) / max_exact)
                         / np.float32(np.log(MAX_DISTANCE / max_exact)) * (half - max_exact)).astype(np.int32)
    large = np.minimum(large, half - 1)
    return np.where(n < max_exact, n, large) + np.where(rel > 0, half, 0)


def _bias_kernel(bucket_ref, rel_ref, o_ref, *, far_bucket):
    h = pl.program_id(0)
    for t in range(2):
        b = bucket_ref[t]
        acc = jnp.zeros(b.shape, F32)
        for i in range(NUM_BUCKETS):
            acc = jnp.where(b == i, rel_ref[i, h], acc)
        o_ref[0, t] = (acc - rel_ref[far_bucket, h]) * LOG2E


def _bias_tiles(rel_bias, tk, tq):
    r = np.arange(tk)[:, None]
    c = np.arange(tq)[None, :]
    far_bucket = int(_rel_bucket(np.array(-tk)))
    assert np.all(_rel_bucket(-tk - np.arange(1, tk + tq)) == far_bucket)
    buckets = jnp.asarray(np.stack([_rel_bucket(r - tk - c), _rel_bucket(r - c)]).astype(np.int32))
    return pl.pallas_call(
        functools.partial(_bias_kernel, far_bucket=far_bucket),
        out_shape=jax.ShapeDtypeStruct((DSA_HEADS, 2, tk, tq), F32),
        grid=(DSA_HEADS,),
        in_specs=[
            pl.BlockSpec((2, tk, tq), lambda h: (0, 0, 0)),
            pl.BlockSpec(memory_space=pltpu.SMEM),
        ],
        out_specs=pl.BlockSpec((1, 2, tk, tq), lambda h: (h, 0, 0, 0)),
        compiler_params=_params(1),
        name="t5_bias_tiles",
    )(buckets, rel_bias)


def _rope_group(g, c_ref, s1_ref, s2_ref):
    return (g * c_ref[...] + pltpu.roll(g, LANES - ROPE_DIM // 2, 1) * s1_ref[...]
            + pltpu.roll(g, ROPE_DIM // 2, 1) * s2_ref[...])


def _proj_kernel(x_ref, shift_ref, scale_ref, gn_ref, wa_ref, gq_ref, wuq_ref, gkv_ref, wkn_ref, wv_ref,
                 qc_ref, qs1_ref, qs2_ref, kc_ref, ks1_ref, ks2_ref,
                 ckv_o, kr_o, kb_o, vb_o, ki_o, q_o, qb_o, qi_o, wit_o, *kv_outs, emit_kv, tk):
    x = x_ref[0]
    h = _rms(x, gn_ref[...]) * (1.0 + scale_ref[0]) + shift_ref[0]
    h16 = h.astype(BF16)

    def zcols(a):
        return _dot(h16, wa_ref[:, a[0]:a[1]])

    cqn = _rms(zcols(A_CQ), gq_ref[...]).astype(BF16)
    q = _dot(cqn, wuq_ref[...])
    for hd in range(MLA_HEADS):
        sl = slice(hd * LANES, (hd + 1) * LANES)
        q_o[0, :, sl] = _rope_group(q[:, sl], qc_ref, qs1_ref, qs2_ref).astype(BF16)

    ckvn = _rms(zcols(A_CKV), gkv_ref[...])
    ckv_o[0] = ckvn
    krg = _rope_group(zcols(A_KR), kc_ref, ks1_ref, ks2_ref)
    kr_o[0] = krg[:, 0:ROPE_DIM]

    sm = zcols(A_SM)
    ki = sm[:, 0:IDX_DIM]
    ki_o[0] = ki
    wit_o[0] = sm.T[IDX_DIM:IDX_DIM + IDX_HEADS, :] * IDX_W_SCALE
    qb_o[0] = (zcols(A_QB) * (DSA_SCALE * LOG2E)).astype(BF16)
    qi_o[0] = zcols(A_QI).astype(BF16)
    kb = zcols(A_KB)
    vb = zcols(A_VB)
    kb_o[0] = kb
    vb_o[0] = vb

    if emit_kv:
        kcat_o, vat_o, kb16_o, vbt_o, ki16_o = kv_outs
        ckv16 = ckvn.astype(BF16)
        kn = _dot(ckv16, wkn_ref[...])
        lane = lax.broadcasted_iota(jnp.int32, krg.shape, 1)
        k_add = jnp.where(lane >= NOPE_DIM, krg, 0.0)
        for hd in range(MLA_HEADS):
            sl = slice(hd * LANES, (hd + 1) * LANES)
            kcat_o[0, :, sl] = (kn[:, sl] + k_add).astype(BF16)
        va = _dot(ckv16, wv_ref[...])
        kb16_o[0] = kb.astype(BF16)
        ki16_o[0] = ki.astype(BF16)
        for c in range(x.shape[0] // tk):
            rows = slice(c * tk, (c + 1) * tk)
            vat_o[0, c] = va[rows, :].T.astype(BF16)
            vbt_o[0, c] = vb[rows, :].T.astype(BF16)


def _proj(x, shift, scale, gn, wa, gq, wuq, gkv, wkn, wv, q_tabs, k_tabs, *, tm, tk, emit_kv):
    nb, t, _ = x.shape
    r = shift.shape[1]
    rb = 1 if r == 1 else tm
    grid = (nb, t // tm)
    tok = lambda w: pl.BlockSpec((1, tm, w), lambda b, i: (b, i, 0))
    mod = pl.BlockSpec((1, rb, D_MODEL), (lambda b, i: (b, 0, 0)) if r == 1 else (lambda b, i: (b, i, 0)))
    const = lambda a: pl.BlockSpec(a.shape, lambda b, i: (0,) * a.ndim)
    tab = pl.BlockSpec((tm, LANES), lambda b, i: (i, 0))
    in_specs = [tok(D_MODEL), mod, mod, const(gn), const(wa), const(gq), const(wuq), const(gkv), const(wkn),
                const(wv)] + [tab] * 6
    out_shape = [
        jax.ShapeDtypeStruct((nb, t, KV_LORA), F32),
        jax.ShapeDtypeStruct((nb, t, ROPE_DIM), F32),
        jax.ShapeDtypeStruct((nb, t, KV_DIM), F32),
        jax.ShapeDtypeStruct((nb, t, KV_DIM), F32),
        jax.ShapeDtypeStruct((nb, t, IDX_DIM), F32),
        jax.ShapeDtypeStruct((nb, t, QK_PAD), BF16),
        jax.ShapeDtypeStruct((nb, t, W_B), BF16),
        jax.ShapeDtypeStruct((nb, t, IDX_HEADS * IDX_DIM), BF16),
        jax.ShapeDtypeStruct((nb, IDX_HEADS, t), F32),
    ]
    out_specs = [tok(KV_LORA), tok(ROPE_DIM), tok(KV_DIM), tok(KV_DIM), tok(IDX_DIM), tok(QK_PAD), tok(W_B),
                 tok(IDX_HEADS * IDX_DIM), pl.BlockSpec((1, IDX_HEADS, tm), lambda b, i: (b, 0, i))]
    if emit_kv:
        nkt = tm // tk
        out_shape += [
            jax.ShapeDtypeStruct((nb, t, QK_PAD), BF16),
            jax.ShapeDtypeStruct((nb, t // tk, W_A, tk), BF16),
            jax.ShapeDtypeStruct((nb, t, KV_DIM), BF16),
            jax.ShapeDtypeStruct((nb, t // tk, KV_DIM, tk), BF16),
            jax.ShapeDtypeStruct((nb, t, IDX_DIM), BF16),
        ]
        out_specs += [tok(QK_PAD), pl.BlockSpec((1, nkt, W_A, tk), lambda b, i: (b, i, 0, 0)), tok(KV_DIM),
                      pl.BlockSpec((1, nkt, KV_DIM, tk), lambda b, i: (b, i, 0, 0)), tok(IDX_DIM)]
    return pl.pallas_call(
        functools.partial(_proj_kernel, emit_kv=emit_kv, tk=tk),
        out_shape=out_shape, grid=grid, in_specs=in_specs, out_specs=out_specs,
        compiler_params=_params(2), name="proj_kv" if emit_kv else "proj_q",
    )(x, shift, scale, gn, wa, gq, wuq, gkv, wkn, wv, *q_tabs, *k_tabs)


def _expand_kernel(ckv_ref, kr_ref, wkn_ref, wv_ref, place_ref, kcat_o, vat_o, *, tk):
    ckv16 = ckv_ref[0].astype(BF16)
    kcat_o[0] = (_dot(ckv16, wkn_ref[...]) + _dot(kr_ref[0].astype(BF16), place_ref[...])).astype(BF16)
    va = _dot(ckv16, wv_ref[...])
    for c in range(va.shape[0] // tk):
        vat_o[0, c] = va[c * tk:(c + 1) * tk, :].T.astype(BF16)


def _expand(ckv, kr, wkn, wv, place, *, tm, tk):
    nb, t, _ = ckv.shape
    const = lambda a: pl.BlockSpec(a.shape, lambda b, i: (0,) * a.ndim)
    return pl.pallas_call(
        functools.partial(_expand_kernel, tk=tk),
        out_shape=[jax.ShapeDtypeStruct((nb, t, QK_PAD), BF16),
                   jax.ShapeDtypeStruct((nb, t // tk, W_A, tk), BF16)],
        grid=(nb, t // tm),
        in_specs=[pl.BlockSpec((1, tm, KV_LORA), lambda b, i: (b, i, 0)),
                  pl.BlockSpec((1, tm, ROPE_DIM), lambda b, i: (b, i, 0)),
                  const(wkn), const(wv), const(place)],
        out_specs=[pl.BlockSpec((1, tm, QK_PAD), lambda b, i: (b, i, 0)),
                   pl.BlockSpec((1, tm // tk, W_A, tk), lambda b, i: (b, i, 0, 0))],
        compiler_params=_params(2), name="kv_expand",
    )(ckv, kr, wkn, wv, place)


def _softmax_reset(mx_scr, sum_scr, acc_scr):
    mx_scr[...] = jnp.full(mx_scr.shape, NEG, F32)
    sum_scr[...] = jnp.zeros(sum_scr.shape, F32)
    acc_scr[...] = jnp.zeros(acc_scr.shape, F32)


def _softmax_pass1(hd, j, s, s_scr, mx_scr):
    s_scr[hd, j] = s
    mx_scr[hd] = jnp.maximum(mx_scr[hd], _rows8(s, jnp.maximum))


def _softmax_fix_max(mx_scr):
    for hd in range(mx_scr.shape[0]):
        mx_scr[hd] = jnp.broadcast_to(jnp.max(mx_scr[hd], axis=0, keepdims=True), mx_scr.shape[1:])


def _softmax_pass2(hd, j, vt_j, s_scr, mx_scr, sum_scr, acc_scr):
    p = jnp.exp2(s_scr[hd, j] - mx_scr[hd, 0:1, :])
    sum_scr[hd] = sum_scr[hd] + _rows8(p, jnp.add)
    acc_scr[hd] = acc_scr[hd] + _dot(vt_j, p.astype(BF16))


def _softmax_finish(sum_scr, acc_scr, o_ref, transpose_out):
    o_t = jnp.concatenate([acc_scr[hd] / jnp.sum(sum_scr[hd], axis=0, keepdims=True)
                           for hd in range(acc_scr.shape[0])], axis=0)
    o_ref[0] = o_t.T if transpose_out else o_t


def _softmax_scratch(n_heads, n_tiles, dv, tk, tq):
    return [pltpu.VMEM((n_heads, n_tiles, tk, tq), F32), pltpu.VMEM((n_heads, SUBLANES, tq), F32),
            pltpu.VMEM((n_heads, SUBLANES, tq), F32), pltpu.VMEM((n_heads, dv, tq), F32)]


def _last_tile_visible(tk, tq, causal, n_valid_last):
    r = lax.broadcasted_iota(jnp.int32, (tk, tq), 0)
    if causal:
        c = lax.broadcasted_iota(jnp.int32, (tk, tq), 1)
        return (r // CHUNK) <= (c // CHUNK)
    return r < n_valid_last


def _mla_kernel(q_ref, k_ref, vt_ref, o_ref, s_scr, mx_scr, sum_scr, acc_scr, *, tq, tk, n_tiles, causal,
                n_valid_last, transpose_out):
    last = pl.program_id(1) if causal else n_tiles - 1
    _softmax_reset(mx_scr, sum_scr, acc_scr)

    def logits(j, add):
        for hd in range(MLA_HEADS):
            sl = slice(hd * LANES, (hd + 1) * LANES)
            s = _dot_nt(k_ref[0, _key_rows(j, tk), sl], q_ref[0, :, sl])
            _softmax_pass1(hd, j, s if add is None else s + add, s_scr, mx_scr)

    def pass1(j, _):
        logits(j, None)
        return 0

    def pass2(j, _):
        for hd in range(MLA_HEADS):
            _softmax_pass2(hd, j, vt_ref[0, j, hd * V_DIM:(hd + 1) * V_DIM, :], s_scr, mx_scr, sum_scr, acc_scr)
        return 0

    lax.fori_loop(0, last, pass1, 0)
    logits(last, jnp.where(_last_tile_visible(tk, tq, causal, n_valid_last), 0.0, NEG))
    _softmax_fix_max(mx_scr)
    lax.fori_loop(0, last + 1, pass2, 0)
    _softmax_finish(sum_scr, acc_scr, o_ref, transpose_out)


def _mla(q, k, vt, *, tq, tk, causal, n_valid_last, transpose_out):
    nb, t, _ = q.shape
    tkeys = k.shape[1]
    n_tiles = tkeys // tk
    if transpose_out:
        out_shape = jax.ShapeDtypeStruct((nb, t, W_A), F32)
        out_spec = pl.BlockSpec((1, tq, W_A), lambda b, i: (b, i, 0))
    else:
        out_shape = jax.ShapeDtypeStruct((nb, W_A, t), F32)
        out_spec = pl.BlockSpec((1, W_A, tq), lambda b, i: (b, 0, i))
    return pl.pallas_call(
        functools.partial(_mla_kernel, tq=tq, tk=tk, n_tiles=n_tiles, causal=causal, n_valid_last=n_valid_last,
                          transpose_out=transpose_out),
        out_shape=out_shape, grid=(nb, t // tq),
        in_specs=[pl.BlockSpec((1, tq, QK_PAD), lambda b, i: (b, i, 0)),
                  pl.BlockSpec((1, tkeys, QK_PAD), lambda b, i: (b, 0, 0)),
                  pl.BlockSpec((1, n_tiles, W_A, tk), lambda b, i: (b, 0, 0, 0))],
        out_specs=out_spec, scratch_shapes=_softmax_scratch(MLA_HEADS, n_tiles, V_DIM, tk, tq),
        compiler_params=_params(2), name="mla_causal" if causal else "mla_cached",
    )(q, k, vt)


def _dsa_kernel(qi_ref, wit_ref, ki_ref, qb_ref, kb_ref, vbt_ref, bias_ref, o_ref, idx_scr, s_scr, mx_scr, sum_scr, acc_scr, *,
                tq, tk, n_tiles, causal, n_valid_last, topk, transpose_out):
    i = pl.program_id(1)
    last = i if causal else n_tiles - 1
    n_used = last + 1
    visible = _last_tile_visible(tk, tq, causal, n_valid_last)
    kf = float(topk)

    wit = wit_ref[0]

    def idx_tile(j):
        k_j = ki_ref[0, _key_rows(j, tk), :]
        acc = jnp.zeros((tk, tq), F32)
        for hd in range(IDX_HEADS):
            s = _dot_nt(k_j, qi_ref[0, :, hd * IDX_DIM:(hd + 1) * IDX_DIM])
            acc = acc + jnp.maximum(s, 0.0) * wit[hd:hd + 1, :]
        return acc

    def idx_body(j, carry):
        lo, hi = carry
        t = idx_tile(j)
        idx_scr[j] = t
        return jnp.minimum(lo, jnp.min(t, axis=0, keepdims=True)), jnp.maximum(hi, jnp.max(t, axis=0, keepdims=True))

    lo, hi = lax.fori_loop(0, last, idx_body, (jnp.full((1, tq), jnp.inf, F32), jnp.full((1, tq), -jnp.inf, F32)))
    t_last = idx_tile(last)
    idx_scr[last] = jnp.where(visible, t_last, -jnp.inf)
    lo = jnp.minimum(lo, jnp.min(jnp.where(visible, t_last, jnp.inf), axis=0, keepdims=True))
    hi = jnp.maximum(hi, jnp.max(jnp.where(visible, t_last, -jnp.inf), axis=0, keepdims=True))

    def count(*preds):
        def body(j, cs):
            t = idx_scr[j]
            return tuple(c + _rows8(jnp.where(p(t), 1.0, 0.0), jnp.add) for c, p in zip(cs, preds))
        cs = lax.fori_loop(0, n_used, body, tuple(jnp.zeros((SUBLANES, tq), F32) for _ in preds))
        return tuple(jnp.sum(c, axis=0, keepdims=True) for c in cs)

    lane = lax.broadcasted_iota(jnp.int32, (1, tq), 1)
    n_vis = (last * tk + (lane // CHUNK + 1) * CHUNK if causal else jnp.full((1, tq), last * tk + n_valid_last))
    c_lo = n_vis.astype(F32)
    c_hi, c_z, c_zp = count(lambda t: t >= hi, lambda t: t >= 0.0, lambda t: t > 0.0)
    top_heavy = c_hi >= kf
    zero_up = jnp.logical_not(top_heavy) & (c_z >= kf) & (lo < 0.0)
    zero_dn = jnp.logical_not(top_heavy) & (c_z < kf) & (hi > 0.0)
    zero_tie = jnp.where((c_z >= kf) & (c_zp < kf), 1.0, 0.0)
    lo = jnp.where(top_heavy, hi, jnp.where(zero_up, 0.0, lo))
    c_lo = jnp.where(top_heavy, c_hi, jnp.where(zero_up, c_z, c_lo))
    hi = jnp.where(zero_dn, 0.0, hi)

    def settled(lo, hi, c_lo):
        mid = 0.5 * lo + 0.5 * hi
        return jnp.where((c_lo <= kf) | (mid <= lo) | (mid >= hi), 1.0, 0.0)

    def bis_cond(st):
        _, _, _, done, it = st
        return jnp.logical_and(it < MAX_BISECT, jnp.min(done) < 0.5)

    def bis_body(st):
        lo, hi, c_lo, done, it = st
        for _ in range(BISECT_PER_CHECK):
            mid = 0.5 * lo + 0.5 * hi
            (c,) = count(lambda t: t >= mid)
            live = done < 0.5
            up = live & (c >= kf)
            dn = live & (c < kf)
            lo = jnp.where(up, mid, lo)
            c_lo = jnp.where(up, c, c_lo)
            hi = jnp.where(dn, mid, hi)
            done = jnp.maximum(done, settled(lo, hi, c_lo))
        return lo, hi, c_lo, done, it + BISECT_PER_CHECK

    done = jnp.maximum(settled(lo, hi, c_lo), zero_tie)
    lo, hi, c_lo, _, _ = lax.while_loop(bis_cond, bis_body, (lo, hi, c_lo, done, jnp.int32(0)))

    any_tie = jnp.max(jnp.where(c_lo > kf, 1.0, 0.0)) > 0.5

    @pl.when(jnp.logical_not(any_tie))
    def _():
        def body(j, _):
            idx_scr[j] = jnp.where(idx_scr[j] >= lo, 0.0, NEG)
            return 0
        lax.fori_loop(0, n_used, body, 0)

    @pl.when(any_tie)
    def _():
        need = kf - count(lambda t: t > lo)[0]
        r = lax.broadcasted_iota(jnp.int32, (tk, tk), 0)
        c = lax.broadcasted_iota(jnp.int32, (tk, tk), 1)
        before = jnp.where(c < r, 1.0, 0.0).astype(BF16)

        def body(j, seen):
            t = idx_scr[j]
            eq = jnp.where(t == lo, 1.0, 0.0)
            rank = _dot(before, eq.astype(BF16)) + seen
            take = (t > lo) | ((t == lo) & (rank < need))
            idx_scr[j] = jnp.where(take, 0.0, NEG)
            return seen + jnp.sum(_rows8(eq, jnp.add), axis=0, keepdims=True)
        lax.fori_loop(0, n_used, body, jnp.zeros((1, tq), F32))

    _softmax_reset(mx_scr, sum_scr, acc_scr)
    kv_lanes = lambda hd: slice((hd // DSA_GROUP) * DSA_HEAD_DIM, (hd // DSA_GROUP + 1) * DSA_HEAD_DIM)

    def logits(j, near):
        mask = idx_scr[j]
        for hd in range(DSA_HEADS):
            s = _dot_nt(kb_ref[0, _key_rows(j, tk), kv_lanes(hd)],
                        qb_ref[0, :, hd * DSA_HEAD_DIM:(hd + 1) * DSA_HEAD_DIM])
            s = s + mask if near is None else s + (mask + bias_ref[hd, near])
            _softmax_pass1(hd, j, s, s_scr, mx_scr)

    def far_body(j, _):
        logits(j, None)
        return 0

    def pass2(j, _):
        for hd in range(DSA_HEADS):
            _softmax_pass2(hd, j, vbt_ref[0, j, kv_lanes(hd), :], s_scr, mx_scr, sum_scr, acc_scr)
        return 0

    lax.fori_loop(0, last - 1, far_body, 0)
    if causal:
        @pl.when(last >= 1)
        def _():
            logits(last - 1, 0)
    elif last >= 1:
        logits(last - 1, 0)
    logits(last, 1)
    _softmax_fix_max(mx_scr)
    lax.fori_loop(0, n_used, pass2, 0)
    _softmax_finish(sum_scr, acc_scr, o_ref, transpose_out)


def _dsa(qi, wit, ki, qb, kb, vbt, bias, *, tq, tk, causal, n_valid_last, topk, transpose_out):
    nb, t, _ = qb.shape
    tkeys = kb.shape[1]
    n_tiles = tkeys // tk
    if transpose_out:
        out_shape = jax.ShapeDtypeStruct((nb, t, W_B), F32)
        out_spec = pl.BlockSpec((1, tq, W_B), lambda b, i: (b, i, 0))
    else:
        out_shape = jax.ShapeDtypeStruct((nb, W_B, t), F32)
        out_spec = pl.BlockSpec((1, W_B, tq), lambda b, i: (b, 0, i))
    return pl.pallas_call(
        functools.partial(_dsa_kernel, tq=tq, tk=tk, n_tiles=n_tiles, causal=causal, n_valid_last=n_valid_last,
                          topk=topk, transpose_out=transpose_out),
        out_shape=out_shape, grid=(nb, t // tq),
        in_specs=[pl.BlockSpec((1, tq, IDX_HEADS * IDX_DIM), lambda b, i: (b, i, 0)),
                  pl.BlockSpec((1, IDX_HEADS, tq), lambda b, i: (b, 0, i)),
                  pl.BlockSpec((1, tkeys, IDX_DIM), lambda b, i: (b, 0, 0)),
                  pl.BlockSpec((1, tq, W_B), lambda b, i: (b, i, 0)),
                  pl.BlockSpec((1, tkeys, KV_DIM), lambda b, i: (b, 0, 0)),
                  pl.BlockSpec((1, n_tiles, KV_DIM, tk), lambda b, i: (b, 0, 0, 0)),
                  pl.BlockSpec(bias.shape, lambda b, i: (0, 0, 0, 0))],
        out_specs=out_spec,
        scratch_shapes=([pltpu.VMEM((n_tiles, tk, tq), F32)]
                        + _softmax_scratch(DSA_HEADS, n_tiles, DSA_HEAD_DIM, tk, tq)),
        compiler_params=_params(2), name="dsa_causal" if causal else "dsa_cached",
    )(qi, wit, ki, qb, kb, vbt, bias)


def _out_kernel(x_ref, shift_ref, scale_ref, gate_ref, gn_ref, oa_ref, ob_ref, wg_ref, woa_ref, wob_ref, wout_ref,
                gf_ref, o_ref, *, final):
    x = x_ref[0]
    h16 = (_rms(x, gn_ref[...]) * (1.0 + scale_ref[0]) + shift_ref[0]).astype(BF16)

    def gcols(a):
        return _dot(h16, wg_ref[:, a[0]:a[1]])

    ga = gcols(G_GA)
    ya = _dot((oa_ref[0] * (ga * _sigmoid(ga))).astype(BF16), woa_ref[...])
    gb = gcols(G_GB)
    yb = _dot((ob_ref[0] * (gb * _sigmoid(gb))).astype(BF16), wob_ref[...])
    m = _sigmoid(gcols(G_MA)) * ya + _sigmoid(gcols(G_MB)) * yb
    y = x + gate_ref[0] * _dot(m.astype(BF16), wout_ref[...])
    o_ref[0] = _rms(y, gf_ref[...]) if final else y


def _out(x, shift, scale, gate, gn, oa, ob, wg, woa, wob, wout, gf, *, tm, final):
    nb, t, _ = x.shape
    r = shift.shape[1]
    rb = 1 if r == 1 else tm
    tok = lambda w: pl.BlockSpec((1, tm, w), lambda b, i: (b, i, 0))
    mod = pl.BlockSpec((1, rb, D_MODEL), (lambda b, i: (b, 0, 0)) if r == 1 else (lambda b, i: (b, i, 0)))
    const = lambda a: pl.BlockSpec(a.shape, lambda b, i: (0,) * a.ndim)
    return pl.pallas_call(
        functools.partial(_out_kernel, final=final),
        out_shape=jax.ShapeDtypeStruct(x.shape, F32), grid=(nb, t // tm),
        in_specs=[tok(D_MODEL), mod, mod, mod, const(gn), tok(W_A), tok(W_B), const(wg), const(woa), const(wob),
                  const(wout), const(gf)],
        out_specs=tok(D_MODEL), compiler_params=_params(2), name="out_final" if final else "out_mix",
    )(x, shift, scale, gate, gn, oa, ob, wg, woa, wob, wout, gf)


def _pack_weights(w_in, w_uq, w_ukv, w_oa, w_ob, w_out):
    depth = w_in.shape[0]
    offs = np.concatenate([[0], np.cumsum(SPLIT_SIZES)])
    cq, ckv, kr, ga, qb, kb, vb, qi, ki, wi, gb, ma, mb = [w_in[:, :, offs[n]:offs[n + 1]] for n in range(13)]
    z = lambda w: jnp.zeros((depth, D_MODEL, w), F32)
    krg = jnp.concatenate([kr, z(NOPE_DIM - ROPE_DIM), kr, z(LANES - NOPE_DIM - ROPE_DIM)], axis=-1)
    sm = jnp.concatenate([ki, wi, z(LANES - IDX_DIM - IDX_HEADS)], axis=-1)
    wa = jnp.concatenate([cq, ckv, krg, sm, qb, kb, vb, qi], axis=-1).astype(BF16)
    wg = jnp.concatenate([ga, gb, ma, mb], axis=-1).astype(BF16)
    qd = NOPE_DIM + ROPE_DIM
    wuq = jnp.pad(w_uq.reshape(depth, Q_LORA, MLA_HEADS, qd), ((0, 0), (0, 0), (0, 0), (0, LANES - qd)))
    wuq = wuq.reshape(depth, Q_LORA, QK_PAD).astype(BF16)
    ukv = w_ukv.reshape(depth, KV_LORA, MLA_HEADS, NOPE_DIM + V_DIM)
    wkn = jnp.pad(ukv[..., :NOPE_DIM], ((0, 0), (0, 0), (0, 0), (0, LANES - NOPE_DIM)))
    wkn = wkn.reshape(depth, KV_LORA, QK_PAD).astype(BF16)
    wv = ukv[..., NOPE_DIM:].reshape(depth, KV_LORA, W_A).astype(BF16)
    return wa, wg, wuq, wkn, wv, w_oa.astype(BF16), w_ob.astype(BF16), w_out.astype(BF16)


def _rope_tables(pos):
    half = ROPE_DIM // 2
    inv = ROPE_THETA ** (-jnp.arange(half, dtype=F32) / half)
    ang = pos.astype(F32)[:, None] * inv
    cos, sin = jnp.cos(ang), jnp.sin(ang)
    n = pos.shape[0]
    z = lambda w: jnp.zeros((n, w), F32)
    pad = LANES - NOPE_DIM - ROPE_DIM
    q_c = jnp.concatenate([jnp.ones((n, NOPE_DIM), F32), cos, cos, z(pad)], axis=-1) * (MLA_SCALE * LOG2E)
    q_s1 = jnp.concatenate([z(NOPE_DIM), -sin, z(half), z(pad)], axis=-1) * (MLA_SCALE * LOG2E)
    q_s2 = jnp.concatenate([z(NOPE_DIM), z(half), sin, z(pad)], axis=-1) * (MLA_SCALE * LOG2E)
    gap = NOPE_DIM - ROPE_DIM
    k_c = jnp.concatenate([cos, cos, z(gap), cos, cos, z(pad)], axis=-1)
    k_s1 = jnp.concatenate([-sin, z(half), z(gap), -sin, z(half), z(pad)], axis=-1)
    k_s2 = jnp.concatenate([z(half), sin, z(gap), z(half), sin, z(pad)], axis=-1)
    return (q_c, q_s1, q_s2), (k_c, k_s1, k_s2)


def _placement():
    p = np.zeros((ROPE_DIM, QK_PAD), np.float32)
    for hd in range(MLA_HEADS):
        p[np.arange(ROPE_DIM), hd * LANES + NOPE_DIM + np.arange(ROPE_DIM)] = 1.0
    return jnp.asarray(p, dtype=BF16)


def kernel(x_prompt, x_sample, c_prompt, c_sample, cache_ckv, cache_krope, cache_kb, cache_vb, cache_kidx, w_ada,
           b_ada, g_norm, w_in, g_qnorm, w_uq, g_kvnorm, w_ukv, w_oa, w_ob, w_out, rel_bias, g_final):
    depth = w_in.shape[0]
    bp, t_p, _ = x_prompt.shape
    bs, t_s, _ = x_sample.shape
    past = cache_ckv.shape[2]
    topk_p = min(TOPK_MAX, t_p // 4)
    topk_s = min(TOPK_MAX, (past + t_s) // 4)
    assert t_p % TM == 0 and t_p % TQ == 0 and past % TK == 0 and t_s <= TK and t_s % 8 == 0
    keys_s = past + TK
    n_s = bs * t_s

    wa, wg, wuq, wkn, wv, woa, wob, wout = _pack_weights(w_in, w_uq, w_ukv, w_oa, w_ob, w_out)
    place = _placement()
    mod = _modulation(jnp.concatenate([c_prompt, c_sample], axis=0), w_ada, b_ada)
    mod = mod.reshape(depth, bp + bs, 3, 1, D_MODEL)
    tabs_p = _rope_tables(jnp.arange(t_p, dtype=jnp.int32))
    tabs_s = _rope_tables(jnp.tile(past + jnp.arange(t_s, dtype=jnp.int32), bs))
    bias_p = _bias_tiles(rel_bias, TK, TQ)
    bias_s = _bias_tiles(rel_bias, TK, t_s)
    row = lambda v: v.reshape(1, -1)

    xp = x_prompt
    xs = x_sample.reshape(1, n_s, D_MODEL)
    outs_p = [[] for _ in range(5)]
    outs_s = [[] for _ in range(5)]
    for l in range(depth):
        final = l == depth - 1
        gn, gq, gkv, gf = row(g_norm[l]), row(g_qnorm[l]), row(g_kvnorm[l]), row(g_final)

        shift, scale, gate = (mod[l, :bp, n] for n in range(3))
        (ckv, kr, kb, vb, ki, q, qb, qi, wit, kcat, vat, kb16, vbt, ki16) = _proj(
            xp, shift, scale, gn, wa[l], gq, wuq[l], gkv, wkn[l], wv[l], *tabs_p, tm=TM, tk=TK, emit_kv=True)
        oa = _mla(q, kcat, vat, tq=TQ, tk=TK, causal=True, n_valid_last=TK, transpose_out=True)
        ob = _dsa(qi, wit, ki16, qb, kb16, vbt, bias_p, tq=TQ, tk=TK, causal=True, n_valid_last=TK, topk=topk_p,
                  transpose_out=True)
        xp = _out(xp, shift, scale, gate, gn, oa, ob, wg[l], woa[l], wob[l], wout[l], gf, tm=TM, final=final)
        for lst, v in zip(outs_p, (ckv, kr, kb, vb, ki)):
            lst.append(v)

        shift, scale, gate = (jnp.repeat(mod[l, bp:, n, 0], t_s, axis=0)[None] for n in range(3))
        (ckv, kr, kb, vb, ki, q, qb, qi, wit) = _proj(
            xs, shift, scale, gn, wa[l], gq, wuq[l], gkv, wkn[l], wv[l], *tabs_s, tm=n_s, tk=TK, emit_kv=False)
        per_seq = lambda a: a.reshape((bs, t_s) + a.shape[2:])
        ckv, kr, kb, vb, ki, q, qb, qi = (per_seq(a) for a in (ckv, kr, kb, vb, ki, q, qb, qi))
        wit = wit.reshape(IDX_HEADS, bs, t_s).transpose(1, 0, 2)

        def with_cache(cache, new):
            pad = jnp.zeros((bs, TK - t_s) + new.shape[2:], new.dtype)
            return jnp.concatenate([cache, new, pad], axis=1)

        kcat, vat = _expand(with_cache(cache_ckv[l], ckv), with_cache(cache_krope[l], kr), wkn[l], wv[l], place,
                            tm=TK, tk=TK)
        oa = _mla(q, kcat, vat, tq=t_s, tk=TK, causal=False, n_valid_last=t_s, transpose_out=False)
        kb_all = with_cache(cache_kb[l].reshape(bs, past, KV_DIM), kb).astype(BF16)
        vb_all = with_cache(cache_vb[l].reshape(bs, past, KV_DIM), vb).astype(BF16)
        ki_all = with_cache(cache_kidx[l], ki).astype(BF16)
        vbt = vb_all.reshape(bs, keys_s // TK, TK, KV_DIM).transpose(0, 1, 3, 2)
        ob = _dsa(qi, wit, ki_all, qb, kb_all, vbt, bias_s, tq=t_s, tk=TK, causal=False, n_valid_last=t_s,
                  topk=topk_s, transpose_out=False)
        flat = lambda a: a.transpose(0, 2, 1).reshape(1, n_s, a.shape[1])
        xs = _out(xs, shift, scale, gate, gn, flat(oa), flat(ob), wg[l], woa[l], wob[l], wout[l], gf, tm=n_s,
                  final=final)
        for lst, v in zip(outs_s, (ckv, kr, kb, vb, ki)):
            lst.append(v)

    def stacked(lists, nb, t):
        ckv, kr, kb, vb, ki = (jnp.stack(v) for v in lists)
        kvh = (depth, nb, t, DSA_KV_HEADS, DSA_HEAD_DIM)
        return ckv, kr, kb.reshape(kvh), vb.reshape(kvh), ki

    return (xp, xs.reshape(bs, t_s, D_MODEL)) + stacked(outs_p, bp, t_p) + stacked(outs_s, bs, t_s)
```

```python
import functools

import numpy as np
import jax
import jax.numpy as jnp
from jax import lax
from jax.experimental import pallas as pl
from jax.experimental.pallas import tpu as pltpu

F32 = jnp.float32
BF16 = jnp.bfloat16

D_MODEL = 1024
CHUNK = 64
EPS = 1e-6
MLA_HEADS = 8
NOPE_DIM = 64
ROPE_DIM = 32
V_DIM = 64
Q_LORA = 384
KV_LORA = 256
ROPE_THETA = 10000.0
MLA_SCALE = (NOPE_DIM + ROPE_DIM) ** -0.5
DSA_HEADS = 8
DSA_KV_HEADS = 2
DSA_GROUP = DSA_HEADS // DSA_KV_HEADS
DSA_HEAD_DIM = 64
DSA_SCALE = DSA_HEAD_DIM ** -0.5
IDX_HEADS = 8
IDX_DIM = 64
IDX_W_SCALE = (IDX_HEADS ** -0.5) * (IDX_DIM ** -0.5)
TOPK_MAX = 256
NUM_BUCKETS = 32
MAX_DISTANCE = 128
W_A = MLA_HEADS * V_DIM
W_B = DSA_HEADS * DSA_HEAD_DIM
SPLIT_SIZES = (Q_LORA, KV_LORA, ROPE_DIM, W_A, W_B, DSA_KV_HEADS * DSA_HEAD_DIM, DSA_KV_HEADS * DSA_HEAD_DIM,
               IDX_HEADS * IDX_DIM, IDX_DIM, IDX_HEADS, W_B, D_MODEL, D_MODEL)

LANES = 128
SUBLANES = 8
VMEM_LIMIT_BYTES = 56 * 1024 * 1024

TQ = 256
TK = 256
TM = 512
NEG = -1e30
LOG2E = 1.4426950408889634
MAX_BISECT = 200
BISECT_PER_CHECK = 4

QK_PAD = MLA_HEADS * LANES
KV_DIM = DSA_KV_HEADS * DSA_HEAD_DIM

A_CQ = (0, Q_LORA)
A_CKV = (A_CQ[1], A_CQ[1] + KV_LORA)
A_KR = (A_CKV[1], A_CKV[1] + LANES)
A_SM = (A_KR[1], A_KR[1] + LANES)
A_QB = (A_SM[1], A_SM[1] + W_B)
A_KB = (A_QB[1], A_QB[1] + KV_DIM)
A_VB = (A_KB[1], A_KB[1] + KV_DIM)
A_QI = (A_VB[1], A_VB[1] + IDX_HEADS * IDX_DIM)
A_WIDTH = A_QI[1]
G_GA = (0, W_A)
G_GB = (G_GA[1], G_GA[1] + W_B)
G_MA = (G_GB[1], G_GB[1] + D_MODEL)
G_MB = (G_MA[1], G_MA[1] + D_MODEL)
G_WIDTH = G_MB[1]


def _params(n_grid):
    return pltpu.CompilerParams(dimension_semantics=("arbitrary",) * n_grid, vmem_limit_bytes=VMEM_LIMIT_BYTES)


def _sigmoid(x):
    return 1.0 / (1.0 + jnp.exp(-x))


def _rms(x, g):
    return x * lax.rsqrt(jnp.mean(x * x, axis=-1, keepdims=True) + EPS) * g


def _dot(a, b):
    return jnp.dot(a, b, preferred_element_type=F32)


def _dot_nt(a, b):
    return lax.dot_general(a, b, (((1,), (1,)), ((), ())), preferred_element_type=F32)


def _key_rows(j, tk):
    start = j * tk if isinstance(j, int) else pl.multiple_of(j * tk, tk)
    return pl.ds(start, tk)


def _rows8(x, op, ways=None):
    parts = [x[r:r + SUBLANES] for r in range(0, x.shape[0], SUBLANES)]
    if ways is not None:
        chains = parts[:ways]
        for n, part in enumerate(parts[ways:]):
            chains[n % ways] = op(chains[n % ways], part)
        parts = chains
    while len(parts) > 1:
        parts = [op(parts[n], parts[n + 1]) for n in range(0, len(parts) - 1, 2)] + parts[len(parts) & ~1:]
    return parts[0]


def _mod_kernel(c_ref, w_ref, b_ref, o_ref):
    c = c_ref[...]
    a = c * _sigmoid(c)
    w = w_ref[0]
    a_hi = a.astype(BF16)
    a_lo = (a - a_hi.astype(F32)).astype(BF16)
    w_hi = w.astype(BF16)
    w_lo = (w - w_hi.astype(F32)).astype(BF16)
    o_ref[0] = _dot(a_hi, w_hi) + (_dot(a_hi, w_lo) + _dot(a_lo, w_hi)) + b_ref[0]


def _modulation(c_all, w_ada, b_ada):
    depth = w_ada.shape[0]
    n = c_all.shape[0]
    tn = D_MODEL
    return pl.pallas_call(
        _mod_kernel,
        out_shape=jax.ShapeDtypeStruct((depth, n, 3 * D_MODEL), F32),
        grid=(depth, 3 * D_MODEL // tn),
        in_specs=[
            pl.BlockSpec((n, D_MODEL), lambda l, j: (0, 0)),
            pl.BlockSpec((1, D_MODEL, tn), lambda l, j: (l, 0, j)),
            pl.BlockSpec((1, 1, tn), lambda l, j: (l, 0, j)),
        ],
        out_specs=pl.BlockSpec((1, n, tn), lambda l, j: (l, 0, j)),
        compiler_params=_params(2),
        name="adaln_mod",
    )(c_all, w_ada, b_ada.reshape(depth, 1, 3 * D_MODEL))


def _rel_bucket(rel):
    half = NUM_BUCKETS // 2
    max_exact = half // 2
    n = np.abs(rel)
    large = max_exact + (np.log(np.maximum(n, 1).astype(np.float32) / max_exact)
                         / np.float32(np.log(MAX_DISTANCE / max_exact)) * (half - max_exact)).astype(np.int32)
    large = np.minimum(large, half - 1)
    return np.where(n < max_exact, n, large) + np.where(rel > 0, half, 0)


def _bias_kernel(bucket_ref, rel_ref, o_ref, *, far_bucket):
    h = pl.program_id(0)
    for t in range(2):
        b = bucket_ref[t]
        acc = jnp.zeros(b.shape, F32)
        for i in range(NUM_BUCKETS):
            acc = jnp.where(b == i, rel_ref[i, h], acc)
        o_ref[0, t] = (acc - rel_ref[far_bucket, h]) * LOG2E


def _bias_tiles(rel_bias, tk, tq):
    r = np.arange(tk)[:, None]
    c = np.arange(tq)[None, :]
    far_bucket = int(_rel_bucket(np.array(-tk)))
    assert np.all(_rel_bucket(-tk - np.arange(1, tk + tq)) == far_bucket)
    buckets = jnp.asarray(np.stack([_rel_bucket(r - tk - c), _rel_bucket(r - c)]).astype(np.int32))
    return pl.pallas_call(
        functools.partial(_bias_kernel, far_bucket=far_bucket),
        out_shape=jax.ShapeDtypeStruct((DSA_HEADS, 2, tk, tq), F32),
        grid=(DSA_HEADS,),
        in_specs=[
            pl.BlockSpec((2, tk, tq), lambda h: (0, 0, 0)),
            pl.BlockSpec(memory_space=pltpu.SMEM),
        ],
        out_specs=pl.BlockSpec((1, 2, tk, tq), lambda h: (h, 0, 0, 0)),
        compiler_params=_params(1),
        name="t5_bias_tiles",
    )(buckets, rel_bias)


def _rope_group(g, c_ref, s1_ref, s2_ref):
    return (g * c_ref[...] + pltpu.roll(g, LANES - ROPE_DIM // 2, 1) * s1_ref[...]
            + pltpu.roll(g, ROPE_DIM // 2, 1) * s2_ref[...])


def _proj_kernel(x_ref, shift_ref, scale_ref, gn_ref, wa_ref, gq_ref, wuq_ref, gkv_ref, wkn_ref, wv_ref,
                 qc_ref, qs1_ref, qs2_ref, kc_ref, ks1_ref, ks2_ref,
                 ckv_o, kr_o, kb_o, vb_o, ki_o, q_o, qb_o, qi_o, wit_o, *kv_outs, emit_kv, tk):
    x = x_ref[0]
    h = _rms(x, gn_ref[...]) * (1.0 + scale_ref[0]) + shift_ref[0]
    h16 = h.astype(BF16)

    def zcols(a):
        return _dot(h16, wa_ref[:, a[0]:a[1]])

    cqn = _rms(zcols(A_CQ), gq_ref[...]).astype(BF16)
    q = _dot(cqn, wuq_ref[...])
    for hd in range(MLA_HEADS):
        sl = slice(hd * LANES, (hd + 1) * LANES)
        q_o[0, :, sl] = _rope_group(q[:, sl], qc_ref, qs1_ref, qs2_ref).astype(BF16)

    ckvn = _rms(zcols(A_CKV), gkv_ref[...])
    ckv_o[0] = ckvn
    krg = _rope_group(zcols(A_KR), kc_ref, ks1_ref, ks2_ref)
    kr_o[0] = krg[:, 0:ROPE_DIM]

    sm = zcols(A_SM)
    ki = sm[:, 0:IDX_DIM]
    ki_o[0] = ki
    wit_o[0] = sm.T[IDX_DIM:IDX_DIM + IDX_HEADS, :] * IDX_W_SCALE
    qb_o[0] = (zcols(A_QB) * (DSA_SCALE * LOG2E)).astype(BF16)
    qi_o[0] = zcols(A_QI).astype(BF16)
    kb = zcols(A_KB)
    vb = zcols(A_VB)
    kb_o[0] = kb
    vb_o[0] = vb

    if emit_kv:
        kcat_o, vat_o, kb16_o, vbt_o, ki16_o = kv_outs
        ckv16 = ckvn.astype(BF16)
        kn = _dot(ckv16, wkn_ref[...])
        lane = lax.broadcasted_iota(jnp.int32, krg.shape, 1)
        k_add = jnp.where(lane >= NOPE_DIM, krg, 0.0)
        for hd in range(MLA_HEADS):
            sl = slice(hd * LANES, (hd + 1) * LANES)
            kcat_o[0, :, sl] = (kn[:, sl] + k_add).astype(BF16)
        va = _dot(ckv16, wv_ref[...])
        kb16_o[0] = kb.astype(BF16)
        ki16_o[0] = ki.astype(BF16)
        for c in range(x.shape[0] // tk):
            rows = slice(c * tk, (c + 1) * tk)
            vat_o[0, c] = va[rows, :].T.astype(BF16)
            vbt_o[0, c] = vb[rows, :].T.astype(BF16)


def _proj(x, shift, scale, gn, wa, gq, wuq, gkv, wkn, wv, q_tabs, k_tabs, *, tm, tk, emit_kv):
    nb, t, _ = x.shape
    r = shift.shape[1]
    rb = 1 if r == 1 else tm
    grid = (nb, t // tm)
    tok = lambda w: pl.BlockSpec((1, tm, w), lambda b, i: (b, i, 0))
    mod = pl.BlockSpec((1, rb, D_MODEL), (lambda b, i: (b, 0, 0)) if r == 1 else (lambda b, i: (b, i, 0)))
    const = lambda a: pl.BlockSpec(a.shape, lambda b, i: (0,) * a.ndim)
    tab = pl.BlockSpec((tm, LANES), lambda b, i: (i, 0))
    in_specs = [tok(D_MODEL), mod, mod, const(gn), const(wa), const(gq), const(wuq), const(gkv), const(wkn),
                const(wv)] + [tab] * 6
    out_shape = [
        jax.ShapeDtypeStruct((nb, t, KV_LORA), F32),
        jax.ShapeDtypeStruct((nb, t, ROPE_DIM), F32),
        jax.ShapeDtypeStruct((nb, t, KV_DIM), F32),
        jax.ShapeDtypeStruct((nb, t, KV_DIM), F32),
        jax.ShapeDtypeStruct((nb, t, IDX_DIM), F32),
        jax.ShapeDtypeStruct((nb, t, QK_PAD), BF16),
        jax.ShapeDtypeStruct((nb, t, W_B), BF16),
        jax.ShapeDtypeStruct((nb, t, IDX_HEADS * IDX_DIM), BF16),
        jax.ShapeDtypeStruct((nb, IDX_HEADS, t), F32),
    ]
    out_specs = [tok(KV_LORA), tok(ROPE_DIM), tok(KV_DIM), tok(KV_DIM), tok(IDX_DIM), tok(QK_PAD), tok(W_B),
                 tok(IDX_HEADS * IDX_DIM), pl.BlockSpec((1, IDX_HEADS, tm), lambda b, i: (b, 0, i))]
    if emit_kv:
        nkt = tm // tk
        out_shape += [
            jax.ShapeDtypeStruct((nb, t, QK_PAD), BF16),
            jax.ShapeDtypeStruct((nb, t // tk, W_A, tk), BF16),
            jax.ShapeDtypeStruct((nb, t, KV_DIM), BF16),
            jax.ShapeDtypeStruct((nb, t // tk, KV_DIM, tk), BF16),
            jax.ShapeDtypeStruct((nb, t, IDX_DIM), BF16),
        ]
        out_specs += [tok(QK_PAD), pl.BlockSpec((1, nkt, W_A, tk), lambda b, i: (b, i, 0, 0)), tok(KV_DIM),
                      pl.BlockSpec((1, nkt, KV_DIM, tk), lambda b, i: (b, i, 0, 0)), tok(IDX_DIM)]
    return pl.pallas_call(
        functools.partial(_proj_kernel, emit_kv=emit_kv, tk=tk),
        out_shape=out_shape, grid=grid, in_specs=in_specs, out_specs=out_specs,
        compiler_params=_params(2), name="proj_kv" if emit_kv else "proj_q",
    )(x, shift, scale, gn, wa, gq, wuq, gkv, wkn, wv, *q_tabs, *k_tabs)


def _expand_kernel(ckv_ref, kr_ref, wkn_ref, wv_ref, place_ref, kcat_o, vat_o, *, tk):
    ckv16 = ckv_ref[0].astype(BF16)
    kcat_o[0] = (_dot(ckv16, wkn_ref[...]) + _dot(kr_ref[0].astype(BF16), place_ref[...])).astype(BF16)
    va = _dot(ckv16, wv_ref[...])
    for c in range(va.shape[0] // tk):
        vat_o[0, c] = va[c * tk:(c + 1) * tk, :].T.astype(BF16)


def _expand(ckv, kr, wkn, wv, place, *, tm, tk):
    nb, t, _ = ckv.shape
    const = lambda a: pl.BlockSpec(a.shape, lambda b, i: (0,) * a.ndim)
    return pl.pallas_call(
        functools.partial(_expand_kernel, tk=tk),
        out_shape=[jax.ShapeDtypeStruct((nb, t, QK_PAD), BF16),
                   jax.ShapeDtypeStruct((nb, t // tk, W_A, tk), BF16)],
        grid=(nb, t // tm),
        in_specs=[pl.BlockSpec((1, tm, KV_LORA), lambda b, i: (b, i, 0)),
                  pl.BlockSpec((1, tm, ROPE_DIM), lambda b, i: (b, i, 0)),
                  const(wkn), const(wv), const(place)],
        out_specs=[pl.BlockSpec((1, tm, QK_PAD), lambda b, i: (b, i, 0)),
                   pl.BlockSpec((1, tm // tk, W_A, tk), lambda b, i: (b, i, 0, 0))],
        compiler_params=_params(2), name="kv_expand",
    )(ckv, kr, wkn, wv, place)


class _Softmax:
    ONES_ROWS = 16

    def __init__(self, s_scr, mx_scr, m_scr, acc_scr):
        self.s, self.mx, self.m, self.acc = s_scr, mx_scr, m_scr, acc_scr
        self.n_heads = acc_scr.shape[0]
        self.dv = acc_scr.shape[1] - self.ONES_ROWS

    @classmethod
    def scratch(cls, n_heads, dv, tk, tq):
        stat = pltpu.VMEM((n_heads, SUBLANES, tq), F32)
        return [pltpu.VMEM((n_heads, 2, tk, tq), F32), stat, stat, pltpu.VMEM((n_heads, dv + cls.ONES_ROWS, tq), F32)]

    def reset(self):
        self.mx[...] = jnp.full(self.mx.shape, NEG, F32)
        self.m[...] = jnp.full(self.m.shape, NEG, F32)
        self.acc[...] = jnp.zeros(self.acc.shape, F32)

    def produce(self, hd, slot, s):
        self.s[hd, slot] = s
        self.mx[hd] = jnp.maximum(self.mx[hd], _rows8(s, jnp.maximum, ways=4))

    def consume(self, hd, slot, vt):
        m_old = self.m[hd, 0:1, :]
        m_new = jnp.max(self.mx[hd], axis=0, keepdims=True)
        alpha = jnp.exp2(m_old - m_new)
        p = jnp.exp2(self.s[hd, slot] - m_new)
        self.m[hd] = jnp.broadcast_to(m_new, self.m.shape[1:])
        vt1 = jnp.concatenate([vt, jnp.ones((self.ONES_ROWS, vt.shape[1]), vt.dtype)], axis=0)
        self.acc[hd] = alpha * self.acc[hd] + _dot(vt1, p.astype(BF16))

    def run(self, last, logits, values):
        static = isinstance(last, int)

        def step(t_consume, t_produce, kind, slot_produce):
            for hd in range(self.n_heads):
                if t_consume is not None:
                    self.consume(hd, 1 - slot_produce, values(hd, t_consume))
                if t_produce is not None:
                    self.produce(hd, slot_produce, logits(hd, t_produce, kind))

        def when(cond, fn):
            if static:
                if cond:
                    fn()
            else:
                pl.when(cond)(fn)

        self.reset()
        step(None, last, 2, 0)
        when(last >= 1, lambda: step(last, last - 1, 1, 1))
        n_far = last - 1

        def far_pair(n, _):
            t = last - 2 - 2 * n
            step(t + 1, t, 0, 0)
            step(t, t - 1, 0, 1)
            return 0

        lax.fori_loop(0, n_far // 2, far_pair, 0)
        odd = (n_far >= 1) & (n_far % 2 == 1)
        when(odd, lambda: step(1, 0, 0, 0))
        in_slot0 = (last == 0) | odd
        when(in_slot0, lambda: step(0, None, None, 1))
        when((not in_slot0) if static else jnp.logical_not(in_slot0), lambda: step(0, None, None, 0))

    def finish(self, o_ref, transpose_out):
        o_t = jnp.concatenate([self.acc[hd, 0:self.dv, :] / self.acc[hd, self.dv:self.dv + 1, :]
                               for hd in range(self.n_heads)], axis=0)
        o_ref[0] = o_t.T if transpose_out else o_t


def _last_tile_visible(tk, tq, causal, n_valid_last):
    r = lax.broadcasted_iota(jnp.int32, (tk, tq), 0)
    if causal:
        c = lax.broadcasted_iota(jnp.int32, (tk, tq), 1)
        return (r // CHUNK) <= (c // CHUNK)
    return r < n_valid_last


def _mla_kernel(q_ref, k_ref, vt_ref, o_ref, *scratch, tq, tk, n_tiles, causal, n_valid_last, transpose_out):
    last = pl.program_id(1) if causal else n_tiles - 1
    softmax = _Softmax(*scratch)

    def logits(hd, t, kind):
        sl = slice(hd * LANES, (hd + 1) * LANES)
        s = _dot_nt(k_ref[0, _key_rows(t, tk), sl], q_ref[0, :, sl])
        if kind == 2:
            s = s + jnp.where(_last_tile_visible(tk, tq, causal, n_valid_last), 0.0, NEG)
        return s

    softmax.run(last, logits, lambda hd, t: vt_ref[0, t, hd * V_DIM:(hd + 1) * V_DIM, :])
    softmax.finish(o_ref, transpose_out)


def _mla(q, k, vt, *, tq, tk, causal, n_valid_last, transpose_out):
    nb, t, _ = q.shape
    tkeys = k.shape[1]
    n_tiles = tkeys // tk
    if transpose_out:
        out_shape = jax.ShapeDtypeStruct((nb, t, W_A), F32)
        out_spec = pl.BlockSpec((1, tq, W_A), lambda b, i: (b, i, 0))
    else:
        out_shape = jax.ShapeDtypeStruct((nb, W_A, t), F32)
        out_spec = pl.BlockSpec((1, W_A, tq), lambda b, i: (b, 0, i))
    return pl.pallas_call(
        functools.partial(_mla_kernel, tq=tq, tk=tk, n_tiles=n_tiles, causal=causal, n_valid_last=n_valid_last,
                          transpose_out=transpose_out),
        out_shape=out_shape, grid=(nb, t // tq),
        in_specs=[pl.BlockSpec((1, tq, QK_PAD), lambda b, i: (b, i, 0)),
                  pl.BlockSpec((1, tkeys, QK_PAD), lambda b, i: (b, 0, 0)),
                  pl.BlockSpec((1, n_tiles, W_A, tk), lambda b, i: (b, 0, 0, 0))],
        out_specs=out_spec, scratch_shapes=_Softmax.scratch(MLA_HEADS, V_DIM, tk, tq),
        compiler_params=_params(2), name="mla_causal" if causal else "mla_cached",
    )(q, k, vt)


def _dsa_kernel(qi_ref, wit_ref, ki_ref, qb_ref, kb_ref, vbt_ref, bias_ref, o_ref, idx_scr, *scratch,
                tq, tk, n_tiles, causal, n_valid_last, topk, transpose_out):
    i = pl.program_id(1)
    last = i if causal else n_tiles - 1
    n_used = last + 1
    visible = _last_tile_visible(tk, tq, causal, n_valid_last)
    kf = float(topk)

    wit = wit_ref[0]

    def idx_tile(j):
        k_j = ki_ref[0, _key_rows(j, tk), :]
        acc = jnp.zeros((tk, tq), F32)
        for hd in range(IDX_HEADS):
            s = _dot_nt(k_j, qi_ref[0, :, hd * IDX_DIM:(hd + 1) * IDX_DIM])
            acc = acc + jnp.maximum(s, 0.0) * wit[hd:hd + 1, :]
        return acc

    def idx_body(j, carry):
        lo, hi = carry
        t = idx_tile(j)
        idx_scr[j] = t
        return jnp.minimum(lo, jnp.min(t, axis=0, keepdims=True)), jnp.maximum(hi, jnp.max(t, axis=0, keepdims=True))

    lo, hi = lax.fori_loop(0, last, idx_body, (jnp.full((1, tq), jnp.inf, F32), jnp.full((1, tq), -jnp.inf, F32)))
    t_last = idx_tile(last)
    idx_scr[last] = jnp.where(visible, t_last, -jnp.inf)
    lo = jnp.minimum(lo, jnp.min(jnp.where(visible, t_last, jnp.inf), axis=0, keepdims=True))
    hi = jnp.maximum(hi, jnp.max(jnp.where(visible, t_last, -jnp.inf), axis=0, keepdims=True))

    def count(*preds):
        def body(j, cs):
            t = idx_scr[j]
            return tuple(c + _rows8(jnp.where(p(t), 1.0, 0.0), jnp.add) for c, p in zip(cs, preds))
        cs = lax.fori_loop(0, n_used, body, tuple(jnp.zeros((SUBLANES, tq), F32) for _ in preds))
        return tuple(jnp.sum(c, axis=0, keepdims=True) for c in cs)

    lane = lax.broadcasted_iota(jnp.int32, (1, tq), 1)
    n_vis = (last * tk + (lane // CHUNK + 1) * CHUNK if causal else jnp.full((1, tq), last * tk + n_valid_last))
    c_lo = n_vis.astype(F32)
    c_hi, c_z, c_zp = count(lambda t: t >= hi, lambda t: t >= 0.0, lambda t: t > 0.0)
    top_heavy = c_hi >= kf
    zero_up = jnp.logical_not(top_heavy) & (c_z >= kf) & (lo < 0.0)
    zero_dn = jnp.logical_not(top_heavy) & (c_z < kf) & (hi > 0.0)
    zero_tie = jnp.where((c_z >= kf) & (c_zp < kf), 1.0, 0.0)
    lo = jnp.where(top_heavy, hi, jnp.where(zero_up, 0.0, lo))
    c_lo = jnp.where(top_heavy, c_hi, jnp.where(zero_up, c_z, c_lo))
    hi = jnp.where(zero_dn, 0.0, hi)

    def settled(lo, hi, c_lo):
        mid = 0.5 * lo + 0.5 * hi
        return jnp.where((c_lo <= kf) | (mid <= lo) | (mid >= hi), 1.0, 0.0)

    def bis_cond(st):
        _, _, _, done, it = st
        return jnp.logical_and(it < MAX_BISECT, jnp.min(done) < 0.5)

    def bis_body(st):
        lo, hi, c_lo, done, it = st
        for _ in range(BISECT_PER_CHECK):
            mid = 0.5 * lo + 0.5 * hi
            (c,) = count(lambda t: t >= mid)
            live = done < 0.5
            up = live & (c >= kf)
            dn = live & (c < kf)
            lo = jnp.where(up, mid, lo)
            c_lo = jnp.where(up, c, c_lo)
            hi = jnp.where(dn, mid, hi)
            done = jnp.maximum(done, settled(lo, hi, c_lo))
        return lo, hi, c_lo, done, it + BISECT_PER_CHECK

    done = jnp.maximum(settled(lo, hi, c_lo), zero_tie)
    lo, hi, c_lo, _, _ = lax.while_loop(bis_cond, bis_body, (lo, hi, c_lo, done, jnp.int32(0)))

    any_tie = jnp.max(jnp.where(c_lo > kf, 1.0, 0.0)) > 0.5

    @pl.when(jnp.logical_not(any_tie))
    def _():
        def body(j, _):
            idx_scr[j] = jnp.where(idx_scr[j] >= lo, 0.0, NEG)
            return 0
        lax.fori_loop(0, n_used, body, 0)

    @pl.when(any_tie)
    def _():
        need = kf - count(lambda t: t > lo)[0]
        r = lax.broadcasted_iota(jnp.int32, (tk, tk), 0)
        c = lax.broadcasted_iota(jnp.int32, (tk, tk), 1)
        before = jnp.where(c < r, 1.0, 0.0).astype(BF16)

        def body(j, seen):
            t = idx_scr[j]
            eq = jnp.where(t == lo, 1.0, 0.0)
            rank = _dot(before, eq.astype(BF16)) + seen
            take = (t > lo) | ((t == lo) & (rank < need))
            idx_scr[j] = jnp.where(take, 0.0, NEG)
            return seen + jnp.sum(_rows8(eq, jnp.add), axis=0, keepdims=True)
        lax.fori_loop(0, n_used, body, jnp.zeros((1, tq), F32))

    softmax = _Softmax(*scratch)
    kv_lanes = lambda hd: slice((hd // DSA_GROUP) * DSA_HEAD_DIM, (hd // DSA_GROUP + 1) * DSA_HEAD_DIM)

    def logits(hd, t, kind):
        s = _dot_nt(kb_ref[0, _key_rows(t, tk), kv_lanes(hd)], qb_ref[0, :, hd * DSA_HEAD_DIM:(hd + 1) * DSA_HEAD_DIM])
        return s + idx_scr[t] if kind == 0 else s + (idx_scr[t] + bias_ref[hd, kind - 1])

    softmax.run(last, logits, lambda hd, t: vbt_ref[0, t, kv_lanes(hd), :])
    softmax.finish(o_ref, transpose_out)


def _dsa(qi, wit, ki, qb, kb, vbt, bias, *, tq, tk, causal, n_valid_last, topk, transpose_out):
    nb, t, _ = qb.shape
    tkeys = kb.shape[1]
    n_tiles = tkeys // tk
    if transpose_out:
        out_shape = jax.ShapeDtypeStruct((nb, t, W_B), F32)
        out_spec = pl.BlockSpec((1, tq, W_B), lambda b, i: (b, i, 0))
    else:
        out_shape = jax.ShapeDtypeStruct((nb, W_B, t), F32)
        out_spec = pl.BlockSpec((1, W_B, tq), lambda b, i: (b, 0, i))
    return pl.pallas_call(
        functools.partial(_dsa_kernel, tq=tq, tk=tk, n_tiles=n_tiles, causal=causal, n_valid_last=n_valid_last,
                          topk=topk, transpose_out=transpose_out),
        out_shape=out_shape, grid=(nb, t // tq),
        in_specs=[pl.BlockSpec((1, tq, IDX_HEADS * IDX_DIM), lambda b, i: (b, i, 0)),
                  pl.BlockSpec((1, IDX_HEADS, tq), lambda b, i: (b, 0, i)),
                  pl.BlockSpec((1, tkeys, IDX_DIM), lambda b, i: (b, 0, 0)),
                  pl.BlockSpec((1, tq, W_B), lambda b, i: (b, i, 0)),
                  pl.BlockSpec((1, tkeys, KV_DIM), lambda b, i: (b, 0, 0)),
                  pl.BlockSpec((1, n_tiles, KV_DIM, tk), lambda b, i: (b, 0, 0, 0)),
                  pl.BlockSpec(bias.shape, lambda b, i: (0, 0, 0, 0))],
        out_specs=out_spec,
        scratch_shapes=[pltpu.VMEM((n_tiles, tk, tq), F32)] + _Softmax.scratch(DSA_HEADS, DSA_HEAD_DIM, tk, tq),
        compiler_params=_params(2), name="dsa_causal" if causal else "dsa_cached",
    )(qi, wit, ki, qb, kb, vbt, bias)


def _out_kernel(x_ref, shift_ref, scale_ref, gate_ref, gn_ref, oa_ref, ob_ref, wg_ref, woa_ref, wob_ref, wout_ref,
                gf_ref, o_ref, *, final):
    x = x_ref[0]
    h16 = (_rms(x, gn_ref[...]) * (1.0 + scale_ref[0]) + shift_ref[0]).astype(BF16)

    def gcols(a):
        return _dot(h16, wg_ref[:, a[0]:a[1]])

    ga = gcols(G_GA)
    ya = _dot((oa_ref[0] * (ga * _sigmoid(ga))).astype(BF16), woa_ref[...])
    gb = gcols(G_GB)
    yb = _dot((ob_ref[0] * (gb * _sigmoid(gb))).astype(BF16), wob_ref[...])
    m = _sigmoid(gcols(G_MA)) * ya + _sigmoid(gcols(G_MB)) * yb
    y = x + gate_ref[0] * _dot(m.astype(BF16), wout_ref[...])
    o_ref[0] = _rms(y, gf_ref[...]) if final else y


def _out(x, shift, scale, gate, gn, oa, ob, wg, woa, wob, wout, gf, *, tm, final):
    nb, t, _ = x.shape
    r = shift.shape[1]
    rb = 1 if r == 1 else tm
    tok = lambda w: pl.BlockSpec((1, tm, w), lambda b, i: (b, i, 0))
    mod = pl.BlockSpec((1, rb, D_MODEL), (lambda b, i: (b, 0, 0)) if r == 1 else (lambda b, i: (b, i, 0)))
    const = lambda a: pl.BlockSpec(a.shape, lambda b, i: (0,) * a.ndim)
    return pl.pallas_call(
        functools.partial(_out_kernel, final=final),
        out_shape=jax.ShapeDtypeStruct(x.shape, F32), grid=(nb, t // tm),
        in_specs=[tok(D_MODEL), mod, mod, mod, const(gn), tok(W_A), tok(W_B), const(wg), const(woa), const(wob),
                  const(wout), const(gf)],
        out_specs=tok(D_MODEL), compiler_params=_params(2), name="out_final" if final else "out_mix",
    )(x, shift, scale, gate, gn, oa, ob, wg, woa, wob, wout, gf)


def _pack_weights(w_in, w_uq, w_ukv, w_oa, w_ob, w_out):
    depth = w_in.shape[0]
    offs = np.concatenate([[0], np.cumsum(SPLIT_SIZES)])
    cq, ckv, kr, ga, qb, kb, vb, qi, ki, wi, gb, ma, mb = [w_in[:, :, offs[n]:offs[n + 1]] for n in range(13)]
    z = lambda w: jnp.zeros((depth, D_MODEL, w), F32)
    krg = jnp.concatenate([kr, z(NOPE_DIM - ROPE_DIM), kr, z(LANES - NOPE_DIM - ROPE_DIM)], axis=-1)
    sm = jnp.concatenate([ki, wi, z(LANES - IDX_DIM - IDX_HEADS)], axis=-1)
    wa = jnp.concatenate([cq, ckv, krg, sm, qb, kb, vb, qi], axis=-1).astype(BF16)
    wg = jnp.concatenate([ga, gb, ma, mb], axis=-1).astype(BF16)
    qd = NOPE_DIM + ROPE_DIM
    wuq = jnp.pad(w_uq.reshape(depth, Q_LORA, MLA_HEADS, qd), ((0, 0), (0, 0), (0, 0), (0, LANES - qd)))
    wuq = wuq.reshape(depth, Q_LORA, QK_PAD).astype(BF16)
    ukv = w_ukv.reshape(depth, KV_LORA, MLA_HEADS, NOPE_DIM + V_DIM)
    wkn = jnp.pad(ukv[..., :NOPE_DIM], ((0, 0), (0, 0), (0, 0), (0, LANES - NOPE_DIM)))
    wkn = wkn.reshape(depth, KV_LORA, QK_PAD).astype(BF16)
    wv = ukv[..., NOPE_DIM:].reshape(depth, KV_LORA, W_A).astype(BF16)
    return wa, wg, wuq, wkn, wv, w_oa.astype(BF16), w_ob.astype(BF16), w_out.astype(BF16)


def _rope_tables(pos):
    half = ROPE_DIM // 2
    inv = ROPE_THETA ** (-jnp.arange(half, dtype=F32) / half)
    ang = pos.astype(F32)[:, None] * inv
    cos, sin = jnp.cos(ang), jnp.sin(ang)
    n = pos.shape[0]
    z = lambda w: jnp.zeros((n, w), F32)
    pad = LANES - NOPE_DIM - ROPE_DIM
    q_c = jnp.concatenate([jnp.ones((n, NOPE_DIM), F32), cos, cos, z(pad)], axis=-1) * (MLA_SCALE * LOG2E)
    q_s1 = jnp.concatenate([z(NOPE_DIM), -sin, z(half), z(pad)], axis=-1) * (MLA_SCALE * LOG2E)
    q_s2 = jnp.concatenate([z(NOPE_DIM), z(half), sin, z(pad)], axis=-1) * (MLA_SCALE * LOG2E)
    gap = NOPE_DIM - ROPE_DIM
    k_c = jnp.concatenate([cos, cos, z(gap), cos, cos, z(pad)], axis=-1)
    k_s1 = jnp.concatenate([-sin, z(half), z(gap), -sin, z(half), z(pad)], axis=-1)
    k_s2 = jnp.concatenate([z(half), sin, z(gap), z(half), sin, z(pad)], axis=-1)
    return (q_c, q_s1, q_s2), (k_c, k_s1, k_s2)


def _placement():
    p = np.zeros((ROPE_DIM, QK_PAD), np.float32)
    for hd in range(MLA_HEADS):
        p[np.arange(ROPE_DIM), hd * LANES + NOPE_DIM + np.arange(ROPE_DIM)] = 1.0
    return jnp.asarray(p, dtype=BF16)


def kernel(x_prompt, x_sample, c_prompt, c_sample, cache_ckv, cache_krope, cache_kb, cache_vb, cache_kidx, w_ada,
           b_ada, g_norm, w_in, g_qnorm, w_uq, g_kvnorm, w_ukv, w_oa, w_ob, w_out, rel_bias, g_final):
    depth = w_in.shape[0]
    bp, t_p, _ = x_prompt.shape
    bs, t_s, _ = x_sample.shape
    past = cache_ckv.shape[2]
    topk_p = min(TOPK_MAX, t_p // 4)
    topk_s = min(TOPK_MAX, (past + t_s) // 4)
    assert t_p % TM == 0 and t_p % TQ == 0 and past % TK == 0 and t_s <= TK and t_s % 8 == 0
    keys_s = past + TK
    n_s = bs * t_s

    wa, wg, wuq, wkn, wv, woa, wob, wout = _pack_weights(w_in, w_uq, w_ukv, w_oa, w_ob, w_out)
    place = _placement()
    mod = _modulation(jnp.concatenate([c_prompt, c_sample], axis=0), w_ada, b_ada)
    mod = mod.reshape(depth, bp + bs, 3, 1, D_MODEL)
    tabs_p = _rope_tables(jnp.arange(t_p, dtype=jnp.int32))
    tabs_s = _rope_tables(jnp.tile(past + jnp.arange(t_s, dtype=jnp.int32), bs))
    bias_p = _bias_tiles(rel_bias, TK, TQ)
    bias_s = _bias_tiles(rel_bias, TK, t_s)
    row = lambda v: v.reshape(1, -1)

    xp = x_prompt
    xs = x_sample.reshape(1, n_s, D_MODEL)
    outs_p = [[] for _ in range(5)]
    outs_s = [[] for _ in range(5)]
    for l in range(depth):
        final = l == depth - 1
        gn, gq, gkv, gf = row(g_norm[l]), row(g_qnorm[l]), row(g_kvnorm[l]), row(g_final)

        shift, scale, gate = (mod[l, :bp, n] for n in range(3))
        (ckv, kr, kb, vb, ki, q, qb, qi, wit, kcat, vat, kb16, vbt, ki16) = _proj(
            xp, shift, scale, gn, wa[l], gq, wuq[l], gkv, wkn[l], wv[l], *tabs_p, tm=TM, tk=TK, emit_kv=True)
        oa = _mla(q, kcat, vat, tq=TQ, tk=TK, causal=True, n_valid_last=TK, transpose_out=True)
        ob = _dsa(qi, wit, ki16, qb, kb16, vbt, bias_p, tq=TQ, tk=TK, causal=True, n_valid_last=TK, topk=topk_p,
                  transpose_out=True)
        xp = _out(xp, shift, scale, gate, gn, oa, ob, wg[l], woa[l], wob[l], wout[l], gf, tm=TM, final=final)
        for lst, v in zip(outs_p, (ckv, kr, kb, vb, ki)):
            lst.append(v)

        shift, scale, gate = (jnp.repeat(mod[l, bp:, n, 0], t_s, axis=0)[None] for n in range(3))
        (ckv, kr, kb, vb, ki, q, qb, qi, wit) = _proj(
            xs, shift, scale, gn, wa[l], gq, wuq[l], gkv, wkn[l], wv[l], *tabs_s, tm=n_s, tk=TK, emit_kv=False)
        per_seq = lambda a: a.reshape((bs, t_s) + a.shape[2:])
        ckv, kr, kb, vb, ki, q, qb, qi = (per_seq(a) for a in (ckv, kr, kb, vb, ki, q, qb, qi))
        wit = wit.reshape(IDX_HEADS, bs, t_s).transpose(1, 0, 2)

        def with_cache(cache, new):
            pad = jnp.zeros((bs, TK - t_s) + new.shape[2:], new.dtype)
            return jnp.concatenate([cache, new, pad], axis=1)

        kcat, vat = _expand(with_cache(cache_ckv[l], ckv), with_cache(cache_krope[l], kr), wkn[l], wv[l], place,
                            tm=TK, tk=TK)
        oa = _mla(q, kcat, vat, tq=t_s, tk=TK, causal=False, n_valid_last=t_s, transpose_out=False)
        kb_all = with_cache(cache_kb[l].reshape(bs, past, KV_DIM), kb).astype(BF16)
        vb_all = with_cache(cache_vb[l].reshape(bs, past, KV_DIM), vb).astype(BF16)
        ki_all = with_cache(cache_kidx[l], ki).astype(BF16)
        vbt = vb_all.reshape(bs, keys_s // TK, TK, KV_DIM).transpose(0, 1, 3, 2)
        ob = _dsa(qi, wit, ki_all, qb, kb_all, vbt, bias_s, tq=t_s, tk=TK, causal=False, n_valid_last=t_s,
                  topk=topk_s, transpose_out=False)
        flat = lambda a: a.transpose(0, 2, 1).reshape(1, n_s, a.shape[1])
        xs = _out(xs, shift, scale, gate, gn, flat(oa), flat(ob), wg[l], woa[l], wob[l], wout[l], gf, tm=n_s,
                  final=final)
        for lst, v in zip(outs_s, (ckv, kr, kb, vb, ki)):
            lst.append(v)

    def stacked(lists, nb, t):
        ckv, kr, kb, vb, ki = (jnp.stack(v) for v in lists)
        kvh = (depth, nb, t, DSA_KV_HEADS, DSA_HEAD_DIM)
        return ckv, kr, kb.reshape(kvh), vb.reshape(kvh), ki

    return (xp, xs.reshape(bs, t_s, D_MODEL)) + stacked(outs_p, bp, t_p) + stacked(outs_s, bs, t_s)
```

```python
import functools

import numpy as np
import jax
import jax.numpy as jnp
from jax import lax
from jax.experimental import pallas as pl
from jax.experimental.pallas import tpu as pltpu

F32 = jnp.float32
BF16 = jnp.bfloat16

D_MODEL = 1024
CHUNK = 64
EPS = 1e-6
MLA_HEADS = 8
NOPE_DIM = 64
ROPE_DIM = 32
V_DIM = 64
Q_LORA = 384
KV_LORA = 256
ROPE_THETA = 10000.0
MLA_SCALE = (NOPE_DIM + ROPE_DIM) ** -0.5
DSA_HEADS = 8
DSA_KV_HEADS = 2
DSA_GROUP = DSA_HEADS // DSA_KV_HEADS
DSA_HEAD_DIM = 64
DSA_SCALE = DSA_HEAD_DIM ** -0.5
IDX_HEADS = 8
IDX_DIM = 64
IDX_W_SCALE = (IDX_HEADS ** -0.5) * (IDX_DIM ** -0.5)
TOPK_MAX = 256
NUM_BUCKETS = 32
MAX_DISTANCE = 128
W_A = MLA_HEADS * V_DIM
W_B = DSA_HEADS * DSA_HEAD_DIM
SPLIT_SIZES = (Q_LORA, KV_LORA, ROPE_DIM, W_A, W_B, DSA_KV_HEADS * DSA_HEAD_DIM, DSA_KV_HEADS * DSA_HEAD_DIM,
               IDX_HEADS * IDX_DIM, IDX_DIM, IDX_HEADS, W_B, D_MODEL, D_MODEL)

LANES = 128
SUBLANES = 8
VMEM_LIMIT_BYTES = 56 * 1024 * 1024

TK = 256
TM = 512
NEG = -1e30
LOG2E = 1.4426950408889634
MAX_BISECT = 200
BISECT_PER_CHECK = 4

QK_PAD = MLA_HEADS * LANES
KV_DIM = DSA_KV_HEADS * DSA_HEAD_DIM

A_CQ = (0, Q_LORA)
A_CKV = (A_CQ[1], A_CQ[1] + KV_LORA)
A_KR = (A_CKV[1], A_CKV[1] + LANES)
A_SM = (A_KR[1], A_KR[1] + LANES)
A_QB = (A_SM[1], A_SM[1] + W_B)
A_KB = (A_QB[1], A_QB[1] + KV_DIM)
A_VB = (A_KB[1], A_KB[1] + KV_DIM)
A_QI = (A_VB[1], A_VB[1] + IDX_HEADS * IDX_DIM)
A_WIDTH = A_QI[1]
G_GA = (0, W_A)
G_GB = (G_GA[1], G_GA[1] + W_B)
G_MA = (G_GB[1], G_GB[1] + D_MODEL)
G_MB = (G_MA[1], G_MA[1] + D_MODEL)
G_WIDTH = G_MB[1]


def _params(n_grid):
    return pltpu.CompilerParams(dimension_semantics=("arbitrary",) * n_grid, vmem_limit_bytes=VMEM_LIMIT_BYTES)


def _sigmoid(x):
    return 1.0 / (1.0 + jnp.exp(-x))


def _rms(x, g):
    return x * lax.rsqrt(jnp.mean(x * x, axis=-1, keepdims=True) + EPS) * g


def _dot(a, b):
    return jnp.dot(a, b, preferred_element_type=F32)


def _dot_nt(a, b):
    return lax.dot_general(a, b, (((1,), (1,)), ((), ())), preferred_element_type=F32)


def _key_rows(j, tk):
    start = j * tk if isinstance(j, int) else pl.multiple_of(j * tk, tk)
    return pl.ds(start, tk)


def _rows8(x, op, ways=None):
    parts = [x[r:r + SUBLANES] for r in range(0, x.shape[0], SUBLANES)]
    if ways is not None:
        chains = parts[:ways]
        for n, part in enumerate(parts[ways:]):
            chains[n % ways] = op(chains[n % ways], part)
        parts = chains
    while len(parts) > 1:
        parts = [op(parts[n], parts[n + 1]) for n in range(0, len(parts) - 1, 2)] + parts[len(parts) & ~1:]
    return parts[0]


def _mod_kernel(c_ref, w_ref, b_ref, o_ref):
    c = c_ref[...]
    a = c * _sigmoid(c)
    w = w_ref[0]
    a_hi = a.astype(BF16)
    a_lo = (a - a_hi.astype(F32)).astype(BF16)
    w_hi = w.astype(BF16)
    w_lo = (w - w_hi.astype(F32)).astype(BF16)
    o_ref[0] = _dot(a_hi, w_hi) + (_dot(a_hi, w_lo) + _dot(a_lo, w_hi)) + b_ref[0]


def _modulation(c_all, w_ada, b_ada):
    depth = w_ada.shape[0]
    n = c_all.shape[0]
    tn = D_MODEL
    return pl.pallas_call(
        _mod_kernel,
        out_shape=jax.ShapeDtypeStruct((depth, n, 3 * D_MODEL), F32),
        grid=(depth, 3 * D_MODEL // tn),
        in_specs=[
            pl.BlockSpec((n, D_MODEL), lambda l, j: (0, 0)),
            pl.BlockSpec((1, D_MODEL, tn), lambda l, j: (l, 0, j)),
            pl.BlockSpec((1, 1, tn), lambda l, j: (l, 0, j)),
        ],
        out_specs=pl.BlockSpec((1, n, tn), lambda l, j: (l, 0, j)),
        compiler_params=_params(2),
        name="adaln_mod",
    )(c_all, w_ada, b_ada.reshape(depth, 1, 3 * D_MODEL))


def _rel_bucket(rel):
    half = NUM_BUCKETS // 2
    max_exact = half // 2
    n = np.abs(rel)
    large = max_exact + (np.log(np.maximum(n, 1).astype(np.float32) / max_exact)
                         / np.float32(np.log(MAX_DISTANCE / max_exact)) * (half - max_exact)).astype(np.int32)
    large = np.minimum(large, half - 1)
    return np.where(n < max_exact, n, large) + np.where(rel > 0, half, 0)


def _bias_kernel(bucket_ref, rel_ref, o_ref, *, far_bucket):
    h = pl.program_id(0)
    for t in range(2):
        b = bucket_ref[t]
        acc = jnp.zeros(b.shape, F32)
        for i in range(NUM_BUCKETS):
            acc = jnp.where(b == i, rel_ref[i, h], acc)
        o_ref[0, t] = (acc - rel_ref[far_bucket, h]) * LOG2E


def _bias_tiles(rel_bias, tk, tq):
    r = np.arange(tk)[:, None]
    c = np.arange(tq)[None, :]
    far_bucket = int(_rel_bucket(np.array(-tk)))
    assert np.all(_rel_bucket(-tk - np.arange(1, tk + tq)) == far_bucket)
    buckets = jnp.asarray(np.stack([_rel_bucket(r - tk - c), _rel_bucket(r - c)]).astype(np.int32))
    return pl.pallas_call(
        functools.partial(_bias_kernel, far_bucket=far_bucket),
        out_shape=jax.ShapeDtypeStruct((DSA_HEADS, 2, tk, tq), F32),
        grid=(DSA_HEADS,),
        in_specs=[
            pl.BlockSpec((2, tk, tq), lambda h: (0, 0, 0)),
            pl.BlockSpec(memory_space=pltpu.SMEM),
        ],
        out_specs=pl.BlockSpec((1, 2, tk, tq), lambda h: (h, 0, 0, 0)),
        compiler_params=_params(1),
        name="t5_bias_tiles",
    )(buckets, rel_bias)


def _rope_group(g, c_ref, s1_ref, s2_ref):
    return (g * c_ref[...] + pltpu.roll(g, LANES - ROPE_DIM // 2, 1) * s1_ref[...]
            + pltpu.roll(g, ROPE_DIM // 2, 1) * s2_ref[...])


def _proj_kernel(x_ref, shift_ref, scale_ref, gn_ref, wa_ref, gq_ref, wuq_ref, gkv_ref, wkn_ref, wv_ref,
                 qc_ref, qs1_ref, qs2_ref, kc_ref, ks1_ref, ks2_ref,
                 ckv_o, kr_o, kb_o, vb_o, ki_o, q_o, qb_o, qi_o, wit_o, *kv_outs, emit_kv, tk):
    x = x_ref[0]
    h = _rms(x, gn_ref[...]) * (1.0 + scale_ref[0]) + shift_ref[0]
    h16 = h.astype(BF16)

    def zcols(a):
        return _dot(h16, wa_ref[:, a[0]:a[1]])

    cqn = _rms(zcols(A_CQ), gq_ref[...]).astype(BF16)
    q = _dot(cqn, wuq_ref[...])
    for hd in range(MLA_HEADS):
        sl = slice(hd * LANES, (hd + 1) * LANES)
        q_o[0, :, sl] = _rope_group(q[:, sl], qc_ref, qs1_ref, qs2_ref).astype(BF16)

    ckvn = _rms(zcols(A_CKV), gkv_ref[...])
    ckv_o[0] = ckvn
    krg = _rope_group(zcols(A_KR), kc_ref, ks1_ref, ks2_ref)
    kr_o[0] = krg[:, 0:ROPE_DIM]

    sm = zcols(A_SM)
    ki = sm[:, 0:IDX_DIM]
    ki_o[0] = ki
    wit_o[0] = sm.T[IDX_DIM:IDX_DIM + IDX_HEADS, :] * IDX_W_SCALE
    qb_o[0] = (zcols(A_QB) * (DSA_SCALE * LOG2E)).astype(BF16)
    qi_o[0] = zcols(A_QI).astype(BF16)
    kb = zcols(A_KB)
    vb = zcols(A_VB)
    kb_o[0] = kb
    vb_o[0] = vb

    if emit_kv:
        kcat_o, vat_o, kb16_o, vbt_o, ki16_o = kv_outs
        ckv16 = ckvn.astype(BF16)
        kn = _dot(ckv16, wkn_ref[...])
        lane = lax.broadcasted_iota(jnp.int32, krg.shape, 1)
        k_add = jnp.where(lane >= NOPE_DIM, krg, 0.0)
        for hd in range(MLA_HEADS):
            sl = slice(hd * LANES, (hd + 1) * LANES)
            kcat_o[0, :, sl] = (kn[:, sl] + k_add).astype(BF16)
        va = _dot(ckv16, wv_ref[...])
        kb16_o[0] = kb.astype(BF16)
        ki16_o[0] = ki.astype(BF16)
        for c in range(x.shape[0] // tk):
            rows = slice(c * tk, (c + 1) * tk)
            vat_o[0, c] = va[rows, :].T.astype(BF16)
            vbt_o[0, c] = vb[rows, :].T.astype(BF16)


def _proj(x, shift, scale, gn, wa, gq, wuq, gkv, wkn, wv, q_tabs, k_tabs, *, tm, tk, emit_kv):
    nb, t, _ = x.shape
    r = shift.shape[1]
    rb = 1 if r == 1 else tm
    grid = (nb, t // tm)
    tok = lambda w: pl.BlockSpec((1, tm, w), lambda b, i: (b, i, 0))
    mod = pl.BlockSpec((1, rb, D_MODEL), (lambda b, i: (b, 0, 0)) if r == 1 else (lambda b, i: (b, i, 0)))
    const = lambda a: pl.BlockSpec(a.shape, lambda b, i: (0,) * a.ndim)
    tab = pl.BlockSpec((tm, LANES), lambda b, i: (i, 0))
    in_specs = [tok(D_MODEL), mod, mod, const(gn), const(wa), const(gq), const(wuq), const(gkv), const(wkn),
                const(wv)] + [tab] * 6
    out_shape = [
        jax.ShapeDtypeStruct((nb, t, KV_LORA), F32),
        jax.ShapeDtypeStruct((nb, t, ROPE_DIM), F32),
        jax.ShapeDtypeStruct((nb, t, KV_DIM), F32),
        jax.ShapeDtypeStruct((nb, t, KV_DIM), F32),
        jax.ShapeDtypeStruct((nb, t, IDX_DIM), F32),
        jax.ShapeDtypeStruct((nb, t, QK_PAD), BF16),
        jax.ShapeDtypeStruct((nb, t, W_B), BF16),
        jax.ShapeDtypeStruct((nb, t, IDX_HEADS * IDX_DIM), BF16),
        jax.ShapeDtypeStruct((nb, IDX_HEADS, t), F32),
    ]
    out_specs = [tok(KV_LORA), tok(ROPE_DIM), tok(KV_DIM), tok(KV_DIM), tok(IDX_DIM), tok(QK_PAD), tok(W_B),
                 tok(IDX_HEADS * IDX_DIM), pl.BlockSpec((1, IDX_HEADS, tm), lambda b, i: (b, 0, i))]
    if emit_kv:
        nkt = tm // tk
        out_shape += [
            jax.ShapeDtypeStruct((nb, t, QK_PAD), BF16),
            jax.ShapeDtypeStruct((nb, t // tk, W_A, tk), BF16),
            jax.ShapeDtypeStruct((nb, t, KV_DIM), BF16),
            jax.ShapeDtypeStruct((nb, t // tk, KV_DIM, tk), BF16),
            jax.ShapeDtypeStruct((nb, t, IDX_DIM), BF16),
        ]
        out_specs += [tok(QK_PAD), pl.BlockSpec((1, nkt, W_A, tk), lambda b, i: (b, i, 0, 0)), tok(KV_DIM),
                      pl.BlockSpec((1, nkt, KV_DIM, tk), lambda b, i: (b, i, 0, 0)), tok(IDX_DIM)]
    return pl.pallas_call(
        functools.partial(_proj_kernel, emit_kv=emit_kv, tk=tk),
        out_shape=out_shape, grid=grid, in_specs=in_specs, out_specs=out_specs,
        compiler_params=_params(2), name="proj_kv" if emit_kv else "proj_q",
    )(x, shift, scale, gn, wa, gq, wuq, gkv, wkn, wv, *q_tabs, *k_tabs)


class _Softmax:
    ONES_ROWS = 16

    def __init__(self, s_scr, mx_scr, m_scr, acc_scr):
        self.s, self.mx, self.m, self.acc = s_scr, mx_scr, m_scr, acc_scr
        self.n_heads = acc_scr.shape[0]
        self.dv = acc_scr.shape[1] - self.ONES_ROWS

    @classmethod
    def scratch(cls, n_heads, dv, tk, tq):
        stat = pltpu.VMEM((n_heads, SUBLANES, tq), F32)
        return [pltpu.VMEM((n_heads, 2, tk, tq), F32), stat, stat, pltpu.VMEM((n_heads, dv + cls.ONES_ROWS, tq), F32)]

    def reset(self):
        self.mx[...] = jnp.full(self.mx.shape, NEG, F32)
        self.m[...] = jnp.full(self.m.shape, NEG, F32)
        self.acc[...] = jnp.zeros(self.acc.shape, F32)

    def produce(self, hd, slot, s):
        self.s[hd, slot] = s
        self.mx[hd] = jnp.maximum(self.mx[hd], _rows8(s, jnp.maximum, ways=4))

    def consume(self, hd, slot, vt):
        m_old = self.m[hd, 0:1, :]
        m_new = jnp.max(self.mx[hd], axis=0, keepdims=True)
        alpha = jnp.exp2(m_old - m_new)
        p = jnp.exp2(self.s[hd, slot] - m_new)
        self.m[hd] = jnp.broadcast_to(m_new, self.m.shape[1:])
        vt1 = jnp.concatenate([vt, jnp.ones((self.ONES_ROWS, vt.shape[1]), vt.dtype)], axis=0)
        self.acc[hd] = alpha * self.acc[hd] + _dot(vt1, p.astype(BF16))

    def run(self, last, logits, values):
        def step(t_consume, t_produce, kind, slot_produce):
            for hd in range(self.n_heads):
                if t_consume is not None:
                    self.consume(hd, 1 - slot_produce, values(hd, t_consume))
                if t_produce is not None:
                    self.produce(hd, slot_produce, logits(hd, t_produce, kind))

        when = lambda cond, fn: pl.when(cond)(fn)
        self.reset()
        step(None, last, 2, 0)
        when(last >= 1, lambda: step(last, last - 1, 1, 1))
        n_far = last - 1

        def far_pair(n, _):
            t = last - 2 - 2 * n
            step(t + 1, t, 0, 0)
            step(t, t - 1, 0, 1)
            return 0

        lax.fori_loop(0, n_far // 2, far_pair, 0)
        odd = (n_far >= 1) & (n_far % 2 == 1)
        when(odd, lambda: step(1, 0, 0, 0))
        in_slot0 = (last == 0) | odd
        when(in_slot0, lambda: step(0, None, None, 1))
        when(jnp.logical_not(in_slot0), lambda: step(0, None, None, 0))

    def finish(self, o_ref):
        o_t = jnp.concatenate([self.acc[hd, 0:self.dv, :] / self.acc[hd, self.dv:self.dv + 1, :]
                               for hd in range(self.n_heads)], axis=0)
        o_ref[0] = o_t.T


def _diagonal_tile_visible(t):
    r = lax.broadcasted_iota(jnp.int32, (t, t), 0)
    c = lax.broadcasted_iota(jnp.int32, (t, t), 1)
    return (r // CHUNK) <= (c // CHUNK)


def _mla_kernel(q_ref, k_ref, vt_ref, o_ref, *scratch, t):
    softmax = _Softmax(*scratch)

    def logits(hd, j, kind):
        sl = slice(hd * LANES, (hd + 1) * LANES)
        s = _dot_nt(k_ref[0, _key_rows(j, t), sl], q_ref[0, :, sl])
        return s + jnp.where(_diagonal_tile_visible(t), 0.0, NEG) if kind == 2 else s

    softmax.run(pl.program_id(1), logits, lambda hd, j: vt_ref[0, j, hd * V_DIM:(hd + 1) * V_DIM, :])
    softmax.finish(o_ref)


def _mla(q, k, vt, *, t):
    nb, seq, _ = q.shape
    return pl.pallas_call(
        functools.partial(_mla_kernel, t=t),
        out_shape=jax.ShapeDtypeStruct((nb, seq, W_A), F32), grid=(nb, seq // t),
        in_specs=[pl.BlockSpec((1, t, QK_PAD), lambda b, i: (b, i, 0)),
                  pl.BlockSpec((1, seq, QK_PAD), lambda b, i: (b, 0, 0)),
                  pl.BlockSpec((1, seq // t, W_A, t), lambda b, i: (b, 0, 0, 0))],
        out_specs=pl.BlockSpec((1, t, W_A), lambda b, i: (b, i, 0)),
        scratch_shapes=_Softmax.scratch(MLA_HEADS, V_DIM, t, t),
        compiler_params=_params(2), name="mla_causal",
    )(q, k, vt)


def _score_stats_init(tq):
    return (jnp.full((1, tq), jnp.inf, F32), jnp.full((1, tq), -jnp.inf, F32),
            jnp.zeros((SUBLANES, tq), F32), jnp.zeros((SUBLANES, tq), F32))


def _score_stats(stats, t, visible=None):
    lo, hi, c_z, c_zp = stats
    t_min = t if visible is None else jnp.where(visible, t, jnp.inf)
    return (jnp.minimum(lo, jnp.min(t_min, axis=0, keepdims=True)), jnp.maximum(hi, jnp.max(t, axis=0, keepdims=True)),
            c_z + _rows8(jnp.where(t >= 0.0, 1.0, 0.0), jnp.add), c_zp + _rows8(jnp.where(t > 0.0, 1.0, 0.0), jnp.add))


def _select_topk(idx_scr, n_used, stats, n_vis, kf):
    _, tk, tq = idx_scr.shape

    def count(pred):
        def body(j, c):
            return c + _rows8(jnp.where(pred(idx_scr[j]), 1.0, 0.0), jnp.add)
        return jnp.sum(lax.fori_loop(0, n_used, body, jnp.zeros((SUBLANES, tq), F32)), axis=0, keepdims=True)

    lo, hi, c_z, c_zp = stats
    c_z = jnp.sum(c_z, axis=0, keepdims=True)
    c_zp = jnp.sum(c_zp, axis=0, keepdims=True)
    hi = hi + (jnp.abs(hi) * 2.0 ** -20 + 1e-30)
    zero_up = (c_z >= kf) & (lo < 0.0)
    zero_tie = jnp.where((c_z >= kf) & (c_zp < kf), 1.0, 0.0)
    c_lo = jnp.where(zero_up, c_z, n_vis)
    lo = jnp.where(zero_up, 0.0, lo)
    hi = jnp.where((c_z < kf) & (hi > 0.0), 0.0, hi)

    def settled(lo, hi, c_lo):
        mid = 0.5 * lo + 0.5 * hi
        return jnp.where((c_lo <= kf) | (mid <= lo) | (mid >= hi), 1.0, 0.0)

    def bis_cond(st):
        _, _, _, done, it = st
        return jnp.logical_and(it < MAX_BISECT, jnp.min(done) < 0.5)

    def bis_body(st):
        lo, hi, c_lo, done, it = st
        for _ in range(BISECT_PER_CHECK):
            mid = 0.5 * lo + 0.5 * hi
            c = count(lambda t: t >= mid)
            live = done < 0.5
            up = live & (c >= kf)
            dn = live & (c < kf)
            lo = jnp.where(up, mid, lo)
            c_lo = jnp.where(up, c, c_lo)
            hi = jnp.where(dn, mid, hi)
            done = jnp.maximum(done, settled(lo, hi, c_lo))
        return lo, hi, c_lo, done, it + BISECT_PER_CHECK

    done = jnp.maximum(settled(lo, hi, c_lo), zero_tie)
    lo, hi, c_lo, _, _ = lax.while_loop(bis_cond, bis_body, (lo, hi, c_lo, done, jnp.int32(0)))
    any_tie = jnp.max(jnp.where(c_lo > kf, 1.0, 0.0)) > 0.5

    @pl.when(jnp.logical_not(any_tie))
    def _():
        def body(j, _):
            idx_scr[j] = jnp.where(idx_scr[j] >= lo, 0.0, NEG)
            return 0
        lax.fori_loop(0, n_used, body, 0)

    @pl.when(any_tie)
    def _():
        need = kf - count(lambda t: t > lo)
        r = lax.broadcasted_iota(jnp.int32, (tk, tk), 0)
        c = lax.broadcasted_iota(jnp.int32, (tk, tk), 1)
        before = jnp.where(c < r, 1.0, 0.0).astype(BF16)

        def body(j, seen):
            t = idx_scr[j]
            eq = jnp.where(t == lo, 1.0, 0.0)
            rank = _dot(before, eq.astype(BF16)) + seen
            take = (t > lo) | ((t == lo) & (rank < need))
            idx_scr[j] = jnp.where(take, 0.0, NEG)
            return seen + jnp.sum(_rows8(eq, jnp.add), axis=0, keepdims=True)
        lax.fori_loop(0, n_used, body, jnp.zeros((1, tq), F32))


def _dsa_kernel(qi_ref, wit_ref, ki_ref, qb_ref, kb_ref, vbt_ref, bias_ref, o_ref, idx_scr, *scratch, t, topk):
    tq = tk = t
    last = pl.program_id(1)
    n_used = last + 1
    visible = _diagonal_tile_visible(t)

    wit = wit_ref[0]

    def idx_tile(j):
        k_j = ki_ref[0, _key_rows(j, tk), :]
        acc = jnp.zeros((tk, tq), F32)
        for hd in range(IDX_HEADS):
            s = _dot_nt(k_j, qi_ref[0, :, hd * IDX_DIM:(hd + 1) * IDX_DIM])
            acc = acc + jnp.maximum(s, 0.0) * wit[hd:hd + 1, :]
        return acc

    def idx_body(j, stats):
        t = idx_tile(j)
        idx_scr[j] = t
        return _score_stats(stats, t)

    stats = lax.fori_loop(0, last, idx_body, _score_stats_init(tq))
    t_last = jnp.where(visible, idx_tile(last), -jnp.inf)
    idx_scr[last] = t_last
    stats = _score_stats(stats, t_last, visible)
    lane = lax.broadcasted_iota(jnp.int32, (1, tq), 1)
    n_vis = last * tk + (lane // CHUNK + 1) * CHUNK
    _select_topk(idx_scr, n_used, stats, n_vis.astype(F32), float(topk))

    softmax = _Softmax(*scratch)
    kv_lanes = lambda hd: slice((hd // DSA_GROUP) * DSA_HEAD_DIM, (hd // DSA_GROUP + 1) * DSA_HEAD_DIM)

    def logits(hd, t, kind):
        s = _dot_nt(kb_ref[0, _key_rows(t, tk), kv_lanes(hd)], qb_ref[0, :, hd * DSA_HEAD_DIM:(hd + 1) * DSA_HEAD_DIM])
        return s + idx_scr[t] if kind == 0 else s + (idx_scr[t] + bias_ref[hd, kind - 1])

    softmax.run(last, logits, lambda hd, t: vbt_ref[0, t, kv_lanes(hd), :])
    softmax.finish(o_ref)


def _dsa(qi, wit, ki, qb, kb, vbt, bias, *, t, topk):
    nb, seq, _ = qb.shape
    n_tiles = seq // t
    return pl.pallas_call(
        functools.partial(_dsa_kernel, t=t, topk=topk),
        out_shape=jax.ShapeDtypeStruct((nb, seq, W_B), F32), grid=(nb, n_tiles),
        in_specs=[pl.BlockSpec((1, t, IDX_HEADS * IDX_DIM), lambda b, i: (b, i, 0)),
                  pl.BlockSpec((1, IDX_HEADS, t), lambda b, i: (b, 0, i)),
                  pl.BlockSpec((1, seq, IDX_DIM), lambda b, i: (b, 0, 0)),
                  pl.BlockSpec((1, t, W_B), lambda b, i: (b, i, 0)),
                  pl.BlockSpec((1, seq, KV_DIM), lambda b, i: (b, 0, 0)),
                  pl.BlockSpec((1, n_tiles, KV_DIM, t), lambda b, i: (b, 0, 0, 0)),
                  pl.BlockSpec(bias.shape, lambda b, i: (0, 0, 0, 0))],
        out_specs=pl.BlockSpec((1, t, W_B), lambda b, i: (b, i, 0)),
        scratch_shapes=[pltpu.VMEM((n_tiles, t, t), F32)] + _Softmax.scratch(DSA_HEADS, DSA_HEAD_DIM, t, t),
        compiler_params=_params(2), name="dsa_causal",
    )(qi, wit, ki, qb, kb, vbt, bias)


def _two_pass_softmax(n_tiles, logits, values, s_scr, dv):
    mx = None
    for t in range(n_tiles):
        s = logits(t)
        s_scr[t] = s
        part = _rows8(s, jnp.maximum, ways=4)
        mx = part if mx is None else jnp.maximum(mx, part)
    m = jnp.max(mx, axis=0, keepdims=True)
    acc = None
    for t in range(n_tiles):
        p = jnp.exp2(s_scr[t] - m).astype(BF16)
        vt = values(t)
        part = _dot(jnp.concatenate([vt, jnp.ones((_Softmax.ONES_ROWS, vt.shape[1]), BF16)], axis=0), p)
        acc = part if acc is None else acc + part
    return acc[0:dv] / acc[dv:dv + 1]


def _mla_cached_kernel(q_ref, ckv_c_ref, kr_c_ref, ckv_n_ref, kr_n_ref, wuk_ref, wuv_ref, sel_ref, o_ref, s_scr, *,
                       ts, tk, n_cache, n_valid_last):
    lanes = MLA_HEADS * ts
    q_lat, q_rope = [], []
    for hd in range(MLA_HEADS):
        qg = q_ref[0, :, hd * LANES:(hd + 1) * LANES]
        q_lat.append(_dot(qg, wuk_ref[hd]).astype(BF16))
        q_rope.append(_dot(qg, sel_ref[...]).astype(BF16))
    q_lat = jnp.concatenate(q_lat, axis=0)
    q_rope = jnp.concatenate(q_rope, axis=0)

    def latent(t):
        return ckv_c_ref[0, t * tk:(t + 1) * tk, :] if t < n_cache else ckv_n_ref[0]

    def logits(t):
        kr = kr_c_ref[0, t * tk:(t + 1) * tk, :] if t < n_cache else kr_n_ref[0]
        s = _dot_nt(latent(t).astype(BF16), q_lat) + _dot_nt(kr.astype(BF16), q_rope)
        if t == n_cache:
            s = s + jnp.where(lax.broadcasted_iota(jnp.int32, s.shape, 0) < n_valid_last, 0.0, NEG)
        return s

    o_lat = _two_pass_softmax(n_cache + 1, logits, lambda t: latent(t).T.astype(BF16), s_scr, KV_LORA)
    o_lat = o_lat.astype(BF16)
    head_of_lane = lax.broadcasted_iota(jnp.int32, (V_DIM, lanes), 1) // ts
    out = jnp.zeros((V_DIM, lanes), F32)
    for hd in range(MLA_HEADS):
        out = jnp.where(head_of_lane == hd, _dot(wuv_ref[hd], o_lat), out)
    o_ref[0] = out


def _mla_cached(q, ckv_cache, kr_cache, ckv_new, kr_new, wuk, wuv, sel, *, tk, n_valid_last):
    nb, ts, _ = q.shape
    past = ckv_cache.shape[1]
    lanes = MLA_HEADS * ts
    per_b = lambda a: pl.BlockSpec((1,) + a.shape[1:], lambda b: (b,) + (0,) * (a.ndim - 1))
    const = lambda a: pl.BlockSpec(a.shape, lambda b: (0,) * a.ndim)
    return pl.pallas_call(
        functools.partial(_mla_cached_kernel, ts=ts, tk=tk, n_cache=past // tk, n_valid_last=n_valid_last),
        out_shape=jax.ShapeDtypeStruct((nb, V_DIM, lanes), F32), grid=(nb,),
        in_specs=[per_b(q), per_b(ckv_cache), per_b(kr_cache), per_b(ckv_new), per_b(kr_new), const(wuk), const(wuv),
                  const(sel)],
        out_specs=pl.BlockSpec((1, V_DIM, lanes), lambda b: (b, 0, 0)),
        scratch_shapes=[pltpu.VMEM((past // tk + 1, tk, lanes), F32)],
        compiler_params=_params(1), name="mla_cached",
    )(q, ckv_cache, kr_cache, ckv_new, kr_new, wuk, wuv, sel)


def _dsa_cached_kernel(qi_ref, w_ref, ki_c_ref, ki_n_ref, qb_ref, kb_c_ref, kb_n_ref, vb_c_ref, vb_n_ref, bias_ref,
                       o_ref, idx_scr, s_scr, *, ts, tk, n_cache, n_valid_last, topk):
    n_tiles = n_cache + 1
    lanes = DSA_GROUP * ts
    tile = lambda c_ref, n_ref, t: c_ref[0, t * tk:(t + 1) * tk, :] if t < n_cache else n_ref[0]
    visible = lax.broadcasted_iota(jnp.int32, (tk, lanes), 0) < n_valid_last

    stats = _score_stats_init(lanes)
    for t in range(n_tiles):
        s = _dot_nt(tile(ki_c_ref, ki_n_ref, t).astype(BF16), qi_ref[0])
        r = jnp.maximum(s, 0.0) * w_ref[0]
        x = r[:, 0:lanes]
        for blk in range(1, IDX_HEADS * ts // lanes):
            x = x + r[:, blk * lanes:(blk + 1) * lanes]
        shift = lanes // 2
        while shift >= ts:
            x = x + pltpu.roll(x, shift, 1)
            shift //= 2
        if t == n_cache:
            x = jnp.where(visible, x, -jnp.inf)
        idx_scr[t] = x
        stats = _score_stats(stats, x, visible if t == n_cache else None)
    n_vis = jnp.full((1, lanes), float(n_cache * tk + n_valid_last), F32)
    _select_topk(idx_scr, n_tiles, stats, n_vis, float(topk))

    for g in range(DSA_KV_HEADS):
        def logits(t, g=g):
            s = _dot_nt(tile(kb_c_ref, kb_n_ref, t).astype(BF16), qb_ref[0, g]) + idx_scr[t]
            return s + bias_ref[g, t - (n_tiles - 2)] if t >= n_tiles - 2 else s

        def values(t, g=g):
            return tile(vb_c_ref, vb_n_ref, t).T[g * DSA_HEAD_DIM:(g + 1) * DSA_HEAD_DIM, :].astype(BF16)

        o_ref[0, g] = _two_pass_softmax(n_tiles, logits, values, s_scr, DSA_HEAD_DIM)


def _dsa_cached(qi, w, ki_cache, ki_new, qb, kb_cache, kb_new, vb_cache, vb_new, bias, *, ts, tk, n_valid_last,
                topk):
    nb = qi.shape[0]
    past = ki_cache.shape[1]
    n_tiles = past // tk + 1
    lanes = DSA_GROUP * ts
    per_b = lambda a: pl.BlockSpec((1,) + a.shape[1:], lambda b: (b,) + (0,) * (a.ndim - 1))
    return pl.pallas_call(
        functools.partial(_dsa_cached_kernel, ts=ts, tk=tk, n_cache=past // tk, n_valid_last=n_valid_last, topk=topk),
        out_shape=jax.ShapeDtypeStruct((nb, DSA_KV_HEADS, DSA_HEAD_DIM, lanes), F32), grid=(nb,),
        in_specs=[per_b(qi), per_b(w), per_b(ki_cache), per_b(ki_new), per_b(qb), per_b(kb_cache), per_b(kb_new),
                  per_b(vb_cache), per_b(vb_new), pl.BlockSpec(bias.shape, lambda b: (0,) * bias.ndim)],
        out_specs=pl.BlockSpec((1, DSA_KV_HEADS, DSA_HEAD_DIM, lanes), lambda b: (b, 0, 0, 0)),
        scratch_shapes=[pltpu.VMEM((n_tiles, tk, lanes), F32), pltpu.VMEM((n_tiles, tk, lanes), F32)],
        compiler_params=_params(1), name="dsa_cached",
    )(qi, w, ki_cache, ki_new, qb, kb_cache, kb_new, vb_cache, vb_new, bias)


def _out_kernel(x_ref, shift_ref, scale_ref, gate_ref, gn_ref, oa_ref, ob_ref, wg_ref, woa_ref, wob_ref, wout_ref,
                gf_ref, o_ref, *, final):
    x = x_ref[0]
    h16 = (_rms(x, gn_ref[...]) * (1.0 + scale_ref[0]) + shift_ref[0]).astype(BF16)

    def gcols(a):
        return _dot(h16, wg_ref[:, a[0]:a[1]])

    ga = gcols(G_GA)
    ya = _dot((oa_ref[0] * (ga * _sigmoid(ga))).astype(BF16), woa_ref[...])
    gb = gcols(G_GB)
    yb = _dot((ob_ref[0] * (gb * _sigmoid(gb))).astype(BF16), wob_ref[...])
    m = _sigmoid(gcols(G_MA)) * ya + _sigmoid(gcols(G_MB)) * yb
    y = x + gate_ref[0] * _dot(m.astype(BF16), wout_ref[...])
    o_ref[0] = _rms(y, gf_ref[...]) if final else y


def _out(x, shift, scale, gate, gn, oa, ob, wg, woa, wob, wout, gf, *, tm, final):
    nb, t, _ = x.shape
    r = shift.shape[1]
    rb = 1 if r == 1 else tm
    tok = lambda w: pl.BlockSpec((1, tm, w), lambda b, i: (b, i, 0))
    mod = pl.BlockSpec((1, rb, D_MODEL), (lambda b, i: (b, 0, 0)) if r == 1 else (lambda b, i: (b, i, 0)))
    const = lambda a: pl.BlockSpec(a.shape, lambda b, i: (0,) * a.ndim)
    return pl.pallas_call(
        functools.partial(_out_kernel, final=final),
        out_shape=jax.ShapeDtypeStruct(x.shape, F32), grid=(nb, t // tm),
        in_specs=[tok(D_MODEL), mod, mod, mod, const(gn), tok(W_A), tok(W_B), const(wg), const(woa), const(wob),
                  const(wout), const(gf)],
        out_specs=tok(D_MODEL), compiler_params=_params(2), name="out_final" if final else "out_mix",
    )(x, shift, scale, gate, gn, oa, ob, wg, woa, wob, wout, gf)


def _pack_weights(w_in, w_uq, w_ukv, w_oa, w_ob, w_out):
    depth = w_in.shape[0]
    offs = np.concatenate([[0], np.cumsum(SPLIT_SIZES)])
    cq, ckv, kr, ga, qb, kb, vb, qi, ki, wi, gb, ma, mb = [w_in[:, :, offs[n]:offs[n + 1]] for n in range(13)]
    z = lambda w: jnp.zeros((depth, D_MODEL, w), F32)
    krg = jnp.concatenate([kr, z(NOPE_DIM - ROPE_DIM), kr, z(LANES - NOPE_DIM - ROPE_DIM)], axis=-1)
    sm = jnp.concatenate([ki, wi, z(LANES - IDX_DIM - IDX_HEADS)], axis=-1)
    wa = jnp.concatenate([cq, ckv, krg, sm, qb, kb, vb, qi], axis=-1).astype(BF16)
    wg = jnp.concatenate([ga, gb, ma, mb], axis=-1).astype(BF16)
    qd = NOPE_DIM + ROPE_DIM
    wuq = jnp.pad(w_uq.reshape(depth, Q_LORA, MLA_HEADS, qd), ((0, 0), (0, 0), (0, 0), (0, LANES - qd)))
    wuq = wuq.reshape(depth, Q_LORA, QK_PAD).astype(BF16)
    ukv = w_ukv.reshape(depth, KV_LORA, MLA_HEADS, NOPE_DIM + V_DIM)
    wkn = jnp.pad(ukv[..., :NOPE_DIM], ((0, 0), (0, 0), (0, 0), (0, LANES - NOPE_DIM)))
    wkn = wkn.reshape(depth, KV_LORA, QK_PAD).astype(BF16)
    wv = ukv[..., NOPE_DIM:].reshape(depth, KV_LORA, W_A).astype(BF16)
    wuk = jnp.pad(ukv[..., :NOPE_DIM].transpose(0, 2, 3, 1), ((0, 0), (0, 0), (0, LANES - NOPE_DIM), (0, 0)))
    wuv = ukv[..., NOPE_DIM:].transpose(0, 2, 3, 1)
    return (wa, wg, wuq, wkn, wv, w_oa.astype(BF16), w_ob.astype(BF16), w_out.astype(BF16), wuk.astype(BF16),
            wuv.astype(BF16))


def _rope_tables(pos):
    half = ROPE_DIM // 2
    inv = ROPE_THETA ** (-jnp.arange(half, dtype=F32) / half)
    ang = pos.astype(F32)[:, None] * inv
    cos, sin = jnp.cos(ang), jnp.sin(ang)
    n = pos.shape[0]
    z = lambda w: jnp.zeros((n, w), F32)
    pad = LANES - NOPE_DIM - ROPE_DIM
    q_c = jnp.concatenate([jnp.ones((n, NOPE_DIM), F32), cos, cos, z(pad)], axis=-1) * (MLA_SCALE * LOG2E)
    q_s1 = jnp.concatenate([z(NOPE_DIM), -sin, z(half), z(pad)], axis=-1) * (MLA_SCALE * LOG2E)
    q_s2 = jnp.concatenate([z(NOPE_DIM), z(half), sin, z(pad)], axis=-1) * (MLA_SCALE * LOG2E)
    gap = NOPE_DIM - ROPE_DIM
    k_c = jnp.concatenate([cos, cos, z(gap), cos, cos, z(pad)], axis=-1)
    k_s1 = jnp.concatenate([-sin, z(half), z(gap), -sin, z(half), z(pad)], axis=-1)
    k_s2 = jnp.concatenate([z(half), sin, z(gap), z(half), sin, z(pad)], axis=-1)
    return (q_c, q_s1, q_s2), (k_c, k_s1, k_s2)


def _rope_selector():
    p = np.zeros((LANES, ROPE_DIM), np.float32)
    p[NOPE_DIM + np.arange(ROPE_DIM), np.arange(ROPE_DIM)] = 1.0
    return jnp.asarray(p, dtype=BF16)


def kernel(x_prompt, x_sample, c_prompt, c_sample, cache_ckv, cache_krope, cache_kb, cache_vb, cache_kidx, w_ada,
           b_ada, g_norm, w_in, g_qnorm, w_uq, g_kvnorm, w_ukv, w_oa, w_ob, w_out, rel_bias, g_final):
    depth = w_in.shape[0]
    bp, t_p, _ = x_prompt.shape
    bs, t_s, _ = x_sample.shape
    past = cache_ckv.shape[2]
    topk_p = min(TOPK_MAX, t_p // 4)
    topk_s = min(TOPK_MAX, (past + t_s) // 4)
    assert t_p % TM == 0 and TM % TK == 0 and past % TK == 0 and t_s <= TK and t_s % 8 == 0
    assert (DSA_GROUP * t_s) % LANES == 0 and (DSA_GROUP * t_s) & (DSA_GROUP * t_s - 1) == 0
    n_s = bs * t_s

    wa, wg, wuq, wkn, wv, woa, wob, wout, wuk, wuv = _pack_weights(w_in, w_uq, w_ukv, w_oa, w_ob, w_out)
    rope_sel = _rope_selector()
    mod = _modulation(jnp.concatenate([c_prompt, c_sample], axis=0), w_ada, b_ada)
    mod = mod.reshape(depth, bp + bs, 3, 1, D_MODEL)
    tabs_p = _rope_tables(jnp.arange(t_p, dtype=jnp.int32))
    tabs_s = _rope_tables(jnp.tile(past + jnp.arange(t_s, dtype=jnp.int32), bs))
    bias_p = _bias_tiles(rel_bias, TK, TK)
    bias_s = _bias_tiles(rel_bias, TK, t_s).reshape(DSA_KV_HEADS, DSA_GROUP, 2, TK, t_s)
    bias_s = bias_s.transpose(0, 2, 3, 1, 4).reshape(DSA_KV_HEADS, 2, TK, DSA_GROUP * t_s)
    row = lambda v: v.reshape(1, -1)

    xp = x_prompt
    xs = x_sample.reshape(1, n_s, D_MODEL)
    outs_p = [[] for _ in range(5)]
    outs_s = [[] for _ in range(5)]
    for l in range(depth):
        final = l == depth - 1
        gn, gq, gkv, gf = row(g_norm[l]), row(g_qnorm[l]), row(g_kvnorm[l]), row(g_final)

        shift, scale, gate = (mod[l, :bp, n] for n in range(3))
        (ckv, kr, kb, vb, ki, q, qb, qi, wit, kcat, vat, kb16, vbt, ki16) = _proj(
            xp, shift, scale, gn, wa[l], gq, wuq[l], gkv, wkn[l], wv[l], *tabs_p, tm=TM, tk=TK, emit_kv=True)
        oa = _mla(q, kcat, vat, t=TK)
        ob = _dsa(qi, wit, ki16, qb, kb16, vbt, bias_p, t=TK, topk=topk_p)
        xp = _out(xp, shift, scale, gate, gn, oa, ob, wg[l], woa[l], wob[l], wout[l], gf, tm=TM, final=final)
        for lst, v in zip(outs_p, (ckv, kr, kb, vb, ki)):
            lst.append(v)

        shift, scale, gate = (jnp.repeat(mod[l, bp:, n, 0], t_s, axis=0)[None] for n in range(3))
        (ckv, kr, kb, vb, ki, q, qb, qi, wit) = _proj(
            xs, shift, scale, gn, wa[l], gq, wuq[l], gkv, wkn[l], wv[l], *tabs_s, tm=n_s, tk=TK, emit_kv=False)
        per_seq = lambda a: a.reshape((bs, t_s) + a.shape[2:])
        ckv, kr, kb, vb, ki, q, qb, qi = (per_seq(a) for a in (ckv, kr, kb, vb, ki, q, qb, qi))
        new_tile = lambda a: jnp.pad(a, ((0, 0), (0, TK - t_s), (0, 0)))
        oa = _mla_cached(q, cache_ckv[l], cache_krope[l], new_tile(ckv), new_tile(kr), wuk[l], wuv[l], rope_sel,
                         tk=TK, n_valid_last=t_s)
        oa = oa.reshape(bs, V_DIM, MLA_HEADS, t_s).transpose(0, 3, 2, 1).reshape(1, n_s, W_A)
        qi_l = qi.reshape(bs, t_s, IDX_HEADS, IDX_DIM).transpose(0, 2, 1, 3).reshape(bs, IDX_HEADS * t_s, IDX_DIM)
        w_l = wit.reshape(IDX_HEADS, bs, t_s).transpose(1, 0, 2).reshape(bs, 1, IDX_HEADS * t_s)
        qb_l = qb.reshape(bs, t_s, DSA_KV_HEADS, DSA_GROUP, DSA_HEAD_DIM).transpose(0, 2, 3, 1, 4)
        qb_l = qb_l.reshape(bs, DSA_KV_HEADS, DSA_GROUP * t_s, DSA_HEAD_DIM)
        qb_l = jnp.stack([jnp.pad(qb_l[:, g], ((0, 0), (0, 0), (g * DSA_HEAD_DIM, KV_DIM - (g + 1) * DSA_HEAD_DIM)))
                          for g in range(DSA_KV_HEADS)], axis=1)
        ob = _dsa_cached(qi_l, w_l, cache_kidx[l], new_tile(ki), qb_l, cache_kb[l].reshape(bs, past, KV_DIM),
                         new_tile(kb), cache_vb[l].reshape(bs, past, KV_DIM), new_tile(vb), bias_s, ts=t_s, tk=TK,
                         n_valid_last=t_s, topk=topk_s)
        ob = ob.reshape(bs, DSA_KV_HEADS, DSA_HEAD_DIM, DSA_GROUP, t_s).transpose(0, 4, 1, 3, 2).reshape(1, n_s, W_B)
        xs = _out(xs, shift, scale, gate, gn, oa, ob, wg[l], woa[l], wob[l], wout[l], gf, tm=n_s, final=final)
        for lst, v in zip(outs_s, (ckv, kr, kb, vb, ki)):
            lst.append(v)

    def stacked(lists, nb, t):
        ckv, kr, kb, vb, ki = (jnp.stack(v) for v in lists)
        kvh = (depth, nb, t, DSA_KV_HEADS, DSA_HEAD_DIM)
        return ckv, kr, kb.reshape(kvh), vb.reshape(kvh), ki

    return (xp, xs.reshape(bs, t_s, D_MODEL)) + stacked(outs_p, bp, t_p) + stacked(outs_s, bs, t_s)
```

```python
import functools

import numpy as np
import jax
import jax.numpy as jnp
from jax import lax
from jax.experimental import pallas as pl
from jax.experimental.pallas import tpu as pltpu

F32 = jnp.float32
BF16 = jnp.bfloat16

D_MODEL = 1024
CHUNK = 64
EPS = 1e-6
MLA_HEADS = 8
NOPE_DIM = 64
ROPE_DIM = 32
V_DIM = 64
Q_LORA = 384
KV_LORA = 256
ROPE_THETA = 10000.0
MLA_SCALE = (NOPE_DIM + ROPE_DIM) ** -0.5
DSA_HEADS = 8
DSA_KV_HEADS = 2
DSA_GROUP = DSA_HEADS // DSA_KV_HEADS
DSA_HEAD_DIM = 64
DSA_SCALE = DSA_HEAD_DIM ** -0.5
IDX_HEADS = 8
IDX_DIM = 64
IDX_W_SCALE = (IDX_HEADS ** -0.5) * (IDX_DIM ** -0.5)
TOPK_MAX = 256
NUM_BUCKETS = 32
MAX_DISTANCE = 128
W_A = MLA_HEADS * V_DIM
W_B = DSA_HEADS * DSA_HEAD_DIM
SPLIT_SIZES = (Q_LORA, KV_LORA, ROPE_DIM, W_A, W_B, DSA_KV_HEADS * DSA_HEAD_DIM, DSA_KV_HEADS * DSA_HEAD_DIM,
               IDX_HEADS * IDX_DIM, IDX_DIM, IDX_HEADS, W_B, D_MODEL, D_MODEL)

LANES = 128
SUBLANES = 8
VMEM_LIMIT_BYTES = 56 * 1024 * 1024

TK = 256
TM = 512
NEG = -1e30
LOG2E = 1.4426950408889634
MAX_BISECT = 200
BISECT_PER_CHECK = 4

QK_PAD = MLA_HEADS * LANES
KV_DIM = DSA_KV_HEADS * DSA_HEAD_DIM

A_CQ = (0, Q_LORA)
A_CKV = (A_CQ[1], A_CQ[1] + KV_LORA)
A_KR = (A_CKV[1], A_CKV[1] + LANES)
A_SM = (A_KR[1], A_KR[1] + LANES)
A_QB = (A_SM[1], A_SM[1] + W_B)
A_KB = (A_QB[1], A_QB[1] + KV_DIM)
A_VB = (A_KB[1], A_KB[1] + KV_DIM)
A_QI = (A_VB[1], A_VB[1] + IDX_HEADS * IDX_DIM)
A_WIDTH = A_QI[1]
G_GA = (0, W_A)
G_GB = (G_GA[1], G_GA[1] + W_B)
G_MA = (G_GB[1], G_GB[1] + D_MODEL)
G_MB = (G_MA[1], G_MA[1] + D_MODEL)
G_WIDTH = G_MB[1]


def _params(n_grid):
    return pltpu.CompilerParams(dimension_semantics=("arbitrary",) * n_grid, vmem_limit_bytes=VMEM_LIMIT_BYTES)


def _sigmoid(x):
    return 1.0 / (1.0 + jnp.exp(-x))


def _rms(x, g):
    return x * lax.rsqrt(jnp.mean(x * x, axis=-1, keepdims=True) + EPS) * g


def _dot(a, b):
    return jnp.dot(a, b, preferred_element_type=F32)


def _dot_nt(a, b):
    return lax.dot_general(a, b, (((1,), (1,)), ((), ())), preferred_element_type=F32)


def _key_rows(j, tk):
    start = j * tk if isinstance(j, int) else pl.multiple_of(j * tk, tk)
    return pl.ds(start, tk)


def _rows8(x, op, ways=None):
    parts = [x[r:r + SUBLANES] for r in range(0, x.shape[0], SUBLANES)]
    if ways is not None:
        chains = parts[:ways]
        for n, part in enumerate(parts[ways:]):
            chains[n % ways] = op(chains[n % ways], part)
        parts = chains
    while len(parts) > 1:
        parts = [op(parts[n], parts[n + 1]) for n in range(0, len(parts) - 1, 2)] + parts[len(parts) & ~1:]
    return parts[0]


def _mod_kernel(c_ref, w_ref, b_ref, o_ref):
    c = c_ref[...]
    a = c * _sigmoid(c)
    w = w_ref[0]
    a_hi = a.astype(BF16)
    a_lo = (a - a_hi.astype(F32)).astype(BF16)
    w_hi = w.astype(BF16)
    w_lo = (w - w_hi.astype(F32)).astype(BF16)
    o_ref[0] = _dot(a_hi, w_hi) + (_dot(a_hi, w_lo) + _dot(a_lo, w_hi)) + b_ref[0]


def _modulation(c_all, w_ada, b_ada):
    depth = w_ada.shape[0]
    n = c_all.shape[0]
    tn = D_MODEL
    return pl.pallas_call(
        _mod_kernel,
        out_shape=jax.ShapeDtypeStruct((depth, n, 3 * D_MODEL), F32),
        grid=(depth, 3 * D_MODEL // tn),
        in_specs=[
            pl.BlockSpec((n, D_MODEL), lambda l, j: (0, 0)),
            pl.BlockSpec((1, D_MODEL, tn), lambda l, j: (l, 0, j)),
            pl.BlockSpec((1, 1, tn), lambda l, j: (l, 0, j)),
        ],
        out_specs=pl.BlockSpec((1, n, tn), lambda l, j: (l, 0, j)),
        compiler_params=_params(2),
        name="adaln_mod",
    )(c_all, w_ada, b_ada.reshape(depth, 1, 3 * D_MODEL))


def _rel_bucket(rel):
    half = NUM_BUCKETS // 2
    max_exact = half // 2
    n = np.abs(rel)
    large = max_exact + (np.log(np.maximum(n, 1).astype(np.float32) / max_exact)
                         / np.float32(np.log(MAX_DISTANCE / max_exact)) * (half - max_exact)).astype(np.int32)
    large = np.minimum(large, half - 1)
    return np.where(n < max_exact, n, large) + np.where(rel > 0, half, 0)


def _bias_kernel(bucket_ref, rel_ref, o_ref, *, far_bucket):
    h = pl.program_id(0)
    for t in range(2):
        b = bucket_ref[t]
        acc = jnp.zeros(b.shape, F32)
        for i in range(NUM_BUCKETS):
            acc = jnp.where(b == i, rel_ref[i, h], acc)
        o_ref[0, t] = (acc - rel_ref[far_bucket, h]) * LOG2E


def _bias_tiles(rel_bias, tk, tq):
    r = np.arange(tk)[:, None]
    c = np.arange(tq)[None, :]
    far_bucket = int(_rel_bucket(np.array(-tk)))
    assert np.all(_rel_bucket(-tk - np.arange(1, tk + tq)) == far_bucket)
    buckets = jnp.asarray(np.stack([_rel_bucket(r - tk - c), _rel_bucket(r - c)]).astype(np.int32))
    return pl.pallas_call(
        functools.partial(_bias_kernel, far_bucket=far_bucket),
        out_shape=jax.ShapeDtypeStruct((DSA_HEADS, 2, tk, tq), F32),
        grid=(DSA_HEADS,),
        in_specs=[
            pl.BlockSpec((2, tk, tq), lambda h: (0, 0, 0)),
            pl.BlockSpec(memory_space=pltpu.SMEM),
        ],
        out_specs=pl.BlockSpec((1, 2, tk, tq), lambda h: (h, 0, 0, 0)),
        compiler_params=_params(1),
        name="t5_bias_tiles",
    )(buckets, rel_bias)


def _rope_group(g, c_ref, s1_ref, s2_ref):
    return (g * c_ref[...] + pltpu.roll(g, LANES - ROPE_DIM // 2, 1) * s1_ref[...]
            + pltpu.roll(g, ROPE_DIM // 2, 1) * s2_ref[...])


def _proj_kernel(x_ref, shift_ref, scale_ref, gn_ref, wa_ref, gq_ref, wuq_ref, gkv_ref, wkn_ref, wv_ref,
                 qc_ref, qs1_ref, qs2_ref, kc_ref, ks1_ref, ks2_ref, *rest, emit_kv, tk, n_aliased):
    ckv_o, kr_o, kb_o, vb_o, ki_o, q_o, qb_o, qi_o, wit_o, *kv_outs = rest[n_aliased:]

    def put(cache_o, value):
        for n in range(cache_o.shape[0]):
            cache_o[n, 0] = value

    x = x_ref[0]
    h = _rms(x, gn_ref[...]) * (1.0 + scale_ref[0]) + shift_ref[0]
    h16 = h.astype(BF16)

    def zcols(a):
        return _dot(h16, wa_ref[:, a[0]:a[1]])

    cqn = _rms(zcols(A_CQ), gq_ref[...]).astype(BF16)
    q = _dot(cqn, wuq_ref[...])
    for hd in range(MLA_HEADS):
        sl = slice(hd * LANES, (hd + 1) * LANES)
        q_o[0, :, sl] = _rope_group(q[:, sl], qc_ref, qs1_ref, qs2_ref).astype(BF16)

    ckvn = _rms(zcols(A_CKV), gkv_ref[...])
    put(ckv_o, ckvn)
    krg = _rope_group(zcols(A_KR), kc_ref, ks1_ref, ks2_ref)
    put(kr_o, krg[:, 0:ROPE_DIM])

    sm = zcols(A_SM)
    ki = sm[:, 0:IDX_DIM]
    put(ki_o, ki)
    wit_o[0] = sm.T[IDX_DIM:IDX_DIM + IDX_HEADS, :] * IDX_W_SCALE
    qb_o[0] = (zcols(A_QB) * (DSA_SCALE * LOG2E)).astype(BF16)
    qi_o[0] = zcols(A_QI).astype(BF16)
    kb = zcols(A_KB)
    vb = zcols(A_VB)
    put(kb_o, kb)
    put(vb_o, vb)

    if emit_kv:
        kcat_o, vat_o, kb16_o, vbt_o, ki16_o = kv_outs
        ckv16 = ckvn.astype(BF16)
        kn = _dot(ckv16, wkn_ref[...])
        lane = lax.broadcasted_iota(jnp.int32, krg.shape, 1)
        k_add = jnp.where(lane >= NOPE_DIM, krg, 0.0)
        for hd in range(MLA_HEADS):
            sl = slice(hd * LANES, (hd + 1) * LANES)
            kcat_o[0, :, sl] = (kn[:, sl] + k_add).astype(BF16)
        va = _dot(ckv16, wv_ref[...])
        kb16_o[0] = kb.astype(BF16)
        ki16_o[0] = ki.astype(BF16)
        for c in range(x.shape[0] // tk):
            rows = slice(c * tk, (c + 1) * tk)
            vat_o[0, c] = va[rows, :].T.astype(BF16)
            vbt_o[0, c] = vb[rows, :].T.astype(BF16)


def _proj(x, shift, scale, gn, wa, gq, wuq, gkv, wkn, wv, q_tabs, k_tabs, caches, *, layer, depth, tm, tk, emit_kv):
    nb, t, _ = x.shape
    r = shift.shape[1]
    rb = 1 if r == 1 else tm
    grid = (nb, t // tm)
    tok = lambda w: pl.BlockSpec((1, tm, w), lambda b, i: (b, i, 0))
    if caches is None:
        stacked = lambda w: pl.BlockSpec((depth, 1, tm, w), lambda b, i: (0, b, i, 0))
    else:
        stacked = lambda w: pl.BlockSpec((1, 1, tm, w), lambda b, i: (layer, b, i, 0))
    mod = pl.BlockSpec((1, rb, D_MODEL), (lambda b, i: (b, 0, 0)) if r == 1 else (lambda b, i: (b, i, 0)))
    const = lambda a: pl.BlockSpec(a.shape, lambda b, i: (0,) * a.ndim)
    tab = pl.BlockSpec((tm, LANES), lambda b, i: (i, 0))
    in_specs = [tok(D_MODEL), mod, mod, const(gn), const(wa), const(gq), const(wuq), const(gkv), const(wkn),
                const(wv)] + [tab] * 6
    operands = [x, shift, scale, gn, wa, gq, wuq, gkv, wkn, wv, *q_tabs, *k_tabs]
    cache_widths = (KV_LORA, ROPE_DIM, KV_DIM, KV_DIM, IDX_DIM)
    aliases = {}
    if caches is not None:
        aliases = {len(operands) + n: n for n in range(len(cache_widths))}
        in_specs += [pl.BlockSpec(memory_space=pl.ANY)] * len(cache_widths)
        operands += list(caches)
    out_shape = [jax.ShapeDtypeStruct((depth, nb, t, w), F32) for w in cache_widths] + [
        jax.ShapeDtypeStruct((nb, t, QK_PAD), BF16),
        jax.ShapeDtypeStruct((nb, t, W_B), BF16),
        jax.ShapeDtypeStruct((nb, t, IDX_HEADS * IDX_DIM), BF16),
        jax.ShapeDtypeStruct((nb, IDX_HEADS, t), F32),
    ]
    out_specs = [stacked(w) for w in cache_widths] + [
        tok(QK_PAD), tok(W_B), tok(IDX_HEADS * IDX_DIM), pl.BlockSpec((1, IDX_HEADS, tm), lambda b, i: (b, 0, i))]
    if emit_kv:
        nkt = tm // tk
        out_shape += [
            jax.ShapeDtypeStruct((nb, t, QK_PAD), BF16),
            jax.ShapeDtypeStruct((nb, t // tk, W_A, tk), BF16),
            jax.ShapeDtypeStruct((nb, t, KV_DIM), BF16),
            jax.ShapeDtypeStruct((nb, t // tk, KV_DIM, tk), BF16),
            jax.ShapeDtypeStruct((nb, t, IDX_DIM), BF16),
        ]
        out_specs += [tok(QK_PAD), pl.BlockSpec((1, nkt, W_A, tk), lambda b, i: (b, i, 0, 0)), tok(KV_DIM),
                      pl.BlockSpec((1, nkt, KV_DIM, tk), lambda b, i: (b, i, 0, 0)), tok(IDX_DIM)]
    return pl.pallas_call(
        functools.partial(_proj_kernel, emit_kv=emit_kv, tk=tk, n_aliased=len(aliases)),
        out_shape=out_shape, grid=grid, in_specs=in_specs, out_specs=out_specs, input_output_aliases=aliases,
        compiler_params=_params(2), name="proj_kv" if emit_kv else "proj_q",
    )(*operands)


class _Softmax:
    ONES_ROWS = 16

    def __init__(self, s_scr, mx_scr, m_scr, acc_scr):
        self.s, self.mx, self.m, self.acc = s_scr, mx_scr, m_scr, acc_scr
        self.n_heads = acc_scr.shape[0]
        self.dv = acc_scr.shape[1] - self.ONES_ROWS

    @classmethod
    def scratch(cls, n_heads, dv, tk, tq):
        stat = pltpu.VMEM((n_heads, SUBLANES, tq), F32)
        return [pltpu.VMEM((n_heads, 2, tk, tq), F32), stat, stat, pltpu.VMEM((n_heads, dv + cls.ONES_ROWS, tq), F32)]

    def reset(self):
        self.mx[...] = jnp.full(self.mx.shape, NEG, F32)
        self.m[...] = jnp.full(self.m.shape, NEG, F32)
        self.acc[...] = jnp.zeros(self.acc.shape, F32)

    def produce(self, hd, slot, s):
        self.s[hd, slot] = s
        self.mx[hd] = jnp.maximum(self.mx[hd], _rows8(s, jnp.maximum, ways=4))

    def consume(self, hd, slot, vt):
        m_old = self.m[hd, 0:1, :]
        m_new = jnp.max(self.mx[hd], axis=0, keepdims=True)
        alpha = jnp.exp2(m_old - m_new)
        p = jnp.exp2(self.s[hd, slot] - m_new)
        self.m[hd] = jnp.broadcast_to(m_new, self.m.shape[1:])
        vt1 = jnp.concatenate([vt, jnp.ones((self.ONES_ROWS, vt.shape[1]), vt.dtype)], axis=0)
        self.acc[hd] = alpha * self.acc[hd] + _dot(vt1, p.astype(BF16))

    def run(self, last, logits, values):
        def step(t_consume, t_produce, kind, slot_produce):
            for hd in range(self.n_heads):
                if t_consume is not None:
                    self.consume(hd, 1 - slot_produce, values(hd, t_consume))
                if t_produce is not None:
                    self.produce(hd, slot_produce, logits(hd, t_produce, kind))

        when = lambda cond, fn: pl.when(cond)(fn)
        self.reset()
        step(None, last, 2, 0)
        when(last >= 1, lambda: step(last, last - 1, 1, 1))
        n_far = last - 1

        def far_pair(n, _):
            t = last - 2 - 2 * n
            step(t + 1, t, 0, 0)
            step(t, t - 1, 0, 1)
            return 0

        lax.fori_loop(0, n_far // 2, far_pair, 0)
        odd = (n_far >= 1) & (n_far % 2 == 1)
        when(odd, lambda: step(1, 0, 0, 0))
        in_slot0 = (last == 0) | odd
        when(in_slot0, lambda: step(0, None, None, 1))
        when(jnp.logical_not(in_slot0), lambda: step(0, None, None, 0))

    def finish(self, o_ref):
        o_t = jnp.concatenate([self.acc[hd, 0:self.dv, :] / self.acc[hd, self.dv:self.dv + 1, :]
                               for hd in range(self.n_heads)], axis=0)
        o_ref[0] = o_t.T


def _diagonal_tile_visible(t):
    r = lax.broadcasted_iota(jnp.int32, (t, t), 0)
    c = lax.broadcasted_iota(jnp.int32, (t, t), 1)
    return (r // CHUNK) <= (c // CHUNK)


def _mla_kernel(q_ref, k_ref, vt_ref, o_ref, *scratch, t):
    softmax = _Softmax(*scratch)

    def logits(hd, j, kind):
        sl = slice(hd * LANES, (hd + 1) * LANES)
        s = _dot_nt(k_ref[0, _key_rows(j, t), sl], q_ref[0, :, sl])
        return s + jnp.where(_diagonal_tile_visible(t), 0.0, NEG) if kind == 2 else s

    softmax.run(pl.program_id(1), logits, lambda hd, j: vt_ref[0, j, hd * V_DIM:(hd + 1) * V_DIM, :])
    softmax.finish(o_ref)


def _mla(q, k, vt, *, t):
    nb, seq, _ = q.shape
    return pl.pallas_call(
        functools.partial(_mla_kernel, t=t),
        out_shape=jax.ShapeDtypeStruct((nb, seq, W_A), F32), grid=(nb, seq // t),
        in_specs=[pl.BlockSpec((1, t, QK_PAD), lambda b, i: (b, i, 0)),
                  pl.BlockSpec((1, seq, QK_PAD), lambda b, i: (b, 0, 0)),
                  pl.BlockSpec((1, seq // t, W_A, t), lambda b, i: (b, 0, 0, 0))],
        out_specs=pl.BlockSpec((1, t, W_A), lambda b, i: (b, i, 0)),
        scratch_shapes=_Softmax.scratch(MLA_HEADS, V_DIM, t, t),
        compiler_params=_params(2), name="mla_causal",
    )(q, k, vt)


def _score_stats_init(tq):
    return (jnp.full((1, tq), jnp.inf, F32), jnp.full((1, tq), -jnp.inf, F32),
            jnp.zeros((SUBLANES, tq), F32), jnp.zeros((SUBLANES, tq), F32))


def _score_stats(stats, t, visible=None):
    lo, hi, c_z, c_zp = stats
    t_min = t if visible is None else jnp.where(visible, t, jnp.inf)
    return (jnp.minimum(lo, jnp.min(t_min, axis=0, keepdims=True)), jnp.maximum(hi, jnp.max(t, axis=0, keepdims=True)),
            c_z + _rows8(jnp.where(t >= 0.0, 1.0, 0.0), jnp.add), c_zp + _rows8(jnp.where(t > 0.0, 1.0, 0.0), jnp.add))


def _select_topk(idx_scr, n_used, stats, n_vis, kf):
    _, tk, tq = idx_scr.shape

    def count(pred):
        def body(j, c):
            return c + _rows8(jnp.where(pred(idx_scr[j]), 1.0, 0.0), jnp.add)
        return jnp.sum(lax.fori_loop(0, n_used, body, jnp.zeros((SUBLANES, tq), F32)), axis=0, keepdims=True)

    lo, hi, c_z, c_zp = stats
    c_z = jnp.sum(c_z, axis=0, keepdims=True)
    c_zp = jnp.sum(c_zp, axis=0, keepdims=True)
    hi = hi + (jnp.abs(hi) * 2.0 ** -20 + 1e-30)
    zero_up = (c_z >= kf) & (lo < 0.0)
    zero_tie = jnp.where((c_z >= kf) & (c_zp < kf), 1.0, 0.0)
    c_lo = jnp.where(zero_up, c_z, n_vis)
    lo = jnp.where(zero_up, 0.0, lo)
    hi = jnp.where((c_z < kf) & (hi > 0.0), 0.0, hi)

    def settled(lo, hi, c_lo):
        mid = 0.5 * lo + 0.5 * hi
        return jnp.where((c_lo <= kf) | (mid <= lo) | (mid >= hi), 1.0, 0.0)

    def bis_cond(st):
        _, _, _, done, it = st
        return jnp.logical_and(it < MAX_BISECT, jnp.min(done) < 0.5)

    def bis_body(st):
        lo, hi, c_lo, done, it = st
        for _ in range(BISECT_PER_CHECK):
            mid = 0.5 * lo + 0.5 * hi
            c = count(lambda t: t >= mid)
            live = done < 0.5
            up = live & (c >= kf)
            dn = live & (c < kf)
            lo = jnp.where(up, mid, lo)
            c_lo = jnp.where(up, c, c_lo)
            hi = jnp.where(dn, mid, hi)
            done = jnp.maximum(done, settled(lo, hi, c_lo))
        return lo, hi, c_lo, done, it + BISECT_PER_CHECK

    done = jnp.maximum(settled(lo, hi, c_lo), zero_tie)
    lo, hi, c_lo, _, _ = lax.while_loop(bis_cond, bis_body, (lo, hi, c_lo, done, jnp.int32(0)))
    any_tie = jnp.max(jnp.where(c_lo > kf, 1.0, 0.0)) > 0.5

    @pl.when(jnp.logical_not(any_tie))
    def _():
        def body(j, _):
            idx_scr[j] = jnp.where(idx_scr[j] >= lo, 0.0, NEG)
            return 0
        lax.fori_loop(0, n_used, body, 0)

    @pl.when(any_tie)
    def _():
        need = kf - count(lambda t: t > lo)
        r = lax.broadcasted_iota(jnp.int32, (tk, tk), 0)
        c = lax.broadcasted_iota(jnp.int32, (tk, tk), 1)
        before = jnp.where(c < r, 1.0, 0.0).astype(BF16)

        def body(j, seen):
            t = idx_scr[j]
            eq = jnp.where(t == lo, 1.0, 0.0)
            rank = _dot(before, eq.astype(BF16)) + seen
            take = (t > lo) | ((t == lo) & (rank < need))
            idx_scr[j] = jnp.where(take, 0.0, NEG)
            return seen + jnp.sum(_rows8(eq, jnp.add), axis=0, keepdims=True)
        lax.fori_loop(0, n_used, body, jnp.zeros((1, tq), F32))


def _dsa_kernel(qi_ref, wit_ref, ki_ref, qb_ref, kb_ref, vbt_ref, bias_ref, o_ref, idx_scr, *scratch, t, topk):
    tq = tk = t
    last = pl.program_id(1)
    n_used = last + 1
    visible = _diagonal_tile_visible(t)

    wit = wit_ref[0]

    def idx_tile(j):
        k_j = ki_ref[0, _key_rows(j, tk), :]
        acc = jnp.zeros((tk, tq), F32)
        for hd in range(IDX_HEADS):
            s = _dot_nt(k_j, qi_ref[0, :, hd * IDX_DIM:(hd + 1) * IDX_DIM])
            acc = acc + jnp.maximum(s, 0.0) * wit[hd:hd + 1, :]
        return acc

    def idx_body(j, stats):
        t = idx_tile(j)
        idx_scr[j] = t
        return _score_stats(stats, t)

    def idx_pair(n, stats):
        return idx_body(2 * n + 1, idx_body(2 * n, stats))

    stats = lax.fori_loop(0, last // 2, idx_pair, _score_stats_init(tq))
    stats = lax.cond(last % 2 == 1, lambda st: idx_body(last - 1, st), lambda st: st, stats)
    t_last = jnp.where(visible, idx_tile(last), -jnp.inf)
    idx_scr[last] = t_last
    stats = _score_stats(stats, t_last, visible)
    lane = lax.broadcasted_iota(jnp.int32, (1, tq), 1)
    n_vis = last * tk + (lane // CHUNK + 1) * CHUNK
    _select_topk(idx_scr, n_used, stats, n_vis.astype(F32), float(topk))

    softmax = _Softmax(*scratch)
    kv_lanes = lambda hd: slice((hd // DSA_GROUP) * DSA_HEAD_DIM, (hd // DSA_GROUP + 1) * DSA_HEAD_DIM)

    def logits(hd, t, kind):
        s = _dot_nt(kb_ref[0, _key_rows(t, tk), kv_lanes(hd)], qb_ref[0, :, hd * DSA_HEAD_DIM:(hd + 1) * DSA_HEAD_DIM])
        return s + idx_scr[t] if kind == 0 else s + (idx_scr[t] + bias_ref[hd, kind - 1])

    softmax.run(last, logits, lambda hd, t: vbt_ref[0, t, kv_lanes(hd), :])
    softmax.finish(o_ref)


def _dsa(qi, wit, ki, qb, kb, vbt, bias, *, t, topk):
    nb, seq, _ = qb.shape
    n_tiles = seq // t
    return pl.pallas_call(
        functools.partial(_dsa_kernel, t=t, topk=topk),
        out_shape=jax.ShapeDtypeStruct((nb, seq, W_B), F32), grid=(nb, n_tiles),
        in_specs=[pl.BlockSpec((1, t, IDX_HEADS * IDX_DIM), lambda b, i: (b, i, 0)),
                  pl.BlockSpec((1, IDX_HEADS, t), lambda b, i: (b, 0, i)),
                  pl.BlockSpec((1, seq, IDX_DIM), lambda b, i: (b, 0, 0)),
                  pl.BlockSpec((1, t, W_B), lambda b, i: (b, i, 0)),
                  pl.BlockSpec((1, seq, KV_DIM), lambda b, i: (b, 0, 0)),
                  pl.BlockSpec((1, n_tiles, KV_DIM, t), lambda b, i: (b, 0, 0, 0)),
                  pl.BlockSpec(bias.shape, lambda b, i: (0, 0, 0, 0))],
        out_specs=pl.BlockSpec((1, t, W_B), lambda b, i: (b, i, 0)),
        scratch_shapes=[pltpu.VMEM((n_tiles, t, t), F32)] + _Softmax.scratch(DSA_HEADS, DSA_HEAD_DIM, t, t),
        compiler_params=_params(2), name="dsa_causal",
    )(qi, wit, ki, qb, kb, vbt, bias)


def _two_pass_softmax(n_tiles, logits, values, s_scr, dv):
    mx = None
    for t in range(n_tiles):
        s = logits(t)
        s_scr[t] = s
        part = _rows8(s, jnp.maximum, ways=4)
        mx = part if mx is None else jnp.maximum(mx, part)
    m = jnp.max(mx, axis=0, keepdims=True)
    acc = None
    for t in range(n_tiles):
        p = jnp.exp2(s_scr[t] - m).astype(BF16)
        vt = values(t)
        part = _dot(jnp.concatenate([vt, jnp.ones((_Softmax.ONES_ROWS, vt.shape[1]), BF16)], axis=0), p)
        acc = part if acc is None else acc + part
    return acc[0:dv] / acc[dv:dv + 1]


def _mla_cached_kernel(q_ref, ckv_c_ref, kr_c_ref, ckv_n_ref, kr_n_ref, wuk_ref, wuv_ref, sel_ref, o_ref, s_scr, *,
                       ts, tk, n_cache, n_valid_last):
    lanes = MLA_HEADS * ts
    q_lat, q_rope = [], []
    for hd in range(MLA_HEADS):
        qg = q_ref[0, :, hd * LANES:(hd + 1) * LANES]
        q_lat.append(_dot(qg, wuk_ref[hd]).astype(BF16))
        q_rope.append(_dot(qg, sel_ref[...]).astype(BF16))
    q_lat = jnp.concatenate(q_lat, axis=0)
    q_rope = jnp.concatenate(q_rope, axis=0)

    def latent(t):
        return ckv_c_ref[0, t * tk:(t + 1) * tk, :] if t < n_cache else ckv_n_ref[0]

    def logits(t):
        kr = kr_c_ref[0, t * tk:(t + 1) * tk, :] if t < n_cache else kr_n_ref[0]
        s = _dot_nt(latent(t).astype(BF16), q_lat) + _dot_nt(kr.astype(BF16), q_rope)
        if t == n_cache:
            s = s + jnp.where(lax.broadcasted_iota(jnp.int32, s.shape, 0) < n_valid_last, 0.0, NEG)
        return s

    o_lat = _two_pass_softmax(n_cache + 1, logits, lambda t: latent(t).T.astype(BF16), s_scr, KV_LORA)
    o_lat = o_lat.astype(BF16)
    head_of_lane = lax.broadcasted_iota(jnp.int32, (V_DIM, lanes), 1) // ts
    out = jnp.zeros((V_DIM, lanes), F32)
    for hd in range(MLA_HEADS):
        out = jnp.where(head_of_lane == hd, _dot(wuv_ref[hd], o_lat), out)
    o_ref[0] = out


def _mla_cached(q, ckv_cache, kr_cache, ckv_new, kr_new, wuk, wuv, sel, *, layer, tk, n_valid_last):
    nb, ts, _ = q.shape
    past = ckv_cache.shape[2]
    lanes = MLA_HEADS * ts
    per_b = lambda a: pl.BlockSpec((1,) + a.shape[1:], lambda b: (b,) + (0,) * (a.ndim - 1))
    const = lambda a: pl.BlockSpec(a.shape, lambda b: (0,) * a.ndim)
    return pl.pallas_call(
        functools.partial(_mla_cached_kernel, ts=ts, tk=tk, n_cache=past // tk, n_valid_last=n_valid_last),
        out_shape=jax.ShapeDtypeStruct((nb, V_DIM, lanes), F32), grid=(nb,),
        in_specs=[per_b(q), _layer_rows(ckv_cache, layer), _layer_rows(kr_cache, layer), per_b(ckv_new),
                  per_b(kr_new), const(wuk), const(wuv), const(sel)],
        out_specs=pl.BlockSpec((1, V_DIM, lanes), lambda b: (b, 0, 0)),
        scratch_shapes=[pltpu.VMEM((past // tk + 1, tk, lanes), F32)],
        compiler_params=_params(1), name="mla_cached",
    )(q, ckv_cache, kr_cache, ckv_new, kr_new, wuk, wuv, sel)


def _dsa_cached_kernel(qi_ref, w_ref, ki_c_ref, ki_n_ref, qb_ref, kb_c_ref, kb_n_ref, vb_c_ref, vb_n_ref, bias_ref,
                       o_ref, idx_scr, s_scr, *, ts, tk, n_cache, n_valid_last, topk):
    n_tiles = n_cache + 1
    lanes = DSA_GROUP * ts
    tile = lambda c_ref, n_ref, t: c_ref[0, t * tk:(t + 1) * tk, :] if t < n_cache else n_ref[0]
    visible = lax.broadcasted_iota(jnp.int32, (tk, lanes), 0) < n_valid_last

    stats = _score_stats_init(lanes)
    for t in range(n_tiles):
        s = _dot_nt(tile(ki_c_ref, ki_n_ref, t).astype(BF16), qi_ref[0])
        r = jnp.maximum(s, 0.0) * w_ref[0]
        x = r[:, 0:lanes]
        for blk in range(1, IDX_HEADS * ts // lanes):
            x = x + r[:, blk * lanes:(blk + 1) * lanes]
        shift = lanes // 2
        while shift >= ts:
            x = x + pltpu.roll(x, shift, 1)
            shift //= 2
        if t == n_cache:
            x = jnp.where(visible, x, -jnp.inf)
        idx_scr[t] = x
        stats = _score_stats(stats, x, visible if t == n_cache else None)
    n_vis = jnp.full((1, lanes), float(n_cache * tk + n_valid_last), F32)
    _select_topk(idx_scr, n_tiles, stats, n_vis, float(topk))

    for g in range(DSA_KV_HEADS):
        def logits(t, g=g):
            s = _dot_nt(tile(kb_c_ref, kb_n_ref, t).astype(BF16), qb_ref[0, g]) + idx_scr[t]
            return s + bias_ref[g, t - (n_tiles - 2)] if t >= n_tiles - 2 else s

        def values(t, g=g):
            return tile(vb_c_ref, vb_n_ref, t).T[g * DSA_HEAD_DIM:(g + 1) * DSA_HEAD_DIM, :].astype(BF16)

        o_ref[0, g] = _two_pass_softmax(n_tiles, logits, values, s_scr, DSA_HEAD_DIM)


def _layer_rows(cache, layer):
    return pl.BlockSpec((None, 1) + cache.shape[2:], lambda b: (layer, b, 0, 0))


def _dsa_cached(qi, w, ki_cache, ki_new, qb, kb_cache, kb_new, vb_cache, vb_new, bias, *, layer, ts, tk,
                n_valid_last, topk):
    nb = qi.shape[0]
    past = ki_cache.shape[2]
    n_tiles = past // tk + 1
    lanes = DSA_GROUP * ts
    per_b = lambda a: pl.BlockSpec((1,) + a.shape[1:], lambda b: (b,) + (0,) * (a.ndim - 1))
    cached = lambda a: _layer_rows(a, layer)
    return pl.pallas_call(
        functools.partial(_dsa_cached_kernel, ts=ts, tk=tk, n_cache=past // tk, n_valid_last=n_valid_last, topk=topk),
        out_shape=jax.ShapeDtypeStruct((nb, DSA_KV_HEADS, DSA_HEAD_DIM, lanes), F32), grid=(nb,),
        in_specs=[per_b(qi), per_b(w), cached(ki_cache), per_b(ki_new), per_b(qb), cached(kb_cache), per_b(kb_new),
                  cached(vb_cache), per_b(vb_new), pl.BlockSpec(bias.shape, lambda b: (0,) * bias.ndim)],
        out_specs=pl.BlockSpec((1, DSA_KV_HEADS, DSA_HEAD_DIM, lanes), lambda b: (b, 0, 0, 0)),
        scratch_shapes=[pltpu.VMEM((n_tiles, tk, lanes), F32), pltpu.VMEM((n_tiles, tk, lanes), F32)],
        compiler_params=_params(1), name="dsa_cached",
    )(qi, w, ki_cache, ki_new, qb, kb_cache, kb_new, vb_cache, vb_new, bias)


def _out_kernel(x_ref, shift_ref, scale_ref, gate_ref, gn_ref, oa_ref, ob_ref, wg_ref, woa_ref, wob_ref, wout_ref,
                gf_ref, o_ref, *, final):
    x = x_ref[0]
    h16 = (_rms(x, gn_ref[...]) * (1.0 + scale_ref[0]) + shift_ref[0]).astype(BF16)

    def gcols(a):
        return _dot(h16, wg_ref[:, a[0]:a[1]])

    ga = gcols(G_GA)
    ya = _dot((oa_ref[0] * (ga * _sigmoid(ga))).astype(BF16), woa_ref[...])
    gb = gcols(G_GB)
    yb = _dot((ob_ref[0] * (gb * _sigmoid(gb))).astype(BF16), wob_ref[...])
    m = _sigmoid(gcols(G_MA)) * ya + _sigmoid(gcols(G_MB)) * yb
    y = x + gate_ref[0] * _dot(m.astype(BF16), wout_ref[...])
    o_ref[0] = _rms(y, gf_ref[...]) if final else y


def _out(x, shift, scale, gate, gn, oa, ob, wg, woa, wob, wout, gf, *, tm, final):
    nb, t, _ = x.shape
    r = shift.shape[1]
    rb = 1 if r == 1 else tm
    tok = lambda w: pl.BlockSpec((1, tm, w), lambda b, i: (b, i, 0))
    mod = pl.BlockSpec((1, rb, D_MODEL), (lambda b, i: (b, 0, 0)) if r == 1 else (lambda b, i: (b, i, 0)))
    const = lambda a: pl.BlockSpec(a.shape, lambda b, i: (0,) * a.ndim)
    return pl.pallas_call(
        functools.partial(_out_kernel, final=final),
        out_shape=jax.ShapeDtypeStruct(x.shape, F32), grid=(nb, t // tm),
        in_specs=[tok(D_MODEL), mod, mod, mod, const(gn), tok(W_A), tok(W_B), const(wg), const(woa), const(wob),
                  const(wout), const(gf)],
        out_specs=tok(D_MODEL), compiler_params=_params(2), name="out_final" if final else "out_mix",
    )(x, shift, scale, gate, gn, oa, ob, wg, woa, wob, wout, gf)


def _pack_weights(w_in, w_uq, w_ukv, w_oa, w_ob, w_out):
    depth = w_in.shape[0]
    offs = np.concatenate([[0], np.cumsum(SPLIT_SIZES)])
    cq, ckv, kr, ga, qb, kb, vb, qi, ki, wi, gb, ma, mb = [w_in[:, :, offs[n]:offs[n + 1]] for n in range(13)]
    z = lambda w: jnp.zeros((depth, D_MODEL, w), F32)
    krg = jnp.concatenate([kr, z(NOPE_DIM - ROPE_DIM), kr, z(LANES - NOPE_DIM - ROPE_DIM)], axis=-1)
    sm = jnp.concatenate([ki, wi, z(LANES - IDX_DIM - IDX_HEADS)], axis=-1)
    wa = jnp.concatenate([cq, ckv, krg, sm, qb, kb, vb, qi], axis=-1).astype(BF16)
    wg = jnp.concatenate([ga, gb, ma, mb], axis=-1).astype(BF16)
    qd = NOPE_DIM + ROPE_DIM
    wuq = jnp.pad(w_uq.reshape(depth, Q_LORA, MLA_HEADS, qd), ((0, 0), (0, 0), (0, 0), (0, LANES - qd)))
    wuq = wuq.reshape(depth, Q_LORA, QK_PAD).astype(BF16)
    ukv = w_ukv.reshape(depth, KV_LORA, MLA_HEADS, NOPE_DIM + V_DIM)
    wkn = jnp.pad(ukv[..., :NOPE_DIM], ((0, 0), (0, 0), (0, 0), (0, LANES - NOPE_DIM)))
    wkn = wkn.reshape(depth, KV_LORA, QK_PAD).astype(BF16)
    wv = ukv[..., NOPE_DIM:].reshape(depth, KV_LORA, W_A).astype(BF16)
    wuk = jnp.pad(ukv[..., :NOPE_DIM].transpose(0, 2, 3, 1), ((0, 0), (0, 0), (0, LANES - NOPE_DIM), (0, 0)))
    wuv = ukv[..., NOPE_DIM:].transpose(0, 2, 3, 1)
    return (wa, wg, wuq, wkn, wv, w_oa.astype(BF16), w_ob.astype(BF16), w_out.astype(BF16), wuk.astype(BF16),
            wuv.astype(BF16))


def _rope_tables(pos):
    half = ROPE_DIM // 2
    inv = ROPE_THETA ** (-jnp.arange(half, dtype=F32) / half)
    ang = pos.astype(F32)[:, None] * inv
    cos, sin = jnp.cos(ang), jnp.sin(ang)
    n = pos.shape[0]
    z = lambda w: jnp.zeros((n, w), F32)
    pad = LANES - NOPE_DIM - ROPE_DIM
    q_c = jnp.concatenate([jnp.ones((n, NOPE_DIM), F32), cos, cos, z(pad)], axis=-1) * (MLA_SCALE * LOG2E)
    q_s1 = jnp.concatenate([z(NOPE_DIM), -sin, z(half), z(pad)], axis=-1) * (MLA_SCALE * LOG2E)
    q_s2 = jnp.concatenate([z(NOPE_DIM), z(half), sin, z(pad)], axis=-1) * (MLA_SCALE * LOG2E)
    gap = NOPE_DIM - ROPE_DIM
    k_c = jnp.concatenate([cos, cos, z(gap), cos, cos, z(pad)], axis=-1)
    k_s1 = jnp.concatenate([-sin, z(half), z(gap), -sin, z(half), z(pad)], axis=-1)
    k_s2 = jnp.concatenate([z(half), sin, z(gap), z(half), sin, z(pad)], axis=-1)
    return (q_c, q_s1, q_s2), (k_c, k_s1, k_s2)


def _rope_selector():
    p = np.zeros((LANES, ROPE_DIM), np.float32)
    p[NOPE_DIM + np.arange(ROPE_DIM), np.arange(ROPE_DIM)] = 1.0
    return jnp.asarray(p, dtype=BF16)


def kernel(x_prompt, x_sample, c_prompt, c_sample, cache_ckv, cache_krope, cache_kb, cache_vb, cache_kidx, w_ada,
           b_ada, g_norm, w_in, g_qnorm, w_uq, g_kvnorm, w_ukv, w_oa, w_ob, w_out, rel_bias, g_final):
    depth = w_in.shape[0]
    bp, t_p, _ = x_prompt.shape
    bs, t_s, _ = x_sample.shape
    past = cache_ckv.shape[2]
    topk_p = min(TOPK_MAX, t_p // 4)
    topk_s = min(TOPK_MAX, (past + t_s) // 4)
    assert t_p % TM == 0 and TM % TK == 0 and past % TK == 0 and t_s <= TK and t_s % 8 == 0
    assert (DSA_GROUP * t_s) % LANES == 0 and (DSA_GROUP * t_s) & (DSA_GROUP * t_s - 1) == 0
    n_s = bs * t_s

    wa, wg, wuq, wkn, wv, woa, wob, wout, wuk, wuv = _pack_weights(w_in, w_uq, w_ukv, w_oa, w_ob, w_out)
    rope_sel = _rope_selector()
    cache_kb2 = cache_kb.reshape(depth, bs, past, KV_DIM)
    cache_vb2 = cache_vb.reshape(depth, bs, past, KV_DIM)
    mod = _modulation(jnp.concatenate([c_prompt, c_sample], axis=0), w_ada, b_ada)
    mod = mod.reshape(depth, bp + bs, 3, 1, D_MODEL)
    tabs_p = _rope_tables(jnp.arange(t_p, dtype=jnp.int32))
    tabs_s = _rope_tables(jnp.tile(past + jnp.arange(t_s, dtype=jnp.int32), bs))
    bias_p = _bias_tiles(rel_bias, TK, TK)
    bias_s = _bias_tiles(rel_bias, TK, t_s).reshape(DSA_KV_HEADS, DSA_GROUP, 2, TK, t_s)
    bias_s = bias_s.transpose(0, 2, 3, 1, 4).reshape(DSA_KV_HEADS, 2, TK, DSA_GROUP * t_s)
    row = lambda v: v.reshape(1, -1)

    xp = x_prompt
    xs = x_sample.reshape(1, n_s, D_MODEL)
    caches_p = caches_s = None
    for l in range(depth):
        final = l == depth - 1
        gn, gq, gkv, gf = row(g_norm[l]), row(g_qnorm[l]), row(g_kvnorm[l]), row(g_final)

        shift, scale, gate = (mod[l, :bp, n] for n in range(3))
        *caches_p, q, qb, qi, wit, kcat, vat, kb16, vbt, ki16 = _proj(
            xp, shift, scale, gn, wa[l], gq, wuq[l], gkv, wkn[l], wv[l], *tabs_p, caches_p, layer=l, depth=depth,
            tm=TM, tk=TK, emit_kv=True)
        oa = _mla(q, kcat, vat, t=TK)
        ob = _dsa(qi, wit, ki16, qb, kb16, vbt, bias_p, t=TK, topk=topk_p)
        xp = _out(xp, shift, scale, gate, gn, oa, ob, wg[l], woa[l], wob[l], wout[l], gf, tm=TM, final=final)

        shift, scale, gate = (jnp.repeat(mod[l, bp:, n, 0], t_s, axis=0)[None] for n in range(3))
        *caches_s, q, qb, qi, wit = _proj(
            xs, shift, scale, gn, wa[l], gq, wuq[l], gkv, wkn[l], wv[l], *tabs_s, caches_s, layer=l, depth=depth,
            tm=n_s, tk=TK, emit_kv=False)
        per_seq = lambda a: a.reshape((bs, t_s) + a.shape[2:])
        ckv, kr, kb, vb, ki = (per_seq(a[l]) for a in caches_s)
        q, qb, qi = per_seq(q), per_seq(qb), per_seq(qi)
        new_tile = lambda a: jnp.pad(a, ((0, 0), (0, TK - t_s), (0, 0)))
        oa = _mla_cached(q, cache_ckv, cache_krope, new_tile(ckv), new_tile(kr), wuk[l], wuv[l], rope_sel, layer=l,
                         tk=TK, n_valid_last=t_s)
        oa = oa.reshape(bs, V_DIM, MLA_HEADS, t_s).transpose(0, 3, 2, 1).reshape(1, n_s, W_A)
        qi_l = qi.reshape(bs, t_s, IDX_HEADS, IDX_DIM).transpose(0, 2, 1, 3).reshape(bs, IDX_HEADS * t_s, IDX_DIM)
        w_l = wit.reshape(IDX_HEADS, bs, t_s).transpose(1, 0, 2).reshape(bs, 1, IDX_HEADS * t_s)
        qb_l = qb.reshape(bs, t_s, DSA_KV_HEADS, DSA_GROUP, DSA_HEAD_DIM).transpose(0, 2, 3, 1, 4)
        qb_l = qb_l.reshape(bs, DSA_KV_HEADS, DSA_GROUP * t_s, DSA_HEAD_DIM)
        qb_l = jnp.stack([jnp.pad(qb_l[:, g], ((0, 0), (0, 0), (g * DSA_HEAD_DIM, KV_DIM - (g + 1) * DSA_HEAD_DIM)))
                          for g in range(DSA_KV_HEADS)], axis=1)
        ob = _dsa_cached(qi_l, w_l, cache_kidx, new_tile(ki), qb_l, cache_kb2, new_tile(kb), cache_vb2, new_tile(vb),
                         bias_s, layer=l, ts=t_s, tk=TK, n_valid_last=t_s, topk=topk_s)
        ob = ob.reshape(bs, DSA_KV_HEADS, DSA_HEAD_DIM, DSA_GROUP, t_s).transpose(0, 4, 1, 3, 2).reshape(1, n_s, W_B)
        xs = _out(xs, shift, scale, gate, gn, oa, ob, wg[l], woa[l], wob[l], wout[l], gf, tm=n_s, final=final)

    def cache_outputs(stacks, nb, t):
        ckv, kr, kb, vb, ki = (a.reshape((depth, nb, t, a.shape[-1])) for a in stacks)
        kvh = (depth, nb, t, DSA_KV_HEADS, DSA_HEAD_DIM)
        return ckv, kr, kb.reshape(kvh), vb.reshape(kvh), ki

    return (xp, xs.reshape(bs, t_s, D_MODEL)) + cache_outputs(caches_p, bp, t_p) + cache_outputs(caches_s, bs, t_s)
```

```python
import functools

import numpy as np
import jax
import jax.numpy as jnp
from jax import lax
from jax.experimental import pallas as pl
from jax.experimental.pallas import tpu as pltpu

F32 = jnp.float32
BF16 = jnp.bfloat16

D_MODEL = 1024
CHUNK = 64
EPS = 1e-6
MLA_HEADS = 8
NOPE_DIM = 64
ROPE_DIM = 32
V_DIM = 64
Q_LORA = 384
KV_LORA = 256
ROPE_THETA = 10000.0
MLA_SCALE = (NOPE_DIM + ROPE_DIM) ** -0.5
DSA_HEADS = 8
DSA_KV_HEADS = 2
DSA_GROUP = DSA_HEADS // DSA_KV_HEADS
DSA_HEAD_DIM = 64
DSA_SCALE = DSA_HEAD_DIM ** -0.5
IDX_HEADS = 8
IDX_DIM = 64
IDX_W_SCALE = (IDX_HEADS ** -0.5) * (IDX_DIM ** -0.5)
TOPK_MAX = 256
NUM_BUCKETS = 32
MAX_DISTANCE = 128
W_A = MLA_HEADS * V_DIM
W_B = DSA_HEADS * DSA_HEAD_DIM
SPLIT_SIZES = (Q_LORA, KV_LORA, ROPE_DIM, W_A, W_B, DSA_KV_HEADS * DSA_HEAD_DIM, DSA_KV_HEADS * DSA_HEAD_DIM,
               IDX_HEADS * IDX_DIM, IDX_DIM, IDX_HEADS, W_B, D_MODEL, D_MODEL)

LANES = 128
SUBLANES = 8
VMEM_LIMIT_BYTES = 56 * 1024 * 1024

TK = 256
TM = 512
NEG = -1e30
LOG2E = 1.4426950408889634
MAX_BISECT = 200
BISECT_PER_CHECK = 4

QK_PAD = MLA_HEADS * LANES
KV_DIM = DSA_KV_HEADS * DSA_HEAD_DIM

A_CQ = (0, Q_LORA)
A_CKV = (A_CQ[1], A_CQ[1] + KV_LORA)
A_KR = (A_CKV[1], A_CKV[1] + LANES)
A_SM = (A_KR[1], A_KR[1] + LANES)
A_QB = (A_SM[1], A_SM[1] + W_B)
A_KB = (A_QB[1], A_QB[1] + KV_DIM)
A_VB = (A_KB[1], A_KB[1] + KV_DIM)
A_QI = (A_VB[1], A_VB[1] + IDX_HEADS * IDX_DIM)
A_WIDTH = A_QI[1]
G_GA = (0, W_A)
G_GB = (G_GA[1], G_GA[1] + W_B)
G_MA = (G_GB[1], G_GB[1] + D_MODEL)
G_MB = (G_MA[1], G_MA[1] + D_MODEL)
G_WIDTH = G_MB[1]


def _params(n_grid):
    return pltpu.CompilerParams(dimension_semantics=("arbitrary",) * n_grid, vmem_limit_bytes=VMEM_LIMIT_BYTES)


def _sigmoid(x):
    return 1.0 / (1.0 + jnp.exp(-x))


def _rms(x, g):
    return x * lax.rsqrt(jnp.mean(x * x, axis=-1, keepdims=True) + EPS) * g


def _dot(a, b):
    return jnp.dot(a, b, preferred_element_type=F32)


def _dot_nt(a, b):
    return lax.dot_general(a, b, (((1,), (1,)), ((), ())), preferred_element_type=F32)


def _key_rows(j, tk):
    start = j * tk if isinstance(j, int) else pl.multiple_of(j * tk, tk)
    return pl.ds(start, tk)


def _rows8(x, op, ways=None):
    parts = [x[r:r + SUBLANES] for r in range(0, x.shape[0], SUBLANES)]
    if ways is not None:
        chains = parts[:ways]
        for n, part in enumerate(parts[ways:]):
            chains[n % ways] = op(chains[n % ways], part)
        parts = chains
    while len(parts) > 1:
        parts = [op(parts[n], parts[n + 1]) for n in range(0, len(parts) - 1, 2)] + parts[len(parts) & ~1:]
    return parts[0]


def _mod_kernel(c_ref, w_ref, b_ref, o_ref):
    c = c_ref[...]
    a = c * _sigmoid(c)
    w = w_ref[0]
    a_hi = a.astype(BF16)
    a_lo = (a - a_hi.astype(F32)).astype(BF16)
    w_hi = w.astype(BF16)
    w_lo = (w - w_hi.astype(F32)).astype(BF16)
    o_ref[0] = _dot(a_hi, w_hi) + (_dot(a_hi, w_lo) + _dot(a_lo, w_hi)) + b_ref[0]


def _modulation(c_all, w_ada, b_ada):
    depth = w_ada.shape[0]
    n = c_all.shape[0]
    tn = D_MODEL
    return pl.pallas_call(
        _mod_kernel,
        out_shape=jax.ShapeDtypeStruct((depth, n, 3 * D_MODEL), F32),
        grid=(depth, 3 * D_MODEL // tn),
        in_specs=[
            pl.BlockSpec((n, D_MODEL), lambda l, j: (0, 0)),
            pl.BlockSpec((1, D_MODEL, tn), lambda l, j: (l, 0, j)),
            pl.BlockSpec((1, 1, tn), lambda l, j: (l, 0, j)),
        ],
        out_specs=pl.BlockSpec((1, n, tn), lambda l, j: (l, 0, j)),
        compiler_params=_params(2),
        name="adaln_mod",
    )(c_all, w_ada, b_ada.reshape(depth, 1, 3 * D_MODEL))


def _rel_bucket(rel):
    half = NUM_BUCKETS // 2
    max_exact = half // 2
    n = np.abs(rel)
    large = max_exact + (np.log(np.maximum(n, 1).astype(np.float32) / max_exact)
                         / np.float32(np.log(MAX_DISTANCE / max_exact)) * (half - max_exact)).astype(np.int32)
    large = np.minimum(large, half - 1)
    return np.where(n < max_exact, n, large) + np.where(rel > 0, half, 0)


def _bias_kernel(bucket_ref, rel_ref, o_ref, *, far_bucket):
    h = pl.program_id(0)
    for t in range(2):
        b = bucket_ref[t]
        acc = jnp.zeros(b.shape, F32)
        for i in range(NUM_BUCKETS):
            acc = jnp.where(b == i, rel_ref[i, h], acc)
        o_ref[0, t] = (acc - rel_ref[far_bucket, h]) * LOG2E


def _bias_tiles(rel_bias, tk, tq):
    r = np.arange(tk)[:, None]
    c = np.arange(tq)[None, :]
    far_bucket = int(_rel_bucket(np.array(-tk)))
    assert np.all(_rel_bucket(-tk - np.arange(1, tk + tq)) == far_bucket)
    buckets = jnp.asarray(np.stack([_rel_bucket(r - tk - c), _rel_bucket(r - c)]).astype(np.int32))
    return pl.pallas_call(
        functools.partial(_bias_kernel, far_bucket=far_bucket),
        out_shape=jax.ShapeDtypeStruct((DSA_HEADS, 2, tk, tq), F32),
        grid=(DSA_HEADS,),
        in_specs=[
            pl.BlockSpec((2, tk, tq), lambda h: (0, 0, 0)),
            pl.BlockSpec(memory_space=pltpu.SMEM),
        ],
        out_specs=pl.BlockSpec((1, 2, tk, tq), lambda h: (h, 0, 0, 0)),
        compiler_params=_params(1),
        name="t5_bias_tiles",
    )(buckets, rel_bias)


def _rope_group(g, c_ref, s1_ref, s2_ref):
    return (g * c_ref[...] + pltpu.roll(g, LANES - ROPE_DIM // 2, 1) * s1_ref[...]
            + pltpu.roll(g, ROPE_DIM // 2, 1) * s2_ref[...])


def _proj_kernel(x_ref, shift_ref, scale_ref, gn_ref, wa_ref, gq_ref, wuq_ref, gkv_ref, wkn_ref, wv_ref,
                 qc_ref, qs1_ref, qs2_ref, kc_ref, ks1_ref, ks2_ref, *rest, emit_kv, tk, n_aliased):
    ckv_o, kr_o, kb_o, vb_o, ki_o, q_o, qb_o, qi_o, wit_o, *kv_outs = rest[n_aliased:]

    def put(cache_o, value):
        for n in range(cache_o.shape[0]):
            cache_o[n, 0] = value

    x = x_ref[0]
    h = _rms(x, gn_ref[...]) * (1.0 + scale_ref[0]) + shift_ref[0]
    h16 = h.astype(BF16)

    def zcols(a):
        return _dot(h16, wa_ref[:, a[0]:a[1]])

    cqn = _rms(zcols(A_CQ), gq_ref[...]).astype(BF16)
    q = _dot(cqn, wuq_ref[...])
    for hd in range(MLA_HEADS):
        sl = slice(hd * LANES, (hd + 1) * LANES)
        q_hd = _rope_group(q[:, sl], qc_ref, qs1_ref, qs2_ref)
        if emit_kv:
            q_o[0, sl, :] = q_hd.T.astype(BF16)
        else:
            q_o[0, :, sl] = q_hd.astype(BF16)

    ckvn = _rms(zcols(A_CKV), gkv_ref[...])
    put(ckv_o, ckvn)
    krg = _rope_group(zcols(A_KR), kc_ref, ks1_ref, ks2_ref)
    put(kr_o, krg[:, 0:ROPE_DIM])

    sm = zcols(A_SM)
    ki = sm[:, 0:IDX_DIM]
    put(ki_o, ki)
    wit_o[0] = sm.T[IDX_DIM:IDX_DIM + IDX_HEADS, :] * IDX_W_SCALE
    qb = zcols(A_QB) * (DSA_SCALE * LOG2E)
    qi = zcols(A_QI)
    qb_o[0] = (qb.T if emit_kv else qb).astype(BF16)
    qi_o[0] = (qi.T if emit_kv else qi).astype(BF16)
    kb = zcols(A_KB)
    vb = zcols(A_VB)
    put(kb_o, kb)
    put(vb_o, vb)

    if emit_kv:
        kcat_o, vat_o, kb16_o, vbt_o, ki16_o = kv_outs
        ckv16 = ckvn.astype(BF16)
        kn = _dot(ckv16, wkn_ref[...])
        lane = lax.broadcasted_iota(jnp.int32, krg.shape, 1)
        k_add = jnp.where(lane >= NOPE_DIM, krg, 0.0)
        for hd in range(MLA_HEADS):
            sl = slice(hd * LANES, (hd + 1) * LANES)
            kcat_o[0, :, sl] = (kn[:, sl] + k_add).astype(BF16)
        va = _dot(ckv16, wv_ref[...])
        for g in range(DSA_KV_HEADS):
            kb16_o[0, g] = kb[:, g * DSA_HEAD_DIM:(g + 1) * DSA_HEAD_DIM].astype(BF16)
        ki16_o[0] = ki.astype(BF16)
        for c in range(x.shape[0] // tk):
            rows = slice(c * tk, (c + 1) * tk)
            vat_o[0, c] = va[rows, :].T.astype(BF16)
            vbt_o[0, c] = vb[rows, :].T.astype(BF16)


def _proj(x, shift, scale, gn, wa, gq, wuq, gkv, wkn, wv, q_tabs, k_tabs, caches, *, layer, depth, tm, tk, emit_kv):
    nb, t, _ = x.shape
    r = shift.shape[1]
    rb = 1 if r == 1 else tm
    grid = (nb, t // tm)
    tok = lambda w: pl.BlockSpec((1, tm, w), lambda b, i: (b, i, 0))
    if caches is None:
        stacked = lambda w: pl.BlockSpec((depth, 1, tm, w), lambda b, i: (0, b, i, 0))
    else:
        stacked = lambda w: pl.BlockSpec((1, 1, tm, w), lambda b, i: (layer, b, i, 0))
    mod = pl.BlockSpec((1, rb, D_MODEL), (lambda b, i: (b, 0, 0)) if r == 1 else (lambda b, i: (b, i, 0)))
    const = lambda a: pl.BlockSpec(a.shape, lambda b, i: (0,) * a.ndim)
    tab = pl.BlockSpec((tm, LANES), lambda b, i: (i, 0))
    in_specs = [tok(D_MODEL), mod, mod, const(gn), const(wa), const(gq), const(wuq), const(gkv), const(wkn),
                const(wv)] + [tab] * 6
    operands = [x, shift, scale, gn, wa, gq, wuq, gkv, wkn, wv, *q_tabs, *k_tabs]
    cache_widths = (KV_LORA, ROPE_DIM, KV_DIM, KV_DIM, IDX_DIM)
    aliases = {}
    if caches is not None:
        aliases = {len(operands) + n: n for n in range(len(cache_widths))}
        in_specs += [pl.BlockSpec(memory_space=pl.ANY)] * len(cache_widths)
        operands += list(caches)
    tok_t = lambda w: pl.BlockSpec((1, w, tm), lambda b, i: (b, 0, i))
    q_shape = (lambda w: (nb, w, t)) if emit_kv else (lambda w: (nb, t, w))
    q_spec = tok_t if emit_kv else tok
    q_widths = (QK_PAD, W_B, IDX_HEADS * IDX_DIM)
    out_shape = ([jax.ShapeDtypeStruct((depth, nb, t, w), F32) for w in cache_widths]
                 + [jax.ShapeDtypeStruct(q_shape(w), BF16) for w in q_widths]
                 + [jax.ShapeDtypeStruct((nb, IDX_HEADS, t), F32)])
    out_specs = [stacked(w) for w in cache_widths] + [q_spec(w) for w in q_widths] + [tok_t(IDX_HEADS)]
    if emit_kv:
        nkt = tm // tk
        out_shape += [
            jax.ShapeDtypeStruct((nb, t, QK_PAD), BF16),
            jax.ShapeDtypeStruct((nb, t // tk, W_A, tk), BF16),
            jax.ShapeDtypeStruct((nb, DSA_KV_HEADS, t, DSA_HEAD_DIM), BF16),
            jax.ShapeDtypeStruct((nb, t // tk, KV_DIM, tk), BF16),
            jax.ShapeDtypeStruct((nb, t, IDX_DIM), BF16),
        ]
        out_specs += [tok(QK_PAD), pl.BlockSpec((1, nkt, W_A, tk), lambda b, i: (b, i, 0, 0)),
                      pl.BlockSpec((1, DSA_KV_HEADS, tm, DSA_HEAD_DIM), lambda b, i: (b, 0, i, 0)),
                      pl.BlockSpec((1, nkt, KV_DIM, tk), lambda b, i: (b, i, 0, 0)), tok(IDX_DIM)]
    return pl.pallas_call(
        functools.partial(_proj_kernel, emit_kv=emit_kv, tk=tk, n_aliased=len(aliases)),
        out_shape=out_shape, grid=grid, in_specs=in_specs, out_specs=out_specs, input_output_aliases=aliases,
        compiler_params=_params(2), name="proj_kv" if emit_kv else "proj_q",
    )(*operands)


class _Softmax:
    ONES_ROWS = 16

    def __init__(self, s_scr, mx_scr, m_scr, acc_scr):
        self.s, self.mx, self.m, self.acc = s_scr, mx_scr, m_scr, acc_scr
        self.n_heads = acc_scr.shape[0]
        self.dv = acc_scr.shape[1] - self.ONES_ROWS

    @classmethod
    def scratch(cls, n_heads, dv, tk, tq):
        stat = pltpu.VMEM((n_heads, SUBLANES, tq), F32)
        return [pltpu.VMEM((n_heads, 2, tk, tq), F32), stat, stat, pltpu.VMEM((n_heads, dv + cls.ONES_ROWS, tq), F32)]

    def reset(self):
        self.mx[...] = jnp.full(self.mx.shape, NEG, F32)
        self.m[...] = jnp.full(self.m.shape, NEG, F32)
        self.acc[...] = jnp.zeros(self.acc.shape, F32)

    def produce(self, hd, slot, s):
        self.s[hd, slot] = s
        self.mx[hd] = jnp.maximum(self.mx[hd], _rows8(s, jnp.maximum, ways=4))

    def consume(self, hd, slot, vt):
        m_old = self.m[hd, 0:1, :]
        m_new = jnp.max(self.mx[hd], axis=0, keepdims=True)
        alpha = jnp.exp2(m_old - m_new)
        p = jnp.exp2(self.s[hd, slot] - m_new)
        self.m[hd] = jnp.broadcast_to(m_new, self.m.shape[1:])
        vt1 = jnp.concatenate([vt, jnp.ones((self.ONES_ROWS, vt.shape[1]), vt.dtype)], axis=0)
        self.acc[hd] = alpha * self.acc[hd] + _dot(vt1, p.astype(BF16))

    def run(self, last, logits, values):
        def step(t_consume, t_produce, kind, slot_produce):
            for hd in range(self.n_heads):
                if t_consume is not None:
                    self.consume(hd, 1 - slot_produce, values(hd, t_consume))
                if t_produce is not None:
                    self.produce(hd, slot_produce, logits(hd, t_produce, kind))

        when = lambda cond, fn: pl.when(cond)(fn)
        self.reset()
        step(None, last, 2, 0)
        when(last >= 1, lambda: step(last, last - 1, 1, 1))
        n_far = last - 1

        def far_pair(n, _):
            t = last - 2 - 2 * n
            step(t + 1, t, 0, 0)
            step(t, t - 1, 0, 1)
            return 0

        lax.fori_loop(0, n_far // 2, far_pair, 0)
        odd = (n_far >= 1) & (n_far % 2 == 1)
        when(odd, lambda: step(1, 0, 0, 0))
        in_slot0 = (last == 0) | odd
        when(in_slot0, lambda: step(0, None, None, 1))
        when(jnp.logical_not(in_slot0), lambda: step(0, None, None, 0))

    def finish(self, o_ref):
        o_t = jnp.concatenate([self.acc[hd, 0:self.dv, :] / self.acc[hd, self.dv:self.dv + 1, :]
                               for hd in range(self.n_heads)], axis=0)
        o_ref[0] = o_t.T


def _diagonal_tile_visible(t):
    r = lax.broadcasted_iota(jnp.int32, (t, t), 0)
    c = lax.broadcasted_iota(jnp.int32, (t, t), 1)
    return (r // CHUNK) <= (c // CHUNK)


def _mla_kernel(q_ref, k_ref, vt_ref, o_ref, *scratch, t):
    softmax = _Softmax(*scratch)

    def logits(hd, j, kind):
        sl = slice(hd * LANES, (hd + 1) * LANES)
        s = _dot(k_ref[0, _key_rows(j, t), sl], q_ref[0, sl, :])
        return s + jnp.where(_diagonal_tile_visible(t), 0.0, NEG) if kind == 2 else s

    softmax.run(pl.program_id(1), logits, lambda hd, j: vt_ref[0, j, hd * V_DIM:(hd + 1) * V_DIM, :])
    softmax.finish(o_ref)


def _mla(q, k, vt, *, t):
    nb, _, seq = q.shape
    return pl.pallas_call(
        functools.partial(_mla_kernel, t=t),
        out_shape=jax.ShapeDtypeStruct((nb, seq, W_A), F32), grid=(nb, seq // t),
        in_specs=[pl.BlockSpec((1, QK_PAD, t), lambda b, i: (b, 0, i)),
                  pl.BlockSpec((1, seq, QK_PAD), lambda b, i: (b, 0, 0)),
                  pl.BlockSpec((1, seq // t, W_A, t), lambda b, i: (b, 0, 0, 0))],
        out_specs=pl.BlockSpec((1, t, W_A), lambda b, i: (b, i, 0)),
        scratch_shapes=_Softmax.scratch(MLA_HEADS, V_DIM, t, t),
        compiler_params=_params(2), name="mla_causal",
    )(q, k, vt)


def _score_stats_init(tq):
    return (jnp.full((1, tq), jnp.inf, F32), jnp.full((1, tq), -jnp.inf, F32),
            jnp.zeros((SUBLANES, tq), F32), jnp.zeros((SUBLANES, tq), F32))


def _score_stats(stats, t, visible=None):
    lo, hi, c_z, c_zp = stats
    t_min = t if visible is None else jnp.where(visible, t, jnp.inf)
    return (jnp.minimum(lo, jnp.min(t_min, axis=0, keepdims=True)), jnp.maximum(hi, jnp.max(t, axis=0, keepdims=True)),
            c_z + _rows8(jnp.where(t >= 0.0, 1.0, 0.0), jnp.add), c_zp + _rows8(jnp.where(t > 0.0, 1.0, 0.0), jnp.add))


def _select_topk(idx_scr, n_used, stats, n_vis, kf):
    _, tk, tq = idx_scr.shape

    def count(pred):
        def body(j, c):
            return c + _rows8(jnp.where(pred(idx_scr[j]), 1.0, 0.0), jnp.add)
        return jnp.sum(lax.fori_loop(0, n_used, body, jnp.zeros((SUBLANES, tq), F32)), axis=0, keepdims=True)

    lo, hi, c_z, c_zp = stats
    c_z = jnp.sum(c_z, axis=0, keepdims=True)
    c_zp = jnp.sum(c_zp, axis=0, keepdims=True)
    hi = hi + (jnp.abs(hi) * 2.0 ** -20 + 1e-30)
    zero_up = (c_z >= kf) & (lo < 0.0)
    zero_tie = jnp.where((c_z >= kf) & (c_zp < kf), 1.0, 0.0)
    c_lo = jnp.where(zero_up, c_z, n_vis)
    lo = jnp.where(zero_up, 0.0, lo)
    hi = jnp.where((c_z < kf) & (hi > 0.0), 0.0, hi)

    def settled(lo, hi, c_lo):
        mid = 0.5 * lo + 0.5 * hi
        return jnp.where((c_lo <= kf) | (mid <= lo) | (mid >= hi), 1.0, 0.0)

    def bis_cond(st):
        _, _, _, done, it = st
        return jnp.logical_and(it < MAX_BISECT, jnp.min(done) < 0.5)

    def bis_body(st):
        lo, hi, c_lo, done, it = st
        for _ in range(BISECT_PER_CHECK):
            mid = 0.5 * lo + 0.5 * hi
            c = count(lambda t: t >= mid)
            live = done < 0.5
            up = live & (c >= kf)
            dn = live & (c < kf)
            lo = jnp.where(up, mid, lo)
            c_lo = jnp.where(up, c, c_lo)
            hi = jnp.where(dn, mid, hi)
            done = jnp.maximum(done, settled(lo, hi, c_lo))
        return lo, hi, c_lo, done, it + BISECT_PER_CHECK

    done = jnp.maximum(settled(lo, hi, c_lo), zero_tie)
    lo, hi, c_lo, _, _ = lax.while_loop(bis_cond, bis_body, (lo, hi, c_lo, done, jnp.int32(0)))
    any_tie = jnp.max(jnp.where(c_lo > kf, 1.0, 0.0)) > 0.5

    @pl.when(jnp.logical_not(any_tie))
    def _():
        def body(j, _):
            idx_scr[j] = jnp.where(idx_scr[j] >= lo, 0.0, NEG)
            return 0
        lax.fori_loop(0, n_used, body, 0)

    @pl.when(any_tie)
    def _():
        need = kf - count(lambda t: t > lo)
        r = lax.broadcasted_iota(jnp.int32, (tk, tk), 0)
        c = lax.broadcasted_iota(jnp.int32, (tk, tk), 1)
        before = jnp.where(c < r, 1.0, 0.0).astype(BF16)

        def body(j, seen):
            t = idx_scr[j]
            eq = jnp.where(t == lo, 1.0, 0.0)
            rank = _dot(before, eq.astype(BF16)) + seen
            take = (t > lo) | ((t == lo) & (rank < need))
            idx_scr[j] = jnp.where(take, 0.0, NEG)
            return seen + jnp.sum(_rows8(eq, jnp.add), axis=0, keepdims=True)
        lax.fori_loop(0, n_used, body, jnp.zeros((1, tq), F32))


def _dsa_kernel(qi_ref, wit_ref, ki_ref, qb_ref, kb_ref, vbt_ref, bias_ref, o_ref, idx_scr, *scratch, t, topk):
    tq = tk = t
    last = pl.program_id(1)
    n_used = last + 1
    visible = _diagonal_tile_visible(t)

    wit = wit_ref[0]

    def idx_tile(j):
        k_j = ki_ref[0, _key_rows(j, tk), :]
        acc = jnp.zeros((tk, tq), F32)
        for hd in range(IDX_HEADS):
            s = _dot(k_j, qi_ref[0, hd * IDX_DIM:(hd + 1) * IDX_DIM, :])
            acc = acc + jnp.maximum(s, 0.0) * wit[hd:hd + 1, :]
        return acc

    def idx_body(j, stats):
        t = idx_tile(j)
        idx_scr[j] = t
        return _score_stats(stats, t)

    def idx_pair(n, stats):
        return idx_body(2 * n + 1, idx_body(2 * n, stats))

    stats = lax.fori_loop(0, last // 2, idx_pair, _score_stats_init(tq))
    stats = lax.cond(last % 2 == 1, lambda st: idx_body(last - 1, st), lambda st: st, stats)
    t_last = jnp.where(visible, idx_tile(last), -jnp.inf)
    idx_scr[last] = t_last
    stats = _score_stats(stats, t_last, visible)
    lane = lax.broadcasted_iota(jnp.int32, (1, tq), 1)
    n_vis = last * tk + (lane // CHUNK + 1) * CHUNK
    _select_topk(idx_scr, n_used, stats, n_vis.astype(F32), float(topk))

    softmax = _Softmax(*scratch)
    kv_lanes = lambda hd: slice((hd // DSA_GROUP) * DSA_HEAD_DIM, (hd // DSA_GROUP + 1) * DSA_HEAD_DIM)

    def logits(hd, t, kind):
        s = _dot(kb_ref[0, hd // DSA_GROUP, _key_rows(t, tk), :], qb_ref[0, hd * DSA_HEAD_DIM:(hd + 1) * DSA_HEAD_DIM, :])
        return s + idx_scr[t] if kind == 0 else s + (idx_scr[t] + bias_ref[hd, kind - 1])

    softmax.run(last, logits, lambda hd, t: vbt_ref[0, t, kv_lanes(hd), :])
    softmax.finish(o_ref)


def _dsa(qi, wit, ki, qb, kb, vbt, bias, *, t, topk):
    nb, _, seq = qb.shape
    n_tiles = seq // t
    return pl.pallas_call(
        functools.partial(_dsa_kernel, t=t, topk=topk),
        out_shape=jax.ShapeDtypeStruct((nb, seq, W_B), F32), grid=(nb, n_tiles),
        in_specs=[pl.BlockSpec((1, IDX_HEADS * IDX_DIM, t), lambda b, i: (b, 0, i)),
                  pl.BlockSpec((1, IDX_HEADS, t), lambda b, i: (b, 0, i)),
                  pl.BlockSpec((1, seq, IDX_DIM), lambda b, i: (b, 0, 0)),
                  pl.BlockSpec((1, W_B, t), lambda b, i: (b, 0, i)),
                  pl.BlockSpec((1, DSA_KV_HEADS, seq, DSA_HEAD_DIM), lambda b, i: (b, 0, 0, 0)),
                  pl.BlockSpec((1, n_tiles, KV_DIM, t), lambda b, i: (b, 0, 0, 0)),
                  pl.BlockSpec(bias.shape, lambda b, i: (0, 0, 0, 0))],
        out_specs=pl.BlockSpec((1, t, W_B), lambda b, i: (b, i, 0)),
        scratch_shapes=[pltpu.VMEM((n_tiles, t, t), F32)] + _Softmax.scratch(DSA_HEADS, DSA_HEAD_DIM, t, t),
        compiler_params=_params(2), name="dsa_causal",
    )(qi, wit, ki, qb, kb, vbt, bias)


def _two_pass_softmax(n_tiles, logits, values, s_scr, dv):
    mx = None
    for t in range(n_tiles):
        s = logits(t)
        s_scr[t] = s
        part = _rows8(s, jnp.maximum, ways=4)
        mx = part if mx is None else jnp.maximum(mx, part)
    m = jnp.max(mx, axis=0, keepdims=True)
    acc = None
    for t in range(n_tiles):
        p = jnp.exp2(s_scr[t] - m).astype(BF16)
        vt = values(t)
        part = _dot(jnp.concatenate([vt, jnp.ones((_Softmax.ONES_ROWS, vt.shape[1]), BF16)], axis=0), p)
        acc = part if acc is None else acc + part
    return acc[0:dv] / acc[dv:dv + 1]


def _mla_cached_kernel(q_ref, ckv_c_ref, kr_c_ref, ckv_n_ref, kr_n_ref, wuk_ref, wuv_ref, sel_ref, o_ref, s_scr, *,
                       ts, tk, n_cache, n_valid_last):
    lanes = MLA_HEADS * ts
    q_lat, q_rope = [], []
    for hd in range(MLA_HEADS):
        qg = q_ref[0, :, hd * LANES:(hd + 1) * LANES]
        q_lat.append(_dot(qg, wuk_ref[hd]).astype(BF16))
        q_rope.append(_dot(qg, sel_ref[...]).astype(BF16))
    q_lat = jnp.concatenate(q_lat, axis=0)
    q_rope = jnp.concatenate(q_rope, axis=0)

    def latent(t):
        return ckv_c_ref[0, t * tk:(t + 1) * tk, :] if t < n_cache else ckv_n_ref[0]

    def logits(t):
        kr = kr_c_ref[0, t * tk:(t + 1) * tk, :] if t < n_cache else kr_n_ref[0]
        s = _dot_nt(latent(t).astype(BF16), q_lat) + _dot_nt(kr.astype(BF16), q_rope)
        if t == n_cache:
            s = s + jnp.where(lax.broadcasted_iota(jnp.int32, s.shape, 0) < n_valid_last, 0.0, NEG)
        return s

    o_lat = _two_pass_softmax(n_cache + 1, logits, lambda t: latent(t).T.astype(BF16), s_scr, KV_LORA)
    o_lat = o_lat.astype(BF16)
    head_of_lane = lax.broadcasted_iota(jnp.int32, (V_DIM, lanes), 1) // ts
    out = jnp.zeros((V_DIM, lanes), F32)
    for hd in range(MLA_HEADS):
        out = jnp.where(head_of_lane == hd, _dot(wuv_ref[hd], o_lat), out)
    o_ref[0] = out


def _mla_cached(q, ckv_cache, kr_cache, ckv_new, kr_new, wuk, wuv, sel, *, layer, tk, n_valid_last):
    nb, ts, _ = q.shape
    past = ckv_cache.shape[2]
    lanes = MLA_HEADS * ts
    per_b = lambda a: pl.BlockSpec((1,) + a.shape[1:], lambda b: (b,) + (0,) * (a.ndim - 1))
    const = lambda a: pl.BlockSpec(a.shape, lambda b: (0,) * a.ndim)
    return pl.pallas_call(
        functools.partial(_mla_cached_kernel, ts=ts, tk=tk, n_cache=past // tk, n_valid_last=n_valid_last),
        out_shape=jax.ShapeDtypeStruct((nb, V_DIM, lanes), F32), grid=(nb,),
        in_specs=[per_b(q), _layer_rows(ckv_cache, layer), _layer_rows(kr_cache, layer), per_b(ckv_new),
                  per_b(kr_new), const(wuk), const(wuv), const(sel)],
        out_specs=pl.BlockSpec((1, V_DIM, lanes), lambda b: (b, 0, 0)),
        scratch_shapes=[pltpu.VMEM((past // tk + 1, tk, lanes), F32)],
        compiler_params=_params(1), name="mla_cached",
    )(q, ckv_cache, kr_cache, ckv_new, kr_new, wuk, wuv, sel)


def _dsa_cached_kernel(qi_ref, w_ref, ki_c_ref, ki_n_ref, qb_ref, kb_c_ref, kb_n_ref, vb_c_ref, vb_n_ref, bias_ref,
                       o_ref, idx_scr, s_scr, *, ts, tk, n_cache, n_valid_last, topk):
    n_tiles = n_cache + 1
    lanes = DSA_GROUP * ts
    tile = lambda c_ref, n_ref, t: c_ref[0, t * tk:(t + 1) * tk, :] if t < n_cache else n_ref[0]
    visible = lax.broadcasted_iota(jnp.int32, (tk, lanes), 0) < n_valid_last

    stats = _score_stats_init(lanes)
    for t in range(n_tiles):
        s = _dot_nt(tile(ki_c_ref, ki_n_ref, t).astype(BF16), qi_ref[0])
        r = jnp.maximum(s, 0.0) * w_ref[0]
        x = r[:, 0:lanes]
        for blk in range(1, IDX_HEADS * ts // lanes):
            x = x + r[:, blk * lanes:(blk + 1) * lanes]
        shift = lanes // 2
        while shift >= ts:
            x = x + pltpu.roll(x, shift, 1)
            shift //= 2
        if t == n_cache:
            x = jnp.where(visible, x, -jnp.inf)
        idx_scr[t] = x
        stats = _score_stats(stats, x, visible if t == n_cache else None)
    n_vis = jnp.full((1, lanes), float(n_cache * tk + n_valid_last), F32)
    _select_topk(idx_scr, n_tiles, stats, n_vis, float(topk))

    for g in range(DSA_KV_HEADS):
        def logits(t, g=g):
            s = _dot_nt(tile(kb_c_ref, kb_n_ref, t).astype(BF16), qb_ref[0, g]) + idx_scr[t]
            return s + bias_ref[g, t - (n_tiles - 2)] if t >= n_tiles - 2 else s

        def values(t, g=g):
            return tile(vb_c_ref, vb_n_ref, t).T[g * DSA_HEAD_DIM:(g + 1) * DSA_HEAD_DIM, :].astype(BF16)

        o_ref[0, g] = _two_pass_softmax(n_tiles, logits, values, s_scr, DSA_HEAD_DIM)


def _layer_rows(cache, layer):
    return pl.BlockSpec((None, 1) + cache.shape[2:], lambda b: (layer, b, 0, 0))


def _dsa_cached(qi, w, ki_cache, ki_new, qb, kb_cache, kb_new, vb_cache, vb_new, bias, *, layer, ts, tk,
                n_valid_last, topk):
    nb = qi.shape[0]
    past = ki_cache.shape[2]
    n_tiles = past // tk + 1
    lanes = DSA_GROUP * ts
    per_b = lambda a: pl.BlockSpec((1,) + a.shape[1:], lambda b: (b,) + (0,) * (a.ndim - 1))
    cached = lambda a: _layer_rows(a, layer)
    return pl.pallas_call(
        functools.partial(_dsa_cached_kernel, ts=ts, tk=tk, n_cache=past // tk, n_valid_last=n_valid_last, topk=topk),
        out_shape=jax.ShapeDtypeStruct((nb, DSA_KV_HEADS, DSA_HEAD_DIM, lanes), F32), grid=(nb,),
        in_specs=[per_b(qi), per_b(w), cached(ki_cache), per_b(ki_new), per_b(qb), cached(kb_cache), per_b(kb_new),
                  cached(vb_cache), per_b(vb_new), pl.BlockSpec(bias.shape, lambda b: (0,) * bias.ndim)],
        out_specs=pl.BlockSpec((1, DSA_KV_HEADS, DSA_HEAD_DIM, lanes), lambda b: (b, 0, 0, 0)),
        scratch_shapes=[pltpu.VMEM((n_tiles, tk, lanes), F32), pltpu.VMEM((n_tiles, tk, lanes), F32)],
        compiler_params=_params(1), name="dsa_cached",
    )(qi, w, ki_cache, ki_new, qb, kb_cache, kb_new, vb_cache, vb_new, bias)


def _out_kernel(x_ref, shift_ref, scale_ref, gate_ref, gn_ref, oa_ref, ob_ref, wg_ref, woa_ref, wob_ref, wout_ref,
                gf_ref, o_ref, *, final):
    x = x_ref[0]
    h16 = (_rms(x, gn_ref[...]) * (1.0 + scale_ref[0]) + shift_ref[0]).astype(BF16)

    def gcols(a):
        return _dot(h16, wg_ref[:, a[0]:a[1]])

    ga = gcols(G_GA)
    ya = _dot((oa_ref[0] * (ga * _sigmoid(ga))).astype(BF16), woa_ref[...])
    gb = gcols(G_GB)
    yb = _dot((ob_ref[0] * (gb * _sigmoid(gb))).astype(BF16), wob_ref[...])
    m = _sigmoid(gcols(G_MA)) * ya + _sigmoid(gcols(G_MB)) * yb
    y = x + gate_ref[0] * _dot(m.astype(BF16), wout_ref[...])
    o_ref[0] = _rms(y, gf_ref[...]) if final else y


def _out(x, shift, scale, gate, gn, oa, ob, wg, woa, wob, wout, gf, *, tm, final):
    nb, t, _ = x.shape
    r = shift.shape[1]
    rb = 1 if r == 1 else tm
    tok = lambda w: pl.BlockSpec((1, tm, w), lambda b, i: (b, i, 0))
    mod = pl.BlockSpec((1, rb, D_MODEL), (lambda b, i: (b, 0, 0)) if r == 1 else (lambda b, i: (b, i, 0)))
    const = lambda a: pl.BlockSpec(a.shape, lambda b, i: (0,) * a.ndim)
    return pl.pallas_call(
        functools.partial(_out_kernel, final=final),
        out_shape=jax.ShapeDtypeStruct(x.shape, F32), grid=(nb, t // tm),
        in_specs=[tok(D_MODEL), mod, mod, mod, const(gn), tok(W_A), tok(W_B), const(wg), const(woa), const(wob),
                  const(wout), const(gf)],
        out_specs=tok(D_MODEL), compiler_params=_params(2), name="out_final" if final else "out_mix",
    )(x, shift, scale, gate, gn, oa, ob, wg, woa, wob, wout, gf)


def _pack_weights(w_in, w_uq, w_ukv, w_oa, w_ob, w_out):
    depth = w_in.shape[0]
    offs = np.concatenate([[0], np.cumsum(SPLIT_SIZES)])
    cq, ckv, kr, ga, qb, kb, vb, qi, ki, wi, gb, ma, mb = [w_in[:, :, offs[n]:offs[n + 1]] for n in range(13)]
    z = lambda w: jnp.zeros((depth, D_MODEL, w), F32)
    krg = jnp.concatenate([kr, z(NOPE_DIM - ROPE_DIM), kr, z(LANES - NOPE_DIM - ROPE_DIM)], axis=-1)
    sm = jnp.concatenate([ki, wi, z(LANES - IDX_DIM - IDX_HEADS)], axis=-1)
    wa = jnp.concatenate([cq, ckv, krg, sm, qb, kb, vb, qi], axis=-1).astype(BF16)
    wg = jnp.concatenate([ga, gb, ma, mb], axis=-1).astype(BF16)
    qd = NOPE_DIM + ROPE_DIM
    wuq = jnp.pad(w_uq.reshape(depth, Q_LORA, MLA_HEADS, qd), ((0, 0), (0, 0), (0, 0), (0, LANES - qd)))
    wuq = wuq.reshape(depth, Q_LORA, QK_PAD).astype(BF16)
    ukv = w_ukv.reshape(depth, KV_LORA, MLA_HEADS, NOPE_DIM + V_DIM)
    wkn = jnp.pad(ukv[..., :NOPE_DIM], ((0, 0), (0, 0), (0, 0), (0, LANES - NOPE_DIM)))
    wkn = wkn.reshape(depth, KV_LORA, QK_PAD).astype(BF16)
    wv = ukv[..., NOPE_DIM:].reshape(depth, KV_LORA, W_A).astype(BF16)
    wuk = jnp.pad(ukv[..., :NOPE_DIM].transpose(0, 2, 3, 1), ((0, 0), (0, 0), (0, LANES - NOPE_DIM), (0, 0)))
    wuv = ukv[..., NOPE_DIM:].transpose(0, 2, 3, 1)
    return (wa, wg, wuq, wkn, wv, w_oa.astype(BF16), w_ob.astype(BF16), w_out.astype(BF16), wuk.astype(BF16),
            wuv.astype(BF16))


def _rope_tables(pos):
    half = ROPE_DIM // 2
    inv = ROPE_THETA ** (-jnp.arange(half, dtype=F32) / half)
    ang = pos.astype(F32)[:, None] * inv
    cos, sin = jnp.cos(ang), jnp.sin(ang)
    n = pos.shape[0]
    z = lambda w: jnp.zeros((n, w), F32)
    pad = LANES - NOPE_DIM - ROPE_DIM
    q_c = jnp.concatenate([jnp.ones((n, NOPE_DIM), F32), cos, cos, z(pad)], axis=-1) * (MLA_SCALE * LOG2E)
    q_s1 = jnp.concatenate([z(NOPE_DIM), -sin, z(half), z(pad)], axis=-1) * (MLA_SCALE * LOG2E)
    q_s2 = jnp.concatenate([z(NOPE_DIM), z(half), sin, z(pad)], axis=-1) * (MLA_SCALE * LOG2E)
    gap = NOPE_DIM - ROPE_DIM
    k_c = jnp.concatenate([cos, cos, z(gap), cos, cos, z(pad)], axis=-1)
    k_s1 = jnp.concatenate([-sin, z(half), z(gap), -sin, z(half), z(pad)], axis=-1)
    k_s2 = jnp.concatenate([z(half), sin, z(gap), z(half), sin, z(pad)], axis=-1)
    return (q_c, q_s1, q_s2), (k_c, k_s1, k_s2)


def _rope_selector():
    p = np.zeros((LANES, ROPE_DIM), np.float32)
    p[NOPE_DIM + np.arange(ROPE_DIM), np.arange(ROPE_DIM)] = 1.0
    return jnp.asarray(p, dtype=BF16)


def kernel(x_prompt, x_sample, c_prompt, c_sample, cache_ckv, cache_krope, cache_kb, cache_vb, cache_kidx, w_ada,
           b_ada, g_norm, w_in, g_qnorm, w_uq, g_kvnorm, w_ukv, w_oa, w_ob, w_out, rel_bias, g_final):
    depth = w_in.shape[0]
    bp, t_p, _ = x_prompt.shape
    bs, t_s, _ = x_sample.shape
    past = cache_ckv.shape[2]
    topk_p = min(TOPK_MAX, t_p // 4)
    topk_s = min(TOPK_MAX, (past + t_s) // 4)
    assert t_p % TM == 0 and TM % TK == 0 and past % TK == 0 and t_s <= TK and t_s % 8 == 0
    assert (DSA_GROUP * t_s) % LANES == 0 and (DSA_GROUP * t_s) & (DSA_GROUP * t_s - 1) == 0
    n_s = bs * t_s

    wa, wg, wuq, wkn, wv, woa, wob, wout, wuk, wuv = _pack_weights(w_in, w_uq, w_ukv, w_oa, w_ob, w_out)
    rope_sel = _rope_selector()
    cache_kb2 = cache_kb.reshape(depth, bs, past, KV_DIM)
    cache_vb2 = cache_vb.reshape(depth, bs, past, KV_DIM)
    mod = _modulation(jnp.concatenate([c_prompt, c_sample], axis=0), w_ada, b_ada)
    mod = mod.reshape(depth, bp + bs, 3, 1, D_MODEL)
    tabs_p = _rope_tables(jnp.arange(t_p, dtype=jnp.int32))
    tabs_s = _rope_tables(jnp.tile(past + jnp.arange(t_s, dtype=jnp.int32), bs))
    bias_p = _bias_tiles(rel_bias, TK, TK)
    bias_s = _bias_tiles(rel_bias, TK, t_s).reshape(DSA_KV_HEADS, DSA_GROUP, 2, TK, t_s)
    bias_s = bias_s.transpose(0, 2, 3, 1, 4).reshape(DSA_KV_HEADS, 2, TK, DSA_GROUP * t_s)
    row = lambda v: v.reshape(1, -1)

    xp = x_prompt
    xs = x_sample.reshape(1, n_s, D_MODEL)
    caches_p = caches_s = None
    for l in range(depth):
        final = l == depth - 1
        gn, gq, gkv, gf = row(g_norm[l]), row(g_qnorm[l]), row(g_kvnorm[l]), row(g_final)

        shift, scale, gate = (mod[l, :bp, n] for n in range(3))
        *caches_p, q, qb, qi, wit, kcat, vat, kb16, vbt, ki16 = _proj(
            xp, shift, scale, gn, wa[l], gq, wuq[l], gkv, wkn[l], wv[l], *tabs_p, caches_p, layer=l, depth=depth,
            tm=TM, tk=TK, emit_kv=True)
        oa = _mla(q, kcat, vat, t=TK)
        ob = _dsa(qi, wit, ki16, qb, kb16, vbt, bias_p, t=TK, topk=topk_p)
        xp = _out(xp, shift, scale, gate, gn, oa, ob, wg[l], woa[l], wob[l], wout[l], gf, tm=TM, final=final)

        shift, scale, gate = (jnp.repeat(mod[l, bp:, n, 0], t_s, axis=0)[None] for n in range(3))
        *caches_s, q, qb, qi, wit = _proj(
            xs, shift, scale, gn, wa[l], gq, wuq[l], gkv, wkn[l], wv[l], *tabs_s, caches_s, layer=l, depth=depth,
            tm=n_s, tk=TK, emit_kv=False)
        per_seq = lambda a: a.reshape((bs, t_s) + a.shape[2:])
        ckv, kr, kb, vb, ki = (per_seq(a[l]) for a in caches_s)
        q, qb, qi = per_seq(q), per_seq(qb), per_seq(qi)
        new_tile = lambda a: jnp.pad(a, ((0, 0), (0, TK - t_s), (0, 0)))
        oa = _mla_cached(q, cache_ckv, cache_krope, new_tile(ckv), new_tile(kr), wuk[l], wuv[l], rope_sel, layer=l,
                         tk=TK, n_valid_last=t_s)
        oa = oa.reshape(bs, V_DIM, MLA_HEADS, t_s).transpose(0, 3, 2, 1).reshape(1, n_s, W_A)
        qi_l = qi.reshape(bs, t_s, IDX_HEADS, IDX_DIM).transpose(0, 2, 1, 3).reshape(bs, IDX_HEADS * t_s, IDX_DIM)
        w_l = wit.reshape(IDX_HEADS, bs, t_s).transpose(1, 0, 2).reshape(bs, 1, IDX_HEADS * t_s)
        qb_l = qb.reshape(bs, t_s, DSA_KV_HEADS, DSA_GROUP, DSA_HEAD_DIM).transpose(0, 2, 3, 1, 4)
        qb_l = qb_l.reshape(bs, DSA_KV_HEADS, DSA_GROUP * t_s, DSA_HEAD_DIM)
        qb_l = jnp.stack([jnp.pad(qb_l[:, g], ((0, 0), (0, 0), (g * DSA_HEAD_DIM, KV_DIM - (g + 1) * DSA_HEAD_DIM)))
                          for g in range(DSA_KV_HEADS)], axis=1)
        ob = _dsa_cached(qi_l, w_l, cache_kidx, new_tile(ki), qb_l, cache_kb2, new_tile(kb), cache_vb2, new_tile(vb),
                         bias_s, layer=l, ts=t_s, tk=TK, n_valid_last=t_s, topk=topk_s)
        ob = ob.reshape(bs, DSA_KV_HEADS, DSA_HEAD_DIM, DSA_GROUP, t_s).transpose(0, 4, 1, 3, 2).reshape(1, n_s, W_B)
        xs = _out(xs, shift, scale, gate, gn, oa, ob, wg[l], woa[l], wob[l], wout[l], gf, tm=n_s, final=final)

    def cache_outputs(stacks, nb, t):
        ckv, kr, kb, vb, ki = (a.reshape((depth, nb, t, a.shape[-1])) for a in stacks)
        kvh = (depth, nb, t, DSA_KV_HEADS, DSA_HEAD_DIM)
        return ckv, kr, kb.reshape(kvh), vb.reshape(kvh), ki

    return (xp, xs.reshape(bs, t_s, D_MODEL)) + cache_outputs(caches_p, bp, t_p) + cache_outputs(caches_s, bs, t_s)
```

```python
import functools

import numpy as np
import jax
import jax.numpy as jnp
from jax import lax
from jax.experimental import pallas as pl
from jax.experimental.pallas import tpu as pltpu

F32 = jnp.float32
BF16 = jnp.bfloat16

D_MODEL = 1024
CHUNK = 64
EPS = 1e-6
MLA_HEADS = 8
NOPE_DIM = 64
ROPE_DIM = 32
V_DIM = 64
Q_LORA = 384
KV_LORA = 256
ROPE_THETA = 10000.0
MLA_SCALE = (NOPE_DIM + ROPE_DIM) ** -0.5
DSA_HEADS = 8
DSA_KV_HEADS = 2
DSA_GROUP = DSA_HEADS // DSA_KV_HEADS
DSA_HEAD_DIM = 64
DSA_SCALE = DSA_HEAD_DIM ** -0.5
IDX_HEADS = 8
IDX_DIM = 64
IDX_W_SCALE = (IDX_HEADS ** -0.5) * (IDX_DIM ** -0.5)
TOPK_MAX = 256
NUM_BUCKETS = 32
MAX_DISTANCE = 128
W_A = MLA_HEADS * V_DIM
W_B = DSA_HEADS * DSA_HEAD_DIM
SPLIT_SIZES = (Q_LORA, KV_LORA, ROPE_DIM, W_A, W_B, DSA_KV_HEADS * DSA_HEAD_DIM, DSA_KV_HEADS * DSA_HEAD_DIM,
               IDX_HEADS * IDX_DIM, IDX_DIM, IDX_HEADS, W_B, D_MODEL, D_MODEL)

LANES = 128
SUBLANES = 8
VMEM_LIMIT_BYTES = 56 * 1024 * 1024

TK = 256
TM = 512
NEG = -1e30
LOG2E = 1.4426950408889634
MAX_BISECT = 200
BISECT_PER_CHECK = 4

QK_PAD = MLA_HEADS * LANES
KV_DIM = DSA_KV_HEADS * DSA_HEAD_DIM

A_CQ = (0, Q_LORA)
A_CKV = (A_CQ[1], A_CQ[1] + KV_LORA)
A_KR = (A_CKV[1], A_CKV[1] + LANES)
A_SM = (A_KR[1], A_KR[1] + LANES)
A_QB = (A_SM[1], A_SM[1] + W_B)
A_KB = (A_QB[1], A_QB[1] + KV_DIM)
A_VB = (A_KB[1], A_KB[1] + KV_DIM)
A_QI = (A_VB[1], A_VB[1] + IDX_HEADS * IDX_DIM)
A_WIDTH = A_QI[1]
G_GA = (0, W_A)
G_GB = (G_GA[1], G_GA[1] + W_B)
G_MA = (G_GB[1], G_GB[1] + D_MODEL)
G_MB = (G_MA[1], G_MA[1] + D_MODEL)
G_WIDTH = G_MB[1]


def _params(n_grid):
    return pltpu.CompilerParams(dimension_semantics=("arbitrary",) * n_grid, vmem_limit_bytes=VMEM_LIMIT_BYTES)


def _sigmoid(x):
    return 1.0 / (1.0 + jnp.exp(-x))


def _rms(x, g):
    return x * lax.rsqrt(jnp.mean(x * x, axis=-1, keepdims=True) + EPS) * g


def _dot(a, b):
    return jnp.dot(a, b, preferred_element_type=F32)


def _dot_nt(a, b):
    return lax.dot_general(a, b, (((1,), (1,)), ((), ())), preferred_element_type=F32)


def _key_rows(j, tk):
    start = j * tk if isinstance(j, int) else pl.multiple_of(j * tk, tk)
    return pl.ds(start, tk)


def _rows8(x, op, ways=None):
    parts = [x[r:r + SUBLANES] for r in range(0, x.shape[0], SUBLANES)]
    if ways is not None:
        chains = parts[:ways]
        for n, part in enumerate(parts[ways:]):
            chains[n % ways] = op(chains[n % ways], part)
        parts = chains
    while len(parts) > 1:
        parts = [op(parts[n], parts[n + 1]) for n in range(0, len(parts) - 1, 2)] + parts[len(parts) & ~1:]
    return parts[0]


def _mod_kernel(c_ref, w_ref, b_ref, o_ref):
    c = c_ref[...]
    a = c * _sigmoid(c)
    w = w_ref[0]
    a_hi = a.astype(BF16)
    a_lo = (a - a_hi.astype(F32)).astype(BF16)
    w_hi = w.astype(BF16)
    w_lo = (w - w_hi.astype(F32)).astype(BF16)
    o_ref[0] = _dot(a_hi, w_hi) + (_dot(a_hi, w_lo) + _dot(a_lo, w_hi)) + b_ref[0]


def _modulation(c_all, w_ada, b_ada):
    depth = w_ada.shape[0]
    n = c_all.shape[0]
    tn = D_MODEL
    return pl.pallas_call(
        _mod_kernel,
        out_shape=jax.ShapeDtypeStruct((depth, n, 3 * D_MODEL), F32),
        grid=(depth, 3 * D_MODEL // tn),
        in_specs=[
            pl.BlockSpec((n, D_MODEL), lambda l, j: (0, 0)),
            pl.BlockSpec((1, D_MODEL, tn), lambda l, j: (l, 0, j)),
            pl.BlockSpec((1, 1, tn), lambda l, j: (l, 0, j)),
        ],
        out_specs=pl.BlockSpec((1, n, tn), lambda l, j: (l, 0, j)),
        compiler_params=_params(2),
        name="adaln_mod",
    )(c_all, w_ada, b_ada.reshape(depth, 1, 3 * D_MODEL))


def _rel_bucket(rel):
    half = NUM_BUCKETS // 2
    max_exact = half // 2
    n = np.abs(rel)
    large = max_exact + (np.log(np.maximum(n, 1).astype(np.float32) / max_exact)
                         / np.float32(np.log(MAX_DISTANCE / max_exact)) * (half - max_exact)).astype(np.int32)
    large = np.minimum(large, half - 1)
    return np.where(n < max_exact, n, large) + np.where(rel > 0, half, 0)


def _bias_kernel(bucket_ref, rel_ref, o_ref, *, far_bucket):
    h = pl.program_id(0)
    for t in range(2):
        b = bucket_ref[t]
        acc = jnp.zeros(b.shape, F32)
        for i in range(NUM_BUCKETS):
            acc = jnp.where(b == i, rel_ref[i, h], acc)
        o_ref[0, t] = (acc - rel_ref[far_bucket, h]) * LOG2E


def _bias_tiles(rel_bias, tk, tq):
    r = np.arange(tk)[:, None]
    c = np.arange(tq)[None, :]
    far_bucket = int(_rel_bucket(np.array(-tk)))
    assert np.all(_rel_bucket(-tk - np.arange(1, tk + tq)) == far_bucket)
    buckets = jnp.asarray(np.stack([_rel_bucket(r - tk - c), _rel_bucket(r - c)]).astype(np.int32))
    return pl.pallas_call(
        functools.partial(_bias_kernel, far_bucket=far_bucket),
        out_shape=jax.ShapeDtypeStruct((DSA_HEADS, 2, tk, tq), F32),
        grid=(DSA_HEADS,),
        in_specs=[
            pl.BlockSpec((2, tk, tq), lambda h: (0, 0, 0)),
            pl.BlockSpec(memory_space=pltpu.SMEM),
        ],
        out_specs=pl.BlockSpec((1, 2, tk, tq), lambda h: (h, 0, 0, 0)),
        compiler_params=_params(1),
        name="t5_bias_tiles",
    )(buckets, rel_bias)


def _rope_group(g, c_ref, s1_ref, s2_ref):
    return (g * c_ref[...] + pltpu.roll(g, LANES - ROPE_DIM // 2, 1) * s1_ref[...]
            + pltpu.roll(g, ROPE_DIM // 2, 1) * s2_ref[...])


def _proj_kernel(x_ref, shift_ref, scale_ref, gn_ref, wa_ref, gq_ref, wuq_ref, gkv_ref, wkn_ref, wv_ref,
                 qc_ref, qs1_ref, qs2_ref, kc_ref, ks1_ref, ks2_ref, *rest, emit_kv, tk, n_aliased):
    ckv_o, kr_o, kb_o, vb_o, ki_o, q_o, qb_o, qi_o, wit_o, *kv_outs = rest[n_aliased:]

    def put(cache_o, value):
        for n in range(cache_o.shape[0]):
            cache_o[n, 0] = value

    x = x_ref[0]
    h = _rms(x, gn_ref[...]) * (1.0 + scale_ref[0]) + shift_ref[0]
    h16 = h.astype(BF16)

    def zcols(a):
        return _dot(h16, wa_ref[:, a[0]:a[1]])

    cqn = _rms(zcols(A_CQ), gq_ref[...]).astype(BF16)
    q = _dot(cqn, wuq_ref[...])
    for hd in range(MLA_HEADS):
        sl = slice(hd * LANES, (hd + 1) * LANES)
        q_hd = _rope_group(q[:, sl], qc_ref, qs1_ref, qs2_ref)
        if emit_kv:
            q_o[0, sl, :] = q_hd.T.astype(BF16)
        else:
            q_o[0, :, sl] = q_hd.astype(BF16)

    ckvn = _rms(zcols(A_CKV), gkv_ref[...])
    put(ckv_o, ckvn)
    krg = _rope_group(zcols(A_KR), kc_ref, ks1_ref, ks2_ref)
    put(kr_o, krg[:, 0:ROPE_DIM])

    sm = zcols(A_SM)
    ki = sm[:, 0:IDX_DIM]
    put(ki_o, ki)
    wit_o[0] = sm.T[IDX_DIM:IDX_DIM + IDX_HEADS, :] * IDX_W_SCALE
    qb = zcols(A_QB) * (DSA_SCALE * LOG2E)
    qi = zcols(A_QI)
    qb_o[0] = (qb.T if emit_kv else qb).astype(BF16)
    qi_o[0] = (qi.T if emit_kv else qi).astype(BF16)
    kb = zcols(A_KB)
    vb = zcols(A_VB)
    put(kb_o, kb)
    put(vb_o, vb)

    if emit_kv:
        kcat_o, vat_o, kb16_o, vbt_o, ki16_o = kv_outs
        ckv16 = ckvn.astype(BF16)
        kn = _dot(ckv16, wkn_ref[...])
        lane = lax.broadcasted_iota(jnp.int32, krg.shape, 1)
        k_add = jnp.where(lane >= NOPE_DIM, krg, 0.0)
        for hd in range(MLA_HEADS):
            sl = slice(hd * LANES, (hd + 1) * LANES)
            kcat_o[0, :, sl] = (kn[:, sl] + k_add).astype(BF16)
        va = _dot(ckv16, wv_ref[...])
        for g in range(DSA_KV_HEADS):
            kb16_o[0, g] = kb[:, g * DSA_HEAD_DIM:(g + 1) * DSA_HEAD_DIM].astype(BF16)
        ki16_o[0] = ki.astype(BF16)
        for c in range(x.shape[0] // tk):
            rows = slice(c * tk, (c + 1) * tk)
            vat_o[0, c] = va[rows, :].T.astype(BF16)
            vbt_o[0, c] = vb[rows, :].T.astype(BF16)


def _proj(x, shift, scale, gn, wa, gq, wuq, gkv, wkn, wv, q_tabs, k_tabs, caches, *, layer, depth, tm, tk, emit_kv):
    nb, t, _ = x.shape
    r = shift.shape[1]
    rb = 1 if r == 1 else tm
    grid = (nb, t // tm)
    tok = lambda w: pl.BlockSpec((1, tm, w), lambda b, i: (b, i, 0))
    if caches is None:
        stacked = lambda w: pl.BlockSpec((depth, 1, tm, w), lambda b, i: (0, b, i, 0))
    else:
        stacked = lambda w: pl.BlockSpec((1, 1, tm, w), lambda b, i: (layer, b, i, 0))
    mod = pl.BlockSpec((1, rb, D_MODEL), (lambda b, i: (b, 0, 0)) if r == 1 else (lambda b, i: (b, i, 0)))
    const = lambda a: pl.BlockSpec(a.shape, lambda b, i: (0,) * a.ndim)
    tab = pl.BlockSpec((tm, LANES), lambda b, i: (i, 0))
    in_specs = [tok(D_MODEL), mod, mod, const(gn), const(wa), const(gq), const(wuq), const(gkv), const(wkn),
                const(wv)] + [tab] * 6
    operands = [x, shift, scale, gn, wa, gq, wuq, gkv, wkn, wv, *q_tabs, *k_tabs]
    cache_widths = (KV_LORA, ROPE_DIM, KV_DIM, KV_DIM, IDX_DIM)
    aliases = {}
    if caches is not None:
        aliases = {len(operands) + n: n for n in range(len(cache_widths))}
        in_specs += [pl.BlockSpec(memory_space=pl.ANY)] * len(cache_widths)
        operands += list(caches)
    tok_t = lambda w: pl.BlockSpec((1, w, tm), lambda b, i: (b, 0, i))
    q_shape = (lambda w: (nb, w, t)) if emit_kv else (lambda w: (nb, t, w))
    q_spec = tok_t if emit_kv else tok
    q_widths = (QK_PAD, W_B, IDX_HEADS * IDX_DIM)
    out_shape = ([jax.ShapeDtypeStruct((depth, nb, t, w), F32) for w in cache_widths]
                 + [jax.ShapeDtypeStruct(q_shape(w), BF16) for w in q_widths]
                 + [jax.ShapeDtypeStruct((nb, IDX_HEADS, t), F32)])
    out_specs = [stacked(w) for w in cache_widths] + [q_spec(w) for w in q_widths] + [tok_t(IDX_HEADS)]
    if emit_kv:
        nkt = tm // tk
        out_shape += [
            jax.ShapeDtypeStruct((nb, t, QK_PAD), BF16),
            jax.ShapeDtypeStruct((nb, t // tk, W_A, tk), BF16),
            jax.ShapeDtypeStruct((nb, DSA_KV_HEADS, t, DSA_HEAD_DIM), BF16),
            jax.ShapeDtypeStruct((nb, t // tk, KV_DIM, tk), BF16),
            jax.ShapeDtypeStruct((nb, t, IDX_DIM), BF16),
        ]
        out_specs += [tok(QK_PAD), pl.BlockSpec((1, nkt, W_A, tk), lambda b, i: (b, i, 0, 0)),
                      pl.BlockSpec((1, DSA_KV_HEADS, tm, DSA_HEAD_DIM), lambda b, i: (b, 0, i, 0)),
                      pl.BlockSpec((1, nkt, KV_DIM, tk), lambda b, i: (b, i, 0, 0)), tok(IDX_DIM)]
    return pl.pallas_call(
        functools.partial(_proj_kernel, emit_kv=emit_kv, tk=tk, n_aliased=len(aliases)),
        out_shape=out_shape, grid=grid, in_specs=in_specs, out_specs=out_specs, input_output_aliases=aliases,
        compiler_params=_params(2), name="proj_kv" if emit_kv else "proj_q",
    )(*operands)


class _Softmax:
    ONES_ROWS = 16

    def __init__(self, s_scr, mx_scr, m_scr, acc_scr):
        self.s, self.mx, self.m, self.acc = s_scr, mx_scr, m_scr, acc_scr
        self.n_heads = acc_scr.shape[0]
        self.dv = acc_scr.shape[1] - self.ONES_ROWS

    @classmethod
    def scratch(cls, n_heads, dv, tk, tq):
        stat = pltpu.VMEM((n_heads, SUBLANES, tq), F32)
        return [pltpu.VMEM((n_heads, 2, tk, tq), F32), stat, stat, pltpu.VMEM((n_heads, dv + cls.ONES_ROWS, tq), F32)]

    def reset(self):
        self.mx[...] = jnp.full(self.mx.shape, NEG, F32)
        self.m[...] = jnp.full(self.m.shape, NEG, F32)
        self.acc[...] = jnp.zeros(self.acc.shape, F32)

    def produce(self, hd, slot, s):
        self.s[hd, slot] = s
        self.mx[hd] = jnp.maximum(self.mx[hd], _rows8(s, jnp.maximum, ways=4))

    def consume(self, hd, slot, vt, m_new):
        m_old = self.m[hd, 0:1, :]
        alpha = jnp.exp2(m_old - m_new)
        p = jnp.exp2(self.s[hd, slot] - m_new)
        self.m[hd] = jnp.broadcast_to(m_new, self.m.shape[1:])
        vt1 = jnp.concatenate([vt, jnp.ones((self.ONES_ROWS, vt.shape[1]), vt.dtype)], axis=0)
        self.acc[hd] = alpha * self.acc[hd] + _dot(vt1, p.astype(BF16))

    def run(self, last, logits, values):
        def step(t_consume, t_produce, kind, slot_produce):
            if t_consume is not None:
                m_new = [jnp.max(self.mx[hd], axis=0, keepdims=True) for hd in range(self.n_heads)]
            for hd in range(self.n_heads):
                if t_produce is not None:
                    self.produce(hd, slot_produce, logits(hd, t_produce, kind))
                if t_consume is not None:
                    self.consume(hd, 1 - slot_produce, values(hd, t_consume), m_new[hd])

        when = lambda cond, fn: pl.when(cond)(fn)
        self.reset()
        step(None, last, 2, 0)
        when(last >= 1, lambda: step(last, last - 1, 1, 1))
        n_far = last - 1

        def far_pair(n, _):
            t = last - 2 - 2 * n
            step(t + 1, t, 0, 0)
            step(t, t - 1, 0, 1)
            return 0

        lax.fori_loop(0, n_far // 2, far_pair, 0)
        odd = (n_far >= 1) & (n_far % 2 == 1)
        when(odd, lambda: step(1, 0, 0, 0))
        in_slot0 = (last == 0) | odd
        when(in_slot0, lambda: step(0, None, None, 1))
        when(jnp.logical_not(in_slot0), lambda: step(0, None, None, 0))

    def finish(self, o_ref):
        o_t = jnp.concatenate([self.acc[hd, 0:self.dv, :] / self.acc[hd, self.dv:self.dv + 1, :]
                               for hd in range(self.n_heads)], axis=0)
        o_ref[0] = o_t.T


def _diagonal_tile_visible(t):
    r = lax.broadcasted_iota(jnp.int32, (t, t), 0)
    c = lax.broadcasted_iota(jnp.int32, (t, t), 1)
    return (r // CHUNK) <= (c // CHUNK)


def _mla_kernel(q_ref, k_ref, vt_ref, o_ref, *scratch, t):
    softmax = _Softmax(*scratch)

    def logits(hd, j, kind):
        sl = slice(hd * LANES, (hd + 1) * LANES)
        s = _dot(k_ref[0, _key_rows(j, t), sl], q_ref[0, sl, :])
        return s + jnp.where(_diagonal_tile_visible(t), 0.0, NEG) if kind == 2 else s

    softmax.run(pl.program_id(1), logits, lambda hd, j: vt_ref[0, j, hd * V_DIM:(hd + 1) * V_DIM, :])
    softmax.finish(o_ref)


def _mla(q, k, vt, *, t):
    nb, _, seq = q.shape
    return pl.pallas_call(
        functools.partial(_mla_kernel, t=t),
        out_shape=jax.ShapeDtypeStruct((nb, seq, W_A), F32), grid=(nb, seq // t),
        in_specs=[pl.BlockSpec((1, QK_PAD, t), lambda b, i: (b, 0, i)),
                  pl.BlockSpec((1, seq, QK_PAD), lambda b, i: (b, 0, 0)),
                  pl.BlockSpec((1, seq // t, W_A, t), lambda b, i: (b, 0, 0, 0))],
        out_specs=pl.BlockSpec((1, t, W_A), lambda b, i: (b, i, 0)),
        scratch_shapes=_Softmax.scratch(MLA_HEADS, V_DIM, t, t),
        compiler_params=_params(2), name="mla_causal",
    )(q, k, vt)


def _score_stats_init(tq):
    return (jnp.full((1, tq), jnp.inf, F32), jnp.full((1, tq), -jnp.inf, F32),
            jnp.zeros((SUBLANES, tq), F32), jnp.zeros((SUBLANES, tq), F32))


def _score_stats(stats, t, visible=None):
    lo, hi, c_z, c_zp = stats
    t_min = t if visible is None else jnp.where(visible, t, jnp.inf)
    return (jnp.minimum(lo, jnp.min(t_min, axis=0, keepdims=True)), jnp.maximum(hi, jnp.max(t, axis=0, keepdims=True)),
            c_z + _rows8(jnp.where(t >= 0.0, 1.0, 0.0), jnp.add), c_zp + _rows8(jnp.where(t > 0.0, 1.0, 0.0), jnp.add))


def _select_topk(idx_scr, n_used, stats, n_vis, kf, folded_scr=None, copies=1):
    _, tk, tq = idx_scr.shape
    count_scr = idx_scr if folded_scr is None else folded_scr

    def count(pred):
        def body(j, c):
            return c + _rows8(jnp.where(pred(count_scr[j]), 1.0, 0.0), jnp.add)
        c = jnp.sum(lax.fori_loop(0, n_used, body, jnp.zeros((SUBLANES, tq), F32)), axis=0, keepdims=True)
        shift = tq // 2
        while shift >= tq // copies:
            c = c + pltpu.roll(c, shift, 1)
            shift //= 2
        return c

    lo, hi, c_z, c_zp = stats
    c_z = jnp.sum(c_z, axis=0, keepdims=True)
    c_zp = jnp.sum(c_zp, axis=0, keepdims=True)
    hi = hi + (jnp.abs(hi) * 2.0 ** -20 + 1e-30)
    zero_up = (c_z >= kf) & (lo < 0.0)
    zero_tie = jnp.where((c_z >= kf) & (c_zp < kf), 1.0, 0.0)
    c_lo = jnp.where(zero_up, c_z, n_vis)
    lo = jnp.where(zero_up, 0.0, lo)
    hi = jnp.where((c_z < kf) & (hi > 0.0), 0.0, hi)

    def settled(lo, hi, c_lo):
        mid = 0.5 * lo + 0.5 * hi
        return jnp.where((c_lo <= kf) | (mid <= lo) | (mid >= hi), 1.0, 0.0)

    def bis_cond(st):
        _, _, _, done, it = st
        return jnp.logical_and(it < MAX_BISECT, jnp.min(done) < 0.5)

    def bis_body(st):
        lo, hi, c_lo, done, it = st
        for _ in range(BISECT_PER_CHECK):
            mid = 0.5 * lo + 0.5 * hi
            c = count(lambda t: t >= mid)
            live = done < 0.5
            up = live & (c >= kf)
            dn = live & (c < kf)
            lo = jnp.where(up, mid, lo)
            c_lo = jnp.where(up, c, c_lo)
            hi = jnp.where(dn, mid, hi)
            done = jnp.maximum(done, settled(lo, hi, c_lo))
        return lo, hi, c_lo, done, it + BISECT_PER_CHECK

    done = jnp.maximum(settled(lo, hi, c_lo), zero_tie)
    lo, hi, c_lo, _, _ = lax.while_loop(bis_cond, bis_body, (lo, hi, c_lo, done, jnp.int32(0)))
    any_tie = jnp.max(jnp.where(c_lo > kf, 1.0, 0.0)) > 0.5

    @pl.when(jnp.logical_not(any_tie))
    def _():
        def body(j, _):
            idx_scr[j] = jnp.where(idx_scr[j] >= lo, 0.0, NEG)
            return 0
        lax.fori_loop(0, n_used, body, 0)

    @pl.when(any_tie)
    def _():
        need = kf - count(lambda t: t > lo)
        r = lax.broadcasted_iota(jnp.int32, (tk, tk), 0)
        c = lax.broadcasted_iota(jnp.int32, (tk, tk), 1)
        before = jnp.where(c < r, 1.0, 0.0).astype(BF16)

        def body(j, seen):
            t = idx_scr[j]
            eq = jnp.where(t == lo, 1.0, 0.0)
            rank = _dot(before, eq.astype(BF16)) + seen
            take = (t > lo) | ((t == lo) & (rank < need))
            idx_scr[j] = jnp.where(take, 0.0, NEG)
            return seen + jnp.sum(_rows8(eq, jnp.add), axis=0, keepdims=True)
        lax.fori_loop(0, n_used, body, jnp.zeros((1, tq), F32))


def _dsa_kernel(qi_ref, wit_ref, ki_ref, qb_ref, kb_ref, vbt_ref, bias_ref, o_ref, idx_scr, *scratch, t, topk):
    tq = tk = t
    last = pl.program_id(1)
    n_used = last + 1
    visible = _diagonal_tile_visible(t)

    wit = wit_ref[0]

    def idx_tile(j):
        k_j = ki_ref[0, _key_rows(j, tk), :]
        acc = jnp.zeros((tk, tq), F32)
        for hd in range(IDX_HEADS):
            s = _dot(k_j, qi_ref[0, hd * IDX_DIM:(hd + 1) * IDX_DIM, :])
            acc = acc + jnp.maximum(s, 0.0) * wit[hd:hd + 1, :]
        return acc

    def idx_body(j, stats):
        t = idx_tile(j)
        idx_scr[j] = t
        return _score_stats(stats, t)

    def idx_pair(n, stats):
        return idx_body(2 * n + 1, idx_body(2 * n, stats))

    stats = lax.fori_loop(0, last // 2, idx_pair, _score_stats_init(tq))
    stats = lax.cond(last % 2 == 1, lambda st: idx_body(last - 1, st), lambda st: st, stats)
    t_last = jnp.where(visible, idx_tile(last), -jnp.inf)
    idx_scr[last] = t_last
    stats = _score_stats(stats, t_last, visible)
    lane = lax.broadcasted_iota(jnp.int32, (1, tq), 1)
    n_vis = last * tk + (lane // CHUNK + 1) * CHUNK
    _select_topk(idx_scr, n_used, stats, n_vis.astype(F32), float(topk))

    softmax = _Softmax(*scratch)
    kv_lanes = lambda hd: slice((hd // DSA_GROUP) * DSA_HEAD_DIM, (hd // DSA_GROUP + 1) * DSA_HEAD_DIM)

    def logits(hd, t, kind):
        s = _dot(kb_ref[0, hd // DSA_GROUP, _key_rows(t, tk), :], qb_ref[0, hd * DSA_HEAD_DIM:(hd + 1) * DSA_HEAD_DIM, :])
        return s + idx_scr[t] if kind == 0 else s + (idx_scr[t] + bias_ref[hd, kind - 1])

    softmax.run(last, logits, lambda hd, t: vbt_ref[0, t, kv_lanes(hd), :])
    softmax.finish(o_ref)


def _dsa(qi, wit, ki, qb, kb, vbt, bias, *, t, topk):
    nb, _, seq = qb.shape
    n_tiles = seq // t
    return pl.pallas_call(
        functools.partial(_dsa_kernel, t=t, topk=topk),
        out_shape=jax.ShapeDtypeStruct((nb, seq, W_B), F32), grid=(nb, n_tiles),
        in_specs=[pl.BlockSpec((1, IDX_HEADS * IDX_DIM, t), lambda b, i: (b, 0, i)),
                  pl.BlockSpec((1, IDX_HEADS, t), lambda b, i: (b, 0, i)),
                  pl.BlockSpec((1, seq, IDX_DIM), lambda b, i: (b, 0, 0)),
                  pl.BlockSpec((1, W_B, t), lambda b, i: (b, 0, i)),
                  pl.BlockSpec((1, DSA_KV_HEADS, seq, DSA_HEAD_DIM), lambda b, i: (b, 0, 0, 0)),
                  pl.BlockSpec((1, n_tiles, KV_DIM, t), lambda b, i: (b, 0, 0, 0)),
                  pl.BlockSpec(bias.shape, lambda b, i: (0, 0, 0, 0))],
        out_specs=pl.BlockSpec((1, t, W_B), lambda b, i: (b, i, 0)),
        scratch_shapes=[pltpu.VMEM((n_tiles, t, t), F32)] + _Softmax.scratch(DSA_HEADS, DSA_HEAD_DIM, t, t),
        compiler_params=_params(2), name="dsa_causal",
    )(qi, wit, ki, qb, kb, vbt, bias)


def _two_pass_softmax(n_tiles, logits, values, s_scr, dv):
    mx = None
    for t in range(n_tiles):
        s = logits(t)
        s_scr[t] = s
        part = _rows8(s, jnp.maximum, ways=4)
        mx = part if mx is None else jnp.maximum(mx, part)
    m = jnp.max(mx, axis=0, keepdims=True)
    acc = None
    for t in range(n_tiles):
        p = jnp.exp2(s_scr[t] - m).astype(BF16)
        vt = values(t)
        part = _dot(jnp.concatenate([vt, jnp.ones((_Softmax.ONES_ROWS, vt.shape[1]), BF16)], axis=0), p)
        acc = part if acc is None else acc + part
    return acc[0:dv] / acc[dv:dv + 1]


def _mla_cached_kernel(q_ref, ckv_c_ref, kr_c_ref, ckv_n_ref, kr_n_ref, wuk_ref, wuv_ref, sel_ref, o_ref, s_scr, *,
                       ts, tk, n_cache, n_valid_last):
    lanes = MLA_HEADS * ts
    q_lat, q_rope = [], []
    for hd in range(MLA_HEADS):
        qg = q_ref[0, :, hd * LANES:(hd + 1) * LANES]
        q_lat.append(_dot(qg, wuk_ref[hd]).astype(BF16))
        q_rope.append(_dot(qg, sel_ref[...]).astype(BF16))
    q_lat = jnp.concatenate(q_lat, axis=0)
    q_rope = jnp.concatenate(q_rope, axis=0)

    def latent(t):
        return ckv_c_ref[0, t * tk:(t + 1) * tk, :] if t < n_cache else ckv_n_ref[0]

    def logits(t):
        kr = kr_c_ref[0, t * tk:(t + 1) * tk, :] if t < n_cache else kr_n_ref[0]
        s = _dot_nt(latent(t).astype(BF16), q_lat) + _dot_nt(kr.astype(BF16), q_rope)
        if t == n_cache:
            s = s + jnp.where(lax.broadcasted_iota(jnp.int32, s.shape, 0) < n_valid_last, 0.0, NEG)
        return s

    o_lat = _two_pass_softmax(n_cache + 1, logits, lambda t: latent(t).T.astype(BF16), s_scr, KV_LORA)
    o_lat = o_lat.astype(BF16)
    head_of_lane = lax.broadcasted_iota(jnp.int32, (V_DIM, lanes), 1) // ts
    out = jnp.zeros((V_DIM, lanes), F32)
    for hd in range(MLA_HEADS):
        out = jnp.where(head_of_lane == hd, _dot(wuv_ref[hd], o_lat), out)
    o_ref[0] = out


def _mla_cached(q, ckv_cache, kr_cache, ckv_new, kr_new, wuk, wuv, sel, *, layer, tk, n_valid_last):
    nb, ts, _ = q.shape
    past = ckv_cache.shape[2]
    lanes = MLA_HEADS * ts
    per_b = lambda a: pl.BlockSpec((1,) + a.shape[1:], lambda b: (b,) + (0,) * (a.ndim - 1))
    const = lambda a: pl.BlockSpec(a.shape, lambda b: (0,) * a.ndim)
    return pl.pallas_call(
        functools.partial(_mla_cached_kernel, ts=ts, tk=tk, n_cache=past // tk, n_valid_last=n_valid_last),
        out_shape=jax.ShapeDtypeStruct((nb, V_DIM, lanes), F32), grid=(nb,),
        in_specs=[per_b(q), _layer_rows(ckv_cache, layer), _layer_rows(kr_cache, layer), per_b(ckv_new),
                  per_b(kr_new), const(wuk), const(wuv), const(sel)],
        out_specs=pl.BlockSpec((1, V_DIM, lanes), lambda b: (b, 0, 0)),
        scratch_shapes=[pltpu.VMEM((past // tk + 1, tk, lanes), F32)],
        compiler_params=_params(1), name="mla_cached",
    )(q, ckv_cache, kr_cache, ckv_new, kr_new, wuk, wuv, sel)


def _dsa_cached_kernel(qi_ref, w_ref, ki_c_ref, ki_n_ref, qb_ref, kb_c_ref, kb_n_ref, vb_c_ref, vb_n_ref, bias_ref,
                       o_ref, idx_scr, fold_scr, s_scr, *, ts, tk, n_cache, n_valid_last, topk):
    n_tiles = n_cache + 1
    lanes = DSA_GROUP * ts
    copies = DSA_GROUP
    tile = lambda c_ref, n_ref, t: c_ref[0, t * tk:(t + 1) * tk, :] if t < n_cache else n_ref[0]
    visible = lax.broadcasted_iota(jnp.int32, (tk, lanes), 0) < n_valid_last
    copy_of_lane = lax.broadcasted_iota(jnp.int32, (tk // copies, lanes), 1) // ts

    stats = _score_stats_init(lanes)
    for t in range(n_tiles):
        s = _dot_nt(tile(ki_c_ref, ki_n_ref, t).astype(BF16), qi_ref[0])
        r = jnp.maximum(s, 0.0) * w_ref[0]
        x = r[:, 0:lanes]
        for blk in range(1, IDX_HEADS * ts // lanes):
            x = x + r[:, blk * lanes:(blk + 1) * lanes]
        shift = lanes // 2
        while shift >= ts:
            x = x + pltpu.roll(x, shift, 1)
            shift //= 2
        if t == n_cache:
            x = jnp.where(visible, x, -jnp.inf)
        idx_scr[t] = x
        share = tk // copies
        folded = x[0:share]
        for c in range(1, copies):
            folded = jnp.where(copy_of_lane == c, x[c * share:(c + 1) * share], folded)
        fold_scr[t] = folded
        stats = _score_stats(stats, x, visible if t == n_cache else None)
    n_vis = jnp.full((1, lanes), float(n_cache * tk + n_valid_last), F32)
    _select_topk(idx_scr, n_tiles, stats, n_vis, float(topk), fold_scr, copies)

    for g in range(DSA_KV_HEADS):
        def logits(t, g=g):
            s = _dot_nt(tile(kb_c_ref, kb_n_ref, t).astype(BF16), qb_ref[0, g]) + idx_scr[t]
            return s + bias_ref[g, t - (n_tiles - 2)] if t >= n_tiles - 2 else s

        def values(t, g=g):
            return tile(vb_c_ref, vb_n_ref, t).T[g * DSA_HEAD_DIM:(g + 1) * DSA_HEAD_DIM, :].astype(BF16)

        o_ref[0, g] = _two_pass_softmax(n_tiles, logits, values, s_scr, DSA_HEAD_DIM)


def _layer_rows(cache, layer):
    return pl.BlockSpec((None, 1) + cache.shape[2:], lambda b: (layer, b, 0, 0))


def _dsa_cached(qi, w, ki_cache, ki_new, qb, kb_cache, kb_new, vb_cache, vb_new, bias, *, layer, ts, tk,
                n_valid_last, topk):
    nb = qi.shape[0]
    past = ki_cache.shape[2]
    n_tiles = past // tk + 1
    lanes = DSA_GROUP * ts
    per_b = lambda a: pl.BlockSpec((1,) + a.shape[1:], lambda b: (b,) + (0,) * (a.ndim - 1))
    cached = lambda a: _layer_rows(a, layer)
    return pl.pallas_call(
        functools.partial(_dsa_cached_kernel, ts=ts, tk=tk, n_cache=past // tk, n_valid_last=n_valid_last, topk=topk),
        out_shape=jax.ShapeDtypeStruct((nb, DSA_KV_HEADS, DSA_HEAD_DIM, lanes), F32), grid=(nb,),
        in_specs=[per_b(qi), per_b(w), cached(ki_cache), per_b(ki_new), per_b(qb), cached(kb_cache), per_b(kb_new),
                  cached(vb_cache), per_b(vb_new), pl.BlockSpec(bias.shape, lambda b: (0,) * bias.ndim)],
        out_specs=pl.BlockSpec((1, DSA_KV_HEADS, DSA_HEAD_DIM, lanes), lambda b: (b, 0, 0, 0)),
        scratch_shapes=[pltpu.VMEM((n_tiles, tk, lanes), F32), pltpu.VMEM((n_tiles, tk // DSA_GROUP, lanes), F32),
                        pltpu.VMEM((n_tiles, tk, lanes), F32)],
        compiler_params=_params(1), name="dsa_cached",
    )(qi, w, ki_cache, ki_new, qb, kb_cache, kb_new, vb_cache, vb_new, bias)


def _out_kernel(x_ref, shift_ref, scale_ref, gate_ref, gn_ref, oa_ref, ob_ref, wg_ref, woa_ref, wob_ref, wout_ref,
                gf_ref, o_ref, *, final):
    x = x_ref[0]
    h16 = (_rms(x, gn_ref[...]) * (1.0 + scale_ref[0]) + shift_ref[0]).astype(BF16)

    def gcols(a):
        return _dot(h16, wg_ref[:, a[0]:a[1]])

    ga = gcols(G_GA)
    ya = _dot((oa_ref[0] * (ga * _sigmoid(ga))).astype(BF16), woa_ref[...])
    gb = gcols(G_GB)
    yb = _dot((ob_ref[0] * (gb * _sigmoid(gb))).astype(BF16), wob_ref[...])
    m = _sigmoid(gcols(G_MA)) * ya + _sigmoid(gcols(G_MB)) * yb
    y = x + gate_ref[0] * _dot(m.astype(BF16), wout_ref[...])
    o_ref[0] = _rms(y, gf_ref[...]) if final else y


def _out(x, shift, scale, gate, gn, oa, ob, wg, woa, wob, wout, gf, *, tm, final):
    nb, t, _ = x.shape
    r = shift.shape[1]
    rb = 1 if r == 1 else tm
    tok = lambda w: pl.BlockSpec((1, tm, w), lambda b, i: (b, i, 0))
    mod = pl.BlockSpec((1, rb, D_MODEL), (lambda b, i: (b, 0, 0)) if r == 1 else (lambda b, i: (b, i, 0)))
    const = lambda a: pl.BlockSpec(a.shape, lambda b, i: (0,) * a.ndim)
    return pl.pallas_call(
        functools.partial(_out_kernel, final=final),
        out_shape=jax.ShapeDtypeStruct(x.shape, F32), grid=(nb, t // tm),
        in_specs=[tok(D_MODEL), mod, mod, mod, const(gn), tok(W_A), tok(W_B), const(wg), const(woa), const(wob),
                  const(wout), const(gf)],
        out_specs=tok(D_MODEL), compiler_params=_params(2), name="out_final" if final else "out_mix",
    )(x, shift, scale, gate, gn, oa, ob, wg, woa, wob, wout, gf)


def _pack_weights(w_in, w_uq, w_ukv, w_oa, w_ob, w_out):
    depth = w_in.shape[0]
    offs = np.concatenate([[0], np.cumsum(SPLIT_SIZES)])
    cq, ckv, kr, ga, qb, kb, vb, qi, ki, wi, gb, ma, mb = [w_in[:, :, offs[n]:offs[n + 1]] for n in range(13)]
    z = lambda w: jnp.zeros((depth, D_MODEL, w), F32)
    krg = jnp.concatenate([kr, z(NOPE_DIM - ROPE_DIM), kr, z(LANES - NOPE_DIM - ROPE_DIM)], axis=-1)
    sm = jnp.concatenate([ki, wi, z(LANES - IDX_DIM - IDX_HEADS)], axis=-1)
    wa = jnp.concatenate([cq, ckv, krg, sm, qb, kb, vb, qi], axis=-1).astype(BF16)
    wg = jnp.concatenate([ga, gb, ma, mb], axis=-1).astype(BF16)
    qd = NOPE_DIM + ROPE_DIM
    wuq = jnp.pad(w_uq.reshape(depth, Q_LORA, MLA_HEADS, qd), ((0, 0), (0, 0), (0, 0), (0, LANES - qd)))
    wuq = wuq.reshape(depth, Q_LORA, QK_PAD).astype(BF16)
    ukv = w_ukv.reshape(depth, KV_LORA, MLA_HEADS, NOPE_DIM + V_DIM)
    wkn = jnp.pad(ukv[..., :NOPE_DIM], ((0, 0), (0, 0), (0, 0), (0, LANES - NOPE_DIM)))
    wkn = wkn.reshape(depth, KV_LORA, QK_PAD).astype(BF16)
    wv = ukv[..., NOPE_DIM:].reshape(depth, KV_LORA, W_A).astype(BF16)
    wuk = jnp.pad(ukv[..., :NOPE_DIM].transpose(0, 2, 3, 1), ((0, 0), (0, 0), (0, LANES - NOPE_DIM), (0, 0)))
    wuv = ukv[..., NOPE_DIM:].transpose(0, 2, 3, 1)
    return (wa, wg, wuq, wkn, wv, w_oa.astype(BF16), w_ob.astype(BF16), w_out.astype(BF16), wuk.astype(BF16),
            wuv.astype(BF16))


def _rope_tables(pos):
    half = ROPE_DIM // 2
    inv = ROPE_THETA ** (-jnp.arange(half, dtype=F32) / half)
    ang = pos.astype(F32)[:, None] * inv
    cos, sin = jnp.cos(ang), jnp.sin(ang)
    n = pos.shape[0]
    z = lambda w: jnp.zeros((n, w), F32)
    pad = LANES - NOPE_DIM - ROPE_DIM
    q_c = jnp.concatenate([jnp.ones((n, NOPE_DIM), F32), cos, cos, z(pad)], axis=-1) * (MLA_SCALE * LOG2E)
    q_s1 = jnp.concatenate([z(NOPE_DIM), -sin, z(half), z(pad)], axis=-1) * (MLA_SCALE * LOG2E)
    q_s2 = jnp.concatenate([z(NOPE_DIM), z(half), sin, z(pad)], axis=-1) * (MLA_SCALE * LOG2E)
    gap = NOPE_DIM - ROPE_DIM
    k_c = jnp.concatenate([cos, cos, z(gap), cos, cos, z(pad)], axis=-1)
    k_s1 = jnp.concatenate([-sin, z(half), z(gap), -sin, z(half), z(pad)], axis=-1)
    k_s2 = jnp.concatenate([z(half), sin, z(gap), z(half), sin, z(pad)], axis=-1)
    return (q_c, q_s1, q_s2), (k_c, k_s1, k_s2)


def _rope_selector():
    p = np.zeros((LANES, ROPE_DIM), np.float32)
    p[NOPE_DIM + np.arange(ROPE_DIM), np.arange(ROPE_DIM)] = 1.0
    return jnp.asarray(p, dtype=BF16)


def kernel(x_prompt, x_sample, c_prompt, c_sample, cache_ckv, cache_krope, cache_kb, cache_vb, cache_kidx, w_ada,
           b_ada, g_norm, w_in, g_qnorm, w_uq, g_kvnorm, w_ukv, w_oa, w_ob, w_out, rel_bias, g_final):
    depth = w_in.shape[0]
    bp, t_p, _ = x_prompt.shape
    bs, t_s, _ = x_sample.shape
    past = cache_ckv.shape[2]
    topk_p = min(TOPK_MAX, t_p // 4)
    topk_s = min(TOPK_MAX, (past + t_s) // 4)
    assert t_p % TM == 0 and TM % TK == 0 and past % TK == 0 and t_s <= TK and t_s % 8 == 0
    assert (DSA_GROUP * t_s) % LANES == 0 and (DSA_GROUP * t_s) & (DSA_GROUP * t_s - 1) == 0
    n_s = bs * t_s

    wa, wg, wuq, wkn, wv, woa, wob, wout, wuk, wuv = _pack_weights(w_in, w_uq, w_ukv, w_oa, w_ob, w_out)
    rope_sel = _rope_selector()
    cache_kb2 = cache_kb.reshape(depth, bs, past, KV_DIM)
    cache_vb2 = cache_vb.reshape(depth, bs, past, KV_DIM)
    mod = _modulation(jnp.concatenate([c_prompt, c_sample], axis=0), w_ada, b_ada)
    mod = mod.reshape(depth, bp + bs, 3, 1, D_MODEL)
    tabs_p = _rope_tables(jnp.arange(t_p, dtype=jnp.int32))
    tabs_s = _rope_tables(jnp.tile(past + jnp.arange(t_s, dtype=jnp.int32), bs))
    bias_p = _bias_tiles(rel_bias, TK, TK)
    bias_s = _bias_tiles(rel_bias, TK, t_s).reshape(DSA_KV_HEADS, DSA_GROUP, 2, TK, t_s)
    bias_s = bias_s.transpose(0, 2, 3, 1, 4).reshape(DSA_KV_HEADS, 2, TK, DSA_GROUP * t_s)
    row = lambda v: v.reshape(1, -1)

    xp = x_prompt
    xs = x_sample.reshape(1, n_s, D_MODEL)
    caches_p = caches_s = None
    for l in range(depth):
        final = l == depth - 1
        gn, gq, gkv, gf = row(g_norm[l]), row(g_qnorm[l]), row(g_kvnorm[l]), row(g_final)

        shift, scale, gate = (mod[l, :bp, n] for n in range(3))
        *caches_p, q, qb, qi, wit, kcat, vat, kb16, vbt, ki16 = _proj(
            xp, shift, scale, gn, wa[l], gq, wuq[l], gkv, wkn[l], wv[l], *tabs_p, caches_p, layer=l, depth=depth,
            tm=TM, tk=TK, emit_kv=True)
        oa = _mla(q, kcat, vat, t=TK)
        ob = _dsa(qi, wit, ki16, qb, kb16, vbt, bias_p, t=TK, topk=topk_p)
        xp = _out(xp, shift, scale, gate, gn, oa, ob, wg[l], woa[l], wob[l], wout[l], gf, tm=TM, final=final)

        shift, scale, gate = (jnp.repeat(mod[l, bp:, n, 0], t_s, axis=0)[None] for n in range(3))
        *caches_s, q, qb, qi, wit = _proj(
            xs, shift, scale, gn, wa[l], gq, wuq[l], gkv, wkn[l], wv[l], *tabs_s, caches_s, layer=l, depth=depth,
            tm=n_s, tk=TK, emit_kv=False)
        per_seq = lambda a: a.reshape((bs, t_s) + a.shape[2:])
        ckv, kr, kb, vb, ki = (per_seq(a[l]) for a in caches_s)
        q, qb, qi = per_seq(q), per_seq(qb), per_seq(qi)
        new_tile = lambda a: jnp.pad(a, ((0, 0), (0, TK - t_s), (0, 0)))
        oa = _mla_cached(q, cache_ckv, cache_krope, new_tile(ckv), new_tile(kr), wuk[l], wuv[l], rope_sel, layer=l,
                         tk=TK, n_valid_last=t_s)
        oa = oa.reshape(bs, V_DIM, MLA_HEADS, t_s).transpose(0, 3, 2, 1).reshape(1, n_s, W_A)
        qi_l = qi.reshape(bs, t_s, IDX_HEADS, IDX_DIM).transpose(0, 2, 1, 3).reshape(bs, IDX_HEADS * t_s, IDX_DIM)
        w_l = wit.reshape(IDX_HEADS, bs, t_s).transpose(1, 0, 2).reshape(bs, 1, IDX_HEADS * t_s)
        qb_l = qb.reshape(bs, t_s, DSA_KV_HEADS, DSA_GROUP, DSA_HEAD_DIM).transpose(0, 2, 3, 1, 4)
        qb_l = qb_l.reshape(bs, DSA_KV_HEADS, DSA_GROUP * t_s, DSA_HEAD_DIM)
        qb_l = jnp.stack([jnp.pad(qb_l[:, g], ((0, 0), (0, 0), (g * DSA_HEAD_DIM, KV_DIM - (g + 1) * DSA_HEAD_DIM)))
                          for g in range(DSA_KV_HEADS)], axis=1)
        ob = _dsa_cached(qi_l, w_l, cache_kidx, new_tile(ki), qb_l, cache_kb2, new_tile(kb), cache_vb2, new_tile(vb),
                         bias_s, layer=l, ts=t_s, tk=TK, n_valid_last=t_s, topk=topk_s)
        ob = ob.reshape(bs, DSA_KV_HEADS, DSA_HEAD_DIM, DSA_GROUP, t_s).transpose(0, 4, 1, 3, 2).reshape(1, n_s, W_B)
        xs = _out(xs, shift, scale, gate, gn, oa, ob, wg[l], woa[l], wob[l], wout[l], gf, tm=n_s, final=final)

    def cache_outputs(stacks, nb, t):
        ckv, kr, kb, vb, ki = (a.reshape((depth, nb, t, a.shape[-1])) for a in stacks)
        kvh = (depth, nb, t, DSA_KV_HEADS, DSA_HEAD_DIM)
        return ckv, kr, kb.reshape(kvh), vb.reshape(kvh), ki

    return (xp, xs.reshape(bs, t_s, D_MODEL)) + cache_outputs(caches_p, bp, t_p) + cache_outputs(caches_s, bs, t_s)
```

```python
import functools

import numpy as np
import jax
import jax.numpy as jnp
from jax import lax
from jax.experimental import pallas as pl
from jax.experimental.pallas import tpu as pltpu

F32 = jnp.float32
BF16 = jnp.bfloat16

D_MODEL = 1024
CHUNK = 64
EPS = 1e-6
MLA_HEADS = 8
NOPE_DIM = 64
ROPE_DIM = 32
V_DIM = 64
Q_LORA = 384
KV_LORA = 256
ROPE_THETA = 10000.0
MLA_SCALE = (NOPE_DIM + ROPE_DIM) ** -0.5
DSA_HEADS = 8
DSA_KV_HEADS = 2
DSA_GROUP = DSA_HEADS // DSA_KV_HEADS
DSA_HEAD_DIM = 64
DSA_SCALE = DSA_HEAD_DIM ** -0.5
IDX_HEADS = 8
IDX_DIM = 64
IDX_W_SCALE = (IDX_HEADS ** -0.5) * (IDX_DIM ** -0.5)
TOPK_MAX = 256
NUM_BUCKETS = 32
MAX_DISTANCE = 128
W_A = MLA_HEADS * V_DIM
W_B = DSA_HEADS * DSA_HEAD_DIM
SPLIT_SIZES = (Q_LORA, KV_LORA, ROPE_DIM, W_A, W_B, DSA_KV_HEADS * DSA_HEAD_DIM, DSA_KV_HEADS * DSA_HEAD_DIM,
               IDX_HEADS * IDX_DIM, IDX_DIM, IDX_HEADS, W_B, D_MODEL, D_MODEL)

LANES = 128
SUBLANES = 8
VMEM_LIMIT_BYTES = 56 * 1024 * 1024

TK = 256
TM = 512
NEG = -1e30
LOG2E = 1.4426950408889634
MAX_BISECT = 200
BISECT_PER_CHECK = 4

QK_PAD = MLA_HEADS * LANES
KV_DIM = DSA_KV_HEADS * DSA_HEAD_DIM

A_CQ = (0, Q_LORA)
A_CKV = (A_CQ[1], A_CQ[1] + KV_LORA)
A_KR = (A_CKV[1], A_CKV[1] + LANES)
A_SM = (A_KR[1], A_KR[1] + LANES)
A_QB = (A_SM[1], A_SM[1] + W_B)
A_KB = (A_QB[1], A_QB[1] + KV_DIM)
A_VB = (A_KB[1], A_KB[1] + KV_DIM)
A_QI = (A_VB[1], A_VB[1] + IDX_HEADS * IDX_DIM)
A_WIDTH = A_QI[1]
G_GA = (0, W_A)
G_GB = (G_GA[1], G_GA[1] + W_B)
G_MA = (G_GB[1], G_GB[1] + D_MODEL)
G_MB = (G_MA[1], G_MA[1] + D_MODEL)
G_WIDTH = G_MB[1]


def _params(n_grid):
    return pltpu.CompilerParams(dimension_semantics=("arbitrary",) * n_grid, vmem_limit_bytes=VMEM_LIMIT_BYTES)


def _sigmoid(x):
    return 1.0 / (1.0 + jnp.exp(-x))


def _rms(x, g):
    return x * lax.rsqrt(jnp.mean(x * x, axis=-1, keepdims=True) + EPS) * g


def _dot(a, b):
    return jnp.dot(a, b, preferred_element_type=F32)


def _dot_nt(a, b):
    return lax.dot_general(a, b, (((1,), (1,)), ((), ())), preferred_element_type=F32)


def _key_rows(j, tk):
    start = j * tk if isinstance(j, int) else pl.multiple_of(j * tk, tk)
    return pl.ds(start, tk)


def _rows8(x, op, ways=None):
    parts = [x[r:r + SUBLANES] for r in range(0, x.shape[0], SUBLANES)]
    if ways is not None:
        chains = parts[:ways]
        for n, part in enumerate(parts[ways:]):
            chains[n % ways] = op(chains[n % ways], part)
        parts = chains
    while len(parts) > 1:
        parts = [op(parts[n], parts[n + 1]) for n in range(0, len(parts) - 1, 2)] + parts[len(parts) & ~1:]
    return parts[0]


def _mod_kernel(c_ref, w_ref, b_ref, o_ref):
    c = c_ref[...]
    a = c * _sigmoid(c)
    w = w_ref[0]
    a_hi = a.astype(BF16)
    a_lo = (a - a_hi.astype(F32)).astype(BF16)
    w_hi = w.astype(BF16)
    w_lo = (w - w_hi.astype(F32)).astype(BF16)
    o_ref[0] = _dot(a_hi, w_hi) + (_dot(a_hi, w_lo) + _dot(a_lo, w_hi)) + b_ref[0]


def _modulation(c_all, w_ada, b_ada):
    depth = w_ada.shape[0]
    n = c_all.shape[0]
    tn = D_MODEL
    return pl.pallas_call(
        _mod_kernel,
        out_shape=jax.ShapeDtypeStruct((depth, n, 3 * D_MODEL), F32),
        grid=(depth, 3 * D_MODEL // tn),
        in_specs=[
            pl.BlockSpec((n, D_MODEL), lambda l, j: (0, 0)),
            pl.BlockSpec((1, D_MODEL, tn), lambda l, j: (l, 0, j)),
            pl.BlockSpec((1, 1, tn), lambda l, j: (l, 0, j)),
        ],
        out_specs=pl.BlockSpec((1, n, tn), lambda l, j: (l, 0, j)),
        compiler_params=_params(2),
        name="adaln_mod",
    )(c_all, w_ada, b_ada.reshape(depth, 1, 3 * D_MODEL))


def _rel_bucket(rel):
    half = NUM_BUCKETS // 2
    max_exact = half // 2
    n = np.abs(rel)
    large = max_exact + (np.log(np.maximum(n, 1).astype(np.float32) / max_exact)
                         / np.float32(np.log(MAX_DISTANCE / max_exact)) * (half - max_exact)).astype(np.int32)
    large = np.minimum(large, half - 1)
    return np.where(n < max_exact, n, large) + np.where(rel > 0, half, 0)


def _bias_kernel(bucket_ref, rel_ref, o_ref, *, far_bucket):
    h = pl.program_id(0)
    for t in range(2):
        b = bucket_ref[t]
        acc = jnp.zeros(b.shape, F32)
        for i in range(NUM_BUCKETS):
            acc = jnp.where(b == i, rel_ref[i, h], acc)
        o_ref[0, t] = (acc - rel_ref[far_bucket, h]) * LOG2E


def _bias_tiles(rel_bias, tk, tq):
    r = np.arange(tk)[:, None]
    c = np.arange(tq)[None, :]
    far_bucket = int(_rel_bucket(np.array(-tk)))
    assert np.all(_rel_bucket(-tk - np.arange(1, tk + tq)) == far_bucket)
    buckets = jnp.asarray(np.stack([_rel_bucket(r - tk - c), _rel_bucket(r - c)]).astype(np.int32))
    return pl.pallas_call(
        functools.partial(_bias_kernel, far_bucket=far_bucket),
        out_shape=jax.ShapeDtypeStruct((DSA_HEADS, 2, tk, tq), F32),
        grid=(DSA_HEADS,),
        in_specs=[
            pl.BlockSpec((2, tk, tq), lambda h: (0, 0, 0)),
            pl.BlockSpec(memory_space=pltpu.SMEM),
        ],
        out_specs=pl.BlockSpec((1, 2, tk, tq), lambda h: (h, 0, 0, 0)),
        compiler_params=_params(1),
        name="t5_bias_tiles",
    )(buckets, rel_bias)


def _rope_group(g, c_ref, s1_ref, s2_ref):
    return (g * c_ref[...] + pltpu.roll(g, LANES - ROPE_DIM // 2, 1) * s1_ref[...]
            + pltpu.roll(g, ROPE_DIM // 2, 1) * s2_ref[...])


def _proj_kernel(x_ref, shift_ref, scale_ref, gn_ref, wa_ref, gq_ref, wuq_ref, gkv_ref, wkn_ref, wv_ref,
                 qc_ref, qs1_ref, qs2_ref, kc_ref, ks1_ref, ks2_ref, *rest, emit_kv, tk, n_aliased):
    ckv_o, kr_o, kb_o, vb_o, ki_o, q_o, qb_o, qi_o, wit_o, *kv_outs = rest[n_aliased:]

    def put(cache_o, value):
        for n in range(cache_o.shape[0]):
            cache_o[n, 0] = value

    x = x_ref[0]
    h = _rms(x, gn_ref[...]) * (1.0 + scale_ref[0]) + shift_ref[0]
    h16 = h.astype(BF16)

    def zcols(a):
        return _dot(h16, wa_ref[:, a[0]:a[1]])

    cqn = _rms(zcols(A_CQ), gq_ref[...]).astype(BF16)
    q = _dot(cqn, wuq_ref[...])
    for hd in range(MLA_HEADS):
        sl = slice(hd * LANES, (hd + 1) * LANES)
        q_hd = _rope_group(q[:, sl], qc_ref, qs1_ref, qs2_ref)
        if emit_kv:
            q_o[0, sl, :] = q_hd.T.astype(BF16)
        else:
            q_o[0, :, sl] = q_hd.astype(BF16)

    ckvn = _rms(zcols(A_CKV), gkv_ref[...])
    put(ckv_o, ckvn)
    krg = _rope_group(zcols(A_KR), kc_ref, ks1_ref, ks2_ref)
    put(kr_o, krg[:, 0:ROPE_DIM])

    sm = zcols(A_SM)
    ki = sm[:, 0:IDX_DIM]
    put(ki_o, ki)
    wit_o[0] = sm.T[IDX_DIM:IDX_DIM + IDX_HEADS, :] * IDX_W_SCALE
    qb = zcols(A_QB) * (DSA_SCALE * LOG2E)
    qi = zcols(A_QI)
    qb_o[0] = (qb.T if emit_kv else qb).astype(BF16)
    qi_o[0] = (qi.T if emit_kv else qi).astype(BF16)
    kb = zcols(A_KB)
    vb = zcols(A_VB)
    put(kb_o, kb)
    put(vb_o, vb)

    if emit_kv:
        kcat_o, vat_o, kb16_o, vbt_o, ki16_o = kv_outs
        ckv16 = ckvn.astype(BF16)
        kn = _dot(ckv16, wkn_ref[...])
        lane = lax.broadcasted_iota(jnp.int32, krg.shape, 1)
        k_add = jnp.where(lane >= NOPE_DIM, krg, 0.0)
        for hd in range(MLA_HEADS):
            sl = slice(hd * LANES, (hd + 1) * LANES)
            kcat_o[0, :, sl] = (kn[:, sl] + k_add).astype(BF16)
        va = _dot(ckv16, wv_ref[...])
        for g in range(DSA_KV_HEADS):
            kb16_o[0, g] = kb[:, g * DSA_HEAD_DIM:(g + 1) * DSA_HEAD_DIM].astype(BF16)
        ki16_o[0] = ki.astype(BF16)
        for c in range(x.shape[0] // tk):
            rows = slice(c * tk, (c + 1) * tk)
            vat_o[0, c] = va[rows, :].T.astype(BF16)
            vbt_o[0, c] = vb[rows, :].T.astype(BF16)


def _proj(x, shift, scale, gn, wa, gq, wuq, gkv, wkn, wv, q_tabs, k_tabs, caches, *, layer, depth, tm, tk, emit_kv):
    nb, t, _ = x.shape
    r = shift.shape[1]
    rb = 1 if r == 1 else tm
    grid = (nb, t // tm)
    tok = lambda w: pl.BlockSpec((1, tm, w), lambda b, i: (b, i, 0))
    if caches is None:
        stacked = lambda w: pl.BlockSpec((depth, 1, tm, w), lambda b, i: (0, b, i, 0))
    else:
        stacked = lambda w: pl.BlockSpec((1, 1, tm, w), lambda b, i: (layer, b, i, 0))
    mod = pl.BlockSpec((1, rb, D_MODEL), (lambda b, i: (b, 0, 0)) if r == 1 else (lambda b, i: (b, i, 0)))
    const = lambda a: pl.BlockSpec(a.shape, lambda b, i: (0,) * a.ndim)
    tab = pl.BlockSpec((tm, LANES), lambda b, i: (i, 0))
    in_specs = [tok(D_MODEL), mod, mod, const(gn), const(wa), const(gq), const(wuq), const(gkv), const(wkn),
                const(wv)] + [tab] * 6
    operands = [x, shift, scale, gn, wa, gq, wuq, gkv, wkn, wv, *q_tabs, *k_tabs]
    cache_widths = (KV_LORA, ROPE_DIM, KV_DIM, KV_DIM, IDX_DIM)
    aliases = {}
    if caches is not None:
        aliases = {len(operands) + n: n for n in range(len(cache_widths))}
        in_specs += [pl.BlockSpec(memory_space=pl.ANY)] * len(cache_widths)
        operands += list(caches)
    tok_t = lambda w: pl.BlockSpec((1, w, tm), lambda b, i: (b, 0, i))
    q_shape = (lambda w: (nb, w, t)) if emit_kv else (lambda w: (nb, t, w))
    q_spec = tok_t if emit_kv else tok
    q_widths = (QK_PAD, W_B, IDX_HEADS * IDX_DIM)
    out_shape = ([jax.ShapeDtypeStruct((depth, nb, t, w), F32) for w in cache_widths]
                 + [jax.ShapeDtypeStruct(q_shape(w), BF16) for w in q_widths]
                 + [jax.ShapeDtypeStruct((nb, IDX_HEADS, t), F32)])
    out_specs = [stacked(w) for w in cache_widths] + [q_spec(w) for w in q_widths] + [tok_t(IDX_HEADS)]
    if emit_kv:
        nkt = tm // tk
        out_shape += [
            jax.ShapeDtypeStruct((nb, t, QK_PAD), BF16),
            jax.ShapeDtypeStruct((nb, t // tk, W_A, tk), BF16),
            jax.ShapeDtypeStruct((nb, DSA_KV_HEADS, t, DSA_HEAD_DIM), BF16),
            jax.ShapeDtypeStruct((nb, t // tk, KV_DIM, tk), BF16),
            jax.ShapeDtypeStruct((nb, t, IDX_DIM), BF16),
        ]
        out_specs += [tok(QK_PAD), pl.BlockSpec((1, nkt, W_A, tk), lambda b, i: (b, i, 0, 0)),
                      pl.BlockSpec((1, DSA_KV_HEADS, tm, DSA_HEAD_DIM), lambda b, i: (b, 0, i, 0)),
                      pl.BlockSpec((1, nkt, KV_DIM, tk), lambda b, i: (b, i, 0, 0)), tok(IDX_DIM)]
    return pl.pallas_call(
        functools.partial(_proj_kernel, emit_kv=emit_kv, tk=tk, n_aliased=len(aliases)),
        out_shape=out_shape, grid=grid, in_specs=in_specs, out_specs=out_specs, input_output_aliases=aliases,
        compiler_params=_params(2), name="proj_kv" if emit_kv else "proj_q",
    )(*operands)


class _Softmax:
    ONES_ROWS = 16

    def __init__(self, s_scr, mx_scr, m_scr, acc_scr):
        self.s, self.mx, self.m, self.acc = s_scr, mx_scr, m_scr, acc_scr
        self.n_heads = acc_scr.shape[0]
        self.dv = acc_scr.shape[1] - self.ONES_ROWS

    @classmethod
    def scratch(cls, n_heads, dv, tk, tq):
        stat = pltpu.VMEM((n_heads, SUBLANES, tq), F32)
        return [pltpu.VMEM((n_heads, 2, tk, tq), F32), stat, stat, pltpu.VMEM((n_heads, dv + cls.ONES_ROWS, tq), F32)]

    def reset(self):
        self.mx[...] = jnp.full(self.mx.shape, NEG, F32)
        self.m[...] = jnp.full(self.m.shape, NEG, F32)
        self.acc[...] = jnp.zeros(self.acc.shape, F32)

    def produce(self, hd, slot, s):
        self.s[hd, slot] = s
        self.mx[hd] = jnp.maximum(self.mx[hd], _rows8(s, jnp.maximum, ways=4))

    def consume(self, hd, slot, vt):
        m_old = self.m[hd, 0:1, :]
        m_new = jnp.max(self.mx[hd], axis=0, keepdims=True)
        alpha = jnp.exp2(m_old - m_new)
        p = jnp.exp2(self.s[hd, slot] - m_new)
        self.m[hd] = jnp.broadcast_to(m_new, self.m.shape[1:])
        vt1 = jnp.concatenate([vt, jnp.ones((self.ONES_ROWS, vt.shape[1]), vt.dtype)], axis=0)
        self.acc[hd] = alpha * self.acc[hd] + _dot(vt1, p.astype(BF16))

    def run(self, last, logits, values):
        def step(t_consume, t_produce, kind, slot_produce):
            for hd in range(self.n_heads):
                if t_consume is not None:
                    self.consume(hd, 1 - slot_produce, values(hd, t_consume))
                if t_produce is not None:
                    self.produce(hd, slot_produce, logits(hd, t_produce, kind))

        when = lambda cond, fn: pl.when(cond)(fn)
        self.reset()
        step(None, last, 2, 0)
        when(last >= 1, lambda: step(last, last - 1, 1, 1))
        n_far = last - 1

        def far_pair(n, _):
            t = last - 2 - 2 * n
            step(t + 1, t, 0, 0)
            step(t, t - 1, 0, 1)
            return 0

        lax.fori_loop(0, n_far // 2, far_pair, 0)
        odd = (n_far >= 1) & (n_far % 2 == 1)
        when(odd, lambda: step(1, 0, 0, 0))
        in_slot0 = (last == 0) | odd
        when(in_slot0, lambda: step(0, None, None, 1))
        when(jnp.logical_not(in_slot0), lambda: step(0, None, None, 0))

    def finish(self, o_ref):
        o_t = jnp.concatenate([self.acc[hd, 0:self.dv, :] / self.acc[hd, self.dv:self.dv + 1, :]
                               for hd in range(self.n_heads)], axis=0)
        o_ref[0] = o_t.T


def _diagonal_tile_visible(t):
    r = lax.broadcasted_iota(jnp.int32, (t, t), 0)
    c = lax.broadcasted_iota(jnp.int32, (t, t), 1)
    return (r // CHUNK) <= (c // CHUNK)


def _mla_kernel(q_ref, k_ref, vt_ref, o_ref, *scratch, t):
    softmax = _Softmax(*scratch)

    def logits(hd, j, kind):
        sl = slice(hd * LANES, (hd + 1) * LANES)
        s = _dot(k_ref[0, _key_rows(j, t), sl], q_ref[0, sl, :])
        return s + jnp.where(_diagonal_tile_visible(t), 0.0, NEG) if kind == 2 else s

    softmax.run(pl.program_id(1), logits, lambda hd, j: vt_ref[0, j, hd * V_DIM:(hd + 1) * V_DIM, :])
    softmax.finish(o_ref)


def _mla(q, k, vt, *, t):
    nb, _, seq = q.shape
    return pl.pallas_call(
        functools.partial(_mla_kernel, t=t),
        out_shape=jax.ShapeDtypeStruct((nb, seq, W_A), F32), grid=(nb, seq // t),
        in_specs=[pl.BlockSpec((1, QK_PAD, t), lambda b, i: (b, 0, i)),
                  pl.BlockSpec((1, seq, QK_PAD), lambda b, i: (b, 0, 0)),
                  pl.BlockSpec((1, seq // t, W_A, t), lambda b, i: (b, 0, 0, 0))],
        out_specs=pl.BlockSpec((1, t, W_A), lambda b, i: (b, i, 0)),
        scratch_shapes=_Softmax.scratch(MLA_HEADS, V_DIM, t, t),
        compiler_params=_params(2), name="mla_causal",
    )(q, k, vt)


def _score_stats_init(tq):
    return (jnp.full((1, tq), jnp.inf, F32), jnp.full((1, tq), -jnp.inf, F32),
            jnp.zeros((SUBLANES, tq), F32), jnp.zeros((SUBLANES, tq), F32))


def _score_stats(stats, t, visible=None):
    lo, hi, c_z, c_zp = stats
    t_min = t if visible is None else jnp.where(visible, t, jnp.inf)
    return (jnp.minimum(lo, jnp.min(t_min, axis=0, keepdims=True)), jnp.maximum(hi, jnp.max(t, axis=0, keepdims=True)),
            c_z + _rows8(jnp.where(t >= 0.0, 1.0, 0.0), jnp.add), c_zp + _rows8(jnp.where(t > 0.0, 1.0, 0.0), jnp.add))


def _select_topk(idx_scr, n_used, stats, n_vis, kf):
    _, tk, tq = idx_scr.shape

    def count(pred):
        def body(j, c):
            return c + _rows8(jnp.where(pred(idx_scr[j]), 1.0, 0.0), jnp.add)
        return jnp.sum(lax.fori_loop(0, n_used, body, jnp.zeros((SUBLANES, tq), F32)), axis=0, keepdims=True)

    lo, hi, c_z, c_zp = stats
    c_z = jnp.sum(c_z, axis=0, keepdims=True)
    c_zp = jnp.sum(c_zp, axis=0, keepdims=True)
    hi = hi + (jnp.abs(hi) * 2.0 ** -20 + 1e-30)
    zero_up = (c_z >= kf) & (lo < 0.0)
    zero_tie = jnp.where((c_z >= kf) & (c_zp < kf), 1.0, 0.0)
    c_lo = jnp.where(zero_up, c_z, n_vis)
    lo = jnp.where(zero_up, 0.0, lo)
    hi = jnp.where((c_z < kf) & (hi > 0.0), 0.0, hi)

    def settled(lo, hi, c_lo):
        mid = 0.5 * lo + 0.5 * hi
        return jnp.where((c_lo <= kf) | (mid <= lo) | (mid >= hi), 1.0, 0.0)

    def bis_cond(st):
        _, _, _, done, it = st
        return jnp.logical_and(it < MAX_BISECT, jnp.min(done) < 0.5)

    def bis_body(st):
        lo, hi, c_lo, done, it = st
        for _ in range(BISECT_PER_CHECK):
            mid = 0.5 * lo + 0.5 * hi
            c = count(lambda t: t >= mid)
            live = done < 0.5
            up = live & (c >= kf)
            dn = live & (c < kf)
            lo = jnp.where(up, mid, lo)
            c_lo = jnp.where(up, c, c_lo)
            hi = jnp.where(dn, mid, hi)
            done = jnp.maximum(done, settled(lo, hi, c_lo))
        return lo, hi, c_lo, done, it + BISECT_PER_CHECK

    done = jnp.maximum(settled(lo, hi, c_lo), zero_tie)
    lo, hi, c_lo, _, _ = lax.while_loop(bis_cond, bis_body, (lo, hi, c_lo, done, jnp.int32(0)))
    any_tie = jnp.max(jnp.where(c_lo > kf, 1.0, 0.0)) > 0.5

    @pl.when(jnp.logical_not(any_tie))
    def _():
        def body(j, _):
            idx_scr[j] = jnp.where(idx_scr[j] >= lo, 0.0, NEG)
            return 0
        lax.fori_loop(0, n_used, body, 0)

    @pl.when(any_tie)
    def _():
        need = kf - count(lambda t: t > lo)
        r = lax.broadcasted_iota(jnp.int32, (tk, tk), 0)
        c = lax.broadcasted_iota(jnp.int32, (tk, tk), 1)
        before = jnp.where(c < r, 1.0, 0.0).astype(BF16)

        def body(j, seen):
            t = idx_scr[j]
            eq = jnp.where(t == lo, 1.0, 0.0)
            rank = _dot(before, eq.astype(BF16)) + seen
            take = (t > lo) | ((t == lo) & (rank < need))
            idx_scr[j] = jnp.where(take, 0.0, NEG)
            return seen + jnp.sum(_rows8(eq, jnp.add), axis=0, keepdims=True)
        lax.fori_loop(0, n_used, body, jnp.zeros((1, tq), F32))


def _dsa_kernel(qi_ref, wit_ref, ki_ref, qb_ref, kb_ref, vbt_ref, bias_ref, o_ref, idx_scr, *scratch, t, topk):
    tq = tk = t
    last = pl.program_id(1)
    n_used = last + 1
    visible = _diagonal_tile_visible(t)

    wit = wit_ref[0]

    def idx_tile(j):
        k_j = ki_ref[0, _key_rows(j, tk), :]
        acc = jnp.zeros((tk, tq), F32)
        for hd in range(IDX_HEADS):
            s = _dot(k_j, qi_ref[0, hd * IDX_DIM:(hd + 1) * IDX_DIM, :])
            acc = acc + jnp.maximum(s, 0.0) * wit[hd:hd + 1, :]
        return acc

    def idx_body(j, stats):
        t = idx_tile(j)
        idx_scr[j] = t
        return _score_stats(stats, t)

    def idx_pair(n, stats):
        return idx_body(2 * n + 1, idx_body(2 * n, stats))

    stats = lax.fori_loop(0, last // 2, idx_pair, _score_stats_init(tq))
    stats = lax.cond(last % 2 == 1, lambda st: idx_body(last - 1, st), lambda st: st, stats)
    t_last = jnp.where(visible, idx_tile(last), -jnp.inf)
    idx_scr[last] = t_last
    stats = _score_stats(stats, t_last, visible)
    lane = lax.broadcasted_iota(jnp.int32, (1, tq), 1)
    n_vis = last * tk + (lane // CHUNK + 1) * CHUNK
    _select_topk(idx_scr, n_used, stats, n_vis.astype(F32), float(topk))

    softmax = _Softmax(*scratch)
    kv_lanes = lambda hd: slice((hd // DSA_GROUP) * DSA_HEAD_DIM, (hd // DSA_GROUP + 1) * DSA_HEAD_DIM)

    def logits(hd, t, kind):
        s = _dot(kb_ref[0, hd // DSA_GROUP, _key_rows(t, tk), :], qb_ref[0, hd * DSA_HEAD_DIM:(hd + 1) * DSA_HEAD_DIM, :])
        return s + idx_scr[t] if kind == 0 else s + (idx_scr[t] + bias_ref[hd, kind - 1])

    softmax.run(last, logits, lambda hd, t: vbt_ref[0, t, kv_lanes(hd), :])
    softmax.finish(o_ref)


def _dsa(qi, wit, ki, qb, kb, vbt, bias, *, t, topk):
    nb, _, seq = qb.shape
    n_tiles = seq // t
    return pl.pallas_call(
        functools.partial(_dsa_kernel, t=t, topk=topk),
        out_shape=jax.ShapeDtypeStruct((nb, seq, W_B), F32), grid=(nb, n_tiles),
        in_specs=[pl.BlockSpec((1, IDX_HEADS * IDX_DIM, t), lambda b, i: (b, 0, i)),
                  pl.BlockSpec((1, IDX_HEADS, t), lambda b, i: (b, 0, i)),
                  pl.BlockSpec((1, seq, IDX_DIM), lambda b, i: (b, 0, 0)),
                  pl.BlockSpec((1, W_B, t), lambda b, i: (b, 0, i)),
                  pl.BlockSpec((1, DSA_KV_HEADS, seq, DSA_HEAD_DIM), lambda b, i: (b, 0, 0, 0)),
                  pl.BlockSpec((1, n_tiles, KV_DIM, t), lambda b, i: (b, 0, 0, 0)),
                  pl.BlockSpec(bias.shape, lambda b, i: (0, 0, 0, 0))],
        out_specs=pl.BlockSpec((1, t, W_B), lambda b, i: (b, i, 0)),
        scratch_shapes=[pltpu.VMEM((n_tiles, t, t), F32)] + _Softmax.scratch(DSA_HEADS, DSA_HEAD_DIM, t, t),
        compiler_params=_params(2), name="dsa_causal",
    )(qi, wit, ki, qb, kb, vbt, bias)


def _two_pass_softmax(n_tiles, logits, values, s_scr, dv):
    mx = None
    for t in range(n_tiles):
        s = logits(t)
        s_scr[t] = s
        part = _rows8(s, jnp.maximum, ways=4)
        mx = part if mx is None else jnp.maximum(mx, part)
    m = jnp.max(mx, axis=0, keepdims=True)
    acc = None
    for t in range(n_tiles):
        p = jnp.exp2(s_scr[t] - m).astype(BF16)
        vt = values(t)
        part = _dot(jnp.concatenate([vt, jnp.ones((_Softmax.ONES_ROWS, vt.shape[1]), BF16)], axis=0), p)
        acc = part if acc is None else acc + part
    return acc[0:dv] / acc[dv:dv + 1]


def _mla_cached_kernel(q_ref, ckv_c_ref, kr_c_ref, ckv_n_ref, kr_n_ref, wuk_ref, wuv_ref, sel_ref, o_ref, s_scr, *,
                       ts, tk, n_cache, n_valid_last):
    lanes = MLA_HEADS * ts
    q_lat, q_rope = [], []
    for hd in range(MLA_HEADS):
        qg = q_ref[0, :, hd * LANES:(hd + 1) * LANES]
        q_lat.append(_dot(qg, wuk_ref[hd]).astype(BF16))
        q_rope.append(_dot(qg, sel_ref[...]).astype(BF16))
    q_lat = jnp.concatenate(q_lat, axis=0)
    q_rope = jnp.concatenate(q_rope, axis=0)

    def latent(t):
        return ckv_c_ref[0, t * tk:(t + 1) * tk, :] if t < n_cache else ckv_n_ref[0]

    def logits(t):
        kr = kr_c_ref[0, t * tk:(t + 1) * tk, :] if t < n_cache else kr_n_ref[0]
        s = _dot_nt(latent(t).astype(BF16), q_lat) + _dot_nt(kr.astype(BF16), q_rope)
        if t == n_cache:
            s = s + jnp.where(lax.broadcasted_iota(jnp.int32, s.shape, 0) < n_valid_last, 0.0, NEG)
        return s

    o_lat = _two_pass_softmax(n_cache + 1, logits, lambda t: latent(t).T.astype(BF16), s_scr, KV_LORA)
    o_lat = o_lat.astype(BF16)
    head_of_lane = lax.broadcasted_iota(jnp.int32, (V_DIM, lanes), 1) // ts
    out = jnp.zeros((V_DIM, lanes), F32)
    for hd in range(MLA_HEADS):
        out = jnp.where(head_of_lane == hd, _dot(wuv_ref[hd], o_lat), out)
    o_ref[0] = out


def _mla_cached(q, ckv_cache, kr_cache, ckv_new, kr_new, wuk, wuv, sel, *, layer, tk, n_valid_last):
    nb, ts, _ = q.shape
    past = ckv_cache.shape[2]
    lanes = MLA_HEADS * ts
    per_b = lambda a: pl.BlockSpec((1,) + a.shape[1:], lambda b: (b,) + (0,) * (a.ndim - 1))
    const = lambda a: pl.BlockSpec(a.shape, lambda b: (0,) * a.ndim)
    return pl.pallas_call(
        functools.partial(_mla_cached_kernel, ts=ts, tk=tk, n_cache=past // tk, n_valid_last=n_valid_last),
        out_shape=jax.ShapeDtypeStruct((nb, V_DIM, lanes), F32), grid=(nb,),
        in_specs=[per_b(q), _layer_rows(ckv_cache, layer), _layer_rows(kr_cache, layer), per_b(ckv_new),
                  per_b(kr_new), const(wuk), const(wuv), const(sel)],
        out_specs=pl.BlockSpec((1, V_DIM, lanes), lambda b: (b, 0, 0)),
        scratch_shapes=[pltpu.VMEM((past // tk + 1, tk, lanes), F32)],
        compiler_params=_params(1), name="mla_cached",
    )(q, ckv_cache, kr_cache, ckv_new, kr_new, wuk, wuv, sel)


def _dsa_cached_kernel(qi_ref, w_ref, ki_c_ref, ki_n_ref, qb_ref, kb_c_ref, kb_n_ref, vb_c_ref, vb_n_ref, bias_ref,
                       o_ref, idx_scr, s_scr, *, ts, tk, n_cache, n_valid_last, topk):
    n_tiles = n_cache + 1
    lanes = DSA_GROUP * ts
    tile = lambda c_ref, n_ref, t: c_ref[0, t * tk:(t + 1) * tk, :] if t < n_cache else n_ref[0]
    visible = lax.broadcasted_iota(jnp.int32, (tk, lanes), 0) < n_valid_last

    stats = _score_stats_init(lanes)
    for t in range(n_tiles):
        s = _dot_nt(tile(ki_c_ref, ki_n_ref, t).astype(BF16), qi_ref[0])
        r = jnp.maximum(s, 0.0) * w_ref[0]
        x = r[:, 0:lanes]
        for blk in range(1, IDX_HEADS * ts // lanes):
            x = x + r[:, blk * lanes:(blk + 1) * lanes]
        shift = lanes // 2
        while shift >= ts:
            x = x + pltpu.roll(x, shift, 1)
            shift //= 2
        if t == n_cache:
            x = jnp.where(visible, x, -jnp.inf)
        idx_scr[t] = x
        stats = _score_stats(stats, x, visible if t == n_cache else None)
    n_vis = jnp.full((1, lanes), float(n_cache * tk + n_valid_last), F32)
    _select_topk(idx_scr, n_tiles, stats, n_vis, float(topk))

    for g in range(DSA_KV_HEADS):
        def logits(t, g=g):
            s = _dot_nt(tile(kb_c_ref, kb_n_ref, t).astype(BF16), qb_ref[0, g]) + idx_scr[t]
            return s + bias_ref[g, t - (n_tiles - 2)] if t >= n_tiles - 2 else s

        def values(t, g=g):
            return tile(vb_c_ref, vb_n_ref, t).T[g * DSA_HEAD_DIM:(g + 1) * DSA_HEAD_DIM, :].astype(BF16)

        o_ref[0, g] = _two_pass_softmax(n_tiles, logits, values, s_scr, DSA_HEAD_DIM)


def _layer_rows(cache, layer):
    return pl.BlockSpec((None, 1) + cache.shape[2:], lambda b: (layer, b, 0, 0))


def _dsa_cached(qi, w, ki_cache, ki_new, qb, kb_cache, kb_new, vb_cache, vb_new, bias, *, layer, ts, tk,
                n_valid_last, topk):
    nb = qi.shape[0]
    past = ki_cache.shape[2]
    n_tiles = past // tk + 1
    lanes = DSA_GROUP * ts
    per_b = lambda a: pl.BlockSpec((1,) + a.shape[1:], lambda b: (b,) + (0,) * (a.ndim - 1))
    cached = lambda a: _layer_rows(a, layer)
    return pl.pallas_call(
        functools.partial(_dsa_cached_kernel, ts=ts, tk=tk, n_cache=past // tk, n_valid_last=n_valid_last, topk=topk),
        out_shape=jax.ShapeDtypeStruct((nb, DSA_KV_HEADS, DSA_HEAD_DIM, lanes), F32), grid=(nb,),
        in_specs=[per_b(qi), per_b(w), cached(ki_cache), per_b(ki_new), per_b(qb), cached(kb_cache), per_b(kb_new),
                  cached(vb_cache), per_b(vb_new), pl.BlockSpec(bias.shape, lambda b: (0,) * bias.ndim)],
        out_specs=pl.BlockSpec((1, DSA_KV_HEADS, DSA_HEAD_DIM, lanes), lambda b: (b, 0, 0, 0)),
        scratch_shapes=[pltpu.VMEM((n_tiles, tk, lanes), F32), pltpu.VMEM((n_tiles, tk, lanes), F32)],
        compiler_params=_params(1), name="dsa_cached",
    )(qi, w, ki_cache, ki_new, qb, kb_cache, kb_new, vb_cache, vb_new, bias)


def _out_kernel(x_ref, shift_ref, scale_ref, gate_ref, gn_ref, oa_ref, ob_ref, wg_ref, woa_ref, wob_ref, wout_ref,
                gf_ref, o_ref, *, final):
    x = x_ref[0]
    h16 = (_rms(x, gn_ref[...]) * (1.0 + scale_ref[0]) + shift_ref[0]).astype(BF16)

    def gcols(a):
        return _dot(h16, wg_ref[:, a[0]:a[1]])

    ga = gcols(G_GA)
    ya = _dot((oa_ref[0] * (ga * _sigmoid(ga))).astype(BF16), woa_ref[...])
    gb = gcols(G_GB)
    yb = _dot((ob_ref[0] * (gb * _sigmoid(gb))).astype(BF16), wob_ref[...])
    m = _sigmoid(gcols(G_MA)) * ya + _sigmoid(gcols(G_MB)) * yb
    y = x + gate_ref[0] * _dot(m.astype(BF16), wout_ref[...])
    o_ref[0] = _rms(y, gf_ref[...]) if final else y


def _out(x, shift, scale, gate, gn, oa, ob, wg, woa, wob, wout, gf, *, tm, final):
    nb, t, _ = x.shape
    r = shift.shape[1]
    rb = 1 if r == 1 else tm
    tok = lambda w: pl.BlockSpec((1, tm, w), lambda b, i: (b, i, 0))
    mod = pl.BlockSpec((1, rb, D_MODEL), (lambda b, i: (b, 0, 0)) if r == 1 else (lambda b, i: (b, i, 0)))
    const = lambda a: pl.BlockSpec(a.shape, lambda b, i: (0,) * a.ndim)
    return pl.pallas_call(
        functools.partial(_out_kernel, final=final),
        out_shape=jax.ShapeDtypeStruct(x.shape, F32), grid=(nb, t // tm),
        in_specs=[tok(D_MODEL), mod, mod, mod, const(gn), tok(W_A), tok(W_B), const(wg), const(woa), const(wob),
                  const(wout), const(gf)],
        out_specs=tok(D_MODEL), compiler_params=_params(2), name="out_final" if final else "out_mix",
    )(x, shift, scale, gate, gn, oa, ob, wg, woa, wob, wout, gf)


def _pack_weights(w_in, w_uq, w_ukv, w_oa, w_ob, w_out):
    depth = w_in.shape[0]
    offs = np.concatenate([[0], np.cumsum(SPLIT_SIZES)])
    cq, ckv, kr, ga, qb, kb, vb, qi, ki, wi, gb, ma, mb = [w_in[:, :, offs[n]:offs[n + 1]] for n in range(13)]
    z = lambda w: jnp.zeros((depth, D_MODEL, w), F32)
    krg = jnp.concatenate([kr, z(NOPE_DIM - ROPE_DIM), kr, z(LANES - NOPE_DIM - ROPE_DIM)], axis=-1)
    sm = jnp.concatenate([ki, wi, z(LANES - IDX_DIM - IDX_HEADS)], axis=-1)
    wa = jnp.concatenate([cq, ckv, krg, sm, qb, kb, vb, qi], axis=-1).astype(BF16)
    wg = jnp.concatenate([ga, gb, ma, mb], axis=-1).astype(BF16)
    qd = NOPE_DIM + ROPE_DIM
    wuq = jnp.pad(w_uq.reshape(depth, Q_LORA, MLA_HEADS, qd), ((0, 0), (0, 0), (0, 0), (0, LANES - qd)))
    wuq = wuq.reshape(depth, Q_LORA, QK_PAD).astype(BF16)
    ukv = w_ukv.reshape(depth, KV_LORA, MLA_HEADS, NOPE_DIM + V_DIM)
    wkn = jnp.pad(ukv[..., :NOPE_DIM], ((0, 0), (0, 0), (0, 0), (0, LANES - NOPE_DIM)))
    wkn = wkn.reshape(depth, KV_LORA, QK_PAD).astype(BF16)
    wv = ukv[..., NOPE_DIM:].reshape(depth, KV_LORA, W_A).astype(BF16)
    wuk = jnp.pad(ukv[..., :NOPE_DIM].transpose(0, 2, 3, 1), ((0, 0), (0, 0), (0, LANES - NOPE_DIM), (0, 0)))
    wuv = ukv[..., NOPE_DIM:].transpose(0, 2, 3, 1)
    return (wa, wg, wuq, wkn, wv, w_oa.astype(BF16), w_ob.astype(BF16), w_out.astype(BF16), wuk.astype(BF16),
            wuv.astype(BF16))


def _rope_tables(pos):
    half = ROPE_DIM // 2
    inv = ROPE_THETA ** (-jnp.arange(half, dtype=F32) / half)
    ang = pos.astype(F32)[:, None] * inv
    cos, sin = jnp.cos(ang), jnp.sin(ang)
    n = pos.shape[0]
    z = lambda w: jnp.zeros((n, w), F32)
    pad = LANES - NOPE_DIM - ROPE_DIM
    q_c = jnp.concatenate([jnp.ones((n, NOPE_DIM), F32), cos, cos, z(pad)], axis=-1) * (MLA_SCALE * LOG2E)
    q_s1 = jnp.concatenate([z(NOPE_DIM), -sin, z(half), z(pad)], axis=-1) * (MLA_SCALE * LOG2E)
    q_s2 = jnp.concatenate([z(NOPE_DIM), z(half), sin, z(pad)], axis=-1) * (MLA_SCALE * LOG2E)
    gap = NOPE_DIM - ROPE_DIM
    k_c = jnp.concatenate([cos, cos, z(gap), cos, cos, z(pad)], axis=-1)
    k_s1 = jnp.concatenate([-sin, z(half), z(gap), -sin, z(half), z(pad)], axis=-1)
    k_s2 = jnp.concatenate([z(half), sin, z(gap), z(half), sin, z(pad)], axis=-1)
    return (q_c, q_s1, q_s2), (k_c, k_s1, k_s2)


def _rope_selector():
    p = np.zeros((LANES, ROPE_DIM), np.float32)
    p[NOPE_DIM + np.arange(ROPE_DIM), np.arange(ROPE_DIM)] = 1.0
    return jnp.asarray(p, dtype=BF16)


def kernel(x_prompt, x_sample, c_prompt, c_sample, cache_ckv, cache_krope, cache_kb, cache_vb, cache_kidx, w_ada,
           b_ada, g_norm, w_in, g_qnorm, w_uq, g_kvnorm, w_ukv, w_oa, w_ob, w_out, rel_bias, g_final):
    depth = w_in.shape[0]
    bp, t_p, _ = x_prompt.shape
    bs, t_s, _ = x_sample.shape
    past = cache_ckv.shape[2]
    topk_p = min(TOPK_MAX, t_p // 4)
    topk_s = min(TOPK_MAX, (past + t_s) // 4)
    assert t_p % TM == 0 and TM % TK == 0 and past % TK == 0 and t_s <= TK and t_s % 8 == 0
    assert (DSA_GROUP * t_s) % LANES == 0 and (DSA_GROUP * t_s) & (DSA_GROUP * t_s - 1) == 0
    n_s = bs * t_s

    wa, wg, wuq, wkn, wv, woa, wob, wout, wuk, wuv = _pack_weights(w_in, w_uq, w_ukv, w_oa, w_ob, w_out)
    rope_sel = _rope_selector()
    cache_kb2 = cache_kb.reshape(depth, bs, past, KV_DIM)
    cache_vb2 = cache_vb.reshape(depth, bs, past, KV_DIM)
    mod = _modulation(jnp.concatenate([c_prompt, c_sample], axis=0), w_ada, b_ada)
    mod = mod.reshape(depth, bp + bs, 3, 1, D_MODEL)
    tabs_p = _rope_tables(jnp.arange(t_p, dtype=jnp.int32))
    tabs_s = _rope_tables(jnp.tile(past + jnp.arange(t_s, dtype=jnp.int32), bs))
    bias_p = _bias_tiles(rel_bias, TK, TK)
    bias_s = _bias_tiles(rel_bias, TK, t_s).reshape(DSA_KV_HEADS, DSA_GROUP, 2, TK, t_s)
    bias_s = bias_s.transpose(0, 2, 3, 1, 4).reshape(DSA_KV_HEADS, 2, TK, DSA_GROUP * t_s)
    row = lambda v: v.reshape(1, -1)

    xp = x_prompt
    xs = x_sample.reshape(1, n_s, D_MODEL)
    caches_p = caches_s = None
    for l in range(depth):
        final = l == depth - 1
        gn, gq, gkv, gf = row(g_norm[l]), row(g_qnorm[l]), row(g_kvnorm[l]), row(g_final)

        shift, scale, gate = (mod[l, :bp, n] for n in range(3))
        *caches_p, q, qb, qi, wit, kcat, vat, kb16, vbt, ki16 = _proj(
            xp, shift, scale, gn, wa[l], gq, wuq[l], gkv, wkn[l], wv[l], *tabs_p, caches_p, layer=l, depth=depth,
            tm=TM, tk=TK, emit_kv=True)
        oa = _mla(q, kcat, vat, t=TK)
        ob = _dsa(qi, wit, ki16, qb, kb16, vbt, bias_p, t=TK, topk=topk_p)
        xp = _out(xp, shift, scale, gate, gn, oa, ob, wg[l], woa[l], wob[l], wout[l], gf, tm=TM, final=final)

        shift, scale, gate = (jnp.repeat(mod[l, bp:, n, 0], t_s, axis=0)[None] for n in range(3))
        *caches_s, q, qb, qi, wit = _proj(
            xs, shift, scale, gn, wa[l], gq, wuq[l], gkv, wkn[l], wv[l], *tabs_s, caches_s, layer=l, depth=depth,
            tm=n_s, tk=TK, emit_kv=False)
        per_seq = lambda a: a.reshape((bs, t_s) + a.shape[2:])
        ckv, kr, kb, vb, ki = (per_seq(a[l]) for a in caches_s)
        q, qb, qi = per_seq(q), per_seq(qb), per_seq(qi)
        new_tile = lambda a: jnp.pad(a, ((0, 0), (0, TK - t_s), (0, 0)))
        oa = _mla_cached(q, cache_ckv, cache_krope, new_tile(ckv), new_tile(kr), wuk[l], wuv[l], rope_sel, layer=l,
                         tk=TK, n_valid_last=t_s)
        oa = oa.reshape(bs, V_DIM, MLA_HEADS, t_s).transpose(0, 3, 2, 1).reshape(1, n_s, W_A)
        qi_l = qi.reshape(bs, t_s, IDX_HEADS, IDX_DIM).transpose(0, 2, 1, 3).reshape(bs, IDX_HEADS * t_s, IDX_DIM)
        w_l = wit.reshape(IDX_HEADS, bs, t_s).transpose(1, 0, 2).reshape(bs, 1, IDX_HEADS * t_s)
        qb_l = qb.reshape(bs, t_s, DSA_KV_HEADS, DSA_GROUP, DSA_HEAD_DIM).transpose(0, 2, 3, 1, 4)
        qb_l = qb_l.reshape(bs, DSA_KV_HEADS, DSA_GROUP * t_s, DSA_HEAD_DIM)
        qb_l = jnp.stack([jnp.pad(qb_l[:, g], ((0, 0), (0, 0), (g * DSA_HEAD_DIM, KV_DIM - (g + 1) * DSA_HEAD_DIM)))
                          for g in range(DSA_KV_HEADS)], axis=1)
        ob = _dsa_cached(qi_l, w_l, cache_kidx, new_tile(ki), qb_l, cache_kb2, new_tile(kb), cache_vb2, new_tile(vb),
                         bias_s, layer=l, ts=t_s, tk=TK, n_valid_last=t_s, topk=topk_s)
        ob = ob.reshape(bs, DSA_KV_HEADS, DSA_HEAD_DIM, DSA_GROUP, t_s).transpose(0, 4, 1, 3, 2).reshape(1, n_s, W_B)
        xs = _out(xs, shift, scale, gate, gn, oa, ob, wg[l], woa[l], wob[l], wout[l], gf, tm=n_s, final=final)

    def cache_outputs(stacks, nb, t):
        ckv, kr, kb, vb, ki = (a.reshape((depth, nb, t, a.shape[-1])) for a in stacks)
        kvh = (depth, nb, t, DSA_KV_HEADS, DSA_HEAD_DIM)
        return ckv, kr, kb.reshape(kvh), vb.reshape(kvh), ki

    return (xp, xs.reshape(bs, t_s, D_MODEL)) + cache_outputs(caches_p, bp, t_p) + cache_outputs(caches_s, bs, t_s)
```

```python
import functools

import numpy as np
import jax
import jax.numpy as jnp
from jax import lax
from jax.experimental import pallas as pl
from jax.experimental.pallas import tpu as pltpu

F32 = jnp.float32
BF16 = jnp.bfloat16

D_MODEL = 1024
CHUNK = 64
EPS = 1e-6
MLA_HEADS = 8
NOPE_DIM = 64
ROPE_DIM = 32
V_DIM = 64
Q_LORA = 384
KV_LORA = 256
ROPE_THETA = 10000.0
MLA_SCALE = (NOPE_DIM + ROPE_DIM) ** -0.5
DSA_HEADS = 8
DSA_KV_HEADS = 2
DSA_GROUP = DSA_HEADS // DSA_KV_HEADS
DSA_HEAD_DIM = 64
DSA_SCALE = DSA_HEAD_DIM ** -0.5
IDX_HEADS = 8
IDX_DIM = 64
IDX_W_SCALE = (IDX_HEADS ** -0.5) * (IDX_DIM ** -0.5)
TOPK_MAX = 256
NUM_BUCKETS = 32
MAX_DISTANCE = 128
W_A = MLA_HEADS * V_DIM
W_B = DSA_HEADS * DSA_HEAD_DIM
SPLIT_SIZES = (Q_LORA, KV_LORA, ROPE_DIM, W_A, W_B, DSA_KV_HEADS * DSA_HEAD_DIM, DSA_KV_HEADS * DSA_HEAD_DIM,
               IDX_HEADS * IDX_DIM, IDX_DIM, IDX_HEADS, W_B, D_MODEL, D_MODEL)

LANES = 128
SUBLANES = 8
VMEM_LIMIT_BYTES = 56 * 1024 * 1024

TK = 256
TM = 512
NEG = -1e30
LOG2E = 1.4426950408889634
MAX_BISECT = 200
BISECT_PER_CHECK = 4

QK_PAD = MLA_HEADS * LANES
KV_DIM = DSA_KV_HEADS * DSA_HEAD_DIM

A_CQ = (0, Q_LORA)
A_CKV = (A_CQ[1], A_CQ[1] + KV_LORA)
A_KR = (A_CKV[1], A_CKV[1] + LANES)
A_SM = (A_KR[1], A_KR[1] + LANES)
A_QB = (A_SM[1], A_SM[1] + W_B)
A_KB = (A_QB[1], A_QB[1] + KV_DIM)
A_VB = (A_KB[1], A_KB[1] + KV_DIM)
A_QI = (A_VB[1], A_VB[1] + IDX_HEADS * IDX_DIM)
A_WIDTH = A_QI[1]
G_GA = (0, W_A)
G_GB = (G_GA[1], G_GA[1] + W_B)
G_MA = (G_GB[1], G_GB[1] + D_MODEL)
G_MB = (G_MA[1], G_MA[1] + D_MODEL)
G_WIDTH = G_MB[1]


def _params(n_grid):
    return pltpu.CompilerParams(dimension_semantics=("arbitrary",) * n_grid, vmem_limit_bytes=VMEM_LIMIT_BYTES)


def _sigmoid(x):
    return 1.0 / (1.0 + jnp.exp(-x))


def _rms(x, g):
    return x * lax.rsqrt(jnp.mean(x * x, axis=-1, keepdims=True) + EPS) * g


def _dot(a, b):
    return jnp.dot(a, b, preferred_element_type=F32)


def _dot_nt(a, b):
    return lax.dot_general(a, b, (((1,), (1,)), ((), ())), preferred_element_type=F32)


def _key_rows(j, tk):
    start = j * tk if isinstance(j, int) else pl.multiple_of(j * tk, tk)
    return pl.ds(start, tk)


def _rows8(x, op, ways=None):
    parts = [x[r:r + SUBLANES] for r in range(0, x.shape[0], SUBLANES)]
    if ways is not None:
        chains = parts[:ways]
        for n, part in enumerate(parts[ways:]):
            chains[n % ways] = op(chains[n % ways], part)
        parts = chains
    while len(parts) > 1:
        parts = [op(parts[n], parts[n + 1]) for n in range(0, len(parts) - 1, 2)] + parts[len(parts) & ~1:]
    return parts[0]


def _mod_kernel(c_ref, w_ref, b_ref, o_ref):
    c = c_ref[...]
    a = c * _sigmoid(c)
    w = w_ref[0]
    a_hi = a.astype(BF16)
    a_lo = (a - a_hi.astype(F32)).astype(BF16)
    w_hi = w.astype(BF16)
    w_lo = (w - w_hi.astype(F32)).astype(BF16)
    o_ref[0] = _dot(a_hi, w_hi) + (_dot(a_hi, w_lo) + _dot(a_lo, w_hi)) + b_ref[0]


def _modulation(c_all, w_ada, b_ada):
    depth = w_ada.shape[0]
    n = c_all.shape[0]
    tn = D_MODEL
    return pl.pallas_call(
        _mod_kernel,
        out_shape=jax.ShapeDtypeStruct((depth, n, 3 * D_MODEL), F32),
        grid=(depth, 3 * D_MODEL // tn),
        in_specs=[
            pl.BlockSpec((n, D_MODEL), lambda l, j: (0, 0)),
            pl.BlockSpec((1, D_MODEL, tn), lambda l, j: (l, 0, j)),
            pl.BlockSpec((1, 1, tn), lambda l, j: (l, 0, j)),
        ],
        out_specs=pl.BlockSpec((1, n, tn), lambda l, j: (l, 0, j)),
        compiler_params=_params(2),
        name="adaln_mod",
    )(c_all, w_ada, b_ada.reshape(depth, 1, 3 * D_MODEL))


def _rel_bucket(rel):
    half = NUM_BUCKETS // 2
    max_exact = half // 2
    n = np.abs(rel)
    large = max_exact + (np.log(np.maximum(n, 1).astype(np.float32) / max_exact)
                         / np.float32(np.log(MAX_DISTANCE / max_exact)) * (half - max_exact)).astype(np.int32)
    large = np.minimum(large, half - 1)
    return np.where(n < max_exact, n, large) + np.where(rel > 0, half, 0)


def _bias_kernel(bucket_ref, rel_ref, o_ref, *, far_bucket):
    h = pl.program_id(0)
    for t in range(2):
        b = bucket_ref[t]
        acc = jnp.zeros(b.shape, F32)
        for i in range(NUM_BUCKETS):
            acc = jnp.where(b == i, rel_ref[i, h], acc)
        o_ref[0, t] = (acc - rel_ref[far_bucket, h]) * LOG2E


def _bias_tiles(rel_bias, tk, tq):
    r = np.arange(tk)[:, None]
    c = np.arange(tq)[None, :]
    far_bucket = int(_rel_bucket(np.array(-tk)))
    assert np.all(_rel_bucket(-tk - np.arange(1, tk + tq)) == far_bucket)
    buckets = jnp.asarray(np.stack([_rel_bucket(r - tk - c), _rel_bucket(r - c)]).astype(np.int32))
    return pl.pallas_call(
        functools.partial(_bias_kernel, far_bucket=far_bucket),
        out_shape=jax.ShapeDtypeStruct((DSA_HEADS, 2, tk, tq), F32),
        grid=(DSA_HEADS,),
        in_specs=[
            pl.BlockSpec((2, tk, tq), lambda h: (0, 0, 0)),
            pl.BlockSpec(memory_space=pltpu.SMEM),
        ],
        out_specs=pl.BlockSpec((1, 2, tk, tq), lambda h: (h, 0, 0, 0)),
        compiler_params=_params(1),
        name="t5_bias_tiles",
    )(buckets, rel_bias)


def _rope_group(g, c_ref, s1_ref, s2_ref):
    return (g * c_ref[...] + pltpu.roll(g, LANES - ROPE_DIM // 2, 1) * s1_ref[...]
            + pltpu.roll(g, ROPE_DIM // 2, 1) * s2_ref[...])


def _proj_kernel(x_ref, shift_ref, scale_ref, gn_ref, wa_ref, gq_ref, wuq_ref, gkv_ref, wkn_ref, wv_ref,
                 qc_ref, qs1_ref, qs2_ref, kc_ref, ks1_ref, ks2_ref, *rest, emit_kv, tk, n_aliased):
    ckv_o, kr_o, kb_o, vb_o, ki_o, q_o, qb_o, qi_o, wit_o, *kv_outs = rest[n_aliased:]

    def put(cache_o, value):
        for n in range(cache_o.shape[0]):
            cache_o[n, 0] = value

    x = x_ref[0]
    h = _rms(x, gn_ref[...]) * (1.0 + scale_ref[0]) + shift_ref[0]
    h16 = h.astype(BF16)

    def zcols(a):
        return _dot(h16, wa_ref[:, a[0]:a[1]])

    cqn = _rms(zcols(A_CQ), gq_ref[...]).astype(BF16)
    q = _dot(cqn, wuq_ref[...])
    for hd in range(MLA_HEADS):
        sl = slice(hd * LANES, (hd + 1) * LANES)
        q_hd = _rope_group(q[:, sl], qc_ref, qs1_ref, qs2_ref)
        if emit_kv:
            q_o[0, sl, :] = q_hd.T.astype(BF16)
        else:
            q_o[0, :, sl] = q_hd.astype(BF16)

    ckvn = _rms(zcols(A_CKV), gkv_ref[...])
    put(ckv_o, ckvn)
    krg = _rope_group(zcols(A_KR), kc_ref, ks1_ref, ks2_ref)
    put(kr_o, krg[:, 0:ROPE_DIM])

    sm = zcols(A_SM)
    ki = sm[:, 0:IDX_DIM]
    put(ki_o, ki)
    wit_o[0] = sm.T[IDX_DIM:IDX_DIM + IDX_HEADS, :] * IDX_W_SCALE
    qb = zcols(A_QB) * (DSA_SCALE * LOG2E)
    qi = zcols(A_QI)
    qb_o[0] = (qb.T if emit_kv else qb).astype(BF16)
    qi_o[0] = (qi.T if emit_kv else qi).astype(BF16)
    kb = zcols(A_KB)
    vb = zcols(A_VB)
    put(kb_o, kb)
    put(vb_o, vb)

    if emit_kv:
        kcat_o, vat_o, kb16_o, vbt_o, ki16_o = kv_outs
        ckv16 = ckvn.astype(BF16)
        kn = _dot(ckv16, wkn_ref[...])
        lane = lax.broadcasted_iota(jnp.int32, krg.shape, 1)
        k_add = jnp.where(lane >= NOPE_DIM, krg, 0.0)
        for hd in range(MLA_HEADS):
            sl = slice(hd * LANES, (hd + 1) * LANES)
            kcat_o[0, :, sl] = (kn[:, sl] + k_add).astype(BF16)
        va = _dot(ckv16, wv_ref[...])
        for g in range(DSA_KV_HEADS):
            kb16_o[0, g] = kb[:, g * DSA_HEAD_DIM:(g + 1) * DSA_HEAD_DIM].astype(BF16)
        ki16_o[0] = ki.astype(BF16)
        for c in range(x.shape[0] // tk):
            rows = slice(c * tk, (c + 1) * tk)
            vat_o[0, c] = va[rows, :].T.astype(BF16)
            vbt_o[0, c] = vb[rows, :].T.astype(BF16)


def _proj(x, shift, scale, gn, wa, gq, wuq, gkv, wkn, wv, q_tabs, k_tabs, caches, *, layer, depth, tm, tk, emit_kv):
    nb, t, _ = x.shape
    r = shift.shape[1]
    rb = 1 if r == 1 else tm
    grid = (nb, t // tm)
    tok = lambda w: pl.BlockSpec((1, tm, w), lambda b, i: (b, i, 0))
    if caches is None:
        stacked = lambda w: pl.BlockSpec((depth, 1, tm, w), lambda b, i: (0, b, i, 0))
    else:
        stacked = lambda w: pl.BlockSpec((1, 1, tm, w), lambda b, i: (layer, b, i, 0))
    mod = pl.BlockSpec((1, rb, D_MODEL), (lambda b, i: (b, 0, 0)) if r == 1 else (lambda b, i: (b, i, 0)))
    const = lambda a: pl.BlockSpec(a.shape, lambda b, i: (0,) * a.ndim)
    tab = pl.BlockSpec((tm, LANES), lambda b, i: (i, 0))
    in_specs = [tok(D_MODEL), mod, mod, const(gn), const(wa), const(gq), const(wuq), const(gkv), const(wkn),
                const(wv)] + [tab] * 6
    operands = [x, shift, scale, gn, wa, gq, wuq, gkv, wkn, wv, *q_tabs, *k_tabs]
    cache_widths = (KV_LORA, ROPE_DIM, KV_DIM, KV_DIM, IDX_DIM)
    aliases = {}
    if caches is not None:
        aliases = {len(operands) + n: n for n in range(len(cache_widths))}
        in_specs += [pl.BlockSpec(memory_space=pl.ANY)] * len(cache_widths)
        operands += list(caches)
    tok_t = lambda w: pl.BlockSpec((1, w, tm), lambda b, i: (b, 0, i))
    q_shape = (lambda w: (nb, w, t)) if emit_kv else (lambda w: (nb, t, w))
    q_spec = tok_t if emit_kv else tok
    q_widths = (QK_PAD, W_B, IDX_HEADS * IDX_DIM)
    out_shape = ([jax.ShapeDtypeStruct((depth, nb, t, w), F32) for w in cache_widths]
                 + [jax.ShapeDtypeStruct(q_shape(w), BF16) for w in q_widths]
                 + [jax.ShapeDtypeStruct((nb, IDX_HEADS, t), F32)])
    out_specs = [stacked(w) for w in cache_widths] + [q_spec(w) for w in q_widths] + [tok_t(IDX_HEADS)]
    if emit_kv:
        nkt = tm // tk
        out_shape += [
            jax.ShapeDtypeStruct((nb, t, QK_PAD), BF16),
            jax.ShapeDtypeStruct((nb, t // tk, W_A, tk), BF16),
            jax.ShapeDtypeStruct((nb, DSA_KV_HEADS, t, DSA_HEAD_DIM), BF16),
            jax.ShapeDtypeStruct((nb, t // tk, KV_DIM, tk), BF16),
            jax.ShapeDtypeStruct((nb, t, IDX_DIM), BF16),
        ]
        out_specs += [tok(QK_PAD), pl.BlockSpec((1, nkt, W_A, tk), lambda b, i: (b, i, 0, 0)),
                      pl.BlockSpec((1, DSA_KV_HEADS, tm, DSA_HEAD_DIM), lambda b, i: (b, 0, i, 0)),
                      pl.BlockSpec((1, nkt, KV_DIM, tk), lambda b, i: (b, i, 0, 0)), tok(IDX_DIM)]
    return pl.pallas_call(
        functools.partial(_proj_kernel, emit_kv=emit_kv, tk=tk, n_aliased=len(aliases)),
        out_shape=out_shape, grid=grid, in_specs=in_specs, out_specs=out_specs, input_output_aliases=aliases,
        compiler_params=_params(2), name="proj_kv" if emit_kv else "proj_q",
    )(*operands)


class _Softmax:
    ONES_ROWS = 16

    def __init__(self, s_scr, mx_scr, m_scr, acc_scr):
        self.s, self.mx, self.m, self.acc = s_scr, mx_scr, m_scr, acc_scr
        self.n_heads = acc_scr.shape[0]
        self.dv = acc_scr.shape[1] - self.ONES_ROWS

    @classmethod
    def scratch(cls, n_heads, dv, tk, tq):
        stat = pltpu.VMEM((n_heads, SUBLANES, tq), F32)
        return [pltpu.VMEM((n_heads, 2, tk, tq), F32), stat, stat, pltpu.VMEM((n_heads, dv + cls.ONES_ROWS, tq), F32)]

    def reset(self):
        self.mx[...] = jnp.full(self.mx.shape, NEG, F32)
        self.m[...] = jnp.full(self.m.shape, NEG, F32)
        self.acc[...] = jnp.zeros(self.acc.shape, F32)

    def produce(self, hd, slot, s):
        self.s[hd, slot] = s
        self.mx[hd] = jnp.maximum(self.mx[hd], _rows8(s, jnp.maximum, ways=4))

    def consume(self, hd, slot, vt):
        m_old = self.m[hd, 0:1, :]
        m_new = jnp.max(self.mx[hd], axis=0, keepdims=True)
        alpha = jnp.exp2(m_old - m_new)
        p = jnp.exp2(self.s[hd, slot] - m_new)
        self.m[hd] = jnp.broadcast_to(m_new, self.m.shape[1:])
        vt1 = jnp.concatenate([vt, jnp.ones((self.ONES_ROWS, vt.shape[1]), vt.dtype)], axis=0)
        self.acc[hd] = alpha * self.acc[hd] + _dot(vt1, p.astype(BF16))

    def run(self, last, logits, values):
        def step(t_consume, t_produce, kind, slot_produce):
            for hd in range(self.n_heads):
                if t_consume is not None:
                    self.consume(hd, 1 - slot_produce, values(hd, t_consume))
                if t_produce is not None:
                    self.produce(hd, slot_produce, logits(hd, t_produce, kind))

        when = lambda cond, fn: pl.when(cond)(fn)
        self.reset()
        step(None, last, 2, 0)
        when(last >= 1, lambda: step(last, last - 1, 1, 1))
        n_far = last - 1

        def far_pair(n, _):
            t = last - 2 - 2 * n
            step(t + 1, t, 0, 0)
            step(t, t - 1, 0, 1)
            return 0

        lax.fori_loop(0, n_far // 2, far_pair, 0)
        odd = (n_far >= 1) & (n_far % 2 == 1)
        when(odd, lambda: step(1, 0, 0, 0))
        in_slot0 = (last == 0) | odd
        when(in_slot0, lambda: step(0, None, None, 1))
        when(jnp.logical_not(in_slot0), lambda: step(0, None, None, 0))

    def finish(self, *o_refs):
        per_out = self.n_heads // len(o_refs)
        for n, o_ref in enumerate(o_refs):
            o_t = jnp.concatenate([self.acc[hd, 0:self.dv, :] / self.acc[hd, self.dv:self.dv + 1, :]
                                   for hd in range(n * per_out, (n + 1) * per_out)], axis=0)
            o_ref[0] = o_t.T


def _diagonal_tile_visible(t):
    r = lax.broadcasted_iota(jnp.int32, (t, t), 0)
    c = lax.broadcasted_iota(jnp.int32, (t, t), 1)
    return (r // CHUNK) <= (c // CHUNK)


def _score_stats_init(tq):
    return (jnp.full((1, tq), jnp.inf, F32), jnp.full((1, tq), -jnp.inf, F32),
            jnp.zeros((SUBLANES, tq), F32), jnp.zeros((SUBLANES, tq), F32))


def _score_stats(stats, t, visible=None):
    lo, hi, c_z, c_zp = stats
    t_min = t if visible is None else jnp.where(visible, t, jnp.inf)
    return (jnp.minimum(lo, jnp.min(t_min, axis=0, keepdims=True)), jnp.maximum(hi, jnp.max(t, axis=0, keepdims=True)),
            c_z + _rows8(jnp.where(t >= 0.0, 1.0, 0.0), jnp.add), c_zp + _rows8(jnp.where(t > 0.0, 1.0, 0.0), jnp.add))


def _select_topk(idx_scr, n_used, stats, n_vis, kf):
    _, tk, tq = idx_scr.shape

    def count(pred):
        def body(j, c):
            return c + _rows8(jnp.where(pred(idx_scr[j]), 1.0, 0.0), jnp.add)
        return jnp.sum(lax.fori_loop(0, n_used, body, jnp.zeros((SUBLANES, tq), F32)), axis=0, keepdims=True)

    lo, hi, c_z, c_zp = stats
    c_z = jnp.sum(c_z, axis=0, keepdims=True)
    c_zp = jnp.sum(c_zp, axis=0, keepdims=True)
    hi = hi + (jnp.abs(hi) * 2.0 ** -20 + 1e-30)
    zero_up = (c_z >= kf) & (lo < 0.0)
    zero_tie = jnp.where((c_z >= kf) & (c_zp < kf), 1.0, 0.0)
    c_lo = jnp.where(zero_up, c_z, n_vis)
    lo = jnp.where(zero_up, 0.0, lo)
    hi = jnp.where((c_z < kf) & (hi > 0.0), 0.0, hi)

    def settled(lo, hi, c_lo):
        mid = 0.5 * lo + 0.5 * hi
        return jnp.where((c_lo <= kf) | (mid <= lo) | (mid >= hi), 1.0, 0.0)

    def bis_cond(st):
        _, _, _, done, it = st
        return jnp.logical_and(it < MAX_BISECT, jnp.min(done) < 0.5)

    def bis_body(st):
        lo, hi, c_lo, done, it = st
        for _ in range(BISECT_PER_CHECK):
            mid = 0.5 * lo + 0.5 * hi
            c = count(lambda t: t >= mid)
            live = done < 0.5
            up = live & (c >= kf)
            dn = live & (c < kf)
            lo = jnp.where(up, mid, lo)
            c_lo = jnp.where(up, c, c_lo)
            hi = jnp.where(dn, mid, hi)
            done = jnp.maximum(done, settled(lo, hi, c_lo))
        return lo, hi, c_lo, done, it + BISECT_PER_CHECK

    done = jnp.maximum(settled(lo, hi, c_lo), zero_tie)
    lo, hi, c_lo, _, _ = lax.while_loop(bis_cond, bis_body, (lo, hi, c_lo, done, jnp.int32(0)))
    any_tie = jnp.max(jnp.where(c_lo > kf, 1.0, 0.0)) > 0.5

    @pl.when(jnp.logical_not(any_tie))
    def _():
        def body(j, _):
            idx_scr[j] = jnp.where(idx_scr[j] >= lo, 0.0, NEG)
            return 0
        lax.fori_loop(0, n_used, body, 0)

    @pl.when(any_tie)
    def _():
        need = kf - count(lambda t: t > lo)
        r = lax.broadcasted_iota(jnp.int32, (tk, tk), 0)
        c = lax.broadcasted_iota(jnp.int32, (tk, tk), 1)
        before = jnp.where(c < r, 1.0, 0.0).astype(BF16)

        def body(j, seen):
            t = idx_scr[j]
            eq = jnp.where(t == lo, 1.0, 0.0)
            rank = _dot(before, eq.astype(BF16)) + seen
            take = (t > lo) | ((t == lo) & (rank < need))
            idx_scr[j] = jnp.where(take, 0.0, NEG)
            return seen + jnp.sum(_rows8(eq, jnp.add), axis=0, keepdims=True)
        lax.fori_loop(0, n_used, body, jnp.zeros((1, tq), F32))


def _attend_kernel(qi_ref, wit_ref, ki_ref, qb_ref, kb_ref, vbt_ref, bias_ref, q_ref, k_ref, vt_ref, oa_ref, ob_ref,
                   idx_scr, *scratch, t, topk):
    tq = tk = t
    last = pl.program_id(1)
    n_used = last + 1
    visible = _diagonal_tile_visible(t)

    wit = wit_ref[0]

    def idx_tile(j):
        k_j = ki_ref[0, _key_rows(j, tk), :]
        acc = jnp.zeros((tk, tq), F32)
        for hd in range(IDX_HEADS):
            s = _dot(k_j, qi_ref[0, hd * IDX_DIM:(hd + 1) * IDX_DIM, :])
            acc = acc + jnp.maximum(s, 0.0) * wit[hd:hd + 1, :]
        return acc

    def idx_body(j, stats):
        t = idx_tile(j)
        idx_scr[j] = t
        return _score_stats(stats, t)

    def idx_pair(n, stats):
        return idx_body(2 * n + 1, idx_body(2 * n, stats))

    stats = lax.fori_loop(0, last // 2, idx_pair, _score_stats_init(tq))
    stats = lax.cond(last % 2 == 1, lambda st: idx_body(last - 1, st), lambda st: st, stats)
    t_last = jnp.where(visible, idx_tile(last), -jnp.inf)
    idx_scr[last] = t_last
    stats = _score_stats(stats, t_last, visible)
    lane = lax.broadcasted_iota(jnp.int32, (1, tq), 1)
    n_vis = last * tk + (lane // CHUNK + 1) * CHUNK
    _select_topk(idx_scr, n_used, stats, n_vis.astype(F32), float(topk))

    softmax = _Softmax(*scratch)
    kv_lanes = lambda hd: slice((hd // DSA_GROUP) * DSA_HEAD_DIM, (hd // DSA_GROUP + 1) * DSA_HEAD_DIM)

    def logits(hd, j, kind):
        if hd < MLA_HEADS:
            sl = slice(hd * LANES, (hd + 1) * LANES)
            s = _dot(k_ref[0, _key_rows(j, tk), sl], q_ref[0, sl, :])
            return s + jnp.where(visible, 0.0, NEG) if kind == 2 else s
        hd -= MLA_HEADS
        s = _dot(kb_ref[0, hd // DSA_GROUP, _key_rows(j, tk), :], qb_ref[0, hd * DSA_HEAD_DIM:(hd + 1) * DSA_HEAD_DIM, :])
        return s + idx_scr[j] if kind == 0 else s + (idx_scr[j] + bias_ref[hd, kind - 1])

    def values(hd, j):
        if hd < MLA_HEADS:
            return vt_ref[0, j, hd * V_DIM:(hd + 1) * V_DIM, :]
        return vbt_ref[0, j, kv_lanes(hd - MLA_HEADS), :]

    softmax.run(last, logits, values)
    softmax.finish(oa_ref, ob_ref)


def _attend(qi, wit, ki, qb, kb, vbt, bias, q, k, vt, *, t, topk):
    assert V_DIM == DSA_HEAD_DIM and MLA_HEADS == DSA_HEADS
    nb, _, seq = qb.shape
    n_tiles = seq // t
    per_q = lambda w: pl.BlockSpec((1, w, t), lambda b, i: (b, 0, i))
    per_b = lambda a: pl.BlockSpec((1,) + a.shape[1:], lambda b, i: (b,) + (0,) * (a.ndim - 1))
    out = lambda w: pl.BlockSpec((1, t, w), lambda b, i: (b, i, 0))
    return pl.pallas_call(
        functools.partial(_attend_kernel, t=t, topk=topk),
        out_shape=[jax.ShapeDtypeStruct((nb, seq, W_A), F32), jax.ShapeDtypeStruct((nb, seq, W_B), F32)],
        grid=(nb, n_tiles),
        in_specs=[per_q(IDX_HEADS * IDX_DIM), per_q(IDX_HEADS), per_b(ki), per_q(W_B), per_b(kb), per_b(vbt),
                  pl.BlockSpec(bias.shape, lambda b, i: (0, 0, 0, 0)), per_q(QK_PAD), per_b(k), per_b(vt)],
        out_specs=[out(W_A), out(W_B)],
        scratch_shapes=([pltpu.VMEM((n_tiles, t, t), F32)]
                        + _Softmax.scratch(MLA_HEADS + DSA_HEADS, DSA_HEAD_DIM, t, t)),
        compiler_params=_params(2), name="attend_causal",
    )(qi, wit, ki, qb, kb, vbt, bias, q, k, vt)


def _two_pass_softmax(n_tiles, logits, values, s_scr, dv):
    mx = None
    for t in range(n_tiles):
        s = logits(t)
        s_scr[t] = s
        part = _rows8(s, jnp.maximum, ways=4)
        mx = part if mx is None else jnp.maximum(mx, part)
    m = jnp.max(mx, axis=0, keepdims=True)
    acc = None
    for t in range(n_tiles):
        p = jnp.exp2(s_scr[t] - m).astype(BF16)
        vt = values(t)
        part = _dot(jnp.concatenate([vt, jnp.ones((_Softmax.ONES_ROWS, vt.shape[1]), BF16)], axis=0), p)
        acc = part if acc is None else acc + part
    return acc[0:dv] / acc[dv:dv + 1]


def _mla_cached_kernel(q_ref, ckv_c_ref, kr_c_ref, ckv_n_ref, kr_n_ref, wuk_ref, wuv_ref, sel_ref, o_ref, s_scr, *,
                       ts, tk, n_cache, n_valid_last):
    lanes = MLA_HEADS * ts
    q_lat, q_rope = [], []
    for hd in range(MLA_HEADS):
        qg = q_ref[0, :, hd * LANES:(hd + 1) * LANES]
        q_lat.append(_dot(qg, wuk_ref[hd]).astype(BF16))
        q_rope.append(_dot(qg, sel_ref[...]).astype(BF16))
    q_lat = jnp.concatenate(q_lat, axis=0)
    q_rope = jnp.concatenate(q_rope, axis=0)

    def latent(t):
        return ckv_c_ref[0, t * tk:(t + 1) * tk, :] if t < n_cache else ckv_n_ref[0]

    def logits(t):
        kr = kr_c_ref[0, t * tk:(t + 1) * tk, :] if t < n_cache else kr_n_ref[0]
        s = _dot_nt(latent(t).astype(BF16), q_lat) + _dot_nt(kr.astype(BF16), q_rope)
        if t == n_cache:
            s = s + jnp.where(lax.broadcasted_iota(jnp.int32, s.shape, 0) < n_valid_last, 0.0, NEG)
        return s

    o_lat = _two_pass_softmax(n_cache + 1, logits, lambda t: latent(t).T.astype(BF16), s_scr, KV_LORA)
    o_lat = o_lat.astype(BF16)
    head_of_lane = lax.broadcasted_iota(jnp.int32, (V_DIM, lanes), 1) // ts
    out = jnp.zeros((V_DIM, lanes), F32)
    for hd in range(MLA_HEADS):
        out = jnp.where(head_of_lane == hd, _dot(wuv_ref[hd], o_lat), out)
    o_ref[0] = out


def _mla_cached(q, ckv_cache, kr_cache, ckv_new, kr_new, wuk, wuv, sel, *, layer, tk, n_valid_last):
    nb, ts, _ = q.shape
    past = ckv_cache.shape[2]
    lanes = MLA_HEADS * ts
    per_b = lambda a: pl.BlockSpec((1,) + a.shape[1:], lambda b: (b,) + (0,) * (a.ndim - 1))
    const = lambda a: pl.BlockSpec(a.shape, lambda b: (0,) * a.ndim)
    return pl.pallas_call(
        functools.partial(_mla_cached_kernel, ts=ts, tk=tk, n_cache=past // tk, n_valid_last=n_valid_last),
        out_shape=jax.ShapeDtypeStruct((nb, V_DIM, lanes), F32), grid=(nb,),
        in_specs=[per_b(q), _layer_rows(ckv_cache, layer), _layer_rows(kr_cache, layer), per_b(ckv_new),
                  per_b(kr_new), const(wuk), const(wuv), const(sel)],
        out_specs=pl.BlockSpec((1, V_DIM, lanes), lambda b: (b, 0, 0)),
        scratch_shapes=[pltpu.VMEM((past // tk + 1, tk, lanes), F32)],
        compiler_params=_params(1), name="mla_cached",
    )(q, ckv_cache, kr_cache, ckv_new, kr_new, wuk, wuv, sel)


def _dsa_cached_kernel(qi_ref, w_ref, ki_c_ref, ki_n_ref, qb_ref, kb_c_ref, kb_n_ref, vb_c_ref, vb_n_ref, bias_ref,
                       o_ref, idx_scr, s_scr, *, ts, tk, n_cache, n_valid_last, topk):
    n_tiles = n_cache + 1
    lanes = DSA_GROUP * ts
    tile = lambda c_ref, n_ref, t: c_ref[0, t * tk:(t + 1) * tk, :] if t < n_cache else n_ref[0]
    visible = lax.broadcasted_iota(jnp.int32, (tk, lanes), 0) < n_valid_last

    stats = _score_stats_init(lanes)
    for t in range(n_tiles):
        s = _dot_nt(tile(ki_c_ref, ki_n_ref, t).astype(BF16), qi_ref[0])
        r = jnp.maximum(s, 0.0) * w_ref[0]
        x = r[:, 0:lanes]
        for blk in range(1, IDX_HEADS * ts // lanes):
            x = x + r[:, blk * lanes:(blk + 1) * lanes]
        shift = lanes // 2
        while shift >= ts:
            x = x + pltpu.roll(x, shift, 1)
            shift //= 2
        if t == n_cache:
            x = jnp.where(visible, x, -jnp.inf)
        idx_scr[t] = x
        stats = _score_stats(stats, x, visible if t == n_cache else None)
    n_vis = jnp.full((1, lanes), float(n_cache * tk + n_valid_last), F32)
    _select_topk(idx_scr, n_tiles, stats, n_vis, float(topk))

    for g in range(DSA_KV_HEADS):
        def logits(t, g=g):
            s = _dot_nt(tile(kb_c_ref, kb_n_ref, t).astype(BF16), qb_ref[0, g]) + idx_scr[t]
            return s + bias_ref[g, t - (n_tiles - 2)] if t >= n_tiles - 2 else s

        def values(t, g=g):
            return tile(vb_c_ref, vb_n_ref, t).T[g * DSA_HEAD_DIM:(g + 1) * DSA_HEAD_DIM, :].astype(BF16)

        o_ref[0, g] = _two_pass_softmax(n_tiles, logits, values, s_scr, DSA_HEAD_DIM)


def _layer_rows(cache, layer):
    return pl.BlockSpec((None, 1) + cache.shape[2:], lambda b: (layer, b, 0, 0))


def _dsa_cached(qi, w, ki_cache, ki_new, qb, kb_cache, kb_new, vb_cache, vb_new, bias, *, layer, ts, tk,
                n_valid_last, topk):
    nb = qi.shape[0]
    past = ki_cache.shape[2]
    n_tiles = past // tk + 1
    lanes = DSA_GROUP * ts
    per_b = lambda a: pl.BlockSpec((1,) + a.shape[1:], lambda b: (b,) + (0,) * (a.ndim - 1))
    cached = lambda a: _layer_rows(a, layer)
    return pl.pallas_call(
        functools.partial(_dsa_cached_kernel, ts=ts, tk=tk, n_cache=past // tk, n_valid_last=n_valid_last, topk=topk),
        out_shape=jax.ShapeDtypeStruct((nb, DSA_KV_HEADS, DSA_HEAD_DIM, lanes), F32), grid=(nb,),
        in_specs=[per_b(qi), per_b(w), cached(ki_cache), per_b(ki_new), per_b(qb), cached(kb_cache), per_b(kb_new),
                  cached(vb_cache), per_b(vb_new), pl.BlockSpec(bias.shape, lambda b: (0,) * bias.ndim)],
        out_specs=pl.BlockSpec((1, DSA_KV_HEADS, DSA_HEAD_DIM, lanes), lambda b: (b, 0, 0, 0)),
        scratch_shapes=[pltpu.VMEM((n_tiles, tk, lanes), F32), pltpu.VMEM((n_tiles, tk, lanes), F32)],
        compiler_params=_params(1), name="dsa_cached",
    )(qi, w, ki_cache, ki_new, qb, kb_cache, kb_new, vb_cache, vb_new, bias)


def _out_kernel(x_ref, shift_ref, scale_ref, gate_ref, gn_ref, oa_ref, ob_ref, wg_ref, woa_ref, wob_ref, wout_ref,
                gf_ref, o_ref, *, final):
    x = x_ref[0]
    h16 = (_rms(x, gn_ref[...]) * (1.0 + scale_ref[0]) + shift_ref[0]).astype(BF16)

    def gcols(a):
        return _dot(h16, wg_ref[:, a[0]:a[1]])

    ga = gcols(G_GA)
    ya = _dot((oa_ref[0] * (ga * _sigmoid(ga))).astype(BF16), woa_ref[...])
    gb = gcols(G_GB)
    yb = _dot((ob_ref[0] * (gb * _sigmoid(gb))).astype(BF16), wob_ref[...])
    m = _sigmoid(gcols(G_MA)) * ya + _sigmoid(gcols(G_MB)) * yb
    y = x + gate_ref[0] * _dot(m.astype(BF16), wout_ref[...])
    o_ref[0] = _rms(y, gf_ref[...]) if final else y


def _out(x, shift, scale, gate, gn, oa, ob, wg, woa, wob, wout, gf, *, tm, final):
    nb, t, _ = x.shape
    r = shift.shape[1]
    rb = 1 if r == 1 else tm
    tok = lambda w: pl.BlockSpec((1, tm, w), lambda b, i: (b, i, 0))
    mod = pl.BlockSpec((1, rb, D_MODEL), (lambda b, i: (b, 0, 0)) if r == 1 else (lambda b, i: (b, i, 0)))
    const = lambda a: pl.BlockSpec(a.shape, lambda b, i: (0,) * a.ndim)
    return pl.pallas_call(
        functools.partial(_out_kernel, final=final),
        out_shape=jax.ShapeDtypeStruct(x.shape, F32), grid=(nb, t // tm),
        in_specs=[tok(D_MODEL), mod, mod, mod, const(gn), tok(W_A), tok(W_B), const(wg), const(woa), const(wob),
                  const(wout), const(gf)],
        out_specs=tok(D_MODEL), compiler_params=_params(2), name="out_final" if final else "out_mix",
    )(x, shift, scale, gate, gn, oa, ob, wg, woa, wob, wout, gf)


def _pack_weights(w_in, w_uq, w_ukv, w_oa, w_ob, w_out):
    depth = w_in.shape[0]
    offs = np.concatenate([[0], np.cumsum(SPLIT_SIZES)])
    cq, ckv, kr, ga, qb, kb, vb, qi, ki, wi, gb, ma, mb = [w_in[:, :, offs[n]:offs[n + 1]] for n in range(13)]
    z = lambda w: jnp.zeros((depth, D_MODEL, w), F32)
    krg = jnp.concatenate([kr, z(NOPE_DIM - ROPE_DIM), kr, z(LANES - NOPE_DIM - ROPE_DIM)], axis=-1)
    sm = jnp.concatenate([ki, wi, z(LANES - IDX_DIM - IDX_HEADS)], axis=-1)
    wa = jnp.concatenate([cq, ckv, krg, sm, qb, kb, vb, qi], axis=-1).astype(BF16)
    wg = jnp.concatenate([ga, gb, ma, mb], axis=-1).astype(BF16)
    qd = NOPE_DIM + ROPE_DIM
    wuq = jnp.pad(w_uq.reshape(depth, Q_LORA, MLA_HEADS, qd), ((0, 0), (0, 0), (0, 0), (0, LANES - qd)))
    wuq = wuq.reshape(depth, Q_LORA, QK_PAD).astype(BF16)
    ukv = w_ukv.reshape(depth, KV_LORA, MLA_HEADS, NOPE_DIM + V_DIM)
    wkn = jnp.pad(ukv[..., :NOPE_DIM], ((0, 0), (0, 0), (0, 0), (0, LANES - NOPE_DIM)))
    wkn = wkn.reshape(depth, KV_LORA, QK_PAD).astype(BF16)
    wv = ukv[..., NOPE_DIM:].reshape(depth, KV_LORA, W_A).astype(BF16)
    wuk = jnp.pad(ukv[..., :NOPE_DIM].transpose(0, 2, 3, 1), ((0, 0), (0, 0), (0, LANES - NOPE_DIM), (0, 0)))
    wuv = ukv[..., NOPE_DIM:].transpose(0, 2, 3, 1)
    return (wa, wg, wuq, wkn, wv, w_oa.astype(BF16), w_ob.astype(BF16), w_out.astype(BF16), wuk.astype(BF16),
            wuv.astype(BF16))


def _rope_tables(pos):
    half = ROPE_DIM // 2
    inv = ROPE_THETA ** (-jnp.arange(half, dtype=F32) / half)
    ang = pos.astype(F32)[:, None] * inv
    cos, sin = jnp.cos(ang), jnp.sin(ang)
    n = pos.shape[0]
    z = lambda w: jnp.zeros((n, w), F32)
    pad = LANES - NOPE_DIM - ROPE_DIM
    q_c = jnp.concatenate([jnp.ones((n, NOPE_DIM), F32), cos, cos, z(pad)], axis=-1) * (MLA_SCALE * LOG2E)
    q_s1 = jnp.concatenate([z(NOPE_DIM), -sin, z(half), z(pad)], axis=-1) * (MLA_SCALE * LOG2E)
    q_s2 = jnp.concatenate([z(NOPE_DIM), z(half), sin, z(pad)], axis=-1) * (MLA_SCALE * LOG2E)
    gap = NOPE_DIM - ROPE_DIM
    k_c = jnp.concatenate([cos, cos, z(gap), cos, cos, z(pad)], axis=-1)
    k_s1 = jnp.concatenate([-sin, z(half), z(gap), -sin, z(half), z(pad)], axis=-1)
    k_s2 = jnp.concatenate([z(half), sin, z(gap), z(half), sin, z(pad)], axis=-1)
    return (q_c, q_s1, q_s2), (k_c, k_s1, k_s2)


def _rope_selector():
    p = np.zeros((LANES, ROPE_DIM), np.float32)
    p[NOPE_DIM + np.arange(ROPE_DIM), np.arange(ROPE_DIM)] = 1.0
    return jnp.asarray(p, dtype=BF16)


def kernel(x_prompt, x_sample, c_prompt, c_sample, cache_ckv, cache_krope, cache_kb, cache_vb, cache_kidx, w_ada,
           b_ada, g_norm, w_in, g_qnorm, w_uq, g_kvnorm, w_ukv, w_oa, w_ob, w_out, rel_bias, g_final):
    depth = w_in.shape[0]
    bp, t_p, _ = x_prompt.shape
    bs, t_s, _ = x_sample.shape
    past = cache_ckv.shape[2]
    topk_p = min(TOPK_MAX, t_p // 4)
    topk_s = min(TOPK_MAX, (past + t_s) // 4)
    assert t_p % TM == 0 and TM % TK == 0 and past % TK == 0 and t_s <= TK and t_s % 8 == 0
    assert (DSA_GROUP * t_s) % LANES == 0 and (DSA_GROUP * t_s) & (DSA_GROUP * t_s - 1) == 0
    n_s = bs * t_s

    wa, wg, wuq, wkn, wv, woa, wob, wout, wuk, wuv = _pack_weights(w_in, w_uq, w_ukv, w_oa, w_ob, w_out)
    rope_sel = _rope_selector()
    cache_kb2 = cache_kb.reshape(depth, bs, past, KV_DIM)
    cache_vb2 = cache_vb.reshape(depth, bs, past, KV_DIM)
    mod = _modulation(jnp.concatenate([c_prompt, c_sample], axis=0), w_ada, b_ada)
    mod = mod.reshape(depth, bp + bs, 3, 1, D_MODEL)
    tabs_p = _rope_tables(jnp.arange(t_p, dtype=jnp.int32))
    tabs_s = _rope_tables(jnp.tile(past + jnp.arange(t_s, dtype=jnp.int32), bs))
    bias_p = _bias_tiles(rel_bias, TK, TK)
    bias_s = _bias_tiles(rel_bias, TK, t_s).reshape(DSA_KV_HEADS, DSA_GROUP, 2, TK, t_s)
    bias_s = bias_s.transpose(0, 2, 3, 1, 4).reshape(DSA_KV_HEADS, 2, TK, DSA_GROUP * t_s)
    row = lambda v: v.reshape(1, -1)

    xp = x_prompt
    xs = x_sample.reshape(1, n_s, D_MODEL)
    caches_p = caches_s = None
    for l in range(depth):
        final = l == depth - 1
        gn, gq, gkv, gf = row(g_norm[l]), row(g_qnorm[l]), row(g_kvnorm[l]), row(g_final)

        shift, scale, gate = (mod[l, :bp, n] for n in range(3))
        *caches_p, q, qb, qi, wit, kcat, vat, kb16, vbt, ki16 = _proj(
            xp, shift, scale, gn, wa[l], gq, wuq[l], gkv, wkn[l], wv[l], *tabs_p, caches_p, layer=l, depth=depth,
            tm=TM, tk=TK, emit_kv=True)
        oa, ob = _attend(qi, wit, ki16, qb, kb16, vbt, bias_p, q, kcat, vat, t=TK, topk=topk_p)
        xp = _out(xp, shift, scale, gate, gn, oa, ob, wg[l], woa[l], wob[l], wout[l], gf, tm=TM, final=final)

        shift, scale, gate = (jnp.repeat(mod[l, bp:, n, 0], t_s, axis=0)[None] for n in range(3))
        *caches_s, q, qb, qi, wit = _proj(
            xs, shift, scale, gn, wa[l], gq, wuq[l], gkv, wkn[l], wv[l], *tabs_s, caches_s, layer=l, depth=depth,
            tm=n_s, tk=TK, emit_kv=False)
        per_seq = lambda a: a.reshape((bs, t_s) + a.shape[2:])
        ckv, kr, kb, vb, ki = (per_seq(a[l]) for a in caches_s)
        q, qb, qi = per_seq(q), per_seq(qb), per_seq(qi)
        new_tile = lambda a: jnp.pad(a, ((0, 0), (0, TK - t_s), (0, 0)))
        oa = _mla_cached(q, cache_ckv, cache_krope, new_tile(ckv), new_tile(kr), wuk[l], wuv[l], rope_sel, layer=l,
                         tk=TK, n_valid_last=t_s)
        oa = oa.reshape(bs, V_DIM, MLA_HEADS, t_s).transpose(0, 3, 2, 1).reshape(1, n_s, W_A)
        qi_l = qi.reshape(bs, t_s, IDX_HEADS, IDX_DIM).transpose(0, 2, 1, 3).reshape(bs, IDX_HEADS * t_s, IDX_DIM)
        w_l = wit.reshape(IDX_HEADS, bs, t_s).transpose(1, 0, 2).reshape(bs, 1, IDX_HEADS * t_s)
        qb_l = qb.reshape(bs, t_s, DSA_KV_HEADS, DSA_GROUP, DSA_HEAD_DIM).transpose(0, 2, 3, 1, 4)
        qb_l = qb_l.reshape(bs, DSA_KV_HEADS, DSA_GROUP * t_s, DSA_HEAD_DIM)
        qb_l = jnp.stack([jnp.pad(qb_l[:, g], ((0, 0), (0, 0), (g * DSA_HEAD_DIM, KV_DIM - (g + 1) * DSA_HEAD_DIM)))
                          for g in range(DSA_KV_HEADS)], axis=1)
        ob = _dsa_cached(qi_l, w_l, cache_kidx, new_tile(ki), qb_l, cache_kb2, new_tile(kb), cache_vb2, new_tile(vb),
                         bias_s, layer=l, ts=t_s, tk=TK, n_valid_last=t_s, topk=topk_s)
        ob = ob.reshape(bs, DSA_KV_HEADS, DSA_HEAD_DIM, DSA_GROUP, t_s).transpose(0, 4, 1, 3, 2).reshape(1, n_s, W_B)
        xs = _out(xs, shift, scale, gate, gn, oa, ob, wg[l], woa[l], wob[l], wout[l], gf, tm=n_s, final=final)

    def cache_outputs(stacks, nb, t):
        ckv, kr, kb, vb, ki = (a.reshape((depth, nb, t, a.shape[-1])) for a in stacks)
        kvh = (depth, nb, t, DSA_KV_HEADS, DSA_HEAD_DIM)
        return ckv, kr, kb.reshape(kvh), vb.reshape(kvh), ki

    return (xp, xs.reshape(bs, t_s, D_MODEL)) + cache_outputs(caches_p, bp, t_p) + cache_outputs(caches_s, bs, t_s)
```

```python
import functools

import numpy as np
import jax
import jax.numpy as jnp
from jax import lax
from jax.experimental import pallas as pl
from jax.experimental.pallas import tpu as pltpu

F32 = jnp.float32
BF16 = jnp.bfloat16

D_MODEL = 1024
CHUNK = 64
EPS = 1e-6
MLA_HEADS = 8
NOPE_DIM = 64
ROPE_DIM = 32
V_DIM = 64
Q_LORA = 384
KV_LORA = 256
ROPE_THETA = 10000.0
MLA_SCALE = (NOPE_DIM + ROPE_DIM) ** -0.5
DSA_HEADS = 8
DSA_KV_HEADS = 2
DSA_GROUP = DSA_HEADS // DSA_KV_HEADS
DSA_HEAD_DIM = 64
DSA_SCALE = DSA_HEAD_DIM ** -0.5
IDX_HEADS = 8
IDX_DIM = 64
IDX_W_SCALE = (IDX_HEADS ** -0.5) * (IDX_DIM ** -0.5)
TOPK_MAX = 256
NUM_BUCKETS = 32
MAX_DISTANCE = 128
W_A = MLA_HEADS * V_DIM
W_B = DSA_HEADS * DSA_HEAD_DIM
SPLIT_SIZES = (Q_LORA, KV_LORA, ROPE_DIM, W_A, W_B, DSA_KV_HEADS * DSA_HEAD_DIM, DSA_KV_HEADS * DSA_HEAD_DIM,
               IDX_HEADS * IDX_DIM, IDX_DIM, IDX_HEADS, W_B, D_MODEL, D_MODEL)

LANES = 128
SUBLANES = 8
VMEM_LIMIT_BYTES = 56 * 1024 * 1024

TK = 256
TM = 512
NEG = -1e30
LOG2E = 1.4426950408889634
MAX_BISECT = 200
BISECT_PER_CHECK = 4

QK_PAD = MLA_HEADS * LANES
KV_DIM = DSA_KV_HEADS * DSA_HEAD_DIM

A_CQ = (0, Q_LORA)
A_CKV = (A_CQ[1], A_CQ[1] + KV_LORA)
A_KR = (A_CKV[1], A_CKV[1] + LANES)
A_SM = (A_KR[1], A_KR[1] + LANES)
A_QB = (A_SM[1], A_SM[1] + W_B)
A_KB = (A_QB[1], A_QB[1] + KV_DIM)
A_VB = (A_KB[1], A_KB[1] + KV_DIM)
A_QI = (A_VB[1], A_VB[1] + IDX_HEADS * IDX_DIM)
A_WIDTH = A_QI[1]
G_GA = (0, W_A)
G_GB = (G_GA[1], G_GA[1] + W_B)
G_MA = (G_GB[1], G_GB[1] + D_MODEL)
G_MB = (G_MA[1], G_MA[1] + D_MODEL)
G_WIDTH = G_MB[1]


def _params(n_grid):
    return pltpu.CompilerParams(dimension_semantics=("arbitrary",) * n_grid, vmem_limit_bytes=VMEM_LIMIT_BYTES)


def _sigmoid(x):
    return 1.0 / (1.0 + jnp.exp(-x))


def _rms(x, g):
    return x * lax.rsqrt(jnp.mean(x * x, axis=-1, keepdims=True) + EPS) * g


def _dot(a, b):
    return jnp.dot(a, b, preferred_element_type=F32)


def _dot_nt(a, b):
    return lax.dot_general(a, b, (((1,), (1,)), ((), ())), preferred_element_type=F32)


def _key_rows(j, tk):
    start = j * tk if isinstance(j, int) else pl.multiple_of(j * tk, tk)
    return pl.ds(start, tk)


def _rows8(x, op, ways=None):
    parts = [x[r:r + SUBLANES] for r in range(0, x.shape[0], SUBLANES)]
    if ways is not None:
        chains = parts[:ways]
        for n, part in enumerate(parts[ways:]):
            chains[n % ways] = op(chains[n % ways], part)
        parts = chains
    while len(parts) > 1:
        parts = [op(parts[n], parts[n + 1]) for n in range(0, len(parts) - 1, 2)] + parts[len(parts) & ~1:]
    return parts[0]


def _mod_kernel(c_ref, w_ref, b_ref, o_ref):
    c = c_ref[...]
    a = c * _sigmoid(c)
    w = w_ref[0]
    a_hi = a.astype(BF16)
    a_lo = (a - a_hi.astype(F32)).astype(BF16)
    w_hi = w.astype(BF16)
    w_lo = (w - w_hi.astype(F32)).astype(BF16)
    o_ref[0] = _dot(a_hi, w_hi) + (_dot(a_hi, w_lo) + _dot(a_lo, w_hi)) + b_ref[0]


def _modulation(c_all, w_ada, b_ada):
    depth = w_ada.shape[0]
    n = c_all.shape[0]
    tn = D_MODEL
    return pl.pallas_call(
        _mod_kernel,
        out_shape=jax.ShapeDtypeStruct((depth, n, 3 * D_MODEL), F32),
        grid=(depth, 3 * D_MODEL // tn),
        in_specs=[
            pl.BlockSpec((n, D_MODEL), lambda l, j: (0, 0)),
            pl.BlockSpec((1, D_MODEL, tn), lambda l, j: (l, 0, j)),
            pl.BlockSpec((1, 1, tn), lambda l, j: (l, 0, j)),
        ],
        out_specs=pl.BlockSpec((1, n, tn), lambda l, j: (l, 0, j)),
        compiler_params=_params(2),
        name="adaln_mod",
    )(c_all, w_ada, b_ada.reshape(depth, 1, 3 * D_MODEL))


def _rel_bucket(rel):
    half = NUM_BUCKETS // 2
    max_exact = half // 2
    n = np.abs(rel)
    large = max_exact + (np.log(np.maximum(n, 1).astype(np.float32) / max_exact)
                         / np.float32(np.log(MAX_DISTANCE / max_exact)) * (half - max_exact)).astype(np.int32)
    large = np.minimum(large, half - 1)
    return np.where(n < max_exact, n, large) + np.where(rel > 0, half, 0)


def _bias_kernel(bucket_ref, rel_ref, o_ref, *, far_bucket):
    h = pl.program_id(0)
    for t in range(2):
        b = bucket_ref[t]
        acc = jnp.zeros(b.shape, F32)
        for i in range(NUM_BUCKETS):
            acc = jnp.where(b == i, rel_ref[i, h], acc)
        o_ref[0, t] = (acc - rel_ref[far_bucket, h]) * LOG2E


def _bias_tiles(rel_bias, tk, tq):
    r = np.arange(tk)[:, None]
    c = np.arange(tq)[None, :]
    far_bucket = int(_rel_bucket(np.array(-tk)))
    assert np.all(_rel_bucket(-tk - np.arange(1, tk + tq)) == far_bucket)
    buckets = jnp.asarray(np.stack([_rel_bucket(r - tk - c), _rel_bucket(r - c)]).astype(np.int32))
    return pl.pallas_call(
        functools.partial(_bias_kernel, far_bucket=far_bucket),
        out_shape=jax.ShapeDtypeStruct((DSA_HEADS, 2, tk, tq), F32),
        grid=(DSA_HEADS,),
        in_specs=[
            pl.BlockSpec((2, tk, tq), lambda h: (0, 0, 0)),
            pl.BlockSpec(memory_space=pltpu.SMEM),
        ],
        out_specs=pl.BlockSpec((1, 2, tk, tq), lambda h: (h, 0, 0, 0)),
        compiler_params=_params(1),
        name="t5_bias_tiles",
    )(buckets, rel_bias)


def _rope_group(g, c_ref, s1_ref, s2_ref):
    return (g * c_ref[...] + pltpu.roll(g, LANES - ROPE_DIM // 2, 1) * s1_ref[...]
            + pltpu.roll(g, ROPE_DIM // 2, 1) * s2_ref[...])


def _proj_kernel(x_ref, shift_ref, scale_ref, gn_ref, wa_ref, gq_ref, wuq_ref, gkv_ref, wkn_ref, wv_ref,
                 qc_ref, qs1_ref, qs2_ref, kc_ref, ks1_ref, ks2_ref, *rest, emit_kv, tk, n_aliased):
    ckv_o, kr_o, kb_o, vb_o, ki_o, q_o, qb_o, qi_o, wit_o, *kv_outs = rest[n_aliased:]

    def put(cache_o, value):
        for n in range(cache_o.shape[0]):
            cache_o[n, 0] = value

    x = x_ref[0]
    h = _rms(x, gn_ref[...]) * (1.0 + scale_ref[0]) + shift_ref[0]
    h16 = h.astype(BF16)

    def zcols(a):
        return _dot(h16, wa_ref[:, a[0]:a[1]])

    cqn = _rms(zcols(A_CQ), gq_ref[...]).astype(BF16)
    q = _dot(cqn, wuq_ref[...])
    for hd in range(MLA_HEADS):
        sl = slice(hd * LANES, (hd + 1) * LANES)
        q_hd = _rope_group(q[:, sl], qc_ref, qs1_ref, qs2_ref)
        if emit_kv:
            q_o[0, sl, :] = q_hd.T.astype(BF16)
        else:
            q_o[0, :, sl] = q_hd.astype(BF16)

    ckvn = _rms(zcols(A_CKV), gkv_ref[...])
    put(ckv_o, ckvn)
    krg = _rope_group(zcols(A_KR), kc_ref, ks1_ref, ks2_ref)
    put(kr_o, krg[:, 0:ROPE_DIM])

    sm = zcols(A_SM)
    ki = sm[:, 0:IDX_DIM]
    put(ki_o, ki)
    wit_o[0] = sm.T[IDX_DIM:IDX_DIM + IDX_HEADS, :] * IDX_W_SCALE
    qb = zcols(A_QB) * (DSA_SCALE * LOG2E)
    qi = zcols(A_QI)
    qb_o[0] = (qb.T if emit_kv else qb).astype(BF16)
    qi_o[0] = (qi.T if emit_kv else qi).astype(BF16)
    kb = zcols(A_KB)
    vb = zcols(A_VB)
    put(kb_o, kb)
    put(vb_o, vb)

    if emit_kv:
        kcat_o, vat_o, kb16_o, vbt_o, ki16_o = kv_outs
        ckv16 = ckvn.astype(BF16)
        kn = _dot(ckv16, wkn_ref[...])
        lane = lax.broadcasted_iota(jnp.int32, krg.shape, 1)
        k_add = jnp.where(lane >= NOPE_DIM, krg, 0.0)
        for hd in range(MLA_HEADS):
            sl = slice(hd * LANES, (hd + 1) * LANES)
            kcat_o[0, :, sl] = (kn[:, sl] + k_add).astype(BF16)
        va = _dot(ckv16, wv_ref[...])
        for g in range(DSA_KV_HEADS):
            kb16_o[0, g] = kb[:, g * DSA_HEAD_DIM:(g + 1) * DSA_HEAD_DIM].astype(BF16)
        ki16_o[0] = ki.astype(BF16)
        for c in range(x.shape[0] // tk):
            rows = slice(c * tk, (c + 1) * tk)
            vat_o[0, c] = va[rows, :].T.astype(BF16)
            vbt_o[0, c] = vb[rows, :].T.astype(BF16)


def _proj(x, shift, scale, gn, wa, gq, wuq, gkv, wkn, wv, q_tabs, k_tabs, caches, *, layer, depth, tm, tk, emit_kv):
    nb, t, _ = x.shape
    r = shift.shape[1]
    rb = 1 if r == 1 else tm
    grid = (nb, t // tm)
    tok = lambda w: pl.BlockSpec((1, tm, w), lambda b, i: (b, i, 0))
    if caches is None:
        stacked = lambda w: pl.BlockSpec((depth, 1, tm, w), lambda b, i: (0, b, i, 0))
    else:
        stacked = lambda w: pl.BlockSpec((1, 1, tm, w), lambda b, i: (layer, b, i, 0))
    mod = pl.BlockSpec((1, rb, D_MODEL), (lambda b, i: (b, 0, 0)) if r == 1 else (lambda b, i: (b, i, 0)))
    const = lambda a: pl.BlockSpec(a.shape, lambda b, i: (0,) * a.ndim)
    tab = pl.BlockSpec((tm, LANES), lambda b, i: (i, 0))
    in_specs = [tok(D_MODEL), mod, mod, const(gn), const(wa), const(gq), const(wuq), const(gkv), const(wkn),
                const(wv)] + [tab] * 6
    operands = [x, shift, scale, gn, wa, gq, wuq, gkv, wkn, wv, *q_tabs, *k_tabs]
    cache_widths = (KV_LORA, ROPE_DIM, KV_DIM, KV_DIM, IDX_DIM)
    aliases = {}
    if caches is not None:
        aliases = {len(operands) + n: n for n in range(len(cache_widths))}
        in_specs += [pl.BlockSpec(memory_space=pl.ANY)] * len(cache_widths)
        operands += list(caches)
    tok_t = lambda w: pl.BlockSpec((1, w, tm), lambda b, i: (b, 0, i))
    q_shape = (lambda w: (nb, w, t)) if emit_kv else (lambda w: (nb, t, w))
    q_spec = tok_t if emit_kv else tok
    q_widths = (QK_PAD, W_B, IDX_HEADS * IDX_DIM)
    out_shape = ([jax.ShapeDtypeStruct((depth, nb, t, w), F32) for w in cache_widths]
                 + [jax.ShapeDtypeStruct(q_shape(w), BF16) for w in q_widths]
                 + [jax.ShapeDtypeStruct((nb, IDX_HEADS, t), F32)])
    out_specs = [stacked(w) for w in cache_widths] + [q_spec(w) for w in q_widths] + [tok_t(IDX_HEADS)]
    if emit_kv:
        nkt = tm // tk
        out_shape += [
            jax.ShapeDtypeStruct((nb, t, QK_PAD), BF16),
            jax.ShapeDtypeStruct((nb, t // tk, W_A, tk), BF16),
            jax.ShapeDtypeStruct((nb, DSA_KV_HEADS, t, DSA_HEAD_DIM), BF16),
            jax.ShapeDtypeStruct((nb, t // tk, KV_DIM, tk), BF16),
            jax.ShapeDtypeStruct((nb, t, IDX_DIM), BF16),
        ]
        out_specs += [tok(QK_PAD), pl.BlockSpec((1, nkt, W_A, tk), lambda b, i: (b, i, 0, 0)),
                      pl.BlockSpec((1, DSA_KV_HEADS, tm, DSA_HEAD_DIM), lambda b, i: (b, 0, i, 0)),
                      pl.BlockSpec((1, nkt, KV_DIM, tk), lambda b, i: (b, i, 0, 0)), tok(IDX_DIM)]
    return pl.pallas_call(
        functools.partial(_proj_kernel, emit_kv=emit_kv, tk=tk, n_aliased=len(aliases)),
        out_shape=out_shape, grid=grid, in_specs=in_specs, out_specs=out_specs, input_output_aliases=aliases,
        compiler_params=_params(2), name="proj_kv" if emit_kv else "proj_q",
    )(*operands)


class _Softmax:
    ONES_ROWS = 16

    def __init__(self, s_scr, mx_scr, m_scr, acc_scr):
        self.s, self.mx, self.m, self.acc = s_scr, mx_scr, m_scr, acc_scr
        self.n_heads = acc_scr.shape[0]
        self.dv = acc_scr.shape[1] - self.ONES_ROWS

    @classmethod
    def scratch(cls, n_heads, dv, tk, tq):
        stat = pltpu.VMEM((n_heads, SUBLANES, tq), F32)
        return [pltpu.VMEM((n_heads, 2, tk, tq), F32), stat, stat, pltpu.VMEM((n_heads, dv + cls.ONES_ROWS, tq), F32)]

    def reset(self):
        self.mx[...] = jnp.full(self.mx.shape, NEG, F32)
        self.m[...] = jnp.full(self.m.shape, NEG, F32)
        self.acc[...] = jnp.zeros(self.acc.shape, F32)

    def produce(self, hd, slot, s):
        self.s[hd, slot] = s
        self.mx[hd] = jnp.maximum(self.mx[hd], _rows8(s, jnp.maximum, ways=4))

    def consume(self, hd, slot, vt):
        m_old = self.m[hd, 0:1, :]
        m_new = jnp.max(self.mx[hd], axis=0, keepdims=True)
        alpha = jnp.exp2(m_old - m_new)
        p = jnp.exp2(self.s[hd, slot] - m_new)
        self.m[hd] = jnp.broadcast_to(m_new, self.m.shape[1:])
        vt1 = jnp.concatenate([vt, jnp.ones((self.ONES_ROWS, vt.shape[1]), vt.dtype)], axis=0)
        self.acc[hd] = alpha * self.acc[hd] + _dot(vt1, p.astype(BF16))

    def run(self, last, logits, values):
        def step(t_consume, t_produce, kind, slot_produce):
            for hd in range(self.n_heads):
                if t_consume is not None:
                    self.consume(hd, 1 - slot_produce, values(hd, t_consume))
                if t_produce is not None:
                    self.produce(hd, slot_produce, logits(hd, t_produce, kind))

        when = lambda cond, fn: pl.when(cond)(fn)
        self.reset()
        step(None, last, 2, 0)
        when(last >= 1, lambda: step(last, last - 1, 1, 1))
        n_far = last - 1

        def far_pair(n, _):
            t = last - 2 - 2 * n
            step(t + 1, t, 0, 0)
            step(t, t - 1, 0, 1)
            return 0

        lax.fori_loop(0, n_far // 2, far_pair, 0)
        odd = (n_far >= 1) & (n_far % 2 == 1)
        when(odd, lambda: step(1, 0, 0, 0))
        in_slot0 = (last == 0) | odd
        when(in_slot0, lambda: step(0, None, None, 1))
        when(jnp.logical_not(in_slot0), lambda: step(0, None, None, 0))

    def run_staggered(self, last, logits, values, lead):
        a, b = range(lead), range(lead, self.n_heads)

        def act(heads, t_consume, t_produce, kind, slot_produce):
            for hd in heads:
                if t_consume is not None:
                    self.consume(hd, 1 - slot_produce, values(hd, t_consume))
                if t_produce is not None:
                    self.produce(hd, slot_produce, logits(hd, t_produce, kind))

        when = lambda cond, fn: pl.when(cond)(fn)
        self.reset()
        act(a, None, last, 2, 0)
        act(a, last, last - 1, 1, 1)
        act(b, None, last, 2, 0)
        act(a, last - 1, last - 2, 0, 0)
        act(b, last, last - 1, 1, 1)

        def pair(n, _):
            k = 3 + 2 * n
            act(a, last - k + 1, last - k, 0, 1)
            act(b, last - k + 2, last - k + 1, 0, 0)
            act(a, last - k, last - k - 1, 0, 0)
            act(b, last - k + 1, last - k, 0, 1)
            return 0

        lax.fori_loop(0, (last - 2) // 2, pair, 0)

        def odd_tail():
            act(a, 1, 0, 0, 1)
            act(b, 2, 1, 0, 0)
            act(a, 0, None, None, 0)
            act(b, 1, 0, 0, 1)
            act(b, 0, None, None, 0)

        def even_tail():
            act(a, 0, None, None, 1)
            act(b, 1, 0, 0, 0)
            act(b, 0, None, None, 1)

        when(last % 2 == 1, odd_tail)
        when(last % 2 == 0, even_tail)

    def finish(self, *o_refs):
        per_out = self.n_heads // len(o_refs)
        for n, o_ref in enumerate(o_refs):
            o_t = jnp.concatenate([self.acc[hd, 0:self.dv, :] / self.acc[hd, self.dv:self.dv + 1, :]
                                   for hd in range(n * per_out, (n + 1) * per_out)], axis=0)
            o_ref[0] = o_t.T


def _diagonal_tile_visible(t):
    r = lax.broadcasted_iota(jnp.int32, (t, t), 0)
    c = lax.broadcasted_iota(jnp.int32, (t, t), 1)
    return (r // CHUNK) <= (c // CHUNK)


def _score_stats_init(tq):
    return (jnp.full((1, tq), jnp.inf, F32), jnp.full((1, tq), -jnp.inf, F32),
            jnp.zeros((SUBLANES, tq), F32), jnp.zeros((SUBLANES, tq), F32))


def _score_stats(stats, t, visible=None):
    lo, hi, c_z, c_zp = stats
    t_min = t if visible is None else jnp.where(visible, t, jnp.inf)
    return (jnp.minimum(lo, jnp.min(t_min, axis=0, keepdims=True)), jnp.maximum(hi, jnp.max(t, axis=0, keepdims=True)),
            c_z + _rows8(jnp.where(t >= 0.0, 1.0, 0.0), jnp.add), c_zp + _rows8(jnp.where(t > 0.0, 1.0, 0.0), jnp.add))


def _select_topk(idx_scr, n_used, stats, n_vis, kf):
    _, tk, tq = idx_scr.shape

    def count(pred):
        def body(j, c):
            return c + _rows8(jnp.where(pred(idx_scr[j]), 1.0, 0.0), jnp.add)
        return jnp.sum(lax.fori_loop(0, n_used, body, jnp.zeros((SUBLANES, tq), F32)), axis=0, keepdims=True)

    lo, hi, c_z, c_zp = stats
    c_z = jnp.sum(c_z, axis=0, keepdims=True)
    c_zp = jnp.sum(c_zp, axis=0, keepdims=True)
    hi = hi + (jnp.abs(hi) * 2.0 ** -20 + 1e-30)
    zero_up = (c_z >= kf) & (lo < 0.0)
    zero_tie = jnp.where((c_z >= kf) & (c_zp < kf), 1.0, 0.0)
    c_lo = jnp.where(zero_up, c_z, n_vis)
    lo = jnp.where(zero_up, 0.0, lo)
    hi = jnp.where((c_z < kf) & (hi > 0.0), 0.0, hi)

    def settled(lo, hi, c_lo):
        mid = 0.5 * lo + 0.5 * hi
        return jnp.where((c_lo <= kf) | (mid <= lo) | (mid >= hi), 1.0, 0.0)

    def bis_cond(st):
        _, _, _, done, it = st
        return jnp.logical_and(it < MAX_BISECT, jnp.min(done) < 0.5)

    def bis_body(st):
        lo, hi, c_lo, done, it = st
        for _ in range(BISECT_PER_CHECK):
            mid = 0.5 * lo + 0.5 * hi
            c = count(lambda t: t >= mid)
            live = done < 0.5
            up = live & (c >= kf)
            dn = live & (c < kf)
            lo = jnp.where(up, mid, lo)
            c_lo = jnp.where(up, c, c_lo)
            hi = jnp.where(dn, mid, hi)
            done = jnp.maximum(done, settled(lo, hi, c_lo))
        return lo, hi, c_lo, done, it + BISECT_PER_CHECK

    done = jnp.maximum(settled(lo, hi, c_lo), zero_tie)
    lo, hi, c_lo, _, _ = lax.while_loop(bis_cond, bis_body, (lo, hi, c_lo, done, jnp.int32(0)))
    any_tie = jnp.max(jnp.where(c_lo > kf, 1.0, 0.0)) > 0.5

    @pl.when(jnp.logical_not(any_tie))
    def _():
        def body(j, _):
            idx_scr[j] = jnp.where(idx_scr[j] >= lo, 0.0, NEG)
            return 0
        lax.fori_loop(0, n_used, body, 0)

    @pl.when(any_tie)
    def _():
        need = kf - count(lambda t: t > lo)
        r = lax.broadcasted_iota(jnp.int32, (tk, tk), 0)
        c = lax.broadcasted_iota(jnp.int32, (tk, tk), 1)
        before = jnp.where(c < r, 1.0, 0.0).astype(BF16)

        def body(j, seen):
            t = idx_scr[j]
            eq = jnp.where(t == lo, 1.0, 0.0)
            rank = _dot(before, eq.astype(BF16)) + seen
            take = (t > lo) | ((t == lo) & (rank < need))
            idx_scr[j] = jnp.where(take, 0.0, NEG)
            return seen + jnp.sum(_rows8(eq, jnp.add), axis=0, keepdims=True)
        lax.fori_loop(0, n_used, body, jnp.zeros((1, tq), F32))


def _attend_kernel(qi_ref, wit_ref, ki_ref, qb_ref, kb_ref, vbt_ref, bias_ref, q_ref, k_ref, vt_ref, oa_ref, ob_ref,
                   idx_scr, *scratch, t, topk):
    tq = tk = t
    last = pl.program_id(1)
    n_used = last + 1
    visible = _diagonal_tile_visible(t)

    wit = wit_ref[0]

    def idx_tile(j):
        k_j = ki_ref[0, _key_rows(j, tk), :]
        acc = jnp.zeros((tk, tq), F32)
        for hd in range(IDX_HEADS):
            s = _dot(k_j, qi_ref[0, hd * IDX_DIM:(hd + 1) * IDX_DIM, :])
            acc = acc + jnp.maximum(s, 0.0) * wit[hd:hd + 1, :]
        return acc

    def idx_body(j, stats):
        t = idx_tile(j)
        idx_scr[j] = t
        return _score_stats(stats, t)

    def idx_pair(n, stats):
        return idx_body(2 * n + 1, idx_body(2 * n, stats))

    stats = lax.fori_loop(0, last // 2, idx_pair, _score_stats_init(tq))
    stats = lax.cond(last % 2 == 1, lambda st: idx_body(last - 1, st), lambda st: st, stats)
    t_last = jnp.where(visible, idx_tile(last), -jnp.inf)
    idx_scr[last] = t_last
    stats = _score_stats(stats, t_last, visible)
    lane = lax.broadcasted_iota(jnp.int32, (1, tq), 1)
    n_vis = last * tk + (lane // CHUNK + 1) * CHUNK
    _select_topk(idx_scr, n_used, stats, n_vis.astype(F32), float(topk))

    softmax = _Softmax(*scratch)
    kv_lanes = lambda hd: slice((hd // DSA_GROUP) * DSA_HEAD_DIM, (hd // DSA_GROUP + 1) * DSA_HEAD_DIM)

    def logits(hd, j, kind):
        if hd < MLA_HEADS:
            sl = slice(hd * LANES, (hd + 1) * LANES)
            s = _dot(k_ref[0, _key_rows(j, tk), sl], q_ref[0, sl, :])
            return s + jnp.where(visible, 0.0, NEG) if kind == 2 else s
        hd -= MLA_HEADS
        s = _dot(kb_ref[0, hd // DSA_GROUP, _key_rows(j, tk), :], qb_ref[0, hd * DSA_HEAD_DIM:(hd + 1) * DSA_HEAD_DIM, :])
        return s + idx_scr[j] if kind == 0 else s + (idx_scr[j] + bias_ref[hd, kind - 1])

    def values(hd, j):
        if hd < MLA_HEADS:
            return vt_ref[0, j, hd * V_DIM:(hd + 1) * V_DIM, :]
        return vbt_ref[0, j, kv_lanes(hd - MLA_HEADS), :]

    pl.when(last < 2)(lambda: softmax.run(last, logits, values))
    pl.when(last >= 2)(lambda: softmax.run_staggered(last, logits, values, MLA_HEADS))
    softmax.finish(oa_ref, ob_ref)


def _attend(qi, wit, ki, qb, kb, vbt, bias, q, k, vt, *, t, topk):
    assert V_DIM == DSA_HEAD_DIM and MLA_HEADS == DSA_HEADS
    nb, _, seq = qb.shape
    n_tiles = seq // t
    per_q = lambda w: pl.BlockSpec((1, w, t), lambda b, i: (b, 0, i))
    per_b = lambda a: pl.BlockSpec((1,) + a.shape[1:], lambda b, i: (b,) + (0,) * (a.ndim - 1))
    out = lambda w: pl.BlockSpec((1, t, w), lambda b, i: (b, i, 0))
    return pl.pallas_call(
        functools.partial(_attend_kernel, t=t, topk=topk),
        out_shape=[jax.ShapeDtypeStruct((nb, seq, W_A), F32), jax.ShapeDtypeStruct((nb, seq, W_B), F32)],
        grid=(nb, n_tiles),
        in_specs=[per_q(IDX_HEADS * IDX_DIM), per_q(IDX_HEADS), per_b(ki), per_q(W_B), per_b(kb), per_b(vbt),
                  pl.BlockSpec(bias.shape, lambda b, i: (0, 0, 0, 0)), per_q(QK_PAD), per_b(k), per_b(vt)],
        out_specs=[out(W_A), out(W_B)],
        scratch_shapes=([pltpu.VMEM((n_tiles, t, t), F32)]
                        + _Softmax.scratch(MLA_HEADS + DSA_HEADS, DSA_HEAD_DIM, t, t)),
        compiler_params=_params(2), name="attend_causal",
    )(qi, wit, ki, qb, kb, vbt, bias, q, k, vt)


def _two_pass_softmax(n_tiles, logits, values, s_scr, dv):
    mx = None
    for t in range(n_tiles):
        s = logits(t)
        s_scr[t] = s
        part = _rows8(s, jnp.maximum, ways=4)
        mx = part if mx is None else jnp.maximum(mx, part)
    m = jnp.max(mx, axis=0, keepdims=True)
    acc = None
    for t in range(n_tiles):
        p = jnp.exp2(s_scr[t] - m).astype(BF16)
        vt = values(t)
        part = _dot(jnp.concatenate([vt, jnp.ones((_Softmax.ONES_ROWS, vt.shape[1]), BF16)], axis=0), p)
        acc = part if acc is None else acc + part
    return acc[0:dv] / acc[dv:dv + 1]


def _mla_cached_kernel(q_ref, ckv_c_ref, kr_c_ref, ckv_n_ref, kr_n_ref, wuk_ref, wuv_ref, sel_ref, o_ref, s_scr, *,
                       ts, tk, n_cache, n_valid_last):
    lanes = MLA_HEADS * ts
    q_lat, q_rope = [], []
    for hd in range(MLA_HEADS):
        qg = q_ref[0, :, hd * LANES:(hd + 1) * LANES]
        q_lat.append(_dot(qg, wuk_ref[hd]).astype(BF16))
        q_rope.append(_dot(qg, sel_ref[...]).astype(BF16))
    q_lat = jnp.concatenate(q_lat, axis=0)
    q_rope = jnp.concatenate(q_rope, axis=0)

    def latent(t):
        return ckv_c_ref[0, t * tk:(t + 1) * tk, :] if t < n_cache else ckv_n_ref[0]

    def logits(t):
        kr = kr_c_ref[0, t * tk:(t + 1) * tk, :] if t < n_cache else kr_n_ref[0]
        s = _dot_nt(latent(t).astype(BF16), q_lat) + _dot_nt(kr.astype(BF16), q_rope)
        if t == n_cache:
            s = s + jnp.where(lax.broadcasted_iota(jnp.int32, s.shape, 0) < n_valid_last, 0.0, NEG)
        return s

    o_lat = _two_pass_softmax(n_cache + 1, logits, lambda t: latent(t).T.astype(BF16), s_scr, KV_LORA)
    o_lat = o_lat.astype(BF16)
    head_of_lane = lax.broadcasted_iota(jnp.int32, (V_DIM, lanes), 1) // ts
    out = jnp.zeros((V_DIM, lanes), F32)
    for hd in range(MLA_HEADS):
        out = jnp.where(head_of_lane == hd, _dot(wuv_ref[hd], o_lat), out)
    o_ref[0] = out


def _mla_cached(q, ckv_cache, kr_cache, ckv_new, kr_new, wuk, wuv, sel, *, layer, tk, n_valid_last):
    nb, ts, _ = q.shape
    past = ckv_cache.shape[2]
    lanes = MLA_HEADS * ts
    per_b = lambda a: pl.BlockSpec((1,) + a.shape[1:], lambda b: (b,) + (0,) * (a.ndim - 1))
    const = lambda a: pl.BlockSpec(a.shape, lambda b: (0,) * a.ndim)
    return pl.pallas_call(
        functools.partial(_mla_cached_kernel, ts=ts, tk=tk, n_cache=past // tk, n_valid_last=n_valid_last),
        out_shape=jax.ShapeDtypeStruct((nb, V_DIM, lanes), F32), grid=(nb,),
        in_specs=[per_b(q), _layer_rows(ckv_cache, layer), _layer_rows(kr_cache, layer), per_b(ckv_new),
                  per_b(kr_new), const(wuk), const(wuv), const(sel)],
        out_specs=pl.BlockSpec((1, V_DIM, lanes), lambda b: (b, 0, 0)),
        scratch_shapes=[pltpu.VMEM((past // tk + 1, tk, lanes), F32)],
        compiler_params=_params(1), name="mla_cached",
    )(q, ckv_cache, kr_cache, ckv_new, kr_new, wuk, wuv, sel)


def _dsa_cached_kernel(qi_ref, w_ref, ki_c_ref, ki_n_ref, qb_ref, kb_c_ref, kb_n_ref, vb_c_ref, vb_n_ref, bias_ref,
                       o_ref, idx_scr, s_scr, *, ts, tk, n_cache, n_valid_last, topk):
    n_tiles = n_cache + 1
    lanes = DSA_GROUP * ts
    tile = lambda c_ref, n_ref, t: c_ref[0, t * tk:(t + 1) * tk, :] if t < n_cache else n_ref[0]
    visible = lax.broadcasted_iota(jnp.int32, (tk, lanes), 0) < n_valid_last

    stats = _score_stats_init(lanes)
    for t in range(n_tiles):
        s = _dot_nt(tile(ki_c_ref, ki_n_ref, t).astype(BF16), qi_ref[0])
        r = jnp.maximum(s, 0.0) * w_ref[0]
        x = r[:, 0:lanes]
        for blk in range(1, IDX_HEADS * ts // lanes):
            x = x + r[:, blk * lanes:(blk + 1) * lanes]
        shift = lanes // 2
        while shift >= ts:
            x = x + pltpu.roll(x, shift, 1)
            shift //= 2
        if t == n_cache:
            x = jnp.where(visible, x, -jnp.inf)
        idx_scr[t] = x
        stats = _score_stats(stats, x, visible if t == n_cache else None)
    n_vis = jnp.full((1, lanes), float(n_cache * tk + n_valid_last), F32)
    _select_topk(idx_scr, n_tiles, stats, n_vis, float(topk))

    for g in range(DSA_KV_HEADS):
        def logits(t, g=g):
            s = _dot_nt(tile(kb_c_ref, kb_n_ref, t).astype(BF16), qb_ref[0, g]) + idx_scr[t]
            return s + bias_ref[g, t - (n_tiles - 2)] if t >= n_tiles - 2 else s

        def values(t, g=g):
            return tile(vb_c_ref, vb_n_ref, t).T[g * DSA_HEAD_DIM:(g + 1) * DSA_HEAD_DIM, :].astype(BF16)

        o_ref[0, g] = _two_pass_softmax(n_tiles, logits, values, s_scr, DSA_HEAD_DIM)


def _layer_rows(cache, layer):
    return pl.BlockSpec((None, 1) + cache.shape[2:], lambda b: (layer, b, 0, 0))


def _dsa_cached(qi, w, ki_cache, ki_new, qb, kb_cache, kb_new, vb_cache, vb_new, bias, *, layer, ts, tk,
                n_valid_last, topk):
    nb = qi.shape[0]
    past = ki_cache.shape[2]
    n_tiles = past // tk + 1
    lanes = DSA_GROUP * ts
    per_b = lambda a: pl.BlockSpec((1,) + a.shape[1:], lambda b: (b,) + (0,) * (a.ndim - 1))
    cached = lambda a: _layer_rows(a, layer)
    return pl.pallas_call(
        functools.partial(_dsa_cached_kernel, ts=ts, tk=tk, n_cache=past // tk, n_valid_last=n_valid_last, topk=topk),
        out_shape=jax.ShapeDtypeStruct((nb, DSA_KV_HEADS, DSA_HEAD_DIM, lanes), F32), grid=(nb,),
        in_specs=[per_b(qi), per_b(w), cached(ki_cache), per_b(ki_new), per_b(qb), cached(kb_cache), per_b(kb_new),
                  cached(vb_cache), per_b(vb_new), pl.BlockSpec(bias.shape, lambda b: (0,) * bias.ndim)],
        out_specs=pl.BlockSpec((1, DSA_KV_HEADS, DSA_HEAD_DIM, lanes), lambda b: (b, 0, 0, 0)),
        scratch_shapes=[pltpu.VMEM((n_tiles, tk, lanes), F32), pltpu.VMEM((n_tiles, tk, lanes), F32)],
        compiler_params=_params(1), name="dsa_cached",
    )(qi, w, ki_cache, ki_new, qb, kb_cache, kb_new, vb_cache, vb_new, bias)


def _out_kernel(x_ref, shift_ref, scale_ref, gate_ref, gn_ref, oa_ref, ob_ref, wg_ref, woa_ref, wob_ref, wout_ref,
                gf_ref, o_ref, *, final):
    x = x_ref[0]
    h16 = (_rms(x, gn_ref[...]) * (1.0 + scale_ref[0]) + shift_ref[0]).astype(BF16)

    def gcols(a):
        return _dot(h16, wg_ref[:, a[0]:a[1]])

    ga = gcols(G_GA)
    ya = _dot((oa_ref[0] * (ga * _sigmoid(ga))).astype(BF16), woa_ref[...])
    gb = gcols(G_GB)
    yb = _dot((ob_ref[0] * (gb * _sigmoid(gb))).astype(BF16), wob_ref[...])
    m = _sigmoid(gcols(G_MA)) * ya + _sigmoid(gcols(G_MB)) * yb
    y = x + gate_ref[0] * _dot(m.astype(BF16), wout_ref[...])
    o_ref[0] = _rms(y, gf_ref[...]) if final else y


def _out(x, shift, scale, gate, gn, oa, ob, wg, woa, wob, wout, gf, *, tm, final):
    nb, t, _ = x.shape
    r = shift.shape[1]
    rb = 1 if r == 1 else tm
    tok = lambda w: pl.BlockSpec((1, tm, w), lambda b, i: (b, i, 0))
    mod = pl.BlockSpec((1, rb, D_MODEL), (lambda b, i: (b, 0, 0)) if r == 1 else (lambda b, i: (b, i, 0)))
    const = lambda a: pl.BlockSpec(a.shape, lambda b, i: (0,) * a.ndim)
    return pl.pallas_call(
        functools.partial(_out_kernel, final=final),
        out_shape=jax.ShapeDtypeStruct(x.shape, F32), grid=(nb, t // tm),
        in_specs=[tok(D_MODEL), mod, mod, mod, const(gn), tok(W_A), tok(W_B), const(wg), const(woa), const(wob),
                  const(wout), const(gf)],
        out_specs=tok(D_MODEL), compiler_params=_params(2), name="out_final" if final else "out_mix",
    )(x, shift, scale, gate, gn, oa, ob, wg, woa, wob, wout, gf)


def _pack_weights(w_in, w_uq, w_ukv, w_oa, w_ob, w_out):
    depth = w_in.shape[0]
    offs = np.concatenate([[0], np.cumsum(SPLIT_SIZES)])
    cq, ckv, kr, ga, qb, kb, vb, qi, ki, wi, gb, ma, mb = [w_in[:, :, offs[n]:offs[n + 1]] for n in range(13)]
    z = lambda w: jnp.zeros((depth, D_MODEL, w), F32)
    krg = jnp.concatenate([kr, z(NOPE_DIM - ROPE_DIM), kr, z(LANES - NOPE_DIM - ROPE_DIM)], axis=-1)
    sm = jnp.concatenate([ki, wi, z(LANES - IDX_DIM - IDX_HEADS)], axis=-1)
    wa = jnp.concatenate([cq, ckv, krg, sm, qb, kb, vb, qi], axis=-1).astype(BF16)
    wg = jnp.concatenate([ga, gb, ma, mb], axis=-1).astype(BF16)
    qd = NOPE_DIM + ROPE_DIM
    wuq = jnp.pad(w_uq.reshape(depth, Q_LORA, MLA_HEADS, qd), ((0, 0), (0, 0), (0, 0), (0, LANES - qd)))
    wuq = wuq.reshape(depth, Q_LORA, QK_PAD).astype(BF16)
    ukv = w_ukv.reshape(depth, KV_LORA, MLA_HEADS, NOPE_DIM + V_DIM)
    wkn = jnp.pad(ukv[..., :NOPE_DIM], ((0, 0), (0, 0), (0, 0), (0, LANES - NOPE_DIM)))
    wkn = wkn.reshape(depth, KV_LORA, QK_PAD).astype(BF16)
    wv = ukv[..., NOPE_DIM:].reshape(depth, KV_LORA, W_A).astype(BF16)
    wuk = jnp.pad(ukv[..., :NOPE_DIM].transpose(0, 2, 3, 1), ((0, 0), (0, 0), (0, LANES - NOPE_DIM), (0, 0)))
    wuv = ukv[..., NOPE_DIM:].transpose(0, 2, 3, 1)
    return (wa, wg, wuq, wkn, wv, w_oa.astype(BF16), w_ob.astype(BF16), w_out.astype(BF16), wuk.astype(BF16),
            wuv.astype(BF16))


def _rope_tables(pos):
    half = ROPE_DIM // 2
    inv = ROPE_THETA ** (-jnp.arange(half, dtype=F32) / half)
    ang = pos.astype(F32)[:, None] * inv
    cos, sin = jnp.cos(ang), jnp.sin(ang)
    n = pos.shape[0]
    z = lambda w: jnp.zeros((n, w), F32)
    pad = LANES - NOPE_DIM - ROPE_DIM
    q_c = jnp.concatenate([jnp.ones((n, NOPE_DIM), F32), cos, cos, z(pad)], axis=-1) * (MLA_SCALE * LOG2E)
    q_s1 = jnp.concatenate([z(NOPE_DIM), -sin, z(half), z(pad)], axis=-1) * (MLA_SCALE * LOG2E)
    q_s2 = jnp.concatenate([z(NOPE_DIM), z(half), sin, z(pad)], axis=-1) * (MLA_SCALE * LOG2E)
    gap = NOPE_DIM - ROPE_DIM
    k_c = jnp.concatenate([cos, cos, z(gap), cos, cos, z(pad)], axis=-1)
    k_s1 = jnp.concatenate([-sin, z(half), z(gap), -sin, z(half), z(pad)], axis=-1)
    k_s2 = jnp.concatenate([z(half), sin, z(gap), z(half), sin, z(pad)], axis=-1)
    return (q_c, q_s1, q_s2), (k_c, k_s1, k_s2)


def _rope_selector():
    p = np.zeros((LANES, ROPE_DIM), np.float32)
    p[NOPE_DIM + np.arange(ROPE_DIM), np.arange(ROPE_DIM)] = 1.0
    return jnp.asarray(p, dtype=BF16)


def kernel(x_prompt, x_sample, c_prompt, c_sample, cache_ckv, cache_krope, cache_kb, cache_vb, cache_kidx, w_ada,
           b_ada, g_norm, w_in, g_qnorm, w_uq, g_kvnorm, w_ukv, w_oa, w_ob, w_out, rel_bias, g_final):
    depth = w_in.shape[0]
    bp, t_p, _ = x_prompt.shape
    bs, t_s, _ = x_sample.shape
    past = cache_ckv.shape[2]
    topk_p = min(TOPK_MAX, t_p // 4)
    topk_s = min(TOPK_MAX, (past + t_s) // 4)
    assert t_p % TM == 0 and TM % TK == 0 and past % TK == 0 and t_s <= TK and t_s % 8 == 0
    assert (DSA_GROUP * t_s) % LANES == 0 and (DSA_GROUP * t_s) & (DSA_GROUP * t_s - 1) == 0
    n_s = bs * t_s

    wa, wg, wuq, wkn, wv, woa, wob, wout, wuk, wuv = _pack_weights(w_in, w_uq, w_ukv, w_oa, w_ob, w_out)
    rope_sel = _rope_selector()
    cache_kb2 = cache_kb.reshape(depth, bs, past, KV_DIM)
    cache_vb2 = cache_vb.reshape(depth, bs, past, KV_DIM)
    mod = _modulation(jnp.concatenate([c_prompt, c_sample], axis=0), w_ada, b_ada)
    mod = mod.reshape(depth, bp + bs, 3, 1, D_MODEL)
    tabs_p = _rope_tables(jnp.arange(t_p, dtype=jnp.int32))
    tabs_s = _rope_tables(jnp.tile(past + jnp.arange(t_s, dtype=jnp.int32), bs))
    bias_p = _bias_tiles(rel_bias, TK, TK)
    bias_s = _bias_tiles(rel_bias, TK, t_s).reshape(DSA_KV_HEADS, DSA_GROUP, 2, TK, t_s)
    bias_s = bias_s.transpose(0, 2, 3, 1, 4).reshape(DSA_KV_HEADS, 2, TK, DSA_GROUP * t_s)
    row = lambda v: v.reshape(1, -1)

    xp = x_prompt
    xs = x_sample.reshape(1, n_s, D_MODEL)
    caches_p = caches_s = None
    for l in range(depth):
        final = l == depth - 1
        gn, gq, gkv, gf = row(g_norm[l]), row(g_qnorm[l]), row(g_kvnorm[l]), row(g_final)

        shift, scale, gate = (mod[l, :bp, n] for n in range(3))
        *caches_p, q, qb, qi, wit, kcat, vat, kb16, vbt, ki16 = _proj(
            xp, shift, scale, gn, wa[l], gq, wuq[l], gkv, wkn[l], wv[l], *tabs_p, caches_p, layer=l, depth=depth,
            tm=TM, tk=TK, emit_kv=True)
        oa, ob = _attend(qi, wit, ki16, qb, kb16, vbt, bias_p, q, kcat, vat, t=TK, topk=topk_p)
        xp = _out(xp, shift, scale, gate, gn, oa, ob, wg[l], woa[l], wob[l], wout[l], gf, tm=TM, final=final)

        shift, scale, gate = (jnp.repeat(mod[l, bp:, n, 0], t_s, axis=0)[None] for n in range(3))
        *caches_s, q, qb, qi, wit = _proj(
            xs, shift, scale, gn, wa[l], gq, wuq[l], gkv, wkn[l], wv[l], *tabs_s, caches_s, layer=l, depth=depth,
            tm=n_s, tk=TK, emit_kv=False)
        per_seq = lambda a: a.reshape((bs, t_s) + a.shape[2:])
        ckv, kr, kb, vb, ki = (per_seq(a[l]) for a in caches_s)
        q, qb, qi = per_seq(q), per_seq(qb), per_seq(qi)
        new_tile = lambda a: jnp.pad(a, ((0, 0), (0, TK - t_s), (0, 0)))
        oa = _mla_cached(q, cache_ckv, cache_krope, new_tile(ckv), new_tile(kr), wuk[l], wuv[l], rope_sel, layer=l,
                         tk=TK, n_valid_last=t_s)
        oa = oa.reshape(bs, V_DIM, MLA_HEADS, t_s).transpose(0, 3, 2, 1).reshape(1, n_s, W_A)
        qi_l = qi.reshape(bs, t_s, IDX_HEADS, IDX_DIM).transpose(0, 2, 1, 3).reshape(bs, IDX_HEADS * t_s, IDX_DIM)
        w_l = wit.reshape(IDX_HEADS, bs, t_s).transpose(1, 0, 2).reshape(bs, 1, IDX_HEADS * t_s)
        qb_l = qb.reshape(bs, t_s, DSA_KV_HEADS, DSA_GROUP, DSA_HEAD_DIM).transpose(0, 2, 3, 1, 4)
        qb_l = qb_l.reshape(bs, DSA_KV_HEADS, DSA_GROUP * t_s, DSA_HEAD_DIM)
        qb_l = jnp.stack([jnp.pad(qb_l[:, g], ((0, 0), (0, 0), (g * DSA_HEAD_DIM, KV_DIM - (g + 1) * DSA_HEAD_DIM)))
                          for g in range(DSA_KV_HEADS)], axis=1)
        ob = _dsa_cached(qi_l, w_l, cache_kidx, new_tile(ki), qb_l, cache_kb2, new_tile(kb), cache_vb2, new_tile(vb),
                         bias_s, layer=l, ts=t_s, tk=TK, n_valid_last=t_s, topk=topk_s)
        ob = ob.reshape(bs, DSA_KV_HEADS, DSA_HEAD_DIM, DSA_GROUP, t_s).transpose(0, 4, 1, 3, 2).reshape(1, n_s, W_B)
        xs = _out(xs, shift, scale, gate, gn, oa, ob, wg[l], woa[l], wob[l], wout[l], gf, tm=n_s, final=final)

    def cache_outputs(stacks, nb, t):
        ckv, kr, kb, vb, ki = (a.reshape((depth, nb, t, a.shape[-1])) for a in stacks)
        kvh = (depth, nb, t, DSA_KV_HEADS, DSA_HEAD_DIM)
        return ckv, kr, kb.reshape(kvh), vb.reshape(kvh), ki

    return (xp, xs.reshape(bs, t_s, D_MODEL)) + cache_outputs(caches_p, bp, t_p) + cache_outputs(caches_s, bs, t_s)
```

```python
import functools

import numpy as np
import jax
import jax.numpy as jnp
from jax import lax
from jax.experimental import pallas as pl
from jax.experimental.pallas import tpu as pltpu

F32 = jnp.float32
BF16 = jnp.bfloat16

D_MODEL = 1024
CHUNK = 64
EPS = 1e-6
MLA_HEADS = 8
NOPE_DIM = 64
ROPE_DIM = 32
V_DIM = 64
Q_LORA = 384
KV_LORA = 256
ROPE_THETA = 10000.0
MLA_SCALE = (NOPE_DIM + ROPE_DIM) ** -0.5
DSA_HEADS = 8
DSA_KV_HEADS = 2
DSA_GROUP = DSA_HEADS // DSA_KV_HEADS
DSA_HEAD_DIM = 64
DSA_SCALE = DSA_HEAD_DIM ** -0.5
IDX_HEADS = 8
IDX_DIM = 64
IDX_W_SCALE = (IDX_HEADS ** -0.5) * (IDX_DIM ** -0.5)
TOPK_MAX = 256
NUM_BUCKETS = 32
MAX_DISTANCE = 128
W_A = MLA_HEADS * V_DIM
W_B = DSA_HEADS * DSA_HEAD_DIM
SPLIT_SIZES = (Q_LORA, KV_LORA, ROPE_DIM, W_A, W_B, DSA_KV_HEADS * DSA_HEAD_DIM, DSA_KV_HEADS * DSA_HEAD_DIM,
               IDX_HEADS * IDX_DIM, IDX_DIM, IDX_HEADS, W_B, D_MODEL, D_MODEL)

LANES = 128
SUBLANES = 8
VMEM_LIMIT_BYTES = 56 * 1024 * 1024

TK = 256
TM = 512
NEG = -1e30
LOG2E = 1.4426950408889634
MAX_BISECT = 200
BISECT_PER_CHECK = 4

QK_PAD = MLA_HEADS * LANES
KV_DIM = DSA_KV_HEADS * DSA_HEAD_DIM

A_CQ = (0, Q_LORA)
A_CKV = (A_CQ[1], A_CQ[1] + KV_LORA)
A_KR = (A_CKV[1], A_CKV[1] + LANES)
A_SM = (A_KR[1], A_KR[1] + LANES)
A_QB = (A_SM[1], A_SM[1] + W_B)
A_KB = (A_QB[1], A_QB[1] + KV_DIM)
A_VB = (A_KB[1], A_KB[1] + KV_DIM)
A_QI = (A_VB[1], A_VB[1] + IDX_HEADS * IDX_DIM)
A_WIDTH = A_QI[1]
G_GA = (0, W_A)
G_GB = (G_GA[1], G_GA[1] + W_B)
G_MA = (G_GB[1], G_GB[1] + D_MODEL)
G_MB = (G_MA[1], G_MA[1] + D_MODEL)
G_WIDTH = G_MB[1]


def _params(n_grid):
    return pltpu.CompilerParams(dimension_semantics=("arbitrary",) * n_grid, vmem_limit_bytes=VMEM_LIMIT_BYTES)


def _sigmoid(x):
    return 1.0 / (1.0 + jnp.exp(-x))


def _rms(x, g):
    return x * lax.rsqrt(jnp.mean(x * x, axis=-1, keepdims=True) + EPS) * g


def _dot(a, b):
    return jnp.dot(a, b, preferred_element_type=F32)


def _dot_nt(a, b):
    return lax.dot_general(a, b, (((1,), (1,)), ((), ())), preferred_element_type=F32)


def _key_rows(j, tk):
    start = j * tk if isinstance(j, int) else pl.multiple_of(j * tk, tk)
    return pl.ds(start, tk)


def _rows8(x, op, ways=None):
    parts = [x[r:r + SUBLANES] for r in range(0, x.shape[0], SUBLANES)]
    if ways is not None:
        chains = parts[:ways]
        for n, part in enumerate(parts[ways:]):
            chains[n % ways] = op(chains[n % ways], part)
        parts = chains
    while len(parts) > 1:
        parts = [op(parts[n], parts[n + 1]) for n in range(0, len(parts) - 1, 2)] + parts[len(parts) & ~1:]
    return parts[0]


def _mod_kernel(c_ref, w_ref, b_ref, o_ref):
    c = c_ref[...]
    a = c * _sigmoid(c)
    w = w_ref[0]
    a_hi = a.astype(BF16)
    a_lo = (a - a_hi.astype(F32)).astype(BF16)
    w_hi = w.astype(BF16)
    w_lo = (w - w_hi.astype(F32)).astype(BF16)
    o_ref[0] = _dot(a_hi, w_hi) + (_dot(a_hi, w_lo) + _dot(a_lo, w_hi)) + b_ref[0]


def _modulation(c_all, w_ada, b_ada):
    depth = w_ada.shape[0]
    n = c_all.shape[0]
    tn = D_MODEL
    return pl.pallas_call(
        _mod_kernel,
        out_shape=jax.ShapeDtypeStruct((depth, n, 3 * D_MODEL), F32),
        grid=(depth, 3 * D_MODEL // tn),
        in_specs=[
            pl.BlockSpec((n, D_MODEL), lambda l, j: (0, 0)),
            pl.BlockSpec((1, D_MODEL, tn), lambda l, j: (l, 0, j)),
            pl.BlockSpec((1, 1, tn), lambda l, j: (l, 0, j)),
        ],
        out_specs=pl.BlockSpec((1, n, tn), lambda l, j: (l, 0, j)),
        compiler_params=_params(2),
        name="adaln_mod",
    )(c_all, w_ada, b_ada.reshape(depth, 1, 3 * D_MODEL))


def _rel_bucket(rel):
    half = NUM_BUCKETS // 2
    max_exact = half // 2
    n = np.abs(rel)
    large = max_exact + (np.log(np.maximum(n, 1).astype(np.float32) / max_exact)
                         / np.float32(np.log(MAX_DISTANCE / max_exact)) * (half - max_exact)).astype(np.int32)
    large = np.minimum(large, half - 1)
    return np.where(n < max_exact, n, large) + np.where(rel > 0, half, 0)


def _bias_kernel(bucket_ref, rel_ref, o_ref, *, far_bucket):
    h = pl.program_id(0)
    for t in range(2):
        b = bucket_ref[t]
        acc = jnp.zeros(b.shape, F32)
        for i in range(NUM_BUCKETS):
            acc = jnp.where(b == i, rel_ref[i, h], acc)
        o_ref[0, t] = (acc - rel_ref[far_bucket, h]) * LOG2E


def _bias_tiles(rel_bias, tk, tq):
    r = np.arange(tk)[:, None]
    c = np.arange(tq)[None, :]
    far_bucket = int(_rel_bucket(np.array(-tk)))
    assert np.all(_rel_bucket(-tk - np.arange(1, tk + tq)) == far_bucket)
    buckets = jnp.asarray(np.stack([_rel_bucket(r - tk - c), _rel_bucket(r - c)]).astype(np.int32))
    return pl.pallas_call(
        functools.partial(_bias_kernel, far_bucket=far_bucket),
        out_shape=jax.ShapeDtypeStruct((DSA_HEADS, 2, tk, tq), F32),
        grid=(DSA_HEADS,),
        in_specs=[
            pl.BlockSpec((2, tk, tq), lambda h: (0, 0, 0)),
            pl.BlockSpec(memory_space=pltpu.SMEM),
        ],
        out_specs=pl.BlockSpec((1, 2, tk, tq), lambda h: (h, 0, 0, 0)),
        compiler_params=_params(1),
        name="t5_bias_tiles",
    )(buckets, rel_bias)


def _rope_group(g, c_ref, s1_ref, s2_ref):
    return (g * c_ref[...] + pltpu.roll(g, LANES - ROPE_DIM // 2, 1) * s1_ref[...]
            + pltpu.roll(g, ROPE_DIM // 2, 1) * s2_ref[...])


def _proj_kernel(x_ref, shift_ref, scale_ref, gn_ref, wa_ref, gq_ref, wuq_ref, gkv_ref, wkn_ref, wv_ref,
                 qc_ref, qs1_ref, qs2_ref, kc_ref, ks1_ref, ks2_ref, *rest, emit_kv, tk, n_aliased):
    ckv_o, kr_o, kb_o, vb_o, ki_o, q_o, qb_o, qi_o, wit_o, *kv_outs = rest[n_aliased:]

    def put(cache_o, value):
        for n in range(cache_o.shape[0]):
            cache_o[n, 0] = value

    x = x_ref[0]
    h = _rms(x, gn_ref[...]) * (1.0 + scale_ref[0]) + shift_ref[0]
    h16 = h.astype(BF16)

    def zcols(a):
        return _dot(h16, wa_ref[:, a[0]:a[1]])

    cqn = _rms(zcols(A_CQ), gq_ref[...]).astype(BF16)
    q = _dot(cqn, wuq_ref[...])
    for hd in range(MLA_HEADS):
        sl = slice(hd * LANES, (hd + 1) * LANES)
        q_hd = _rope_group(q[:, sl], qc_ref, qs1_ref, qs2_ref)
        if emit_kv:
            q_o[0, sl, :] = q_hd.T.astype(BF16)
        else:
            q_o[0, :, sl] = q_hd.astype(BF16)

    ckvn = _rms(zcols(A_CKV), gkv_ref[...])
    put(ckv_o, ckvn)
    krg = _rope_group(zcols(A_KR), kc_ref, ks1_ref, ks2_ref)
    put(kr_o, krg[:, 0:ROPE_DIM])

    sm = zcols(A_SM)
    ki = sm[:, 0:IDX_DIM]
    put(ki_o, ki)
    wit_o[0] = sm.T[IDX_DIM:IDX_DIM + IDX_HEADS, :] * IDX_W_SCALE
    qb = zcols(A_QB) * (DSA_SCALE * LOG2E)
    qi = zcols(A_QI)
    qb_o[0] = (qb.T if emit_kv else qb).astype(BF16)
    qi_o[0] = (qi.T if emit_kv else qi).astype(BF16)
    kb = zcols(A_KB)
    vb = zcols(A_VB)
    put(kb_o, kb)
    put(vb_o, vb)

    if emit_kv:
        kcat_o, vat_o, kb16_o, vbt_o, ki16_o = kv_outs
        ckv16 = ckvn.astype(BF16)
        kn = _dot(ckv16, wkn_ref[...])
        lane = lax.broadcasted_iota(jnp.int32, krg.shape, 1)
        k_add = jnp.where(lane >= NOPE_DIM, krg, 0.0)
        for hd in range(MLA_HEADS):
            sl = slice(hd * LANES, (hd + 1) * LANES)
            kcat_o[0, :, sl] = (kn[:, sl] + k_add).astype(BF16)
        va = _dot(ckv16, wv_ref[...])
        for g in range(DSA_KV_HEADS):
            kb16_o[0, g] = kb[:, g * DSA_HEAD_DIM:(g + 1) * DSA_HEAD_DIM].astype(BF16)
        ki16_o[0] = ki.astype(BF16)
        for c in range(x.shape[0] // tk):
            rows = slice(c * tk, (c + 1) * tk)
            vat_o[0, c] = va[rows, :].T.astype(BF16)
            vbt_o[0, c] = vb[rows, :].T.astype(BF16)


def _proj(x, shift, scale, gn, wa, gq, wuq, gkv, wkn, wv, q_tabs, k_tabs, caches, *, layer, depth, tm, tk, emit_kv):
    nb, t, _ = x.shape
    r = shift.shape[1]
    rb = 1 if r == 1 else tm
    grid = (nb, t // tm)
    tok = lambda w: pl.BlockSpec((1, tm, w), lambda b, i: (b, i, 0))
    if caches is None:
        stacked = lambda w: pl.BlockSpec((depth, 1, tm, w), lambda b, i: (0, b, i, 0))
    else:
        stacked = lambda w: pl.BlockSpec((1, 1, tm, w), lambda b, i: (layer, b, i, 0))
    mod = pl.BlockSpec((1, rb, D_MODEL), (lambda b, i: (b, 0, 0)) if r == 1 else (lambda b, i: (b, i, 0)))
    const = lambda a: pl.BlockSpec(a.shape, lambda b, i: (0,) * a.ndim)
    tab = pl.BlockSpec((tm, LANES), lambda b, i: (i, 0))
    in_specs = [tok(D_MODEL), mod, mod, const(gn), const(wa), const(gq), const(wuq), const(gkv), const(wkn),
                const(wv)] + [tab] * 6
    operands = [x, shift, scale, gn, wa, gq, wuq, gkv, wkn, wv, *q_tabs, *k_tabs]
    cache_widths = (KV_LORA, ROPE_DIM, KV_DIM, KV_DIM, IDX_DIM)
    aliases = {}
    if caches is not None:
        aliases = {len(operands) + n: n for n in range(len(cache_widths))}
        in_specs += [pl.BlockSpec(memory_space=pl.ANY)] * len(cache_widths)
        operands += list(caches)
    tok_t = lambda w: pl.BlockSpec((1, w, tm), lambda b, i: (b, 0, i))
    q_shape = (lambda w: (nb, w, t)) if emit_kv else (lambda w: (nb, t, w))
    q_spec = tok_t if emit_kv else tok
    q_widths = (QK_PAD, W_B, IDX_HEADS * IDX_DIM)
    out_shape = ([jax.ShapeDtypeStruct((depth, nb, t, w), F32) for w in cache_widths]
                 + [jax.ShapeDtypeStruct(q_shape(w), BF16) for w in q_widths]
                 + [jax.ShapeDtypeStruct((nb, IDX_HEADS, t), F32)])
    out_specs = [stacked(w) for w in cache_widths] + [q_spec(w) for w in q_widths] + [tok_t(IDX_HEADS)]
    if emit_kv:
        nkt = tm // tk
        out_shape += [
            jax.ShapeDtypeStruct((nb, t, QK_PAD), BF16),
            jax.ShapeDtypeStruct((nb, t // tk, W_A, tk), BF16),
            jax.ShapeDtypeStruct((nb, DSA_KV_HEADS, t, DSA_HEAD_DIM), BF16),
            jax.ShapeDtypeStruct((nb, t // tk, KV_DIM, tk), BF16),
            jax.ShapeDtypeStruct((nb, t, IDX_DIM), BF16),
        ]
        out_specs += [tok(QK_PAD), pl.BlockSpec((1, nkt, W_A, tk), lambda b, i: (b, i, 0, 0)),
                      pl.BlockSpec((1, DSA_KV_HEADS, tm, DSA_HEAD_DIM), lambda b, i: (b, 0, i, 0)),
                      pl.BlockSpec((1, nkt, KV_DIM, tk), lambda b, i: (b, i, 0, 0)), tok(IDX_DIM)]
    return pl.pallas_call(
        functools.partial(_proj_kernel, emit_kv=emit_kv, tk=tk, n_aliased=len(aliases)),
        out_shape=out_shape, grid=grid, in_specs=in_specs, out_specs=out_specs, input_output_aliases=aliases,
        compiler_params=_params(2), name="proj_kv" if emit_kv else "proj_q",
    )(*operands)


class _Softmax:
    ONES_ROWS = 16

    def __init__(self, s_scr, mx_scr, m_scr, acc_scr):
        self.s, self.mx, self.m, self.acc = s_scr, mx_scr, m_scr, acc_scr
        self.n_heads = acc_scr.shape[0]
        self.dv = acc_scr.shape[1] - self.ONES_ROWS

    @classmethod
    def scratch(cls, n_heads, dv, tk, tq):
        stat = pltpu.VMEM((n_heads, SUBLANES, tq), F32)
        return [pltpu.VMEM((n_heads, 2, tk, tq), F32), stat, stat, pltpu.VMEM((n_heads, dv + cls.ONES_ROWS, tq), F32)]

    def reset(self):
        self.mx[...] = jnp.full(self.mx.shape, NEG, F32)
        self.m[...] = jnp.full(self.m.shape, NEG, F32)
        self.acc[...] = jnp.zeros(self.acc.shape, F32)

    def produce(self, hd, slot, s):
        self.s[hd, slot] = s
        self.mx[hd] = jnp.maximum(self.mx[hd], _rows8(s, jnp.maximum, ways=4))

    def consume(self, hd, slot, vt):
        m_old = self.m[hd, 0:1, :]
        m_new = jnp.max(self.mx[hd], axis=0, keepdims=True)
        alpha = jnp.exp2(m_old - m_new)
        p = jnp.exp2(self.s[hd, slot] - m_new)
        self.m[hd] = jnp.broadcast_to(m_new, self.m.shape[1:])
        vt1 = jnp.concatenate([vt, jnp.ones((self.ONES_ROWS, vt.shape[1]), vt.dtype)], axis=0)
        self.acc[hd] = alpha * self.acc[hd] + _dot(vt1, p.astype(BF16))

    def run(self, last, logits, values):
        def step(t_consume, t_produce, kind, slot_produce):
            for hd in range(self.n_heads):
                if t_consume is not None:
                    self.consume(hd, 1 - slot_produce, values(hd, t_consume))
                if t_produce is not None:
                    self.produce(hd, slot_produce, logits(hd, t_produce, kind))

        when = lambda cond, fn: pl.when(cond)(fn)
        self.reset()
        step(None, last, 2, 0)
        when(last >= 1, lambda: step(last, last - 1, 1, 1))
        n_far = last - 1

        def far_pair(n, _):
            t = last - 2 - 2 * n
            step(t + 1, t, 0, 0)
            step(t, t - 1, 0, 1)
            return 0

        lax.fori_loop(0, n_far // 2, far_pair, 0)
        odd = (n_far >= 1) & (n_far % 2 == 1)
        when(odd, lambda: step(1, 0, 0, 0))
        in_slot0 = (last == 0) | odd
        when(in_slot0, lambda: step(0, None, None, 1))
        when(jnp.logical_not(in_slot0), lambda: step(0, None, None, 0))

    def run_staggered(self, last, logits, values, lead):
        a, b = range(lead), range(lead, self.n_heads)

        def act(heads, t_consume, t_produce, kind, slot_produce):
            for hd in heads:
                if t_consume is not None:
                    self.consume(hd, 1 - slot_produce, values(hd, t_consume))
                if t_produce is not None:
                    self.produce(hd, slot_produce, logits(hd, t_produce, kind))

        when = lambda cond, fn: pl.when(cond)(fn)
        self.reset()
        act(a, None, last, 2, 0)
        act(a, last, last - 1, 1, 1)
        act(b, None, last, 2, 0)
        act(a, last - 1, last - 2, 0, 0)
        act(b, last, last - 1, 1, 1)

        def pair(n, _):
            k = 3 + 2 * n
            act(a, last - k + 1, last - k, 0, 1)
            act(b, last - k + 2, last - k + 1, 0, 0)
            act(a, last - k, last - k - 1, 0, 0)
            act(b, last - k + 1, last - k, 0, 1)
            return 0

        lax.fori_loop(0, (last - 2) // 2, pair, 0)

        def odd_tail():
            act(a, 1, 0, 0, 1)
            act(b, 2, 1, 0, 0)
            act(a, 0, None, None, 0)
            act(b, 1, 0, 0, 1)
            act(b, 0, None, None, 0)

        def even_tail():
            act(a, 0, None, None, 1)
            act(b, 1, 0, 0, 0)
            act(b, 0, None, None, 1)

        when(last % 2 == 1, odd_tail)
        when(last % 2 == 0, even_tail)

    def finish(self, *o_refs):
        per_out = self.n_heads // len(o_refs)
        for n, o_ref in enumerate(o_refs):
            o_t = jnp.concatenate([self.acc[hd, 0:self.dv, :] / self.acc[hd, self.dv:self.dv + 1, :]
                                   for hd in range(n * per_out, (n + 1) * per_out)], axis=0)
            o_ref[0] = o_t.T


def _diagonal_tile_visible(t):
    r = lax.broadcasted_iota(jnp.int32, (t, t), 0)
    c = lax.broadcasted_iota(jnp.int32, (t, t), 1)
    return (r // CHUNK) <= (c // CHUNK)


def _score_stats_init(tq):
    return (jnp.full((1, tq), jnp.inf, F32), jnp.full((1, tq), -jnp.inf, F32),
            jnp.zeros((SUBLANES, tq), F32), jnp.zeros((SUBLANES, tq), F32))


def _score_stats(stats, t, visible=None):
    lo, hi, c_z, c_zp = stats
    t_min = t if visible is None else jnp.where(visible, t, jnp.inf)
    return (jnp.minimum(lo, jnp.min(t_min, axis=0, keepdims=True)), jnp.maximum(hi, jnp.max(t, axis=0, keepdims=True)),
            c_z + _rows8(jnp.where(t >= 0.0, 1.0, 0.0), jnp.add), c_zp + _rows8(jnp.where(t > 0.0, 1.0, 0.0), jnp.add))


def _select_topk(idx_scr, n_used, stats, n_vis, kf):
    _, tk, tq = idx_scr.shape

    def count(pred):
        def body(j, c):
            return c + _rows8(jnp.where(pred(idx_scr[j]), 1.0, 0.0), jnp.add)
        return jnp.sum(lax.fori_loop(0, n_used, body, jnp.zeros((SUBLANES, tq), F32)), axis=0, keepdims=True)

    lo, hi, c_z, c_zp = stats
    c_z = jnp.sum(c_z, axis=0, keepdims=True)
    c_zp = jnp.sum(c_zp, axis=0, keepdims=True)
    hi = hi + (jnp.abs(hi) * 2.0 ** -20 + 1e-30)
    zero_up = (c_z >= kf) & (lo < 0.0)
    zero_tie = jnp.where((c_z >= kf) & (c_zp < kf), 1.0, 0.0)
    c_lo = jnp.where(zero_up, c_z, n_vis)
    lo = jnp.where(zero_up, 0.0, lo)
    hi = jnp.where((c_z < kf) & (hi > 0.0), 0.0, hi)

    def settled(lo, hi, c_lo):
        mid = 0.5 * lo + 0.5 * hi
        return jnp.where((c_lo <= kf) | (mid <= lo) | (mid >= hi), 1.0, 0.0)

    def bis_cond(st):
        _, _, _, done, it = st
        return jnp.logical_and(it < MAX_BISECT, jnp.min(done) < 0.5)

    def bis_body(st):
        lo, hi, c_lo, done, it = st
        for _ in range(BISECT_PER_CHECK):
            mid = 0.5 * lo + 0.5 * hi
            c = count(lambda t: t >= mid)
            live = done < 0.5
            up = live & (c >= kf)
            dn = live & (c < kf)
            lo = jnp.where(up, mid, lo)
            c_lo = jnp.where(up, c, c_lo)
            hi = jnp.where(dn, mid, hi)
            done = jnp.maximum(done, settled(lo, hi, c_lo))
        return lo, hi, c_lo, done, it + BISECT_PER_CHECK

    done = jnp.maximum(settled(lo, hi, c_lo), zero_tie)
    lo, hi, c_lo, _, _ = lax.while_loop(bis_cond, bis_body, (lo, hi, c_lo, done, jnp.int32(0)))
    any_tie = jnp.max(jnp.where(c_lo > kf, 1.0, 0.0)) > 0.5

    @pl.when(jnp.logical_not(any_tie))
    def _():
        def body(j, _):
            idx_scr[j] = jnp.where(idx_scr[j] >= lo, 0.0, NEG)
            return 0
        lax.fori_loop(0, n_used, body, 0)

    @pl.when(any_tie)
    def _():
        need = kf - count(lambda t: t > lo)
        r = lax.broadcasted_iota(jnp.int32, (tk, tk), 0)
        c = lax.broadcasted_iota(jnp.int32, (tk, tk), 1)
        before = jnp.where(c < r, 1.0, 0.0).astype(BF16)

        def body(j, seen):
            t = idx_scr[j]
            eq = jnp.where(t == lo, 1.0, 0.0)
            rank = _dot(before, eq.astype(BF16)) + seen
            take = (t > lo) | ((t == lo) & (rank < need))
            idx_scr[j] = jnp.where(take, 0.0, NEG)
            return seen + jnp.sum(_rows8(eq, jnp.add), axis=0, keepdims=True)
        lax.fori_loop(0, n_used, body, jnp.zeros((1, tq), F32))


def _attend_kernel(qi_ref, wit_ref, ki_ref, qb_ref, kb_ref, vbt_ref, bias_ref, q_ref, k_ref, vt_ref, oa_ref, ob_ref,
                   idx_scr, *scratch, t, topk):
    tq = tk = t
    last = pl.program_id(1)
    n_used = last + 1
    visible = _diagonal_tile_visible(t)

    wit = wit_ref[0]

    def idx_tile(j):
        k_j = ki_ref[0, _key_rows(j, tk), :]
        acc = jnp.zeros((tk, tq), F32)
        for hd in range(IDX_HEADS):
            s = _dot(k_j, qi_ref[0, hd * IDX_DIM:(hd + 1) * IDX_DIM, :])
            acc = acc + jnp.maximum(s, 0.0) * wit[hd:hd + 1, :]
        return acc

    def idx_body(j, stats):
        t = idx_tile(j)
        idx_scr[j] = t
        return _score_stats(stats, t)

    def idx_pair(n, stats):
        return idx_body(2 * n + 1, idx_body(2 * n, stats))

    stats = lax.fori_loop(0, last // 2, idx_pair, _score_stats_init(tq))
    stats = lax.cond(last % 2 == 1, lambda st: idx_body(last - 1, st), lambda st: st, stats)
    t_last = jnp.where(visible, idx_tile(last), -jnp.inf)
    idx_scr[last] = t_last
    stats = _score_stats(stats, t_last, visible)
    lane = lax.broadcasted_iota(jnp.int32, (1, tq), 1)
    n_vis = last * tk + (lane // CHUNK + 1) * CHUNK
    _select_topk(idx_scr, n_used, stats, n_vis.astype(F32), float(topk))

    softmax = _Softmax(*scratch)
    kv_lanes = lambda hd: slice((hd // DSA_GROUP) * DSA_HEAD_DIM, (hd // DSA_GROUP + 1) * DSA_HEAD_DIM)

    def logits(hd, j, kind):
        if hd < MLA_HEADS:
            sl = slice(hd * LANES, (hd + 1) * LANES)
            s = _dot(k_ref[0, _key_rows(j, tk), sl], q_ref[0, sl, :])
            return s + jnp.where(visible, 0.0, NEG) if kind == 2 else s
        hd -= MLA_HEADS
        s = _dot(kb_ref[0, hd // DSA_GROUP, _key_rows(j, tk), :], qb_ref[0, hd * DSA_HEAD_DIM:(hd + 1) * DSA_HEAD_DIM, :])
        return s + idx_scr[j] if kind == 0 else s + (idx_scr[j] + bias_ref[hd, kind - 1])

    def values(hd, j):
        if hd < MLA_HEADS:
            return vt_ref[0, j, hd * V_DIM:(hd + 1) * V_DIM, :]
        return vbt_ref[0, j, kv_lanes(hd - MLA_HEADS), :]

    pl.when(last < 2)(lambda: softmax.run(last, logits, values))
    pl.when(last >= 2)(lambda: softmax.run_staggered(last, logits, values, MLA_HEADS))
    softmax.finish(oa_ref, ob_ref)


def _attend(qi, wit, ki, qb, kb, vbt, bias, q, k, vt, *, t, topk):
    assert V_DIM == DSA_HEAD_DIM and MLA_HEADS == DSA_HEADS
    nb, _, seq = qb.shape
    n_tiles = seq // t
    per_q = lambda w: pl.BlockSpec((1, w, t), lambda b, i: (b, 0, i))
    per_b = lambda a: pl.BlockSpec((1,) + a.shape[1:], lambda b, i: (b,) + (0,) * (a.ndim - 1))
    out = lambda w: pl.BlockSpec((1, t, w), lambda b, i: (b, i, 0))
    return pl.pallas_call(
        functools.partial(_attend_kernel, t=t, topk=topk),
        out_shape=[jax.ShapeDtypeStruct((nb, seq, W_A), F32), jax.ShapeDtypeStruct((nb, seq, W_B), F32)],
        grid=(nb, n_tiles),
        in_specs=[per_q(IDX_HEADS * IDX_DIM), per_q(IDX_HEADS), per_b(ki), per_q(W_B), per_b(kb), per_b(vbt),
                  pl.BlockSpec(bias.shape, lambda b, i: (0, 0, 0, 0)), per_q(QK_PAD), per_b(k), per_b(vt)],
        out_specs=[out(W_A), out(W_B)],
        scratch_shapes=([pltpu.VMEM((n_tiles, t, t), F32)]
                        + _Softmax.scratch(MLA_HEADS + DSA_HEADS, DSA_HEAD_DIM, t, t)),
        compiler_params=_params(2), name="attend_causal",
    )(qi, wit, ki, qb, kb, vbt, bias, q, k, vt)


def _two_pass_softmax(n_tiles, logits, values, s_scr, dv):
    mx = None
    for t in range(n_tiles):
        s = logits(t)
        s_scr[t] = s
        part = _rows8(s, jnp.maximum, ways=4)
        mx = part if mx is None else jnp.maximum(mx, part)
    m = jnp.max(mx, axis=0, keepdims=True)
    acc = None
    for t in range(n_tiles):
        p = jnp.exp2(s_scr[t] - m).astype(BF16)
        vt = values(t)
        part = _dot(jnp.concatenate([vt, jnp.ones((_Softmax.ONES_ROWS, vt.shape[1]), BF16)], axis=0), p)
        acc = part if acc is None else acc + part
    return acc[0:dv] / acc[dv:dv + 1]


def _mla_cached_kernel(q_ref, ckv_c_ref, kr_c_ref, ckv_n_ref, kr_n_ref, wuk_ref, wuv_ref, sel_ref, o_ref, s_scr, *,
                       ts, tk, n_cache, n_valid_last):
    lanes = MLA_HEADS * ts
    q_lat, q_rope = [], []
    for hd in range(MLA_HEADS):
        qg = q_ref[0, :, hd * LANES:(hd + 1) * LANES]
        q_lat.append(_dot(qg, wuk_ref[hd]).astype(BF16))
        q_rope.append(_dot(qg, sel_ref[...]).astype(BF16))
    q_lat = jnp.concatenate(q_lat, axis=0)
    q_rope = jnp.concatenate(q_rope, axis=0)

    def latent(t):
        return ckv_c_ref[0, t * tk:(t + 1) * tk, :] if t < n_cache else ckv_n_ref[0]

    def logits(t):
        kr = kr_c_ref[0, t * tk:(t + 1) * tk, :] if t < n_cache else kr_n_ref[0]
        s = _dot_nt(latent(t).astype(BF16), q_lat) + _dot_nt(kr.astype(BF16), q_rope)
        if t == n_cache:
            s = s + jnp.where(lax.broadcasted_iota(jnp.int32, s.shape, 0) < n_valid_last, 0.0, NEG)
        return s

    o_lat = _two_pass_softmax(n_cache + 1, logits, lambda t: latent(t).T.astype(BF16), s_scr, KV_LORA)
    o_lat = o_lat.astype(BF16)
    head_of_lane = lax.broadcasted_iota(jnp.int32, (V_DIM, lanes), 1) // ts
    out = jnp.zeros((V_DIM, lanes), F32)
    for hd in range(MLA_HEADS):
        out = jnp.where(head_of_lane == hd, _dot(wuv_ref[hd], o_lat), out)
    o_ref[0] = out


def _mla_cached(q, ckv_cache, kr_cache, ckv_new, kr_new, wuk, wuv, sel, *, layer, tk, n_valid_last):
    nb, ts, _ = q.shape
    past = ckv_cache.shape[2]
    lanes = MLA_HEADS * ts
    per_b = lambda a: pl.BlockSpec((1,) + a.shape[1:], lambda b: (b,) + (0,) * (a.ndim - 1))
    const = lambda a: pl.BlockSpec(a.shape, lambda b: (0,) * a.ndim)
    return pl.pallas_call(
        functools.partial(_mla_cached_kernel, ts=ts, tk=tk, n_cache=past // tk, n_valid_last=n_valid_last),
        out_shape=jax.ShapeDtypeStruct((nb, V_DIM, lanes), F32), grid=(nb,),
        in_specs=[per_b(q), _layer_rows(ckv_cache, layer), _layer_rows(kr_cache, layer), per_b(ckv_new),
                  per_b(kr_new), const(wuk), const(wuv), const(sel)],
        out_specs=pl.BlockSpec((1, V_DIM, lanes), lambda b: (b, 0, 0)),
        scratch_shapes=[pltpu.VMEM((past // tk + 1, tk, lanes), F32)],
        compiler_params=_params(1), name="mla_cached",
    )(q, ckv_cache, kr_cache, ckv_new, kr_new, wuk, wuv, sel)


def _dsa_cached_kernel(qi_ref, w_ref, ki_c_ref, ki_n_ref, qb_ref, kb_c_ref, kb_n_ref, vb_c_ref, vb_n_ref, bias_ref,
                       o_ref, idx_scr, s_scr, *, ts, tk, n_cache, n_valid_last, topk):
    n_tiles = n_cache + 1
    lanes = DSA_GROUP * ts
    tile = lambda c_ref, n_ref, t: c_ref[0, t * tk:(t + 1) * tk, :] if t < n_cache else n_ref[0]
    visible = lax.broadcasted_iota(jnp.int32, (tk, lanes), 0) < n_valid_last

    stats = _score_stats_init(lanes)
    for t in range(n_tiles):
        s = _dot_nt(tile(ki_c_ref, ki_n_ref, t).astype(BF16), qi_ref[0])
        r = jnp.maximum(s, 0.0) * w_ref[0]
        x = r[:, 0:lanes]
        for blk in range(1, IDX_HEADS * ts // lanes):
            x = x + r[:, blk * lanes:(blk + 1) * lanes]
        shift = lanes // 2
        while shift >= ts:
            x = x + pltpu.roll(x, shift, 1)
            shift //= 2
        if t == n_cache:
            x = jnp.where(visible, x, -jnp.inf)
        idx_scr[t] = x
        stats = _score_stats(stats, x, visible if t == n_cache else None)
    n_vis = jnp.full((1, lanes), float(n_cache * tk + n_valid_last), F32)
    _select_topk(idx_scr, n_tiles, stats, n_vis, float(topk))

    for g in range(DSA_KV_HEADS):
        def logits(t, g=g):
            s = _dot_nt(tile(kb_c_ref, kb_n_ref, t).astype(BF16), qb_ref[0, g]) + idx_scr[t]
            return s + bias_ref[g, t - (n_tiles - 2)] if t >= n_tiles - 2 else s

        def values(t, g=g):
            return tile(vb_c_ref, vb_n_ref, t).T[g * DSA_HEAD_DIM:(g + 1) * DSA_HEAD_DIM, :].astype(BF16)

        o_ref[0, g] = _two_pass_softmax(n_tiles, logits, values, s_scr, DSA_HEAD_DIM)


def _layer_rows(cache, layer):
    return pl.BlockSpec((None, 1) + cache.shape[2:], lambda b: (layer, b, 0, 0))


def _dsa_cached(qi, w, ki_cache, ki_new, qb, kb_cache, kb_new, vb_cache, vb_new, bias, *, layer, ts, tk,
                n_valid_last, topk):
    nb = qi.shape[0]
    past = ki_cache.shape[2]
    n_tiles = past // tk + 1
    lanes = DSA_GROUP * ts
    per_b = lambda a: pl.BlockSpec((1,) + a.shape[1:], lambda b: (b,) + (0,) * (a.ndim - 1))
    cached = lambda a: _layer_rows(a, layer)
    return pl.pallas_call(
        functools.partial(_dsa_cached_kernel, ts=ts, tk=tk, n_cache=past // tk, n_valid_last=n_valid_last, topk=topk),
        out_shape=jax.ShapeDtypeStruct((nb, DSA_KV_HEADS, DSA_HEAD_DIM, lanes), F32), grid=(nb,),
        in_specs=[per_b(qi), per_b(w), cached(ki_cache), per_b(ki_new), per_b(qb), cached(kb_cache), per_b(kb_new),
                  cached(vb_cache), per_b(vb_new), pl.BlockSpec(bias.shape, lambda b: (0,) * bias.ndim)],
        out_specs=pl.BlockSpec((1, DSA_KV_HEADS, DSA_HEAD_DIM, lanes), lambda b: (b, 0, 0, 0)),
        scratch_shapes=[pltpu.VMEM((n_tiles, tk, lanes), F32), pltpu.VMEM((n_tiles, tk, lanes), F32)],
        compiler_params=_params(1), name="dsa_cached",
    )(qi, w, ki_cache, ki_new, qb, kb_cache, kb_new, vb_cache, vb_new, bias)


def _out_kernel(x_ref, shift_ref, scale_ref, gate_ref, gn_ref, oa_ref, ob_ref, wg_ref, woa_ref, wob_ref, wout_ref,
                gf_ref, o_ref, *, final):
    x = x_ref[0]
    h16 = (_rms(x, gn_ref[...]) * (1.0 + scale_ref[0]) + shift_ref[0]).astype(BF16)

    def gcols(a):
        return _dot(h16, wg_ref[:, a[0]:a[1]])

    sa = _sigmoid(gcols(G_MA))
    sb = _sigmoid(gcols(G_MB))
    ga = gcols(G_GA)
    gb = gcols(G_GB)
    ya = _dot((oa_ref[0] * (ga * _sigmoid(ga))).astype(BF16), woa_ref[...])
    yb = _dot((ob_ref[0] * (gb * _sigmoid(gb))).astype(BF16), wob_ref[...])
    m = sa * ya + sb * yb
    y = x + gate_ref[0] * _dot(m.astype(BF16), wout_ref[...])
    o_ref[0] = _rms(y, gf_ref[...]) if final else y


def _out(x, shift, scale, gate, gn, oa, ob, wg, woa, wob, wout, gf, *, tm, final):
    nb, t, _ = x.shape
    r = shift.shape[1]
    rb = 1 if r == 1 else tm
    tok = lambda w: pl.BlockSpec((1, tm, w), lambda b, i: (b, i, 0))
    mod = pl.BlockSpec((1, rb, D_MODEL), (lambda b, i: (b, 0, 0)) if r == 1 else (lambda b, i: (b, i, 0)))
    const = lambda a: pl.BlockSpec(a.shape, lambda b, i: (0,) * a.ndim)
    return pl.pallas_call(
        functools.partial(_out_kernel, final=final),
        out_shape=jax.ShapeDtypeStruct(x.shape, F32), grid=(nb, t // tm),
        in_specs=[tok(D_MODEL), mod, mod, mod, const(gn), tok(W_A), tok(W_B), const(wg), const(woa), const(wob),
                  const(wout), const(gf)],
        out_specs=tok(D_MODEL), compiler_params=_params(2), name="out_final" if final else "out_mix",
    )(x, shift, scale, gate, gn, oa, ob, wg, woa, wob, wout, gf)


def _pack_weights(w_in, w_uq, w_ukv, w_oa, w_ob, w_out):
    depth = w_in.shape[0]
    offs = np.concatenate([[0], np.cumsum(SPLIT_SIZES)])
    cq, ckv, kr, ga, qb, kb, vb, qi, ki, wi, gb, ma, mb = [w_in[:, :, offs[n]:offs[n + 1]] for n in range(13)]
    z = lambda w: jnp.zeros((depth, D_MODEL, w), F32)
    krg = jnp.concatenate([kr, z(NOPE_DIM - ROPE_DIM), kr, z(LANES - NOPE_DIM - ROPE_DIM)], axis=-1)
    sm = jnp.concatenate([ki, wi, z(LANES - IDX_DIM - IDX_HEADS)], axis=-1)
    wa = jnp.concatenate([cq, ckv, krg, sm, qb, kb, vb, qi], axis=-1).astype(BF16)
    wg = jnp.concatenate([ga, gb, ma, mb], axis=-1).astype(BF16)
    qd = NOPE_DIM + ROPE_DIM
    wuq = jnp.pad(w_uq.reshape(depth, Q_LORA, MLA_HEADS, qd), ((0, 0), (0, 0), (0, 0), (0, LANES - qd)))
    wuq = wuq.reshape(depth, Q_LORA, QK_PAD).astype(BF16)
    ukv = w_ukv.reshape(depth, KV_LORA, MLA_HEADS, NOPE_DIM + V_DIM)
    wkn = jnp.pad(ukv[..., :NOPE_DIM], ((0, 0), (0, 0), (0, 0), (0, LANES - NOPE_DIM)))
    wkn = wkn.reshape(depth, KV_LORA, QK_PAD).astype(BF16)
    wv = ukv[..., NOPE_DIM:].reshape(depth, KV_LORA, W_A).astype(BF16)
    wuk = jnp.pad(ukv[..., :NOPE_DIM].transpose(0, 2, 3, 1), ((0, 0), (0, 0), (0, LANES - NOPE_DIM), (0, 0)))
    wuv = ukv[..., NOPE_DIM:].transpose(0, 2, 3, 1)
    return (wa, wg, wuq, wkn, wv, w_oa.astype(BF16), w_ob.astype(BF16), w_out.astype(BF16), wuk.astype(BF16),
            wuv.astype(BF16))


def _rope_tables(pos):
    half = ROPE_DIM // 2
    inv = ROPE_THETA ** (-jnp.arange(half, dtype=F32) / half)
    ang = pos.astype(F32)[:, None] * inv
    cos, sin = jnp.cos(ang), jnp.sin(ang)
    n = pos.shape[0]
    z = lambda w: jnp.zeros((n, w), F32)
    pad = LANES - NOPE_DIM - ROPE_DIM
    q_c = jnp.concatenate([jnp.ones((n, NOPE_DIM), F32), cos, cos, z(pad)], axis=-1) * (MLA_SCALE * LOG2E)
    q_s1 = jnp.concatenate([z(NOPE_DIM), -sin, z(half), z(pad)], axis=-1) * (MLA_SCALE * LOG2E)
    q_s2 = jnp.concatenate([z(NOPE_DIM), z(half), sin, z(pad)], axis=-1) * (MLA_SCALE * LOG2E)
    gap = NOPE_DIM - ROPE_DIM
    k_c = jnp.concatenate([cos, cos, z(gap), cos, cos, z(pad)], axis=-1)
    k_s1 = jnp.concatenate([-sin, z(half), z(gap), -sin, z(half), z(pad)], axis=-1)
    k_s2 = jnp.concatenate([z(half), sin, z(gap), z(half), sin, z(pad)], axis=-1)
    return (q_c, q_s1, q_s2), (k_c, k_s1, k_s2)


def _rope_selector():
    p = np.zeros((LANES, ROPE_DIM), np.float32)
    p[NOPE_DIM + np.arange(ROPE_DIM), np.arange(ROPE_DIM)] = 1.0
    return jnp.asarray(p, dtype=BF16)


def kernel(x_prompt, x_sample, c_prompt, c_sample, cache_ckv, cache_krope, cache_kb, cache_vb, cache_kidx, w_ada,
           b_ada, g_norm, w_in, g_qnorm, w_uq, g_kvnorm, w_ukv, w_oa, w_ob, w_out, rel_bias, g_final):
    depth = w_in.shape[0]
    bp, t_p, _ = x_prompt.shape
    bs, t_s, _ = x_sample.shape
    past = cache_ckv.shape[2]
    topk_p = min(TOPK_MAX, t_p // 4)
    topk_s = min(TOPK_MAX, (past + t_s) // 4)
    assert t_p % TM == 0 and TM % TK == 0 and past % TK == 0 and t_s <= TK and t_s % 8 == 0
    assert (DSA_GROUP * t_s) % LANES == 0 and (DSA_GROUP * t_s) & (DSA_GROUP * t_s - 1) == 0
    n_s = bs * t_s

    wa, wg, wuq, wkn, wv, woa, wob, wout, wuk, wuv = _pack_weights(w_in, w_uq, w_ukv, w_oa, w_ob, w_out)
    rope_sel = _rope_selector()
    cache_kb2 = cache_kb.reshape(depth, bs, past, KV_DIM)
    cache_vb2 = cache_vb.reshape(depth, bs, past, KV_DIM)
    mod = _modulation(jnp.concatenate([c_prompt, c_sample], axis=0), w_ada, b_ada)
    mod = mod.reshape(depth, bp + bs, 3, 1, D_MODEL)
    tabs_p = _rope_tables(jnp.arange(t_p, dtype=jnp.int32))
    tabs_s = _rope_tables(jnp.tile(past + jnp.arange(t_s, dtype=jnp.int32), bs))
    bias_p = _bias_tiles(rel_bias, TK, TK)
    bias_s = _bias_tiles(rel_bias, TK, t_s).reshape(DSA_KV_HEADS, DSA_GROUP, 2, TK, t_s)
    bias_s = bias_s.transpose(0, 2, 3, 1, 4).reshape(DSA_KV_HEADS, 2, TK, DSA_GROUP * t_s)
    row = lambda v: v.reshape(1, -1)

    xp = x_prompt
    xs = x_sample.reshape(1, n_s, D_MODEL)
    caches_p = caches_s = None
    for l in range(depth):
        final = l == depth - 1
        gn, gq, gkv, gf = row(g_norm[l]), row(g_qnorm[l]), row(g_kvnorm[l]), row(g_final)

        shift, scale, gate = (mod[l, :bp, n] for n in range(3))
        *caches_p, q, qb, qi, wit, kcat, vat, kb16, vbt, ki16 = _proj(
            xp, shift, scale, gn, wa[l], gq, wuq[l], gkv, wkn[l], wv[l], *tabs_p, caches_p, layer=l, depth=depth,
            tm=TM, tk=TK, emit_kv=True)
        oa, ob = _attend(qi, wit, ki16, qb, kb16, vbt, bias_p, q, kcat, vat, t=TK, topk=topk_p)
        xp = _out(xp, shift, scale, gate, gn, oa, ob, wg[l], woa[l], wob[l], wout[l], gf, tm=TM, final=final)

        shift, scale, gate = (jnp.repeat(mod[l, bp:, n, 0], t_s, axis=0)[None] for n in range(3))
        *caches_s, q, qb, qi, wit = _proj(
            xs, shift, scale, gn, wa[l], gq, wuq[l], gkv, wkn[l], wv[l], *tabs_s, caches_s, layer=l, depth=depth,
            tm=n_s, tk=TK, emit_kv=False)
        per_seq = lambda a: a.reshape((bs, t_s) + a.shape[2:])
        ckv, kr, kb, vb, ki = (per_seq(a[l]) for a in caches_s)
        q, qb, qi = per_seq(q), per_seq(qb), per_seq(qi)
        new_tile = lambda a: jnp.pad(a, ((0, 0), (0, TK - t_s), (0, 0)))
        oa = _mla_cached(q, cache_ckv, cache_krope, new_tile(ckv), new_tile(kr), wuk[l], wuv[l], rope_sel, layer=l,
                         tk=TK, n_valid_last=t_s)
        oa = oa.reshape(bs, V_DIM, MLA_HEADS, t_s).transpose(0, 3, 2, 1).reshape(1, n_s, W_A)
        qi_l = qi.reshape(bs, t_s, IDX_HEADS, IDX_DIM).transpose(0, 2, 1, 3).reshape(bs, IDX_HEADS * t_s, IDX_DIM)
        w_l = wit.reshape(IDX_HEADS, bs, t_s).transpose(1, 0, 2).reshape(bs, 1, IDX_HEADS * t_s)
        qb_l = qb.reshape(bs, t_s, DSA_KV_HEADS, DSA_GROUP, DSA_HEAD_DIM).transpose(0, 2, 3, 1, 4)
        qb_l = qb_l.reshape(bs, DSA_KV_HEADS, DSA_GROUP * t_s, DSA_HEAD_DIM)
        qb_l = jnp.stack([jnp.pad(qb_l[:, g], ((0, 0), (0, 0), (g * DSA_HEAD_DIM, KV_DIM - (g + 1) * DSA_HEAD_DIM)))
                          for g in range(DSA_KV_HEADS)], axis=1)
        ob = _dsa_cached(qi_l, w_l, cache_kidx, new_tile(ki), qb_l, cache_kb2, new_tile(kb), cache_vb2, new_tile(vb),
                         bias_s, layer=l, ts=t_s, tk=TK, n_valid_last=t_s, topk=topk_s)
        ob = ob.reshape(bs, DSA_KV_HEADS, DSA_HEAD_DIM, DSA_GROUP, t_s).transpose(0, 4, 1, 3, 2).reshape(1, n_s, W_B)
        xs = _out(xs, shift, scale, gate, gn, oa, ob, wg[l], woa[l], wob[l], wout[l], gf, tm=n_s, final=final)

    def cache_outputs(stacks, nb, t):
        ckv, kr, kb, vb, ki = (a.reshape((depth, nb, t, a.shape[-1])) for a in stacks)
        kvh = (depth, nb, t, DSA_KV_HEADS, DSA_HEAD_DIM)
        return ckv, kr, kb.reshape(kvh), vb.reshape(kvh), ki

    return (xp, xs.reshape(bs, t_s, D_MODEL)) + cache_outputs(caches_p, bp, t_p) + cache_outputs(caches_s, bs, t_s)
```

```python
import functools

import numpy as np
import jax
import jax.numpy as jnp
from jax import lax
from jax.experimental import pallas as pl
from jax.experimental.pallas import tpu as pltpu

F32 = jnp.float32
BF16 = jnp.bfloat16

D_MODEL = 1024
CHUNK = 64
EPS = 1e-6
MLA_HEADS = 8
NOPE_DIM = 64
ROPE_DIM = 32
V_DIM = 64
Q_LORA = 384
KV_LORA = 256
ROPE_THETA = 10000.0
MLA_SCALE = (NOPE_DIM + ROPE_DIM) ** -0.5
DSA_HEADS = 8
DSA_KV_HEADS = 2
DSA_GROUP = DSA_HEADS // DSA_KV_HEADS
DSA_HEAD_DIM = 64
DSA_SCALE = DSA_HEAD_DIM ** -0.5
IDX_HEADS = 8
IDX_DIM = 64
IDX_W_SCALE = (IDX_HEADS ** -0.5) * (IDX_DIM ** -0.5)
TOPK_MAX = 256
NUM_BUCKETS = 32
MAX_DISTANCE = 128
W_A = MLA_HEADS * V_DIM
W_B = DSA_HEADS * DSA_HEAD_DIM
SPLIT_SIZES = (Q_LORA, KV_LORA, ROPE_DIM, W_A, W_B, DSA_KV_HEADS * DSA_HEAD_DIM, DSA_KV_HEADS * DSA_HEAD_DIM,
               IDX_HEADS * IDX_DIM, IDX_DIM, IDX_HEADS, W_B, D_MODEL, D_MODEL)

LANES = 128
SUBLANES = 8
VMEM_LIMIT_BYTES = 56 * 1024 * 1024

TK = 256
TM = 512
NEG = -1e30
LOG2E = 1.4426950408889634
MAX_BISECT = 200
BISECT_PER_CHECK = 4

QK_PAD = MLA_HEADS * LANES
KV_DIM = DSA_KV_HEADS * DSA_HEAD_DIM

A_CQ = (0, Q_LORA)
A_CKV = (A_CQ[1], A_CQ[1] + KV_LORA)
A_KR = (A_CKV[1], A_CKV[1] + LANES)
A_SM = (A_KR[1], A_KR[1] + LANES)
A_QB = (A_SM[1], A_SM[1] + W_B)
A_KB = (A_QB[1], A_QB[1] + KV_DIM)
A_VB = (A_KB[1], A_KB[1] + KV_DIM)
A_QI = (A_VB[1], A_VB[1] + IDX_HEADS * IDX_DIM)
A_WIDTH = A_QI[1]
G_GA = (0, W_A)
G_GB = (G_GA[1], G_GA[1] + W_B)
G_MA = (G_GB[1], G_GB[1] + D_MODEL)
G_MB = (G_MA[1], G_MA[1] + D_MODEL)
G_WIDTH = G_MB[1]


def _params(n_grid):
    return pltpu.CompilerParams(dimension_semantics=("arbitrary",) * n_grid, vmem_limit_bytes=VMEM_LIMIT_BYTES)


def _sigmoid(x):
    return 1.0 / (1.0 + jnp.exp(-x))


def _rms(x, g):
    return x * lax.rsqrt(jnp.mean(x * x, axis=-1, keepdims=True) + EPS) * g


def _dot(a, b):
    return jnp.dot(a, b, preferred_element_type=F32)


def _dot_nt(a, b):
    return lax.dot_general(a, b, (((1,), (1,)), ((), ())), preferred_element_type=F32)


def _key_rows(j, tk):
    start = j * tk if isinstance(j, int) else pl.multiple_of(j * tk, tk)
    return pl.ds(start, tk)


def _rows8(x, op, ways=None):
    parts = [x[r:r + SUBLANES] for r in range(0, x.shape[0], SUBLANES)]
    if ways is not None:
        chains = parts[:ways]
        for n, part in enumerate(parts[ways:]):
            chains[n % ways] = op(chains[n % ways], part)
        parts = chains
    while len(parts) > 1:
        parts = [op(parts[n], parts[n + 1]) for n in range(0, len(parts) - 1, 2)] + parts[len(parts) & ~1:]
    return parts[0]


def _mod_kernel(c_ref, w_ref, b_ref, o_ref):
    c = c_ref[...]
    a = c * _sigmoid(c)
    w = w_ref[0]
    a_hi = a.astype(BF16)
    a_lo = (a - a_hi.astype(F32)).astype(BF16)
    w_hi = w.astype(BF16)
    w_lo = (w - w_hi.astype(F32)).astype(BF16)
    o_ref[0] = _dot(a_hi, w_hi) + (_dot(a_hi, w_lo) + _dot(a_lo, w_hi)) + b_ref[0]


def _modulation(c_all, w_ada, b_ada):
    depth = w_ada.shape[0]
    n = c_all.shape[0]
    tn = D_MODEL
    return pl.pallas_call(
        _mod_kernel,
        out_shape=jax.ShapeDtypeStruct((depth, n, 3 * D_MODEL), F32),
        grid=(depth, 3 * D_MODEL // tn),
        in_specs=[
            pl.BlockSpec((n, D_MODEL), lambda l, j: (0, 0)),
            pl.BlockSpec((1, D_MODEL, tn), lambda l, j: (l, 0, j)),
            pl.BlockSpec((1, 1, tn), lambda l, j: (l, 0, j)),
        ],
        out_specs=pl.BlockSpec((1, n, tn), lambda l, j: (l, 0, j)),
        compiler_params=_params(2),
        name="adaln_mod",
    )(c_all, w_ada, b_ada.reshape(depth, 1, 3 * D_MODEL))


def _rel_bucket(rel):
    half = NUM_BUCKETS // 2
    max_exact = half // 2
    n = np.abs(rel)
    large = max_exact + (np.log(np.maximum(n, 1).astype(np.float32) / max_exact)
                         / np.float32(np.log(MAX_DISTANCE / max_exact)) * (half - max_exact)).astype(np.int32)
    large = np.minimum(large, half - 1)
    return np.where(n < max_exact, n, large) + np.where(rel > 0, half, 0)


def _bias_kernel(bucket_ref, rel_ref, o_ref, *, far_bucket):
    h = pl.program_id(0)
    for t in range(2):
        b = bucket_ref[t]
        acc = jnp.zeros(b.shape, F32)
        for i in range(NUM_BUCKETS):
            acc = jnp.where(b == i, rel_ref[i, h], acc)
        o_ref[0, t] = (acc - rel_ref[far_bucket, h]) * LOG2E


def _bias_tiles(rel_bias, tk, tq):
    r = np.arange(tk)[:, None]
    c = np.arange(tq)[None, :]
    far_bucket = int(_rel_bucket(np.array(-tk)))
    assert np.all(_rel_bucket(-tk - np.arange(1, tk + tq)) == far_bucket)
    buckets = jnp.asarray(np.stack([_rel_bucket(r - tk - c), _rel_bucket(r - c)]).astype(np.int32))
    return pl.pallas_call(
        functools.partial(_bias_kernel, far_bucket=far_bucket),
        out_shape=jax.ShapeDtypeStruct((DSA_HEADS, 2, tk, tq), F32),
        grid=(DSA_HEADS,),
        in_specs=[
            pl.BlockSpec((2, tk, tq), lambda h: (0, 0, 0)),
            pl.BlockSpec(memory_space=pltpu.SMEM),
        ],
        out_specs=pl.BlockSpec((1, 2, tk, tq), lambda h: (h, 0, 0, 0)),
        compiler_params=_params(1),
        name="t5_bias_tiles",
    )(buckets, rel_bias)


def _rope_group(g, c_ref, s1_ref, s2_ref):
    return (g * c_ref[...] + pltpu.roll(g, LANES - ROPE_DIM // 2, 1) * s1_ref[...]
            + pltpu.roll(g, ROPE_DIM // 2, 1) * s2_ref[...])


def _proj_kernel(x_ref, shift_ref, scale_ref, gn_ref, wa_ref, gq_ref, wuq_ref, gkv_ref, wkn_ref, wv_ref,
                 qc_ref, qs1_ref, qs2_ref, kc_ref, ks1_ref, ks2_ref, *rest, emit_kv, tk, n_aliased):
    ckv_o, kr_o, kb_o, vb_o, ki_o, q_o, qb_o, qi_o, wit_o, *kv_outs = rest[n_aliased:]

    def put(cache_o, value):
        for n in range(cache_o.shape[0]):
            cache_o[n, 0] = value

    x = x_ref[0]
    h = _rms(x, gn_ref[...]) * (1.0 + scale_ref[0]) + shift_ref[0]
    h16 = h.astype(BF16)

    def zcols(a):
        return _dot(h16, wa_ref[:, a[0]:a[1]])

    cqn = _rms(zcols(A_CQ), gq_ref[...]).astype(BF16)
    q = _dot(cqn, wuq_ref[...])
    for hd in range(MLA_HEADS):
        sl = slice(hd * LANES, (hd + 1) * LANES)
        q_hd = _rope_group(q[:, sl], qc_ref, qs1_ref, qs2_ref)
        if emit_kv:
            q_o[0, sl, :] = q_hd.T.astype(BF16)
        else:
            q_o[0, :, sl] = q_hd.astype(BF16)

    ckvn = _rms(zcols(A_CKV), gkv_ref[...])
    put(ckv_o, ckvn)
    krg = _rope_group(zcols(A_KR), kc_ref, ks1_ref, ks2_ref)
    put(kr_o, krg[:, 0:ROPE_DIM])

    sm = zcols(A_SM)
    ki = sm[:, 0:IDX_DIM]
    put(ki_o, ki)
    wit_o[0] = sm.T[IDX_DIM:IDX_DIM + IDX_HEADS, :] * IDX_W_SCALE
    qb = zcols(A_QB) * (DSA_SCALE * LOG2E)
    qi = zcols(A_QI)
    qb_o[0] = (qb.T if emit_kv else qb).astype(BF16)
    qi_o[0] = (qi.T if emit_kv else qi).astype(BF16)
    kb = zcols(A_KB)
    vb = zcols(A_VB)
    put(kb_o, kb)
    put(vb_o, vb)

    if emit_kv:
        kcat_o, vat_o, kb16_o, vbt_o, ki16_o = kv_outs
        ckv16 = ckvn.astype(BF16)
        kn = _dot(ckv16, wkn_ref[...])
        lane = lax.broadcasted_iota(jnp.int32, krg.shape, 1)
        k_add = jnp.where(lane >= NOPE_DIM, krg, 0.0)
        for hd in range(MLA_HEADS):
            sl = slice(hd * LANES, (hd + 1) * LANES)
            kcat_o[0, :, sl] = (kn[:, sl] + k_add).astype(BF16)
        va = _dot(ckv16, wv_ref[...])
        for g in range(DSA_KV_HEADS):
            kb16_o[0, g] = kb[:, g * DSA_HEAD_DIM:(g + 1) * DSA_HEAD_DIM].astype(BF16)
        ki16_o[0] = ki.astype(BF16)
        for c in range(x.shape[0] // tk):
            rows = slice(c * tk, (c + 1) * tk)
            vat_o[0, c] = va[rows, :].T.astype(BF16)
            vbt_o[0, c] = vb[rows, :].T.astype(BF16)


def _proj(x, shift, scale, gn, wa, gq, wuq, gkv, wkn, wv, q_tabs, k_tabs, caches, *, layer, depth, tm, tk, emit_kv):
    nb, t, _ = x.shape
    r = shift.shape[1]
    rb = 1 if r == 1 else tm
    grid = (nb, t // tm)
    tok = lambda w: pl.BlockSpec((1, tm, w), lambda b, i: (b, i, 0))
    if caches is None:
        stacked = lambda w: pl.BlockSpec((depth, 1, tm, w), lambda b, i: (0, b, i, 0))
    else:
        stacked = lambda w: pl.BlockSpec((1, 1, tm, w), lambda b, i: (layer, b, i, 0))
    mod = pl.BlockSpec((1, rb, D_MODEL), (lambda b, i: (b, 0, 0)) if r == 1 else (lambda b, i: (b, i, 0)))
    const = lambda a: pl.BlockSpec(a.shape, lambda b, i: (0,) * a.ndim, pipeline_mode=pl.Buffered(1))
    tab = pl.BlockSpec((tm, LANES), lambda b, i: (i, 0))
    in_specs = [tok(D_MODEL), mod, mod, const(gn), const(wa), const(gq), const(wuq), const(gkv), const(wkn),
                const(wv)] + [tab] * 6
    operands = [x, shift, scale, gn, wa, gq, wuq, gkv, wkn, wv, *q_tabs, *k_tabs]
    cache_widths = (KV_LORA, ROPE_DIM, KV_DIM, KV_DIM, IDX_DIM)
    aliases = {}
    if caches is not None:
        aliases = {len(operands) + n: n for n in range(len(cache_widths))}
        in_specs += [pl.BlockSpec(memory_space=pl.ANY)] * len(cache_widths)
        operands += list(caches)
    tok_t = lambda w: pl.BlockSpec((1, w, tm), lambda b, i: (b, 0, i))
    q_shape = (lambda w: (nb, w, t)) if emit_kv else (lambda w: (nb, t, w))
    q_spec = tok_t if emit_kv else tok
    q_widths = (QK_PAD, W_B, IDX_HEADS * IDX_DIM)
    out_shape = ([jax.ShapeDtypeStruct((depth, nb, t, w), F32) for w in cache_widths]
                 + [jax.ShapeDtypeStruct(q_shape(w), BF16) for w in q_widths]
                 + [jax.ShapeDtypeStruct((nb, IDX_HEADS, t), F32)])
    out_specs = [stacked(w) for w in cache_widths] + [q_spec(w) for w in q_widths] + [tok_t(IDX_HEADS)]
    if emit_kv:
        nkt = tm // tk
        out_shape += [
            jax.ShapeDtypeStruct((nb, t, QK_PAD), BF16),
            jax.ShapeDtypeStruct((nb, t // tk, W_A, tk), BF16),
            jax.ShapeDtypeStruct((nb, DSA_KV_HEADS, t, DSA_HEAD_DIM), BF16),
            jax.ShapeDtypeStruct((nb, t // tk, KV_DIM, tk), BF16),
            jax.ShapeDtypeStruct((nb, t, IDX_DIM), BF16),
        ]
        out_specs += [tok(QK_PAD), pl.BlockSpec((1, nkt, W_A, tk), lambda b, i: (b, i, 0, 0)),
                      pl.BlockSpec((1, DSA_KV_HEADS, tm, DSA_HEAD_DIM), lambda b, i: (b, 0, i, 0)),
                      pl.BlockSpec((1, nkt, KV_DIM, tk), lambda b, i: (b, i, 0, 0)), tok(IDX_DIM)]
    return pl.pallas_call(
        functools.partial(_proj_kernel, emit_kv=emit_kv, tk=tk, n_aliased=len(aliases)),
        out_shape=out_shape, grid=grid, in_specs=in_specs, out_specs=out_specs, input_output_aliases=aliases,
        compiler_params=_params(2), name="proj_kv" if emit_kv else "proj_q",
    )(*operands)


class _Softmax:
    ONES_ROWS = 16

    def __init__(self, s_scr, mx_scr, m_scr, acc_scr):
        self.s, self.mx, self.m, self.acc = s_scr, mx_scr, m_scr, acc_scr
        self.n_heads = acc_scr.shape[0]
        self.dv = acc_scr.shape[1] - self.ONES_ROWS

    @classmethod
    def scratch(cls, n_heads, dv, tk, tq):
        stat = pltpu.VMEM((n_heads, SUBLANES, tq), F32)
        return [pltpu.VMEM((n_heads, 2, tk, tq), F32), stat, stat, pltpu.VMEM((n_heads, dv + cls.ONES_ROWS, tq), F32)]

    def reset(self):
        self.mx[...] = jnp.full(self.mx.shape, NEG, F32)
        self.m[...] = jnp.full(self.m.shape, NEG, F32)
        self.acc[...] = jnp.zeros(self.acc.shape, F32)

    def produce(self, hd, slot, s):
        self.s[hd, slot] = s
        self.mx[hd] = jnp.maximum(self.mx[hd], _rows8(s, jnp.maximum, ways=4))

    def consume(self, hd, slot, vt):
        m_old = self.m[hd, 0:1, :]
        m_new = jnp.max(self.mx[hd], axis=0, keepdims=True)
        alpha = jnp.exp2(m_old - m_new)
        p = jnp.exp2(self.s[hd, slot] - m_new)
        self.m[hd] = jnp.broadcast_to(m_new, self.m.shape[1:])
        vt1 = jnp.concatenate([vt, jnp.ones((self.ONES_ROWS, vt.shape[1]), vt.dtype)], axis=0)
        self.acc[hd] = alpha * self.acc[hd] + _dot(vt1, p.astype(BF16))

    def run(self, last, logits, values):
        def step(t_consume, t_produce, kind, slot_produce):
            for hd in range(self.n_heads):
                if t_consume is not None:
                    self.consume(hd, 1 - slot_produce, values(hd, t_consume))
                if t_produce is not None:
                    self.produce(hd, slot_produce, logits(hd, t_produce, kind))

        when = lambda cond, fn: pl.when(cond)(fn)
        self.reset()
        step(None, last, 2, 0)
        when(last >= 1, lambda: step(last, last - 1, 1, 1))
        n_far = last - 1

        def far_pair(n, _):
            t = last - 2 - 2 * n
            step(t + 1, t, 0, 0)
            step(t, t - 1, 0, 1)
            return 0

        lax.fori_loop(0, n_far // 2, far_pair, 0)
        odd = (n_far >= 1) & (n_far % 2 == 1)
        when(odd, lambda: step(1, 0, 0, 0))
        in_slot0 = (last == 0) | odd
        when(in_slot0, lambda: step(0, None, None, 1))
        when(jnp.logical_not(in_slot0), lambda: step(0, None, None, 0))

    def run_staggered(self, last, logits, values, lead):
        a, b = range(lead), range(lead, self.n_heads)

        def act(heads, t_consume, t_produce, kind, slot_produce):
            for hd in heads:
                if t_consume is not None:
                    self.consume(hd, 1 - slot_produce, values(hd, t_consume))
                if t_produce is not None:
                    self.produce(hd, slot_produce, logits(hd, t_produce, kind))

        when = lambda cond, fn: pl.when(cond)(fn)
        self.reset()
        act(a, None, last, 2, 0)
        act(a, last, last - 1, 1, 1)
        act(b, None, last, 2, 0)
        act(a, last - 1, last - 2, 0, 0)
        act(b, last, last - 1, 1, 1)

        def pair(n, _):
            k = 3 + 2 * n
            act(a, last - k + 1, last - k, 0, 1)
            act(b, last - k + 2, last - k + 1, 0, 0)
            act(a, last - k, last - k - 1, 0, 0)
            act(b, last - k + 1, last - k, 0, 1)
            return 0

        lax.fori_loop(0, (last - 2) // 2, pair, 0)

        def odd_tail():
            act(a, 1, 0, 0, 1)
            act(b, 2, 1, 0, 0)
            act(a, 0, None, None, 0)
            act(b, 1, 0, 0, 1)
            act(b, 0, None, None, 0)

        def even_tail():
            act(a, 0, None, None, 1)
            act(b, 1, 0, 0, 0)
            act(b, 0, None, None, 1)

        when(last % 2 == 1, odd_tail)
        when(last % 2 == 0, even_tail)

    def finish(self, *o_refs):
        per_out = self.n_heads // len(o_refs)
        for n, o_ref in enumerate(o_refs):
            o_t = jnp.concatenate([self.acc[hd, 0:self.dv, :] / self.acc[hd, self.dv:self.dv + 1, :]
                                   for hd in range(n * per_out, (n + 1) * per_out)], axis=0)
            o_ref[0] = o_t.T


def _diagonal_tile_visible(t):
    r = lax.broadcasted_iota(jnp.int32, (t, t), 0)
    c = lax.broadcasted_iota(jnp.int32, (t, t), 1)
    return (r // CHUNK) <= (c // CHUNK)


def _score_stats_init(tq):
    return (jnp.full((1, tq), jnp.inf, F32), jnp.full((1, tq), -jnp.inf, F32),
            jnp.zeros((SUBLANES, tq), F32), jnp.zeros((SUBLANES, tq), F32))


def _score_stats(stats, t, visible=None):
    lo, hi, c_z, c_zp = stats
    t_min = t if visible is None else jnp.where(visible, t, jnp.inf)
    return (jnp.minimum(lo, jnp.min(t_min, axis=0, keepdims=True)), jnp.maximum(hi, jnp.max(t, axis=0, keepdims=True)),
            c_z + _rows8(jnp.where(t >= 0.0, 1.0, 0.0), jnp.add), c_zp + _rows8(jnp.where(t > 0.0, 1.0, 0.0), jnp.add))


def _select_topk(idx_scr, n_used, stats, n_vis, kf):
    _, tk, tq = idx_scr.shape

    def count(pred):
        def body(j, c):
            return c + _rows8(jnp.where(pred(idx_scr[j]), 1.0, 0.0), jnp.add)
        return jnp.sum(lax.fori_loop(0, n_used, body, jnp.zeros((SUBLANES, tq), F32)), axis=0, keepdims=True)

    lo, hi, c_z, c_zp = stats
    c_z = jnp.sum(c_z, axis=0, keepdims=True)
    c_zp = jnp.sum(c_zp, axis=0, keepdims=True)
    hi = hi + (jnp.abs(hi) * 2.0 ** -20 + 1e-30)
    zero_up = (c_z >= kf) & (lo < 0.0)
    zero_tie = jnp.where((c_z >= kf) & (c_zp < kf), 1.0, 0.0)
    c_lo = jnp.where(zero_up, c_z, n_vis)
    lo = jnp.where(zero_up, 0.0, lo)
    hi = jnp.where((c_z < kf) & (hi > 0.0), 0.0, hi)

    def settled(lo, hi, c_lo):
        mid = 0.5 * lo + 0.5 * hi
        return jnp.where((c_lo <= kf) | (mid <= lo) | (mid >= hi), 1.0, 0.0)

    def bis_cond(st):
        _, _, _, done, it = st
        return jnp.logical_and(it < MAX_BISECT, jnp.min(done) < 0.5)

    def bis_body(st):
        lo, hi, c_lo, done, it = st
        for _ in range(BISECT_PER_CHECK):
            mid = 0.5 * lo + 0.5 * hi
            c = count(lambda t: t >= mid)
            live = done < 0.5
            up = live & (c >= kf)
            dn = live & (c < kf)
            lo = jnp.where(up, mid, lo)
            c_lo = jnp.where(up, c, c_lo)
            hi = jnp.where(dn, mid, hi)
            done = jnp.maximum(done, settled(lo, hi, c_lo))
        return lo, hi, c_lo, done, it + BISECT_PER_CHECK

    done = jnp.maximum(settled(lo, hi, c_lo), zero_tie)
    lo, hi, c_lo, _, _ = lax.while_loop(bis_cond, bis_body, (lo, hi, c_lo, done, jnp.int32(0)))
    any_tie = jnp.max(jnp.where(c_lo > kf, 1.0, 0.0)) > 0.5

    @pl.when(jnp.logical_not(any_tie))
    def _():
        def body(j, _):
            idx_scr[j] = jnp.where(idx_scr[j] >= lo, 0.0, NEG)
            return 0
        lax.fori_loop(0, n_used, body, 0)

    @pl.when(any_tie)
    def _():
        need = kf - count(lambda t: t > lo)
        r = lax.broadcasted_iota(jnp.int32, (tk, tk), 0)
        c = lax.broadcasted_iota(jnp.int32, (tk, tk), 1)
        before = jnp.where(c < r, 1.0, 0.0).astype(BF16)

        def body(j, seen):
            t = idx_scr[j]
            eq = jnp.where(t == lo, 1.0, 0.0)
            rank = _dot(before, eq.astype(BF16)) + seen
            take = (t > lo) | ((t == lo) & (rank < need))
            idx_scr[j] = jnp.where(take, 0.0, NEG)
            return seen + jnp.sum(_rows8(eq, jnp.add), axis=0, keepdims=True)
        lax.fori_loop(0, n_used, body, jnp.zeros((1, tq), F32))


def _attend_kernel(qi_ref, wit_ref, ki_ref, qb_ref, kb_ref, vbt_ref, bias_ref, q_ref, k_ref, vt_ref, oa_ref, ob_ref,
                   idx_scr, *scratch, t, topk):
    tq = tk = t
    last = pl.program_id(1)
    n_used = last + 1
    visible = _diagonal_tile_visible(t)

    wit = wit_ref[0]

    def idx_tile(j):
        k_j = ki_ref[0, _key_rows(j, tk), :]
        acc = jnp.zeros((tk, tq), F32)
        for hd in range(IDX_HEADS):
            s = _dot(k_j, qi_ref[0, hd * IDX_DIM:(hd + 1) * IDX_DIM, :])
            acc = acc + jnp.maximum(s, 0.0) * wit[hd:hd + 1, :]
        return acc

    def idx_body(j, stats):
        t = idx_tile(j)
        idx_scr[j] = t
        return _score_stats(stats, t)

    def idx_pair(n, stats):
        return idx_body(2 * n + 1, idx_body(2 * n, stats))

    stats = lax.fori_loop(0, last // 2, idx_pair, _score_stats_init(tq))
    stats = lax.cond(last % 2 == 1, lambda st: idx_body(last - 1, st), lambda st: st, stats)
    t_last = jnp.where(visible, idx_tile(last), -jnp.inf)
    idx_scr[last] = t_last
    stats = _score_stats(stats, t_last, visible)
    lane = lax.broadcasted_iota(jnp.int32, (1, tq), 1)
    n_vis = last * tk + (lane // CHUNK + 1) * CHUNK
    _select_topk(idx_scr, n_used, stats, n_vis.astype(F32), float(topk))

    softmax = _Softmax(*scratch)
    kv_lanes = lambda hd: slice((hd // DSA_GROUP) * DSA_HEAD_DIM, (hd // DSA_GROUP + 1) * DSA_HEAD_DIM)

    def logits(hd, j, kind):
        if hd < MLA_HEADS:
            sl = slice(hd * LANES, (hd + 1) * LANES)
            s = _dot(k_ref[0, _key_rows(j, tk), sl], q_ref[0, sl, :])
            return s + jnp.where(visible, 0.0, NEG) if kind == 2 else s
        hd -= MLA_HEADS
        s = _dot(kb_ref[0, hd // DSA_GROUP, _key_rows(j, tk), :], qb_ref[0, hd * DSA_HEAD_DIM:(hd + 1) * DSA_HEAD_DIM, :])
        return s + idx_scr[j] if kind == 0 else s + (idx_scr[j] + bias_ref[hd, kind - 1])

    def values(hd, j):
        if hd < MLA_HEADS:
            return vt_ref[0, j, hd * V_DIM:(hd + 1) * V_DIM, :]
        return vbt_ref[0, j, kv_lanes(hd - MLA_HEADS), :]

    pl.when(last < 2)(lambda: softmax.run(last, logits, values))
    pl.when(last >= 2)(lambda: softmax.run_staggered(last, logits, values, MLA_HEADS))
    softmax.finish(oa_ref, ob_ref)


def _attend(qi, wit, ki, qb, kb, vbt, bias, q, k, vt, *, t, topk):
    assert V_DIM == DSA_HEAD_DIM and MLA_HEADS == DSA_HEADS
    nb, _, seq = qb.shape
    n_tiles = seq // t
    per_q = lambda w: pl.BlockSpec((1, w, t), lambda b, i: (b, 0, i))
    per_b = lambda a: pl.BlockSpec((1,) + a.shape[1:], lambda b, i: (b,) + (0,) * (a.ndim - 1))
    out = lambda w: pl.BlockSpec((1, t, w), lambda b, i: (b, i, 0))
    return pl.pallas_call(
        functools.partial(_attend_kernel, t=t, topk=topk),
        out_shape=[jax.ShapeDtypeStruct((nb, seq, W_A), F32), jax.ShapeDtypeStruct((nb, seq, W_B), F32)],
        grid=(nb, n_tiles),
        in_specs=[per_q(IDX_HEADS * IDX_DIM), per_q(IDX_HEADS), per_b(ki), per_q(W_B), per_b(kb), per_b(vbt),
                  pl.BlockSpec(bias.shape, lambda b, i: (0, 0, 0, 0)), per_q(QK_PAD), per_b(k), per_b(vt)],
        out_specs=[out(W_A), out(W_B)],
        scratch_shapes=([pltpu.VMEM((n_tiles, t, t), F32)]
                        + _Softmax.scratch(MLA_HEADS + DSA_HEADS, DSA_HEAD_DIM, t, t)),
        compiler_params=_params(2), name="attend_causal",
    )(qi, wit, ki, qb, kb, vbt, bias, q, k, vt)


def _two_pass_softmax(n_tiles, logits, values, s_scr, dv):
    mx = None
    for t in range(n_tiles):
        s = logits(t)
        s_scr[t] = s
        part = _rows8(s, jnp.maximum, ways=4)
        mx = part if mx is None else jnp.maximum(mx, part)
    m = jnp.max(mx, axis=0, keepdims=True)
    acc = None
    for t in range(n_tiles):
        p = jnp.exp2(s_scr[t] - m).astype(BF16)
        vt = values(t)
        part = _dot(jnp.concatenate([vt, jnp.ones((_Softmax.ONES_ROWS, vt.shape[1]), BF16)], axis=0), p)
        acc = part if acc is None else acc + part
    return acc[0:dv] / acc[dv:dv + 1]


def _mla_cached_kernel(q_ref, ckv_c_ref, kr_c_ref, ckv_n_ref, kr_n_ref, wuk_ref, wuv_ref, sel_ref, o_ref, s_scr, *,
                       ts, tk, n_cache, n_valid_last):
    lanes = MLA_HEADS * ts
    q_lat, q_rope = [], []
    for hd in range(MLA_HEADS):
        qg = q_ref[0, :, hd * LANES:(hd + 1) * LANES]
        q_lat.append(_dot(qg, wuk_ref[hd]).astype(BF16))
        q_rope.append(_dot(qg, sel_ref[...]).astype(BF16))
    q_lat = jnp.concatenate(q_lat, axis=0)
    q_rope = jnp.concatenate(q_rope, axis=0)

    def latent(t):
        return ckv_c_ref[0, t * tk:(t + 1) * tk, :] if t < n_cache else ckv_n_ref[0]

    def logits(t):
        kr = kr_c_ref[0, t * tk:(t + 1) * tk, :] if t < n_cache else kr_n_ref[0]
        s = _dot_nt(latent(t).astype(BF16), q_lat) + _dot_nt(kr.astype(BF16), q_rope)
        if t == n_cache:
            s = s + jnp.where(lax.broadcasted_iota(jnp.int32, s.shape, 0) < n_valid_last, 0.0, NEG)
        return s

    o_lat = _two_pass_softmax(n_cache + 1, logits, lambda t: latent(t).T.astype(BF16), s_scr, KV_LORA)
    o_lat = o_lat.astype(BF16)
    head_of_lane = lax.broadcasted_iota(jnp.int32, (V_DIM, lanes), 1) // ts
    out = jnp.zeros((V_DIM, lanes), F32)
    for hd in range(MLA_HEADS):
        out = jnp.where(head_of_lane == hd, _dot(wuv_ref[hd], o_lat), out)
    o_ref[0] = out


def _mla_cached(q, ckv_cache, kr_cache, ckv_new, kr_new, wuk, wuv, sel, *, layer, tk, n_valid_last):
    nb, ts, _ = q.shape
    past = ckv_cache.shape[2]
    lanes = MLA_HEADS * ts
    per_b = lambda a: pl.BlockSpec((1,) + a.shape[1:], lambda b: (b,) + (0,) * (a.ndim - 1))
    const = lambda a: pl.BlockSpec(a.shape, lambda b: (0,) * a.ndim)
    return pl.pallas_call(
        functools.partial(_mla_cached_kernel, ts=ts, tk=tk, n_cache=past // tk, n_valid_last=n_valid_last),
        out_shape=jax.ShapeDtypeStruct((nb, V_DIM, lanes), F32), grid=(nb,),
        in_specs=[per_b(q), _layer_rows(ckv_cache, layer), _layer_rows(kr_cache, layer), per_b(ckv_new),
                  per_b(kr_new), const(wuk), const(wuv), const(sel)],
        out_specs=pl.BlockSpec((1, V_DIM, lanes), lambda b: (b, 0, 0)),
        scratch_shapes=[pltpu.VMEM((past // tk + 1, tk, lanes), F32)],
        compiler_params=_params(1), name="mla_cached",
    )(q, ckv_cache, kr_cache, ckv_new, kr_new, wuk, wuv, sel)


def _dsa_cached_kernel(qi_ref, w_ref, ki_c_ref, ki_n_ref, qb_ref, kb_c_ref, kb_n_ref, vb_c_ref, vb_n_ref, bias_ref,
                       o_ref, idx_scr, s_scr, *, ts, tk, n_cache, n_valid_last, topk):
    n_tiles = n_cache + 1
    lanes = DSA_GROUP * ts
    tile = lambda c_ref, n_ref, t: c_ref[0, t * tk:(t + 1) * tk, :] if t < n_cache else n_ref[0]
    visible = lax.broadcasted_iota(jnp.int32, (tk, lanes), 0) < n_valid_last

    stats = _score_stats_init(lanes)
    for t in range(n_tiles):
        s = _dot_nt(tile(ki_c_ref, ki_n_ref, t).astype(BF16), qi_ref[0])
        r = jnp.maximum(s, 0.0) * w_ref[0]
        x = r[:, 0:lanes]
        for blk in range(1, IDX_HEADS * ts // lanes):
            x = x + r[:, blk * lanes:(blk + 1) * lanes]
        shift = lanes // 2
        while shift >= ts:
            x = x + pltpu.roll(x, shift, 1)
            shift //= 2
        if t == n_cache:
            x = jnp.where(visible, x, -jnp.inf)
        idx_scr[t] = x
        stats = _score_stats(stats, x, visible if t == n_cache else None)
    n_vis = jnp.full((1, lanes), float(n_cache * tk + n_valid_last), F32)
    _select_topk(idx_scr, n_tiles, stats, n_vis, float(topk))

    for g in range(DSA_KV_HEADS):
        def logits(t, g=g):
            s = _dot_nt(tile(kb_c_ref, kb_n_ref, t).astype(BF16), qb_ref[0, g]) + idx_scr[t]
            return s + bias_ref[g, t - (n_tiles - 2)] if t >= n_tiles - 2 else s

        def values(t, g=g):
            return tile(vb_c_ref, vb_n_ref, t).T[g * DSA_HEAD_DIM:(g + 1) * DSA_HEAD_DIM, :].astype(BF16)

        o_ref[0, g] = _two_pass_softmax(n_tiles, logits, values, s_scr, DSA_HEAD_DIM)


def _layer_rows(cache, layer):
    return pl.BlockSpec((None, 1) + cache.shape[2:], lambda b: (layer, b, 0, 0))


def _dsa_cached(qi, w, ki_cache, ki_new, qb, kb_cache, kb_new, vb_cache, vb_new, bias, *, layer, ts, tk,
                n_valid_last, topk):
    nb = qi.shape[0]
    past = ki_cache.shape[2]
    n_tiles = past // tk + 1
    lanes = DSA_GROUP * ts
    per_b = lambda a: pl.BlockSpec((1,) + a.shape[1:], lambda b: (b,) + (0,) * (a.ndim - 1))
    cached = lambda a: _layer_rows(a, layer)
    return pl.pallas_call(
        functools.partial(_dsa_cached_kernel, ts=ts, tk=tk, n_cache=past // tk, n_valid_last=n_valid_last, topk=topk),
        out_shape=jax.ShapeDtypeStruct((nb, DSA_KV_HEADS, DSA_HEAD_DIM, lanes), F32), grid=(nb,),
        in_specs=[per_b(qi), per_b(w), cached(ki_cache), per_b(ki_new), per_b(qb), cached(kb_cache), per_b(kb_new),
                  cached(vb_cache), per_b(vb_new), pl.BlockSpec(bias.shape, lambda b: (0,) * bias.ndim)],
        out_specs=pl.BlockSpec((1, DSA_KV_HEADS, DSA_HEAD_DIM, lanes), lambda b: (b, 0, 0, 0)),
        scratch_shapes=[pltpu.VMEM((n_tiles, tk, lanes), F32), pltpu.VMEM((n_tiles, tk, lanes), F32)],
        compiler_params=_params(1), name="dsa_cached",
    )(qi, w, ki_cache, ki_new, qb, kb_cache, kb_new, vb_cache, vb_new, bias)


def _out_kernel(x_ref, shift_ref, scale_ref, gate_ref, gn_ref, oa_ref, ob_ref, wg_ref, woa_ref, wob_ref, wout_ref,
                gf_ref, o_ref, *, final):
    x = x_ref[0]
    h16 = (_rms(x, gn_ref[...]) * (1.0 + scale_ref[0]) + shift_ref[0]).astype(BF16)

    def gcols(a):
        return _dot(h16, wg_ref[:, a[0]:a[1]])

    sa = _sigmoid(gcols(G_MA))
    sb = _sigmoid(gcols(G_MB))
    ga = gcols(G_GA)
    gb = gcols(G_GB)
    ya = _dot((oa_ref[0] * (ga * _sigmoid(ga))).astype(BF16), woa_ref[...])
    yb = _dot((ob_ref[0] * (gb * _sigmoid(gb))).astype(BF16), wob_ref[...])
    m = sa * ya + sb * yb
    y = x + gate_ref[0] * _dot(m.astype(BF16), wout_ref[...])
    o_ref[0] = _rms(y, gf_ref[...]) if final else y


def _out(x, shift, scale, gate, gn, oa, ob, wg, woa, wob, wout, gf, *, tm, final):
    nb, t, _ = x.shape
    r = shift.shape[1]
    rb = 1 if r == 1 else tm
    tok = lambda w: pl.BlockSpec((1, tm, w), lambda b, i: (b, i, 0))
    mod = pl.BlockSpec((1, rb, D_MODEL), (lambda b, i: (b, 0, 0)) if r == 1 else (lambda b, i: (b, i, 0)))
    const = lambda a: pl.BlockSpec(a.shape, lambda b, i: (0,) * a.ndim, pipeline_mode=pl.Buffered(1))
    return pl.pallas_call(
        functools.partial(_out_kernel, final=final),
        out_shape=jax.ShapeDtypeStruct(x.shape, F32), grid=(nb, t // tm),
        in_specs=[tok(D_MODEL), mod, mod, mod, const(gn), tok(W_A), tok(W_B), const(wg), const(woa), const(wob),
                  const(wout), const(gf)],
        out_specs=tok(D_MODEL), compiler_params=_params(2), name="out_final" if final else "out_mix",
    )(x, shift, scale, gate, gn, oa, ob, wg, woa, wob, wout, gf)


def _pack_weights(w_in, w_uq, w_ukv, w_oa, w_ob, w_out):
    depth = w_in.shape[0]
    offs = np.concatenate([[0], np.cumsum(SPLIT_SIZES)])
    cq, ckv, kr, ga, qb, kb, vb, qi, ki, wi, gb, ma, mb = [w_in[:, :, offs[n]:offs[n + 1]] for n in range(13)]
    z = lambda w: jnp.zeros((depth, D_MODEL, w), F32)
    krg = jnp.concatenate([kr, z(NOPE_DIM - ROPE_DIM), kr, z(LANES - NOPE_DIM - ROPE_DIM)], axis=-1)
    sm = jnp.concatenate([ki, wi, z(LANES - IDX_DIM - IDX_HEADS)], axis=-1)
    wa = jnp.concatenate([cq, ckv, krg, sm, qb, kb, vb, qi], axis=-1).astype(BF16)
    wg = jnp.concatenate([ga, gb, ma, mb], axis=-1).astype(BF16)
    qd = NOPE_DIM + ROPE_DIM
    wuq = jnp.pad(w_uq.reshape(depth, Q_LORA, MLA_HEADS, qd), ((0, 0), (0, 0), (0, 0), (0, LANES - qd)))
    wuq = wuq.reshape(depth, Q_LORA, QK_PAD).astype(BF16)
    ukv = w_ukv.reshape(depth, KV_LORA, MLA_HEADS, NOPE_DIM + V_DIM)
    wkn = jnp.pad(ukv[..., :NOPE_DIM], ((0, 0), (0, 0), (0, 0), (0, LANES - NOPE_DIM)))
    wkn = wkn.reshape(depth, KV_LORA, QK_PAD).astype(BF16)
    wv = ukv[..., NOPE_DIM:].reshape(depth, KV_LORA, W_A).astype(BF16)
    wuk = jnp.pad(ukv[..., :NOPE_DIM].transpose(0, 2, 3, 1), ((0, 0), (0, 0), (0, LANES - NOPE_DIM), (0, 0)))
    wuv = ukv[..., NOPE_DIM:].transpose(0, 2, 3, 1)
    return (wa, wg, wuq, wkn, wv, w_oa.astype(BF16), w_ob.astype(BF16), w_out.astype(BF16), wuk.astype(BF16),
            wuv.astype(BF16))


def _rope_tables(pos):
    half = ROPE_DIM // 2
    inv = ROPE_THETA ** (-jnp.arange(half, dtype=F32) / half)
    ang = pos.astype(F32)[:, None] * inv
    cos, sin = jnp.cos(ang), jnp.sin(ang)
    n = pos.shape[0]
    z = lambda w: jnp.zeros((n, w), F32)
    pad = LANES - NOPE_DIM - ROPE_DIM
    q_c = jnp.concatenate([jnp.ones((n, NOPE_DIM), F32), cos, cos, z(pad)], axis=-1) * (MLA_SCALE * LOG2E)
    q_s1 = jnp.concatenate([z(NOPE_DIM), -sin, z(half), z(pad)], axis=-1) * (MLA_SCALE * LOG2E)
    q_s2 = jnp.concatenate([z(NOPE_DIM), z(half), sin, z(pad)], axis=-1) * (MLA_SCALE * LOG2E)
    gap = NOPE_DIM - ROPE_DIM
    k_c = jnp.concatenate([cos, cos, z(gap), cos, cos, z(pad)], axis=-1)
    k_s1 = jnp.concatenate([-sin, z(half), z(gap), -sin, z(half), z(pad)], axis=-1)
    k_s2 = jnp.concatenate([z(half), sin, z(gap), z(half), sin, z(pad)], axis=-1)
    return (q_c, q_s1, q_s2), (k_c, k_s1, k_s2)


def _rope_selector():
    p = np.zeros((LANES, ROPE_DIM), np.float32)
    p[NOPE_DIM + np.arange(ROPE_DIM), np.arange(ROPE_DIM)] = 1.0
    return jnp.asarray(p, dtype=BF16)


def kernel(x_prompt, x_sample, c_prompt, c_sample, cache_ckv, cache_krope, cache_kb, cache_vb, cache_kidx, w_ada,
           b_ada, g_norm, w_in, g_qnorm, w_uq, g_kvnorm, w_ukv, w_oa, w_ob, w_out, rel_bias, g_final):
    depth = w_in.shape[0]
    bp, t_p, _ = x_prompt.shape
    bs, t_s, _ = x_sample.shape
    past = cache_ckv.shape[2]
    topk_p = min(TOPK_MAX, t_p // 4)
    topk_s = min(TOPK_MAX, (past + t_s) // 4)
    assert t_p % TM == 0 and TM % TK == 0 and past % TK == 0 and t_s <= TK and t_s % 8 == 0
    assert (DSA_GROUP * t_s) % LANES == 0 and (DSA_GROUP * t_s) & (DSA_GROUP * t_s - 1) == 0
    n_s = bs * t_s

    wa, wg, wuq, wkn, wv, woa, wob, wout, wuk, wuv = _pack_weights(w_in, w_uq, w_ukv, w_oa, w_ob, w_out)
    rope_sel = _rope_selector()
    cache_kb2 = cache_kb.reshape(depth, bs, past, KV_DIM)
    cache_vb2 = cache_vb.reshape(depth, bs, past, KV_DIM)
    mod = _modulation(jnp.concatenate([c_prompt, c_sample], axis=0), w_ada, b_ada)
    mod = mod.reshape(depth, bp + bs, 3, 1, D_MODEL)
    tabs_p = _rope_tables(jnp.arange(t_p, dtype=jnp.int32))
    tabs_s = _rope_tables(jnp.tile(past + jnp.arange(t_s, dtype=jnp.int32), bs))
    bias_p = _bias_tiles(rel_bias, TK, TK)
    bias_s = _bias_tiles(rel_bias, TK, t_s).reshape(DSA_KV_HEADS, DSA_GROUP, 2, TK, t_s)
    bias_s = bias_s.transpose(0, 2, 3, 1, 4).reshape(DSA_KV_HEADS, 2, TK, DSA_GROUP * t_s)
    row = lambda v: v.reshape(1, -1)

    xp = x_prompt
    xs = x_sample.reshape(1, n_s, D_MODEL)
    caches_p = caches_s = None
    for l in range(depth):
        final = l == depth - 1
        gn, gq, gkv, gf = row(g_norm[l]), row(g_qnorm[l]), row(g_kvnorm[l]), row(g_final)

        shift, scale, gate = (mod[l, :bp, n] for n in range(3))
        *caches_p, q, qb, qi, wit, kcat, vat, kb16, vbt, ki16 = _proj(
            xp, shift, scale, gn, wa[l], gq, wuq[l], gkv, wkn[l], wv[l], *tabs_p, caches_p, layer=l, depth=depth,
            tm=TM, tk=TK, emit_kv=True)
        oa, ob = _attend(qi, wit, ki16, qb, kb16, vbt, bias_p, q, kcat, vat, t=TK, topk=topk_p)
        xp = _out(xp, shift, scale, gate, gn, oa, ob, wg[l], woa[l], wob[l], wout[l], gf, tm=TM, final=final)

        shift, scale, gate = (jnp.repeat(mod[l, bp:, n, 0], t_s, axis=0)[None] for n in range(3))
        *caches_s, q, qb, qi, wit = _proj(
            xs, shift, scale, gn, wa[l], gq, wuq[l], gkv, wkn[l], wv[l], *tabs_s, caches_s, layer=l, depth=depth,
            tm=n_s, tk=TK, emit_kv=False)
        per_seq = lambda a: a.reshape((bs, t_s) + a.shape[2:])
        ckv, kr, kb, vb, ki = (per_seq(a[l]) for a in caches_s)
        q, qb, qi = per_seq(q), per_seq(qb), per_seq(qi)
        new_tile = lambda a: jnp.pad(a, ((0, 0), (0, TK - t_s), (0, 0)))
        oa = _mla_cached(q, cache_ckv, cache_krope, new_tile(ckv), new_tile(kr), wuk[l], wuv[l], rope_sel, layer=l,
                         tk=TK, n_valid_last=t_s)
        oa = oa.reshape(bs, V_DIM, MLA_HEADS, t_s).transpose(0, 3, 2, 1).reshape(1, n_s, W_A)
        qi_l = qi.reshape(bs, t_s, IDX_HEADS, IDX_DIM).transpose(0, 2, 1, 3).reshape(bs, IDX_HEADS * t_s, IDX_DIM)
        w_l = wit.reshape(IDX_HEADS, bs, t_s).transpose(1, 0, 2).reshape(bs, 1, IDX_HEADS * t_s)
        qb_l = qb.reshape(bs, t_s, DSA_KV_HEADS, DSA_GROUP, DSA_HEAD_DIM).transpose(0, 2, 3, 1, 4)
        qb_l = qb_l.reshape(bs, DSA_KV_HEADS, DSA_GROUP * t_s, DSA_HEAD_DIM)
        qb_l = jnp.stack([jnp.pad(qb_l[:, g], ((0, 0), (0, 0), (g * DSA_HEAD_DIM, KV_DIM - (g + 1) * DSA_HEAD_DIM)))
                          for g in range(DSA_KV_HEADS)], axis=1)
        ob = _dsa_cached(qi_l, w_l, cache_kidx, new_tile(ki), qb_l, cache_kb2, new_tile(kb), cache_vb2, new_tile(vb),
                         bias_s, layer=l, ts=t_s, tk=TK, n_valid_last=t_s, topk=topk_s)
        ob = ob.reshape(bs, DSA_KV_HEADS, DSA_HEAD_DIM, DSA_GROUP, t_s).transpose(0, 4, 1, 3, 2).reshape(1, n_s, W_B)
        xs = _out(xs, shift, scale, gate, gn, oa, ob, wg[l], woa[l], wob[l], wout[l], gf, tm=n_s, final=final)

    def cache_outputs(stacks, nb, t):
        ckv, kr, kb, vb, ki = (a.reshape((depth, nb, t, a.shape[-1])) for a in stacks)
        kvh = (depth, nb, t, DSA_KV_HEADS, DSA_HEAD_DIM)
        return ckv, kr, kb.reshape(kvh), vb.reshape(kvh), ki

    return (xp, xs.reshape(bs, t_s, D_MODEL)) + cache_outputs(caches_p, bp, t_p) + cache_outputs(caches_s, bs, t_s)
```
